```python
import jax, jax.numpy as jnp
from jax import lax
import numpy as np

D_MODEL = 1024
BATCH = 4
SEQ = 4096
DEPTH = 2
DEC_BATCH = 32
DEC_SEQ = 8
PAST_LEN = 8192
PAGE_SIZE = 128

NSA_HEADS = 8
NSA_KV_HEADS = 2
NSA_HPG = NSA_HEADS // NSA_KV_HEADS
NSA_HD = 64
NSA_N_KV = 4
CMP_STRIDE = 16
CMP_BLOCK = 2 * CMP_STRIDE
CMP_HIDDEN = 128
SEL_BLOCK = 64
SEL_TOPK = 16
SEL_FORCE = 1e4
WINDOW = 512
NSA_QBLOCK = 128

RET_HEADS = 4
RET_DK = 128
RET_DV = 128
RET_CHUNK = 128
ROPE_BASE = 10000.0

RNN_WIDTH = 512
RNN_BLOCKS = 8
RNN_BW = RNN_WIDTH // RNN_BLOCKS
RG_CONV = 4
RG_C = 8.0

D_FF = ((8 * D_MODEL // 3 + 127) // 128) * 128
FFN_CONV = 3

N_BRANCH = 3
EPS = 1e-6
IN_WIDTHS = (NSA_HEADS * NSA_HD, 6 * NSA_KV_HEADS * NSA_HD, 3 * NSA_HEADS,
             RET_HEADS * RET_DK, RET_HEADS * RET_DK, RET_HEADS * RET_DV, RET_HEADS * RET_DV,
             RNN_WIDTH, RNN_WIDTH, N_BRANCH * D_MODEL)
D_IN = sum(IN_WIDTHS)

kernel_name = 'hybrid_nsa_retention_rglru_decode_step'


def rmsnorm(x, g):
    xf = x.astype(jnp.float32)
    y = xf * lax.rsqrt(jnp.mean(xf * xf, axis=-1, keepdims=True) + EPS)
    return (y * g.astype(jnp.float32)).astype(x.dtype)


def masked_softmax(s, mask):
    s = jnp.where(mask, s, -jnp.inf)
    m = jnp.max(s, axis=-1, keepdims=True)
    e = jnp.where(mask, jnp.exp(s - jnp.where(jnp.isfinite(m), m, 0.0)), 0.0)
    den = jnp.sum(e, axis=-1, keepdims=True)
    return e / jnp.where(den > 0, den, 1.0)


def split_cols(u, widths):
    cuts = []
    acc = 0
    for w in widths[:-1]:
        acc += w
        cuts.append(acc)
    return jnp.split(u, cuts, axis=-1)


def causal_dwconv(x, buf, w, b):
    width = w.shape[0]
    T = x.shape[1]
    xp = jnp.concatenate([buf.astype(x.dtype), x], axis=1)
    y = b + xp[:, 0:T] * w[0]
    for k in range(1, width):
        y = y + xp[:, k:k + T] * w[k]
    return y.astype(x.dtype), xp[:, xp.shape[1] - (width - 1):]


def rotary(x, pos):
    half = x.shape[-1] // 2
    freq = ROPE_BASE ** (-jnp.arange(half, dtype=jnp.float32) / half)
    ang = pos.astype(jnp.float32)[:, None] * freq[None, :]
    cos = jnp.cos(ang)[None, :, None, :]
    sin = jnp.sin(ang)[None, :, None, :]
    xf = x.astype(jnp.float32)
    x1, x2 = xf[..., :half], xf[..., half:]
    return jnp.concatenate([x1 * cos - x2 * sin, x1 * sin + x2 * cos], axis=-1)


def nsa_compress(rows, pe, w1, b1, w2):
    B, T, K, D = rows.shape
    nch = T // CMP_STRIDE
    ch = rows[:, :nch * CMP_STRIDE].reshape(B, nch, CMP_STRIDE, K, D)
    lo = jnp.einsum('bnpkd,pdh->bnkh', ch + pe[:CMP_STRIDE][:, None, :], w1[:CMP_STRIDE])
    hi = jnp.einsum('bnpkd,pdh->bnkh', ch + pe[CMP_STRIDE:][:, None, :], w1[CMP_STRIDE:])
    hid = jax.nn.gelu(lo[:, :-1] + hi[:, 1:] + b1)
    return jnp.einsum('bnkh,hd->bnkd', hid, w2)


def nsa_branches(q, q_pos, kc, vc, ks_blk, vs_blk, sel_map, kw, vw, kw_pos):
    B, Q, K, R, D = q.shape
    N = kc.shape[1]
    NS = ks_blk.shape[2]
    scale = D ** -0.5
    qf = q.astype(jnp.float32)
    c_end = jnp.arange(N) * CMP_STRIDE + (CMP_BLOCK - 1)
    s_c = jnp.einsum('bqkrd,bnkd->bqkrn', qf, kc.astype(jnp.float32)) * scale
    p_c = masked_softmax(s_c, (c_end[None, :] <= q_pos[:, None])[None, :, None, None, :])
    o_c = jnp.einsum('bqkrn,bnkd->bqkrd', p_c, vc.astype(jnp.float32))
    imp = jnp.einsum('bqkn,ns->bqks', jnp.sum(p_c, axis=3), sel_map)
    j = jnp.arange(NS)[None, :]
    cur = (q_pos // SEL_BLOCK)[:, None]
    forced = (j == 0) | (j == cur) | (j == cur - 1)
    avail = j * SEL_BLOCK <= q_pos[:, None]
    imp = jnp.where(forced[None, :, None, :], SEL_FORCE, imp)
    imp = jnp.where(avail[None, :, None, :], imp, -SEL_FORCE)
    _, idx = lax.top_k(imp, min(SEL_TOPK, NS))
    bi = jnp.arange(B)[:, None, None, None]
    gi = jnp.arange(K)[None, None, :, None]
    k_g = ks_blk[bi, gi, idx].reshape(B, Q, K, -1, D)
    v_g = vs_blk[bi, gi, idx].reshape(B, Q, K, -1, D)
    tok = (idx[..., None] * SEL_BLOCK + jnp.arange(SEL_BLOCK)).reshape(B, Q, K, 1, -1)
    s_s = jnp.einsum('bqkrd,bqkjd->bqkrj', qf, k_g.astype(jnp.float32)) * scale
    p_s = masked_softmax(s_s, tok <= q_pos[None, :, None, None, None])
    o_s = jnp.einsum('bqkrj,bqkjd->bqkrd', p_s, v_g.astype(jnp.float32))
    s_w = jnp.einsum('bqkrd,bskd->bqkrs', qf, kw.astype(jnp.float32)) * scale
    dpos = q_pos[:, None] - kw_pos[None, :]
    wmask = (dpos >= 0) & (dpos <= WINDOW) & (kw_pos[None, :] >= 0)
    p_w = masked_softmax(s_w, wmask[None, :, None, None, :])
    o_w = jnp.einsum('bqkrs,bskd->bqkrd', p_w, vw.astype(jnp.float32))
    return o_c, o_s, o_w


def nsa_mixer(q, kv, gates, q_pos, past_rows, win_buf, pe, w1, b1, w2):
    B, Q = q.shape[0], q.shape[1]
    K, R, D = NSA_KV_HEADS, NSA_HPG, NSA_HD
    kv6 = kv.reshape(B, Q, 6, K, D)
    new_rows = kv6[:, :, :NSA_N_KV]
    win_rows = kv6[:, :, NSA_N_KV:]
    rows = new_rows if past_rows is None else jnp.concatenate([past_rows.astype(kv.dtype), new_rows], axis=1)
    T_all = rows.shape[1]
    kc = nsa_compress(rows[:, :, 0], pe[0], w1[0], b1[0], w2[0])
    vc = nsa_compress(rows[:, :, 1], pe[1], w1[1], b1[1], w2[1])
    N = kc.shape[1]
    NS = -(-T_all // SEL_BLOCK)
    sel = jnp.pad(rows[:, :, 2:], ((0, 0), (0, NS * SEL_BLOCK - T_all), (0, 0), (0, 0), (0, 0)))
    sel = sel.reshape(B, NS, SEL_BLOCK, 2, K, D).transpose(3, 0, 4, 1, 2, 5)
    ks_blk, vs_blk = sel[0], sel[1]
    ci = jnp.arange(N)[:, None] * CMP_STRIDE
    sj = jnp.arange(NS)[None, :] * SEL_BLOCK
    sel_map = ((ci < sj + SEL_BLOCK) & (ci + CMP_BLOCK > sj)).astype(jnp.float32)
    qh = q.reshape(B, Q, K, R, D)
    if past_rows is None:
        win_pad = jnp.pad(win_rows, ((0, 0), (WINDOW, 0), (0, 0), (0, 0), (0, 0)))
        nblk = Q // NSA_QBLOCK

        def body(args):
            qb, pb, b0 = args
            slab = lax.dynamic_slice_in_dim(win_pad, b0 * NSA_QBLOCK, WINDOW + NSA_QBLOCK, axis=1)
            kpos = b0 * NSA_QBLOCK - WINDOW + jnp.arange(WINDOW + NSA_QBLOCK)
            return nsa_branches(qb, pb, kc, vc, ks_blk, vs_blk, sel_map, slab[:, :, 0], slab[:, :, 1], kpos)

        outs = lax.map(body, (jnp.moveaxis(qh.reshape(B, nblk, NSA_QBLOCK, K, R, D), 1, 0),
                              q_pos.reshape(nblk, NSA_QBLOCK), jnp.arange(nblk)))
        o_c, o_s, o_w = [jnp.moveaxis(o, 0, 1).reshape(B, Q, K, R, D) for o in outs]
        new_win = win_rows[:, Q - min(WINDOW, Q):]
    else:
        P = past_rows.shape[1]
        Wb = win_buf.shape[1]
        keys = jnp.concatenate([win_buf.astype(kv.dtype), win_rows], axis=1)
        kpos = P - Wb + jnp.arange(Wb + Q)
        o_c, o_s, o_w = nsa_branches(qh, q_pos, kc, vc, ks_blk, vs_blk, sel_map, keys[:, :, 0], keys[:, :, 1], kpos)
        new_win = keys[:, Q:]
    g = jax.nn.sigmoid(gates.astype(jnp.float32)).reshape(B, Q, K, R, 3)
    o = g[..., 0:1] * o_c + g[..., 1:2] * o_s + g[..., 2:3] * o_w
    return o.reshape(B, Q, NSA_HEADS * NSA_HD).astype(q.dtype), new_rows, new_win


def retention_chunk(S, q, k, v, log_g):
    C = q.shape[1]
    qf, kf, vf = q.astype(jnp.float32), k.astype(jnp.float32), v.astype(jnp.float32)
    idx = jnp.arange(C, dtype=jnp.float32)
    diff = idx[:, None] - idx[None, :]
    decay = jnp.where(diff >= 0, jnp.exp(jnp.maximum(diff, 0.0)[None] * log_g[:, None, None]), 0.0)
    scores = jnp.einsum('bihd,bjhd->bhij', qf, kf) * decay[None]
    o_inner = jnp.einsum('bhij,bjhe->bihe', scores, vf)
    cross = jnp.exp((idx[:, None] + 1.0) * log_g[None, :])
    o_cross = jnp.einsum('bihd,bhde->bihe', qf, S) * cross[None, :, :, None]
    kw = jnp.exp((C - 1.0 - idx)[:, None] * log_g[None, :])
    S_new = jnp.exp(C * log_g)[None, :, None, None] * S + jnp.einsum('bjhd,bjhe->bhde', kf * kw[None, :, :, None], vf)
    return o_inner + o_cross, S_new


def retention(q, k, v, S0):
    B, T = q.shape[0], q.shape[1]
    log_g = jnp.log(1.0 - 2.0 ** (-5.0 - jnp.arange(RET_HEADS, dtype=jnp.float32)))
    if T >= RET_CHUNK and T % RET_CHUNK == 0:
        nC = T // RET_CHUNK

        def step(S, qkv):
            o, S = retention_chunk(S, qkv[0], qkv[1], qkv[2], log_g)
            return S, o

        def resh(a):
            return jnp.moveaxis(a.reshape(B, nC, RET_CHUNK, a.shape[2], a.shape[3]), 1, 0)

        S, o = lax.scan(step, S0, (resh(q), resh(k), resh(v)))
        o = jnp.moveaxis(o, 0, 1).reshape(B, T, RET_HEADS, RET_DV)
    else:
        o, S = retention_chunk(S0, q, k, v, log_g)
    return o, S


def head_groupnorm(o, g):
    B, T, H, E = o.shape
    mu = jnp.mean(o, axis=-1, keepdims=True)
    var = jnp.mean(jnp.square(o - mu), axis=-1, keepdims=True)
    return ((o - mu) * lax.rsqrt(var + EPS)).reshape(B, T, H * E) * g.astype(jnp.float32)


def rglru(xc, h0, w_a, b_a, w_x, b_x, lam):
    B, T, W = xc.shape
    xf = xc.astype(jnp.float32)
    xb = xf.reshape(B, T, RNN_BLOCKS, RNN_BW)
    r = jax.nn.sigmoid(jnp.einsum('btnc,ncd->btnd', xb, w_a.astype(jnp.float32)).reshape(B, T, W) + b_a)
    i = jax.nn.sigmoid(jnp.einsum('btnc,ncd->btnd', xb, w_x.astype(jnp.float32)).reshape(B, T, W) + b_x)
    log_a = -RG_C * r * jax.nn.softplus(-lam.astype(jnp.float32))
    a = jnp.exp(log_a)
    bt = jnp.sqrt(-jnp.expm1(2.0 * log_a)) * (i * xf)
    bt = bt.at[:, 0].add(a[:, 0] * h0)

    def comb(e1, e2):
        return (e1[0] * e2[0], e2[0] * e1[1] + e2[1])

    _, h = lax.associative_scan(comb, (a, bt), axis=1)
    return h, h[:, -1]


def decoder_layer(x, c, lp, q_pos, past_rows, win_buf, S0, h0, conv_buf, ffn_buf):
    B, T, _ = x.shape
    mod = jnp.einsum('bd,de->be', jax.nn.silu(c), lp['w_ada']) + lp['b_ada']
    sh1, sc1, gt1, sh2, sc2, gt2 = [m[:, None, :] for m in jnp.split(mod, 6, axis=-1)]
    h = rmsnorm(x, lp['norm1_g']) * (1.0 + sc1) + sh1
    u = h @ lp['w_in']
    nq, nkv, ngate, rq, rk, rv, rg, rx, rgate, mg = split_cols(u, IN_WIDTHS)
    o_a, nsa_rows, new_win = nsa_mixer(nq, nkv, ngate, q_pos, past_rows, win_buf,
                                       lp['cmp_pe'], lp['cmp_w1'], lp['cmp_b1'], lp['cmp_w2'])
    p_a = o_a @ lp['w_br_a']
    rqh = rotary(rq.reshape(B, T, RET_HEADS, RET_DK), q_pos)
    rkh = rotary(rk.reshape(B, T, RET_HEADS, RET_DK), q_pos) * (RET_DK ** -0.5)
    o_r, S_new = retention(rqh, rkh, rv.reshape(B, T, RET_HEADS, RET_DV), S0)
    o_r = head_groupnorm(o_r, lp['ret_gn_g']) * jax.nn.silu(rg.astype(jnp.float32))
    p_b = o_r.astype(x.dtype) @ lp['w_br_b']
    xc, conv_new = causal_dwconv(rx, conv_buf, lp['rg_conv_w'], lp['rg_conv_b'])
    hs, h_last = rglru(xc, h0, lp['rg_w_a'], lp['rg_b_a'], lp['rg_w_x'], lp['rg_b_x'], lp['rg_lambda'])
    p_c = (hs.astype(x.dtype) * jax.nn.gelu(rgate)) @ lp['w_br_c']
    gm = jax.nn.sigmoid(mg.reshape(B, T, N_BRANCH, D_MODEL))
    merged = gm[:, :, 0] * p_a + gm[:, :, 1] * p_b + gm[:, :, 2] * p_c
    x = x + gt1 * (merged @ lp['w_out'])
    h2 = rmsnorm(x, lp['norm2_g']) * (1.0 + sc2) + sh2
    gpre, val = jnp.split(h2 @ lp['ffn_w_up'], 2, axis=-1)
    gc, ffn_new = causal_dwconv(gpre, ffn_buf, lp['ffn_conv_w'], lp['ffn_conv_b'])
    x = x + gt2 * ((jax.nn.silu(gc) * val) @ lp['ffn_w_down'])
    return x, (nsa_rows, new_win, S_new, h_last, conv_new, ffn_new)


def setup_inputs(seed: int = 0) -> dict:
    key = jax.random.key(seed)
    f32 = jnp.float32
    n_pages = PAST_LEN // PAGE_SIZE
    n_used = DEC_BATCH * n_pages
    n_phys = n_used + max(1, n_used // 4)
    w_buf = min(WINDOW, PAST_LEN)
    counter = [0]

    def nrm(shape, scale):
        counter[0] += 1
        return jax.random.normal(jax.random.fold_in(key, counter[0]), shape, f32) * scale

    def gain(shape):
        return 1.0 + nrm(shape, 0.01)

    page_table = jax.random.permutation(jax.random.fold_in(key, 1000), n_phys)[:n_used]
    page_table = page_table.reshape(DEC_BATCH, n_pages).astype(jnp.int32)
    a0 = jax.random.uniform(jax.random.fold_in(key, 1001), (DEPTH, RNN_WIDTH), f32, 0.9, 0.999) ** (1.0 / RG_C)
    rg_lambda = jnp.log(a0) - jnp.log1p(-a0)
    return {
        'x_prompt': nrm((BATCH, SEQ, D_MODEL), 1.0),
        'x_sample': nrm((DEC_BATCH, DEC_SEQ, D_MODEL), 1.0),
        'cache_nsa': nrm((DEPTH, n_phys, PAGE_SIZE, NSA_N_KV, NSA_KV_HEADS, NSA_HD), 1.0),
        'cache_nsa_win': nrm((DEPTH, DEC_BATCH, w_buf, 2, NSA_KV_HEADS, NSA_HD), 1.0),
        'state_ret': nrm((DEPTH, DEC_BATCH, RET_HEADS, RET_DK, RET_DV), 0.5),
        'state_rglru_h': nrm((DEPTH, DEC_BATCH, RNN_WIDTH), 0.5),
        'state_rglru_conv': nrm((DEPTH, DEC_BATCH, RG_CONV - 1, RNN_WIDTH), 1.0),
        'state_ffn_conv': nrm((DEPTH, DEC_BATCH, FFN_CONV - 1, D_FF), 1.0),
        'page_table': page_table,
        'c_prompt': nrm((BATCH, D_MODEL), 1.0),
        'c_sample': nrm((DEC_BATCH, D_MODEL), 1.0),
        'norm1_g': gain((DEPTH, D_MODEL)),
        'norm2_g': gain((DEPTH, D_MODEL)),
        'w_ada': nrm((DEPTH, D_MODEL, 6 * D_MODEL), 0.5 * D_MODEL ** -0.5),
        'b_ada': nrm((DEPTH, 6 * D_MODEL), 0.01),
        'w_in': nrm((DEPTH, D_MODEL, D_IN), D_MODEL ** -0.5),
        'cmp_pe': nrm((DEPTH, 2, CMP_BLOCK, NSA_HD), 0.1),
        'cmp_w1': nrm((DEPTH, 2, CMP_BLOCK, NSA_HD, CMP_HIDDEN), (CMP_BLOCK * NSA_HD) ** -0.5),
        'cmp_b1': nrm((DEPTH, 2, CMP_HIDDEN), 0.01),
        'cmp_w2': nrm((DEPTH, 2, CMP_HIDDEN, NSA_HD), CMP_HIDDEN ** -0.5),
        'ret_gn_g': gain((DEPTH, RET_HEADS * RET_DV)),
        'rg_conv_w': nrm((DEPTH, RG_CONV, RNN_WIDTH), RG_CONV ** -0.5),
        'rg_conv_b': nrm((DEPTH, RNN_WIDTH), 0.01),
        'rg_w_a': nrm((DEPTH, RNN_BLOCKS, RNN_BW, RNN_BW), RNN_BW ** -0.5),
        'rg_b_a': nrm((DEPTH, RNN_WIDTH), 0.01),
        'rg_w_x': nrm((DEPTH, RNN_BLOCKS, RNN_BW, RNN_BW), RNN_BW ** -0.5),
        'rg_b_x': nrm((DEPTH, RNN_WIDTH), 0.01),
        'rg_lambda': rg_lambda,
        'w_br_a': nrm((DEPTH, NSA_HEADS * NSA_HD, D_MODEL), (NSA_HEADS * NSA_HD) ** -0.5),
        'w_br_b': nrm((DEPTH, RET_HEADS * RET_DV, D_MODEL), (RET_HEADS * RET_DV) ** -0.5),
        'w_br_c': nrm((DEPTH, RNN_WIDTH, D_MODEL), RNN_WIDTH ** -0.5),
        'w_out': nrm((DEPTH, D_MODEL, D_MODEL), D_MODEL ** -0.5),
        'ffn_w_up': nrm((DEPTH, D_MODEL, 2 * D_FF), D_MODEL ** -0.5),
        'ffn_conv_w': nrm((DEPTH, FFN_CONV, D_FF), FFN_CONV ** -0.5),
        'ffn_conv_b': nrm((DEPTH, D_FF), 0.01),
        'ffn_w_down': nrm((DEPTH, D_FF, D_MODEL), D_FF ** -0.5),
        'final_norm_g': gain((D_MODEL,)),
    }


def reference(x_prompt, x_sample, cache_nsa, cache_nsa_win, state_ret, state_rglru_h, state_rglru_conv,
              state_ffn_conv, page_table, c_prompt, c_sample, norm1_g, norm2_g, w_ada, b_ada, w_in, cmp_pe,
              cmp_w1, cmp_b1, cmp_w2, ret_gn_g, rg_conv_w, rg_conv_b, rg_w_a, rg_b_a, rg_w_x, rg_b_x, rg_lambda,
              w_br_a, w_br_b, w_br_c, w_out, ffn_w_up, ffn_conv_w, ffn_conv_b, ffn_w_down, final_norm_g):
    B, T = x_prompt.shape[0], x_prompt.shape[1]
    DB, Q = x_sample.shape[0], x_sample.shape[1]
    n_pages = page_table.shape[1]
    past_len = n_pages * cache_nsa.shape[2]
    pos_p = jnp.arange(T, dtype=jnp.int32)
    pos_s = past_len + jnp.arange(Q, dtype=jnp.int32)
    xp, xs = x_prompt, x_sample
    st_p = [[] for _ in range(6)]
    st_s = [[] for _ in range(6)]
    for l in range(DEPTH):
        lp = {'norm1_g': norm1_g[l], 'norm2_g': norm2_g[l], 'w_ada': w_ada[l], 'b_ada': b_ada[l],
              'w_in': w_in[l], 'cmp_pe': cmp_pe[l], 'cmp_w1': cmp_w1[l], 'cmp_b1': cmp_b1[l],
              'cmp_w2': cmp_w2[l], 'ret_gn_g': ret_gn_g[l], 'rg_conv_w': rg_conv_w[l],
              'rg_conv_b': rg_conv_b[l], 'rg_w_a': rg_w_a[l], 'rg_b_a': rg_b_a[l], 'rg_w_x': rg_w_x[l],
              'rg_b_x': rg_b_x[l], 'rg_lambda': rg_lambda[l], 'w_br_a': w_br_a[l], 'w_br_b': w_br_b[l],
              'w_br_c': w_br_c[l], 'w_out': w_out[l], 'ffn_w_up': ffn_w_up[l], 'ffn_conv_w': ffn_conv_w[l],
              'ffn_conv_b': ffn_conv_b[l], 'ffn_w_down': ffn_w_down[l]}
        xp, new_p = decoder_layer(
            xp, c_prompt, lp, pos_p, None, None,
            jnp.zeros((B, RET_HEADS, RET_DK, RET_DV), jnp.float32),
            jnp.zeros((B, RNN_WIDTH), jnp.float32),
            jnp.zeros((B, RG_CONV - 1, RNN_WIDTH), x_prompt.dtype),
            jnp.zeros((B, FFN_CONV - 1, D_FF), x_prompt.dtype))
        past_rows = cache_nsa[l][page_table].reshape(DB, past_len, NSA_N_KV, NSA_KV_HEADS, NSA_HD)
        xs, new_s = decoder_layer(
            xs, c_sample, lp, pos_s, past_rows, cache_nsa_win[l],
            state_ret[l].astype(jnp.float32), state_rglru_h[l].astype(jnp.float32),
            state_rglru_conv[l], state_ffn_conv[l])
        for i in range(6):
            st_p[i].append(new_p[i])
            st_s[i].append(new_s[i])
    y_prompt = rmsnorm(xp, final_norm_g)
    y_sample = rmsnorm(xs, final_norm_g)
    nsa_rows_p, nsa_win_p, ret_p, h_p, conv_p, ffn_p = [jnp.stack(a) for a in st_p]
    nsa_rows_s, nsa_win_s, ret_s, h_s, conv_s, ffn_s = [jnp.stack(a) for a in st_s]
    return (y_prompt, y_sample, nsa_rows_p, nsa_rows_s, nsa_win_p, nsa_win_s, ret_p, ret_s,
            h_p, h_s, conv_p, conv_s, ffn_p, ffn_s)
```

```python
import functools
import math

import numpy as np
import jax
import jax.numpy as jnp
from jax import lax
from jax.experimental import pallas as pl
from jax.experimental.pallas import tpu as pltpu

F32 = jnp.float32
BF16 = jnp.bfloat16

NSA_KV = 2
NSA_HPG = 4
NSA_HD = 64
CMP_STRIDE = 16
CMP_BLOCK = 32
SEL_BLOCK = 64
SEL_TOPK = 16
SEL_FORCE = 1e4
WINDOW = 512
QBLOCK = 128
RET_HEADS = 4
RET_DK = 128
RET_CHUNK = 128
ROPE_BASE = 10000.0
RG_CONV = 4
RG_C = 8.0
FFN_CONV = 3
EPS = 1e-6

NEG = -1e30
SUBLANES = 8
LANES = 128
VMEM_LIMIT_V7X = 56 * 1024 * 1024
SEL_CHUNK = 256


def _cparams(n_grid):
    return pltpu.CompilerParams(dimension_semantics=("arbitrary",) * n_grid,
                                vmem_limit_bytes=VMEM_LIMIT_V7X)


def _bdot(a, b):
    return jnp.dot(a.astype(BF16), b.astype(BF16), preferred_element_type=F32)


def _bdot_nt(a, b):
    return lax.dot_general(a.astype(BF16), b.astype(BF16), (((1,), (1,)), ((), ())),
                           preferred_element_type=F32)


def _split(a):
    hi = a.astype(BF16)
    lo = (a - hi.astype(F32)).astype(BF16)
    return hi, lo


def _dot3(a, b):
    ah, al = _split(a)
    bh, bl = _split(b)
    d = functools.partial(jnp.dot, preferred_element_type=F32)
    return d(ah, bh) + d(al, bh) + d(ah, bl)


def _sigmoid(x):
    return 1.0 / (1.0 + jnp.exp(-x))


def _gelu(x):
    return 0.5 * x * (1.0 + jnp.tanh(0.7978845608028654 * (x + 0.044715 * (x * x * x))))


def _rms_mod(x, g, sc, sh):
    y = x * lax.rsqrt(jnp.mean(x * x, axis=-1, keepdims=True) + EPS)
    return (y * g) * (1.0 + sc) + sh


def _masked_softmax(s, mask):
    sm = jnp.where(mask, s, NEG)
    m = jnp.max(sm, axis=-1, keepdims=True)
    e = jnp.where(mask, jnp.exp(sm - m), 0.0)
    den = jnp.sum(e, axis=-1, keepdims=True)
    return e * (1.0 / jnp.where(den > 0, den, 1.0))


def _ada_kernel(c_ref, w_ref, b_ref, o_ref):
    c = c_ref[...]
    o_ref[0] = _dot3(c * _sigmoid(c), w_ref[0]) + b_ref[0]


def _ada(c_all, w_ada, b_ada):
    L, D, E = w_ada.shape
    n = c_all.shape[0]
    tn = 1536 if E % 1536 == 0 else E
    return pl.pallas_call(
        _ada_kernel,
        grid=(L, E // tn),
        in_specs=[pl.BlockSpec((n, D), lambda l, j: (0, 0)),
                  pl.BlockSpec((1, D, tn), lambda l, j: (l, 0, j)),
                  pl.BlockSpec((1, 1, tn), lambda l, j: (l, 0, j))],
        out_specs=pl.BlockSpec((1, n, tn), lambda l, j: (l, 0, j)),
        out_shape=jax.ShapeDtypeStruct((L, n, E), F32),
        compiler_params=_cparams(2),
        name="ada_mod",
    )(c_all, w_ada, b_ada.reshape(L, 1, E))


def _mod_spec(sm, tm, d):
    if sm == 1:
        return pl.BlockSpec((1, 1, d), lambda g, t: (g, 0, 0))
    return pl.BlockSpec((1, tm, d), lambda g, t: (g, t, 0))


def _const_spec(shape):
    nd = len(shape)
    return pl.BlockSpec(shape, lambda *a: (0,) * nd, pipeline_mode=pl.Buffered(1))


def _inproj_kernel(x_ref, g_ref, sc_ref, sh_ref, w_ref, *o_refs, segs):
    h = _rms_mod(x_ref[0], g_ref[...], sc_ref[0], sh_ref[0]).astype(BF16)
    for (off, wd), o_ref in zip(segs, o_refs):
        o_ref[0] = jnp.dot(h, w_ref[:, off:off + wd], preferred_element_type=F32)


def _inproj(x, g, sc, sh, w, widths, tm):
    G, Tg, D = x.shape
    segs, off = [], 0
    for wd in widths:
        segs.append((off, wd))
        off += wd
    sm = sc.shape[1]
    return pl.pallas_call(
        functools.partial(_inproj_kernel, segs=tuple(segs)),
        grid=(G, Tg // tm),
        in_specs=[pl.BlockSpec((1, tm, D), lambda g_, t: (g_, t, 0)),
                  _const_spec((1, D)),
                  _mod_spec(sm, tm, D), _mod_spec(sm, tm, D),
                  _const_spec(w.shape)],
        out_specs=[pl.BlockSpec((1, tm, wd), lambda g_, t: (g_, t, 0)) for wd in widths],
        out_shape=[jax.ShapeDtypeStruct((G, Tg, wd), F32) for wd in widths],
        compiler_params=_cparams(2),
        name="in_proj",
    )(x, g, sc, sh, w)


def _compress_compute(load, nch, w1_ref, pe_ref, b1_ref, w2_ref, out_refs):
    last = lax.broadcasted_iota(jnp.int32, (nch, 1), 0) == nch - 1
    for c in range(2):
        lo = jnp.zeros((nch, 2 * LANES), F32)
        hi = jnp.zeros((nch, 2 * LANES), F32)
        for p in range(CMP_STRIDE):
            xp = load(p, c)
            w = w1_ref[c, p]
            lo = lo + _bdot(xp + pe_ref[c, 0, p:p + 1, :], w[:, :2 * LANES])
            hi = hi + _bdot(xp + pe_ref[c, 1, p:p + 1, :], w[:, 2 * LANES:])
        hi_next = jnp.where(last, 0.0, pltpu.roll(hi, nch - 1, 0))
        hid = _gelu(lo + hi_next + b1_ref[c])
        out_refs[c][0] = _bdot(hid, w2_ref[c])


def _compress_prompt_kernel(krows_ref, vrows_ref, w1_ref, pe_ref, b1_ref, w2_ref, kc_ref, vc_ref, *, nch):
    def load(p, c):
        src = vrows_ref if c else krows_ref
        return src[0, pl.ds(p, nch, stride=CMP_STRIDE), :]
    _compress_compute(load, nch, w1_ref, pe_ref, b1_ref, w2_ref, (kc_ref, vc_ref))


def _compress_prompt(rows, cw):
    B, T, _ = rows.shape
    nch = T // CMP_STRIDE
    w1, pe, b1, w2 = cw
    return pl.pallas_call(
        functools.partial(_compress_prompt_kernel, nch=nch),
        grid=(B,),
        in_specs=[pl.BlockSpec((1, T, LANES), lambda b: (b, 0, 0)),
                  pl.BlockSpec((1, T, LANES), lambda b: (b, 0, 1)),
                  _const_spec(w1.shape), _const_spec(pe.shape), _const_spec(b1.shape), _const_spec(w2.shape)],
        out_specs=[pl.BlockSpec((1, nch, LANES), lambda b: (b, 0, 0))] * 2,
        out_shape=[jax.ShapeDtypeStruct((B, nch, LANES), F32)] * 2,
        compiler_params=_cparams(1),
        name="nsa_compress_prompt",
    )(rows, rows, w1, pe, b1, w2)


def _page_copy(cache_hbm, layer, page, col0, dst, j, sem):
    rows = cache_hbm.shape[2]
    return pltpu.make_async_copy(
        cache_hbm.at[layer, page, :, pl.ds(col0, dst.shape[1])],
        dst.at[pl.ds(pl.multiple_of(j * rows, rows), rows), :], sem)


def _gather_start(pt_ref, b, cache_hbm, layer, col0, dst, sem, npages):
    def issue(j, carry):
        _page_copy(cache_hbm, layer, pt_ref[b, j], col0, dst, j, sem).start()
        return carry
    lax.fori_loop(0, npages, issue, 0)


def _gather_wait(cache_hbm, layer, col0, dst, sem, npages):
    def wait(j, carry):
        _page_copy(cache_hbm, layer, 0, col0, dst, j, sem).wait()
        return carry
    lax.fori_loop(0, npages, wait, 0)


def _compress_sample_kernel(pt_ref, cache_hbm, w1_ref, pe_ref, b1_ref, w2_ref, kc_ref, vc_ref, kbuf, vbuf,
                            ksem, vsem, *, nch, layer, npages):
    b = pl.program_id(0)
    _gather_start(pt_ref, b, cache_hbm, layer, 0, kbuf, ksem, npages)
    _gather_start(pt_ref, b, cache_hbm, layer, LANES, vbuf, vsem, npages)
    _gather_wait(cache_hbm, layer, 0, kbuf, ksem, npages)
    _gather_wait(cache_hbm, layer, LANES, vbuf, vsem, npages)

    def load(p, c):
        src = vbuf if c else kbuf
        return src[pl.ds(p, nch, stride=CMP_STRIDE), :]
    _compress_compute(load, nch, w1_ref, pe_ref, b1_ref, w2_ref, (kc_ref, vc_ref))


def _compress_sample(page_table, cache, layer, cw):
    DB, npages = page_table.shape
    page = cache.shape[2]
    P = npages * page
    nch = P // CMP_STRIDE
    w1, pe, b1, w2 = cw
    cs = lambda shape: pl.BlockSpec(shape, lambda b, pt: (0,) * len(shape), pipeline_mode=pl.Buffered(1))
    grid_spec = pltpu.PrefetchScalarGridSpec(
        num_scalar_prefetch=1, grid=(DB,),
        in_specs=[pl.BlockSpec(memory_space=pl.ANY),
                  cs(w1.shape), cs(pe.shape), cs(b1.shape), cs(w2.shape)],
        out_specs=[pl.BlockSpec((1, nch, LANES), lambda b, pt: (b, 0, 0))] * 2,
        scratch_shapes=[pltpu.VMEM((P, LANES), F32), pltpu.VMEM((P, LANES), F32),
                        pltpu.SemaphoreType.DMA(()), pltpu.SemaphoreType.DMA(())])
    return pl.pallas_call(
        functools.partial(_compress_sample_kernel, nch=nch, layer=layer, npages=npages),
        grid_spec=grid_spec,
        out_shape=[jax.ShapeDtypeStruct((DB, nch, LANES), F32)] * 2,
        compiler_params=_cparams(1),
        name="nsa_compress_sample",
    )(page_table, cache, w1, pe, b1, w2)


def _nsa_block(qblk, gate, q0, Qb, kc, vc, n_cmp, n_sel, load_sel, nchunk, wk, wv, wk0, o_ref):
    R = 2 * NSA_HPG * Qb
    R2 = 2 * Qb
    npad = kc.shape[0]
    nspad = -(-n_sel // LANES) * LANES
    lane = lax.broadcasted_iota(jnp.int32, (Qb, LANES), 1)
    lo_half = lane < NSA_HD
    scale = NSA_HD ** -0.5
    pieces = []
    for k in range(NSA_KV):
        for r in range(NSA_HPG):
            sl = qblk[:, r * LANES:(r + 1) * LANES] * scale
            pieces.append(jnp.where(lo_half if k == 0 else jnp.logical_not(lo_half), sl, 0.0))
    qs = jnp.concatenate(pieces, axis=0).astype(BF16)
    qpos = q0 + (lax.broadcasted_iota(jnp.int32, (R, 1), 0) & (Qb - 1))
    qpos2 = q0 + (lax.broadcasted_iota(jnp.int32, (R2, 1), 0) & (Qb - 1))

    n_idx = lax.broadcasted_iota(jnp.int32, (1, npad), 1)
    cmask = (n_idx * CMP_STRIDE + (CMP_BLOCK - 1) <= qpos) & (n_idx < n_cmp)
    p_c = _masked_softmax(_bdot_nt(qs, kc), cmask)
    o_c = _bdot(p_c, vc)

    psum = []
    for k in range(NSA_KV):
        acc = p_c[(k * NSA_HPG) * Qb:(k * NSA_HPG + 1) * Qb]
        for r in range(1, NSA_HPG):
            acc = acc + p_c[(k * NSA_HPG + r) * Qb:(k * NSA_HPG + r + 1) * Qb]
        psum.append(acc)
    psum = jnp.concatenate(psum, axis=0)
    ci = lax.broadcasted_iota(jnp.int32, (npad, nspad), 0) * CMP_STRIDE
    sj = lax.broadcasted_iota(jnp.int32, (npad, nspad), 1) * SEL_BLOCK
    selmap = jnp.where((ci < sj + SEL_BLOCK) & (ci + CMP_BLOCK > sj), 1.0, 0.0).astype(BF16)
    ph, plo = _split(psum)
    imp = (jnp.dot(ph, selmap, preferred_element_type=F32)
           + jnp.dot(plo, selmap, preferred_element_type=F32))
    j = lax.broadcasted_iota(jnp.int32, (R2, nspad), 1)
    jf = j.astype(F32)
    cur = qpos2 >> 6
    forced = (j == 0) | (j == cur) | (j == cur - 1)
    imp = jnp.where(forced, SEL_FORCE, imp)
    imp = jnp.where(j * SEL_BLOCK <= qpos2, imp, -SEL_FORCE)
    imp = jnp.where(j < n_sel, imp, NEG)

    def pick(_, carry):
        imp_c, sel_c = carry
        m = jnp.max(imp_c, axis=-1, keepdims=True)
        first = jnp.min(jnp.where(imp_c == m, jf, float(nspad)), axis=-1, keepdims=True)
        hit = jf == first
        return jnp.where(hit, NEG, imp_c), jnp.where(hit, 1.0, sel_c)

    _, sel = lax.fori_loop(0, min(SEL_TOPK, n_sel), pick, (imp, jnp.zeros((R2, nspad), F32)))
    sel = sel.astype(BF16)

    blk_row = lax.broadcasted_iota(jnp.int32, (nspad, SEL_CHUNK), 0)
    key_lane = lax.broadcasted_iota(jnp.int32, (1, SEL_CHUNK), 1)

    def sweep(c, carry):
        m, l, acc = carry
        kk, vv, k0 = load_sel(c)
        kpos = k0 + key_lane
        expand = jnp.where((kpos >> 6) == blk_row, 1.0, 0.0).astype(BF16)
        sx = jnp.dot(sel, expand, preferred_element_type=F32)
        selx = jnp.concatenate([sx[:Qb]] * NSA_HPG + [sx[Qb:]] * NSA_HPG, axis=0)
        mask = (selx > 0.5) & (kpos <= qpos)
        sm = jnp.where(mask, _bdot_nt(qs, kk), NEG)
        m_new = jnp.maximum(m, jnp.max(sm, axis=-1, keepdims=True))
        alpha = jnp.exp(m - m_new)
        p = jnp.where(mask, jnp.exp(sm - m_new), 0.0)
        l = alpha * l + jnp.sum(p, axis=-1, keepdims=True)
        acc = alpha * acc + _bdot(p, vv)
        return m_new, l, acc

    m0 = jnp.full((R, 1), NEG, F32)
    _, l_s, acc_s = lax.fori_loop(0, nchunk, sweep, (m0, jnp.zeros((R, 1), F32), jnp.zeros((R, LANES), F32)))
    o_s = acc_s * (1.0 / jnp.where(l_s > 0, l_s, 1.0))

    wl = wk.shape[0]
    wpos = wk0 + lax.broadcasted_iota(jnp.int32, (1, wl), 1)
    dpos = qpos - wpos
    wmask = (dpos >= 0) & (dpos <= WINDOW) & (wpos >= 0)
    o_w = _bdot(_masked_softmax(_bdot_nt(qs, wk), wmask), wv)

    g = _sigmoid(gate)
    for r in range(NSA_HPG):
        halves = []
        for k in range(NSA_KV):
            rs = slice((k * NSA_HPG + r) * Qb, (k * NSA_HPG + r + 1) * Qb)
            c = (k * NSA_HPG + r) * 3
            halves.append(g[:, c:c + 1] * o_c[rs] + g[:, c + 1:c + 2] * o_s[rs] + g[:, c + 2:c + 3] * o_w[rs])
        o_ref[0, :, r * LANES:(r + 1) * LANES] = jnp.where(lo_half, halves[0], halves[1])


def _nsa_prompt_kernel(q_ref, gate_ref, kc_ref, vc_ref, rows_ref, win_ref, o_ref, *, T, Qb, n_cmp, n_sel, wl):
    q0 = pl.program_id(1) * Qb

    def load_sel(c):
        k0 = pl.multiple_of(c * SEL_CHUNK, SEL_CHUNK)
        return (rows_ref[0, pl.ds(k0, SEL_CHUNK), 2 * LANES:3 * LANES],
                rows_ref[0, pl.ds(k0, SEL_CHUNK), 3 * LANES:4 * LANES], k0)

    nchunk = (q0 + Qb + SEL_CHUNK - 1) // SEL_CHUNK
    ws = pl.multiple_of(jnp.maximum(q0 - WINDOW, 0), LANES)
    wk = win_ref[0, pl.ds(ws, wl), 0:LANES]
    wv = win_ref[0, pl.ds(ws, wl), LANES:2 * LANES]
    _nsa_block(q_ref[0], gate_ref[0], q0, Qb, kc_ref[0], vc_ref[0], n_cmp, n_sel, load_sel, nchunk,
               wk, wv, ws, o_ref)


def _nsa_prompt(nq, gate, kc, vc, rows, win):
    B, T, HD = nq.shape
    Qb = QBLOCK
    nch = kc.shape[1]
    n_sel = -(-T // SEL_BLOCK)
    wl = WINDOW + Qb
    assert T % SEL_CHUNK == 0 and T >= wl and n_sel >= SEL_TOPK
    return pl.pallas_call(
        functools.partial(_nsa_prompt_kernel, T=T, Qb=Qb, n_cmp=nch - 1, n_sel=n_sel, wl=wl),
        grid=(B, T // Qb),
        in_specs=[pl.BlockSpec((1, Qb, HD), lambda b, i: (b, i, 0)),
                  pl.BlockSpec((1, Qb, LANES), lambda b, i: (b, i, 0)),
                  pl.BlockSpec((1, nch, LANES), lambda b, i: (b, 0, 0)),
                  pl.BlockSpec((1, nch, LANES), lambda b, i: (b, 0, 0)),
                  pl.BlockSpec((1, T, 4 * LANES), lambda b, i: (b, 0, 0)),
                  pl.BlockSpec((1, T, 2 * LANES), lambda b, i: (b, 0, 0))],
        out_specs=pl.BlockSpec((1, Qb, HD), lambda b, i: (b, i, 0)),
        out_shape=jax.ShapeDtypeStruct((B, T, HD), F32),
        compiler_params=_cparams(2),
        name="nsa_attn_prompt",
    )(nq, gate, kc, vc, rows, win)


def _nsa_sample_kernel(pt_ref, q_ref, gate_ref, kc_ref, vc_ref, rows_ref, wbuf_ref, wnew_ref, cache_hbm, o_ref,
                       selbuf, winbuf, sem, *, layer, npages, P, Q, wb, n_cmp, n_sel):
    b = pl.program_id(0)
    _gather_start(pt_ref, b, cache_hbm, layer, 2 * LANES, selbuf, sem, npages)
    selbuf[pl.ds(P, SEL_CHUNK), :] = jnp.zeros((SEL_CHUNK, 2 * LANES), F32)
    selbuf[pl.ds(P, Q), :] = rows_ref[0, :, 2 * LANES:4 * LANES]
    winbuf[pl.ds(0, wb), :] = wbuf_ref[0]
    winbuf[pl.ds(wb, LANES), :] = jnp.zeros((LANES, 2 * LANES), F32)
    winbuf[pl.ds(wb, Q), :] = wnew_ref[0]
    _gather_wait(cache_hbm, layer, 2 * LANES, selbuf, sem, npages)

    def load_sel(c):
        k0 = pl.multiple_of(c * SEL_CHUNK, SEL_CHUNK)
        return selbuf[pl.ds(k0, SEL_CHUNK), 0:LANES], selbuf[pl.ds(k0, SEL_CHUNK), LANES:2 * LANES], k0

    nchunk = (P + Q + SEL_CHUNK - 1) // SEL_CHUNK
    _nsa_block(q_ref[0], gate_ref[0], P, Q, kc_ref[0], vc_ref[0], n_cmp, n_sel, load_sel, nchunk,
               winbuf[:, 0:LANES], winbuf[:, LANES:2 * LANES], P - wb, o_ref)


def _nsa_sample(page_table, nq, gate, kc, vc, rows, win_buf, win_new, cache, layer):
    DB, Q, HD = nq.shape
    npages = page_table.shape[1]
    P = npages * cache.shape[2]
    wb = win_buf.shape[1]
    nch = kc.shape[1]
    n_sel = -(-(P + Q) // SEL_BLOCK)
    assert P % SEL_CHUNK == 0 and Q == SUBLANES and (P + Q) // CMP_STRIDE == nch
    bs = lambda shape: pl.BlockSpec((1,) + shape, lambda b, pt: (b, 0, 0))
    grid_spec = pltpu.PrefetchScalarGridSpec(
        num_scalar_prefetch=1, grid=(DB,),
        in_specs=[bs((Q, HD)), bs((Q, LANES)), bs((nch, LANES)), bs((nch, LANES)), bs((Q, 4 * LANES)),
                  bs((wb, 2 * LANES)), bs((Q, 2 * LANES)), pl.BlockSpec(memory_space=pl.ANY)],
        out_specs=bs((Q, HD)),
        scratch_shapes=[pltpu.VMEM((P + SEL_CHUNK, 2 * LANES), F32),
                        pltpu.VMEM((wb + LANES, 2 * LANES), F32),
                        pltpu.SemaphoreType.DMA(())])
    return pl.pallas_call(
        functools.partial(_nsa_sample_kernel, layer=layer, npages=npages, P=P, Q=Q, wb=wb,
                          n_cmp=nch - 1, n_sel=n_sel),
        grid_spec=grid_spec,
        out_shape=jax.ShapeDtypeStruct((DB, Q, HD), F32),
        compiler_params=_cparams(1),
        name="nsa_attn_sample",
    )(page_table, nq, gate, kc, vc, rows, win_buf, win_new, cache)


def _ret_kernel(q_ref, k_ref, v_ref, g_ref, cos_ref, sin_ref, s0_ref, gn_ref, o_ref, snew_ref, s_sc, *, C, nC):
    c = pl.program_id(1)

    @pl.when(c == 0)
    def _():
        s_sc[...] = s0_ref[0]

    cosf = cos_ref[...]
    sinf = sin_ref[...]
    diff = (lax.broadcasted_iota(jnp.int32, (C, C), 0) - lax.broadcasted_iota(jnp.int32, (C, C), 1)).astype(F32)
    ii = lax.broadcasted_iota(jnp.int32, (C, 1), 0).astype(F32)
    half = RET_DK // 2
    for h in range(RET_HEADS):
        lg = math.log(1.0 - 2.0 ** (-5.0 - h))
        hs = slice(h * RET_DK, (h + 1) * RET_DK)
        q = q_ref[0, :, hs]
        k = k_ref[0, :, hs]
        v = v_ref[0, :, hs]
        qr = q * cosf + pltpu.roll(q, half, 1) * sinf
        kr = (k * cosf + pltpu.roll(k, half, 1) * sinf) * (RET_DK ** -0.5)
        decay = jnp.where(diff >= 0, jnp.exp(jnp.maximum(diff, 0.0) * lg), 0.0)
        o_inner = _bdot(_bdot_nt(qr, kr) * decay, v)
        s_old = s_sc[h]
        o_cross = _bdot(qr, s_old) * jnp.exp((ii + 1.0) * lg)
        kw = kr * jnp.exp((C - 1.0 - ii) * lg)
        kv = lax.dot_general(kw.astype(BF16), v.astype(BF16), (((0,), (0,)), ((), ())),
                             preferred_element_type=F32)
        s_sc[h] = math.exp(C * lg) * s_old + kv
        o = o_inner + o_cross
        mu = jnp.mean(o, axis=-1, keepdims=True)
        var = jnp.mean(jnp.square(o - mu), axis=-1, keepdims=True)
        gate = g_ref[0, :, hs]
        o_ref[0, :, hs] = ((o - mu) * lax.rsqrt(var + EPS)) * gn_ref[:, hs] * (gate * _sigmoid(gate))

    @pl.when(c == nC - 1)
    def _():
        snew_ref[0] = s_sc[...]


def _retention(rq, rk, rv, rg, cosf, sinf, s0, gn):
    B, T, W = rq.shape
    C = RET_CHUNK if (T >= RET_CHUNK and T % RET_CHUNK == 0) else T
    nC = T // C
    tok = pl.BlockSpec((1, C, W), lambda b, c: (b, c, 0))
    tab = pl.BlockSpec((C, RET_DK), lambda b, c: (c, 0))
    st = pl.BlockSpec((1,) + s0.shape[1:], lambda b, c: (b, 0, 0, 0))
    return pl.pallas_call(
        functools.partial(_ret_kernel, C=C, nC=nC),
        grid=(B, nC),
        in_specs=[tok, tok, tok, tok, tab, tab, st, _const_spec((1, W))],
        out_specs=[tok, st],
        out_shape=[jax.ShapeDtypeStruct((B, T, W), F32), jax.ShapeDtypeStruct(s0.shape, F32)],
        scratch_shapes=[pltpu.VMEM(s0.shape[1:], F32)],
        compiler_params=_cparams(2),
        name="retention",
    )(rq, rk, rv, rg, cosf, sinf, s0, gn)


def _shift_carry(x, k, tail8):
    r = pltpu.roll(x, k, 0)
    row8 = lax.broadcasted_iota(jnp.int32, (SUBLANES, 1), 0)
    first = jnp.where(row8 >= k, r[:SUBLANES], pltpu.roll(tail8, k, 0))
    return jnp.concatenate([first, r[SUBLANES:]], axis=0)


def _shift_seg(x, k, fill, tpos):
    return jnp.where(tpos >= k, pltpu.roll(x, k, 0), fill)


def _rglru_kernel(x_ref, gate_ref, st_ref, h0_ref, cw_ref, cb_ref, wa_ref, ba_ref, wx_ref, bx_ref, lam_ref,
                  o_ref, h_ref, tail_sc, h_sc, *, tm, seg):
    carry = seg == 0
    x = x_ref[0]
    rows = lax.broadcasted_iota(jnp.int32, (tm, 1), 0)
    if carry:
        @pl.when(pl.program_id(1) == 0)
        def _():
            tail_sc[...] = st_ref[0]
            h_sc[...] = h0_ref[0]
        tail8 = tail_sc[...]
        shifted = [_shift_carry(x, k, tail8) for k in range(1, RG_CONV)]
        tpos = rows
        seglen = tm
        h_in = jnp.where(rows == 0, h_sc[SUBLANES - 1:SUBLANES, :], 0.0)
    else:
        tpos = rows & (seg - 1)
        shifted = [_shift_seg(x, k, st_ref[k - 1], tpos) for k in range(1, RG_CONV)]
        seglen = seg
        h_in = h0_ref[0]
    xc = cb_ref[...] + cw_ref[RG_CONV - 1:RG_CONV, :] * x
    for k in range(1, RG_CONV):
        xc = xc + cw_ref[RG_CONV - 1 - k:RG_CONV - k, :] * shifted[k - 1]
    r = _sigmoid(_bdot(xc, wa_ref[...]) + ba_ref[...])
    i = _sigmoid(_bdot(xc, wx_ref[...]) + bx_ref[...])
    lam = lam_ref[...]
    softplus = jnp.maximum(-lam, 0.0) + jnp.log(1.0 + jnp.exp(-jnp.abs(lam)))
    log_a = (-RG_C * r) * softplus
    a = jnp.exp(log_a)
    bt = jnp.sqrt(1.0 - a * a) * (i * xc)
    bt = bt + a * h_in
    k = 1
    while k < seglen:
        ok = tpos >= k
        a_prev = jnp.where(ok, pltpu.roll(a, k, 0), 1.0)
        b_prev = jnp.where(ok, pltpu.roll(bt, k, 0), 0.0)
        bt = a * b_prev + bt
        a = a * a_prev
        k *= 2
    o_ref[0] = bt * _gelu(gate_ref[0])
    if carry:
        tail_sc[...] = x[tm - SUBLANES:]
        h_sc[...] = bt[tm - SUBLANES:]
        h_ref[0] = bt[tm - SUBLANES:]
    else:
        h_ref[0] = bt


def _rglru(rx, rgate, st, h0, rw, tm, seg):
    G, Tg, W = rx.shape
    cw, cb, wa, ba, wx, bx, lam = rw
    tok = pl.BlockSpec((1, tm, W), lambda g, t: (g, t, 0))
    if seg == 0:
        st_spec = pl.BlockSpec((1, SUBLANES, W), lambda g, t: (g, 0, 0))
        h0_spec = pl.BlockSpec((1, SUBLANES, W), lambda g, t: (g, 0, 0))
        h_spec = pl.BlockSpec((1, SUBLANES, W), lambda g, t: (g, 0, 0))
        h_shape = (G, SUBLANES, W)
    else:
        st_spec = pl.BlockSpec(st.shape, lambda g, t: (0, 0, 0))
        h0_spec = tok
        h_spec = tok
        h_shape = (G, Tg, W)
    return pl.pallas_call(
        functools.partial(_rglru_kernel, tm=tm, seg=seg),
        grid=(G, Tg // tm),
        in_specs=[tok, tok, st_spec, h0_spec] + [_const_spec(a.shape) for a in rw],
        out_specs=[tok, h_spec],
        out_shape=[jax.ShapeDtypeStruct((G, Tg, W), F32), jax.ShapeDtypeStruct(h_shape, F32)],
        scratch_shapes=[pltpu.VMEM((SUBLANES, W), F32), pltpu.VMEM((SUBLANES, W), F32)],
        compiler_params=_cparams(2),
        name="rglru",
    )(rx, rgate, st, h0, *rw)


def _merge_kernel(x_ref, oa_ref, or_ref, oc_ref, mg_ref, gt_ref, wa_ref, wb_ref, wc_ref, wo_ref, y_ref, *, D):
    pa = _bdot(oa_ref[0], wa_ref[...])
    pb = _bdot(or_ref[0], wb_ref[...])
    pc = _bdot(oc_ref[0], wc_ref[...])
    merged = (_sigmoid(mg_ref[0, :, 0:D]) * pa + _sigmoid(mg_ref[0, :, D:2 * D]) * pb
              + _sigmoid(mg_ref[0, :, 2 * D:3 * D]) * pc)
    y_ref[0] = x_ref[0] + gt_ref[0] * _bdot(merged, wo_ref[...])


def _merge(x, oa, orr, oc, mg, gt, wa, wb, wc, wo, tm):
    G, Tg, D = x.shape
    tok = lambda w: pl.BlockSpec((1, tm, w), lambda g, t: (g, t, 0))
    return pl.pallas_call(
        functools.partial(_merge_kernel, D=D),
        grid=(G, Tg // tm),
        in_specs=[tok(D), tok(oa.shape[2]), tok(orr.shape[2]), tok(oc.shape[2]), tok(3 * D),
                  _mod_spec(gt.shape[1], tm, D),
                  _const_spec(wa.shape), _const_spec(wb.shape), _const_spec(wc.shape), _const_spec(wo.shape)],
        out_specs=tok(D),
        out_shape=jax.ShapeDtypeStruct((G, Tg, D), F32),
        compiler_params=_cparams(2),
        name="merge_out",
    )(x, oa, orr, oc, mg, gt, wa, wb, wc, wo)


def _ffn_kernel(x_ref, g_ref, sc_ref, sh_ref, gt_ref, st_ref, wup_ref, cw_ref, cb_ref, wdn_ref, fg_ref,
                y_ref, fnew_ref, tail_sc, *, tm, seg, F, wck, final):
    carry = seg == 0
    x = x_ref[0]
    h = _rms_mod(x, g_ref[...], sc_ref[0], sh_ref[0]).astype(BF16)
    rows = lax.broadcasted_iota(jnp.int32, (tm, 1), 0)
    if carry:
        @pl.when(pl.program_id(1) == 0)
        def _():
            tail_sc[...] = st_ref[0]
    else:
        tpos = rows & (seg - 1)
    acc = jnp.zeros(x.shape, F32)
    for c0 in range(0, F, wck):
        cs = slice(c0, c0 + wck)
        gp = jnp.dot(h, wup_ref[:, c0:c0 + wck], preferred_element_type=F32)
        val = jnp.dot(h, wup_ref[:, F + c0:F + c0 + wck], preferred_element_type=F32)
        if carry:
            tail8 = tail_sc[:, cs]
            shifted = [_shift_carry(gp, k, tail8) for k in range(1, FFN_CONV)]
            tail_sc[:, cs] = gp[tm - SUBLANES:]
            fnew_ref[0, :, cs] = gp[tm - SUBLANES:]
        else:
            shifted = [_shift_seg(gp, k, st_ref[k - 1, :, cs], tpos) for k in range(1, FFN_CONV)]
            fnew_ref[0, :, cs] = gp
        gc = cb_ref[:, cs] + cw_ref[FFN_CONV - 1:FFN_CONV, cs] * gp
        for k in range(1, FFN_CONV):
            gc = gc + cw_ref[FFN_CONV - 1 - k:FFN_CONV - k, cs] * shifted[k - 1]
        act = (gc * _sigmoid(gc)) * val
        acc = acc + _bdot(act, wdn_ref[cs, :])
    y = x + gt_ref[0] * acc
    if final:
        y = (y * lax.rsqrt(jnp.mean(y * y, axis=-1, keepdims=True) + EPS)) * fg_ref[...]
    y_ref[0] = y


def _ffn(x, g, sc, sh, gt, st, wup, cw, cb, wdn, fg, tm, seg, final):
    G, Tg, D = x.shape
    F = wdn.shape[0]
    wck = F // 2 if (F // 2) % LANES == 0 else F
    tok = pl.BlockSpec((1, tm, D), lambda g_, t: (g_, t, 0))
    sm = sc.shape[1]
    if seg == 0:
        st_spec = pl.BlockSpec((1, SUBLANES, F), lambda g_, t: (g_, 0, 0))
        fn_spec = pl.BlockSpec((1, SUBLANES, F), lambda g_, t: (g_, 0, 0))
        fn_shape = (G, SUBLANES, F)
    else:
        st_spec = pl.BlockSpec(st.shape, lambda g_, t: (0, 0, 0))
        fn_spec = pl.BlockSpec((1, tm, F), lambda g_, t: (g_, t, 0))
        fn_shape = (G, Tg, F)
    return pl.pallas_call(
        functools.partial(_ffn_kernel, tm=tm, seg=seg, F=F, wck=wck, final=final),
        grid=(G, Tg // tm),
        in_specs=[tok, _const_spec((1, D)), _mod_spec(sm, tm, D), _mod_spec(sm, tm, D), _mod_spec(sm, tm, D),
                  st_spec, _const_spec(wup.shape), _const_spec(cw.shape), _const_spec(cb.shape),
                  _const_spec(wdn.shape), _const_spec((1, D))],
        out_specs=[tok, fn_spec],
        out_shape=[jax.ShapeDtypeStruct((G, Tg, D), F32), jax.ShapeDtypeStruct(fn_shape, F32)],
        scratch_shapes=[pltpu.VMEM((SUBLANES, F), F32)],
        compiler_params=_cparams(2),
        name="conv_ffn",
    )(x, g, sc, sh, gt, st, wup, cw, cb, wdn, fg)


def _head_perm():
    return np.array([(k * NSA_HPG + r) * NSA_HD + d
                     for r in range(NSA_HPG) for k in range(NSA_KV) for d in range(NSA_HD)], np.int32)


def _block_diag(w):
    n, a, b = w.shape[-3:]
    eye = jnp.eye(n, dtype=w.dtype)
    out = jnp.einsum('ij,...iab->...iajb', eye, w)
    return out.reshape(w.shape[:-3] + (n * a, n * b))


def _seg_fill(buf, k, seg):
    B, nb, C = buf.shape
    part = jnp.concatenate([buf[:, nb - k:, :], jnp.zeros((B, seg - k, C), buf.dtype)], axis=1)
    return part.reshape(B * seg, C)


def kernel(x_prompt, x_sample, cache_nsa, cache_nsa_win, state_ret, state_rglru_h, state_rglru_conv,
           state_ffn_conv, page_table, c_prompt, c_sample, norm1_g, norm2_g, w_ada, b_ada, w_in, cmp_pe,
           cmp_w1, cmp_b1, cmp_w2, ret_gn_g, rg_conv_w, rg_conv_b, rg_w_a, rg_b_a, rg_w_x, rg_b_x, rg_lambda,
           w_br_a, w_br_b, w_br_c, w_out, ffn_w_up, ffn_conv_w, ffn_conv_b, ffn_w_down, final_norm_g):
    B, T, D = x_prompt.shape
    DB, Q, _ = x_sample.shape
    L = w_in.shape[0]
    npages = page_table.shape[1]
    page = cache_nsa.shape[2]
    P = npages * page
    NQ = NSA_KV * NSA_HPG * NSA_HD
    NKV = NSA_KV * NSA_HD
    RW = RET_HEADS * RET_DK
    W = rg_conv_w.shape[2]
    F = ffn_w_down.shape[1]
    NS = DB * Q
    assert Q == SUBLANES and T >= RG_CONV and P % CMP_STRIDE == 0

    mod = _ada(jnp.concatenate([c_prompt, c_sample], axis=0), w_ada, b_ada)
    cache = cache_nsa.reshape(L, cache_nsa.shape[1], page, 4 * NKV)
    perm = _head_perm()

    half = RET_DK // 2
    freq = jnp.asarray(ROPE_BASE ** (-np.arange(half, dtype=np.float32) / half), F32)

    def rope_tables(pos):
        ang = pos.astype(F32)[:, None] * freq[None, :]
        cos, sin = jnp.cos(ang), jnp.sin(ang)
        return jnp.concatenate([cos, cos], axis=1), jnp.concatenate([-sin, sin], axis=1)

    cos_p, sin_p = rope_tables(jnp.arange(T, dtype=jnp.int32))
    cos_s, sin_s = rope_tables(P + jnp.arange(Q, dtype=jnp.int32))

    widths = (NQ, 4 * NKV, 2 * NKV, LANES, RW, RW, RW, RW, W, W, 3 * D)
    offs = np.cumsum((0, NQ, 6 * NKV, 3 * NSA_KV * NSA_HPG, RW, RW, RW, RW, W, W, 3 * D))
    ngate = 3 * NSA_KV * NSA_HPG

    xp = x_prompt
    xs = x_sample.reshape(1, NS, D)
    outs_p = [[] for _ in range(6)]
    outs_s = [[] for _ in range(6)]
    tm_p = 256 if T % 256 == 0 else T

    for l in range(L):
        wi = w_in[l]
        w_cat = jnp.concatenate([
            wi[:, offs[0]:offs[1]][:, perm],
            wi[:, offs[1]:offs[1] + 4 * NKV],
            wi[:, offs[1] + 4 * NKV:offs[2]],
            jnp.pad(wi[:, offs[2]:offs[3]], ((0, 0), (0, LANES - ngate))),
            wi[:, offs[3]:]], axis=1).astype(BF16)
        w1 = cmp_w1[l]
        bd = lambda w: _block_diag(jnp.broadcast_to(w[:, :, None], w.shape[:2] + (NSA_KV,) + w.shape[2:]))
        cw1 = jnp.concatenate([bd(w1[:, :CMP_STRIDE]), bd(w1[:, CMP_STRIDE:])], axis=-1).astype(BF16)
        pe = jnp.tile(cmp_pe[l], (1, 1, NSA_KV))
        cpe = jnp.stack([pe[:, :CMP_STRIDE], pe[:, CMP_STRIDE:]], axis=1)
        cb1 = jnp.tile(cmp_b1[l], (1, NSA_KV))[:, None, :]
        cw2 = _block_diag(jnp.broadcast_to(cmp_w2[l][:, None], (2, NSA_KV) + cmp_w2.shape[2:])).astype(BF16)
        cw = (cw1, cpe, cb1, cw2)
        rw = (rg_conv_w[l], rg_conv_b[l][None], _block_diag(rg_w_a[l]).astype(BF16), rg_b_a[l][None],
              _block_diag(rg_w_x[l]).astype(BF16), rg_b_x[l][None], rg_lambda[l][None])
        wa = w_br_a[l][perm].astype(BF16)
        wb = w_br_b[l].astype(BF16)
        wc = w_br_c[l].astype(BF16)
        wo = w_out[l].astype(BF16)
        wup = ffn_w_up[l].astype(BF16)
        wdn = ffn_w_down[l].astype(BF16)
        g1 = norm1_g[l][None]
        g2 = norm2_g[l][None]
        gn = ret_gn_g[l][None]
        fcw, fcb = ffn_conv_w[l], ffn_conv_b[l][None]
        fg = final_norm_g[None]
        final = l == L - 1

        m = [mod[l, :B, i * D:(i + 1) * D][:, None, :] for i in range(6)]
        (nq, rows, win, gate, rq, rk, rv, rg, rx, rgate, mg) = _inproj(xp, g1, m[1], m[0], w_cat, widths, tm_p)
        kc, vc = _compress_prompt(rows, cw)
        o_a = _nsa_prompt(nq, gate, kc, vc, rows, win)
        o_r, s_new = _retention(rq, rk, rv, rg, cos_p, sin_p,
                                jnp.zeros((B, RET_HEADS, RET_DK, RET_DK), F32), gn)
        zs = jnp.zeros((B, SUBLANES, W), F32)
        o_c, h_tail = _rglru(rx, rgate, zs, zs, rw, tm_p, 0)
        x1 = _merge(xp, o_a, o_r, o_c, mg, m[2], wa, wb, wc, wo, tm_p)
        xp, f_tail = _ffn(x1, g2, m[4], m[3], m[5], jnp.zeros((B, SUBLANES, F), F32), wup, fcw, fcb, wdn, fg,
                          tm_p, 0, final)
        wn = min(WINDOW, T)
        outs_p[0].append(rows.reshape(B, T, 4, NSA_KV, NSA_HD))
        outs_p[1].append(win[:, T - wn:].reshape(B, wn, 2, NSA_KV, NSA_HD))
        outs_p[2].append(s_new)
        outs_p[3].append(h_tail[:, SUBLANES - 1])
        outs_p[4].append(rx[:, T - (RG_CONV - 1):])
        outs_p[5].append(f_tail[:, SUBLANES - (FFN_CONV - 1):])

        ms = [jnp.repeat(mod[l, B:, i * D:(i + 1) * D], Q, axis=0)[None] for i in range(6)]
        (nq, rows, win, gate, rq, rk, rv, rg, rx, rgate, mg) = _inproj(xs, g1, ms[1], ms[0], w_cat, widths, NS)
        kc, vc = _compress_sample(page_table, cache, l, cw)
        r3 = lambda a: a.reshape(DB, Q, a.shape[-1])
        wbuf = cache_nsa_win[l].reshape(DB, -1, 2 * NKV)
        o_a = _nsa_sample(page_table, r3(nq), r3(gate), kc, vc, r3(rows), wbuf, r3(win), cache, l)
        o_r, s_new = _retention(r3(rq), r3(rk), r3(rv), r3(rg), cos_s, sin_s, state_ret[l].astype(F32), gn)
        cbuf = state_rglru_conv[l]
        st = jnp.stack([_seg_fill(cbuf, k, Q) for k in range(1, RG_CONV)])
        h0 = jnp.pad(state_rglru_h[l].astype(F32)[:, None, :], ((0, 0), (0, Q - 1), (0, 0))).reshape(1, NS, W)
        o_c, h_all = _rglru(rx, rgate, st, h0, rw, NS, Q)
        x1 = _merge(xs, o_a.reshape(1, NS, NQ), o_r.reshape(1, NS, RW), o_c, mg, ms[2], wa, wb, wc, wo, NS)
        fbuf = state_ffn_conv[l]
        fst = jnp.stack([_seg_fill(fbuf, k, Q) for k in range(1, FFN_CONV)])
        xs, g_all = _ffn(x1, g2, ms[4], ms[3], ms[5], fst, wup, fcw, fcb, wdn, fg, NS, Q, final)
        keys = jnp.concatenate([wbuf, r3(win)], axis=1)
        outs_s[0].append(rows.reshape(DB, Q, 4, NSA_KV, NSA_HD))
        outs_s[1].append(keys[:, Q:].reshape(DB, -1, 2, NSA_KV, NSA_HD))
        outs_s[2].append(s_new)
        outs_s[3].append(h_all.reshape(DB, Q, W)[:, Q - 1])
        outs_s[4].append(jnp.concatenate([cbuf, rx.reshape(DB, Q, W)], axis=1)[:, Q:])
        outs_s[5].append(jnp.concatenate([fbuf, g_all.reshape(DB, Q, F)], axis=1)[:, Q:])

    sp = [jnp.stack(a) for a in outs_p]
    ss = [jnp.stack(a) for a in outs_s]
    return (xp, xs.reshape(DB, Q, D), sp[0], ss[0], sp[1], ss[1], sp[2], ss[2],
            sp[3], ss[3], sp[4], ss[4], sp[5], ss[5])
```

```python
import functools
import math

import numpy as np
import jax
import jax.numpy as jnp
from jax import lax
from jax.experimental import pallas as pl
from jax.experimental.pallas import tpu as pltpu

F32 = jnp.float32
BF16 = jnp.bfloat16

NSA_KV = 2
NSA_HPG = 4
NSA_HD = 64
CMP_STRIDE = 16
CMP_BLOCK = 32
SEL_BLOCK = 64
SEL_TOPK = 16
SEL_FORCE = 1e4
WINDOW = 512
QBLOCK = 128
RET_HEADS = 4
RET_DK = 128
RET_CHUNK = 128
ROPE_BASE = 10000.0
RG_CONV = 4
RG_C = 8.0
FFN_CONV = 3
EPS = 1e-6

NEG = -1e30
BIG = float(2 ** 60)
LOG2E = 1.4426950408889634
SUBLANES = 8
LANES = 128
VMEM_LIMIT_V7X = 56 * 1024 * 1024
SEL_CHUNK = 512
CAST_CHUNK = 1024


def _cparams(n_grid):
    return pltpu.CompilerParams(dimension_semantics=("arbitrary",) * n_grid,
                                vmem_limit_bytes=VMEM_LIMIT_V7X)


def _bdot(a, b):
    return jnp.dot(a.astype(BF16), b.astype(BF16), preferred_element_type=F32)


def _bdot_nt(a, b):
    return lax.dot_general(a.astype(BF16), b.astype(BF16), (((1,), (1,)), ((), ())),
                           preferred_element_type=F32)


def _split(a):
    hi = a.astype(BF16)
    lo = (a - hi.astype(F32)).astype(BF16)
    return hi, lo


def _dot3(a, b):
    ah, al = _split(a)
    bh, bl = _split(b)
    d = functools.partial(jnp.dot, preferred_element_type=F32)
    return d(ah, bh) + d(al, bh) + d(ah, bl)


def _sigmoid(x):
    return 1.0 / (1.0 + jnp.exp(-x))


def _gelu(x):
    return 0.5 * x * (1.0 + jnp.tanh(0.7978845608028654 * (x + 0.044715 * (x * x * x))))


def _rms_mod(x, g, sc, sh):
    y = x * lax.rsqrt(jnp.mean(x * x, axis=-1, keepdims=True) + EPS)
    return (y * g) * (1.0 + sc) + sh


def _ada_kernel(c_ref, w_ref, b_ref, o_ref):
    c = c_ref[...]
    o_ref[0] = _dot3(c * _sigmoid(c), w_ref[0]) + b_ref[0]


def _ada(c_all, w_ada, b_ada):
    L, D, E = w_ada.shape
    n = c_all.shape[0]
    tn = 1536 if E % 1536 == 0 else E
    return pl.pallas_call(
        _ada_kernel,
        grid=(L, E // tn),
        in_specs=[pl.BlockSpec((n, D), lambda l, j: (0, 0)),
                  pl.BlockSpec((1, D, tn), lambda l, j: (l, 0, j)),
                  pl.BlockSpec((1, 1, tn), lambda l, j: (l, 0, j))],
        out_specs=pl.BlockSpec((1, n, tn), lambda l, j: (l, 0, j)),
        out_shape=jax.ShapeDtypeStruct((L, n, E), F32),
        compiler_params=_cparams(2),
        name="ada_mod",
    )(c_all, w_ada, b_ada.reshape(L, 1, E))


def _mod_spec(sm, tm, d):
    if sm == 1:
        return pl.BlockSpec((1, 1, d), lambda g, t: (g, 0, 0))
    return pl.BlockSpec((1, tm, d), lambda g, t: (g, t, 0))


def _const_spec(shape):
    nd = len(shape)
    return pl.BlockSpec(shape, lambda *a: (0,) * nd, pipeline_mode=pl.Buffered(1))


def _inproj_kernel(x_ref, g_ref, sc_ref, sh_ref, w_ref, *o_refs, segs):
    h = _rms_mod(x_ref[0], g_ref[...], sc_ref[0], sh_ref[0]).astype(BF16)
    for (off, wd), o_ref in zip(segs, o_refs):
        o_ref[0] = jnp.dot(h, w_ref[:, off:off + wd], preferred_element_type=F32)


def _inproj(x, g, sc, sh, w, widths, tm):
    G, Tg, D = x.shape
    segs, off = [], 0
    for wd in widths:
        segs.append((off, wd))
        off += wd
    sm = sc.shape[1]
    return pl.pallas_call(
        functools.partial(_inproj_kernel, segs=tuple(segs)),
        grid=(G, Tg // tm),
        in_specs=[pl.BlockSpec((1, tm, D), lambda g_, t: (g_, t, 0)),
                  _const_spec((1, D)),
                  _mod_spec(sm, tm, D), _mod_spec(sm, tm, D),
                  _const_spec(w.shape)],
        out_specs=[pl.BlockSpec((1, tm, wd), lambda g_, t: (g_, t, 0)) for wd in widths],
        out_shape=[jax.ShapeDtypeStruct((G, Tg, wd), F32) for wd in widths],
        compiler_params=_cparams(2),
        name="in_proj",
    )(x, g, sc, sh, w)


def _compress_bias(w1_ref, pe_ref, bias_sc):
    for c in range(2):
        halves = []
        for h in range(2):
            pe_rows = jnp.broadcast_to(pe_ref[c, h:h + 1, :], (SUBLANES, pe_ref.shape[2]))
            halves.append(_bdot(pe_rows, w1_ref[c, :, h * 2 * LANES:(h + 1) * 2 * LANES]))
        bias_sc[c] = jnp.concatenate(halves, axis=1)


def _compress_x(xrefs, nch, w1_ref, bias_sc, b1_ref, w2_ref):
    last = lax.broadcasted_iota(jnp.int32, (nch, 1), 0) == nch - 1
    outs = []
    for c in range(2):
        lhs = jnp.concatenate([xrefs[c][pl.ds(p, nch, stride=CMP_STRIDE), :].astype(BF16)
                               for p in range(CMP_STRIDE)], axis=1)
        acc = jnp.dot(lhs, w1_ref[c], preferred_element_type=F32) + bias_sc[c, 0:1, :]
        lo = acc[:, :2 * LANES]
        hi = acc[:, 2 * LANES:]
        hi_next = jnp.where(last, 0.0, pltpu.roll(hi, nch - 1, 0))
        hid = _gelu(lo + hi_next + b1_ref[c])
        outs.append(_bdot(hid, w2_ref[c]))
    return outs


def _compress_prompt_kernel(krows_ref, vrows_ref, w1_ref, pe_ref, b1_ref, w2_ref, kc_ref, vc_ref, bias_sc, *, nch):
    @pl.when(pl.program_id(0) == 0)
    def _():
        _compress_bias(w1_ref, pe_ref, bias_sc)

    kc, vc = _compress_x((krows_ref.at[0], vrows_ref.at[0]), nch, w1_ref, bias_sc, b1_ref, w2_ref)
    kc_ref[0] = kc
    vc_ref[0] = vc


def _compress_prompt(rows, cw):
    B, T, _ = rows.shape
    nch = T // CMP_STRIDE
    w1, pe, b1, w2 = cw
    return pl.pallas_call(
        functools.partial(_compress_prompt_kernel, nch=nch),
        grid=(B,),
        in_specs=[pl.BlockSpec((1, T, LANES), lambda b: (b, 0, 0)),
                  pl.BlockSpec((1, T, LANES), lambda b: (b, 0, 1)),
                  _const_spec(w1.shape), _const_spec(pe.shape), _const_spec(b1.shape), _const_spec(w2.shape)],
        out_specs=[pl.BlockSpec((1, nch, LANES), lambda b: (b, 0, 0))] * 2,
        out_shape=[jax.ShapeDtypeStruct((B, nch, LANES), F32)] * 2,
        scratch_shapes=[pltpu.VMEM((2, SUBLANES, 4 * LANES), F32)],
        compiler_params=_cparams(1),
        name="nsa_compress_prompt",
    )(rows, rows, w1, pe, b1, w2)


def _rep_all(a):
    return jnp.concatenate([a] * (NSA_KV * NSA_HPG), axis=0)


def _rep_heads(a, Qb):
    return jnp.concatenate([a[:Qb]] * NSA_HPG + [a[Qb:]] * NSA_HPG, axis=0)


def _nsa_front(qblk, q0, Qb, kc, vc, n_cmp, n_sel):
    R2 = 2 * Qb
    npad = kc.shape[0]
    nspad = -(-n_sel // LANES) * LANES
    lane = lax.broadcasted_iota(jnp.int32, (Qb, LANES), 1)
    lo_half = lane < NSA_HD
    scale = NSA_HD ** -0.5 * LOG2E
    pieces = []
    for k in range(NSA_KV):
        for r in range(NSA_HPG):
            sl = qblk[:, r * LANES:(r + 1) * LANES] * scale
            pieces.append(jnp.where(lo_half if k == 0 else jnp.logical_not(lo_half), sl, 0.0))
    qs = jnp.concatenate(pieces, axis=0).astype(BF16)
    qp1 = q0 + lax.broadcasted_iota(jnp.int32, (Qb, 1), 0)

    n_idx = lax.broadcasted_iota(jnp.int32, (1, npad), 1)
    visible = (n_idx * CMP_STRIDE + (CMP_BLOCK - 1) <= qp1) & (n_idx < n_cmp)
    s = _bdot_nt(qs, kc) + _rep_all(jnp.where(visible, 0.0, -BIG))
    e = jnp.exp2(s - jnp.max(s, axis=-1, keepdims=True))
    any_visible = _rep_all((qp1 >= CMP_BLOCK - 1) & (n_cmp > 0))
    p_c = e * jnp.where(any_visible, 1.0 / jnp.sum(e, axis=-1, keepdims=True), 0.0)
    o_c = _bdot(p_c, vc)

    psum = []
    for k in range(NSA_KV):
        acc = p_c[(k * NSA_HPG) * Qb:(k * NSA_HPG + 1) * Qb]
        for r in range(1, NSA_HPG):
            acc = acc + p_c[(k * NSA_HPG + r) * Qb:(k * NSA_HPG + r + 1) * Qb]
        psum.append(acc)
    psum = jnp.concatenate(psum, axis=0)
    ci = lax.broadcasted_iota(jnp.int32, (npad, nspad), 0) * CMP_STRIDE
    sj = lax.broadcasted_iota(jnp.int32, (npad, nspad), 1) * SEL_BLOCK
    selmap = jnp.where((ci < sj + SEL_BLOCK) & (ci + CMP_BLOCK > sj), 1.0, 0.0).astype(BF16)
    ph, plo = _split(psum)
    imp = (jnp.dot(ph, selmap, preferred_element_type=F32)
           + jnp.dot(plo, selmap, preferred_element_type=F32))
    j = lax.broadcasted_iota(jnp.int32, (R2, nspad), 1)
    jf = j.astype(F32)
    qpos2 = jnp.concatenate([qp1] * NSA_KV, axis=0)
    cur = qpos2 >> 6
    forced = (j == 0) | (j == cur) | (j == cur - 1)
    imp = jnp.where(forced, SEL_FORCE, imp)
    imp = jnp.where(j * SEL_BLOCK <= qpos2, imp, -SEL_FORCE)
    imp = jnp.where(j < n_sel, imp, NEG)

    def pick(_, carry):
        imp_c, sel_c = carry
        m = jnp.max(imp_c, axis=-1, keepdims=True)
        first = jnp.min(jnp.where(imp_c == m, jf, float(nspad)), axis=-1, keepdims=True)
        hit = jf == first
        return jnp.where(hit, NEG, imp_c), jnp.where(hit, 1.0, sel_c)

    _, sel = lax.fori_loop(0, min(SEL_TOPK, n_sel), pick, (imp, jnp.zeros((R2, nspad), F32)))
    return qs, qp1, o_c, sel


def _nsa_combine(gate, o_c, o_s, o_w, Qb, o_ref):
    lo_half = lax.broadcasted_iota(jnp.int32, (Qb, LANES), 1) < NSA_HD
    g = _sigmoid(gate)
    for r in range(NSA_HPG):
        halves = []
        for k in range(NSA_KV):
            rs = slice((k * NSA_HPG + r) * Qb, (k * NSA_HPG + r + 1) * Qb)
            c = (k * NSA_HPG + r) * 3
            halves.append(g[:, c:c + 1] * o_c[rs] + g[:, c + 1:c + 2] * o_s[rs] + g[:, c + 2:c + 3] * o_w[rs])
        o_ref[0, :, r * LANES:(r + 1) * LANES] = jnp.where(lo_half, halves[0], halves[1])


def _block_columns(k0, n):
    key = k0 + lax.broadcasted_iota(jnp.int32, (n, LANES), 0)
    blk = lax.broadcasted_iota(jnp.int32, (n, LANES), 1)
    return jnp.where((key >> 6) == blk, BIG, 0.0).astype(BF16)


def _block_rows(k0, n):
    key = k0 + lax.broadcasted_iota(jnp.int32, (LANES, n), 1)
    blk = lax.broadcasted_iota(jnp.int32, (LANES, n), 0)
    return jnp.where((key >> 6) == blk, BIG, 0.0).astype(BF16)


def _nsa_prompt_kernel(q_ref, gate_ref, kc_ref, vc_ref, rows_ref, win_ref, o_ref, kaug, vb,
                       *, T, Qb, n_cmp, n_sel, wl):
    i = pl.program_id(1)
    q0 = i * Qb
    R = NSA_KV * NSA_HPG * Qb

    @pl.when(i == 0)
    def _():
        def pack(c, carry):
            r0 = pl.multiple_of(c * SEL_CHUNK, SEL_CHUNK)
            kaug[pl.ds(r0, SEL_CHUNK), 0:LANES] = rows_ref[0, pl.ds(r0, SEL_CHUNK), 2 * LANES:3 * LANES].astype(BF16)
            kaug[pl.ds(r0, SEL_CHUNK), LANES:2 * LANES] = _block_columns(r0, SEL_CHUNK)
            vb[pl.ds(r0, SEL_CHUNK), :] = rows_ref[0, pl.ds(r0, SEL_CHUNK), 3 * LANES:4 * LANES].astype(BF16)
            return carry
        lax.fori_loop(0, T // SEL_CHUNK, pack, 0)

    qs, qp1, o_c, sel = _nsa_front(q_ref[0], q0, Qb, kc_ref[0], vc_ref[0], n_cmp, n_sel)
    qaug = jnp.concatenate([qs, _rep_heads(sel[:, 0:LANES] - 1.0, Qb).astype(BF16)], axis=1)
    qpos = _rep_all(qp1)

    def update(s, vv, carry):
        m, l, acc = carry
        m_new = jnp.maximum(m, jnp.max(s, axis=-1, keepdims=True))
        alpha = jnp.exp2(m - m_new)
        p = jnp.exp2(s - m_new)
        l = alpha * l + jnp.sum(p, axis=-1, keepdims=True)
        acc = alpha * acc + jnp.dot(p.astype(BF16), vv, preferred_element_type=F32)
        return m_new, l, acc

    def scores(k0):
        return lax.dot_general(qaug, kaug[pl.ds(k0, SEL_CHUNK), :], (((1,), (1,)), ((), ())),
                               preferred_element_type=F32)

    def full_chunk(c, carry):
        k0 = pl.multiple_of(c * SEL_CHUNK, SEL_CHUNK)
        return update(scores(k0), vb[pl.ds(k0, SEL_CHUNK), :], carry)

    n_full = q0 // SEL_CHUNK
    init = (jnp.full((R, 1), -4.0 * BIG, F32), jnp.zeros((R, 1), F32), jnp.zeros((R, LANES), F32))
    carry = lax.fori_loop(0, n_full, full_chunk, init)
    k0 = pl.multiple_of(n_full * SEL_CHUNK, SEL_CHUNK)
    kpos = k0 + lax.broadcasted_iota(jnp.int32, (1, SEL_CHUNK), 1)
    s_diag = jnp.where(kpos <= qpos, scores(k0), -2.0 * BIG)
    _, l_s, acc_s = update(s_diag, vb[pl.ds(k0, SEL_CHUNK), :], carry)
    o_s = acc_s * (1.0 / l_s)

    ws = pl.multiple_of(jnp.maximum(q0 - WINDOW, 0), LANES)
    wk = win_ref[0, pl.ds(ws, wl), 0:LANES]
    wv = win_ref[0, pl.ds(ws, wl), LANES:2 * LANES]
    dpos = qp1 - (ws + lax.broadcasted_iota(jnp.int32, (1, wl), 1))
    s_w = _bdot_nt(qs, wk) + _rep_all(jnp.where((dpos >= 0) & (dpos <= WINDOW), 0.0, -BIG))
    e_w = jnp.exp2(s_w - jnp.max(s_w, axis=-1, keepdims=True))
    o_w = _bdot(e_w, wv) * (1.0 / jnp.sum(e_w, axis=-1, keepdims=True))

    _nsa_combine(gate_ref[0], o_c, o_s, o_w, Qb, o_ref)


def _nsa_prompt(nq, gate, kc, vc, rows, win):
    B, T, HD = nq.shape
    Qb = QBLOCK
    nch = kc.shape[1]
    n_sel = -(-T // SEL_BLOCK)
    wl = WINDOW + Qb
    assert T % SEL_CHUNK == 0 and T >= wl and SEL_TOPK <= n_sel <= LANES
    return pl.pallas_call(
        functools.partial(_nsa_prompt_kernel, T=T, Qb=Qb, n_cmp=nch - 1, n_sel=n_sel, wl=wl),
        grid=(B, T // Qb),
        in_specs=[pl.BlockSpec((1, Qb, HD), lambda b, i: (b, i, 0)),
                  pl.BlockSpec((1, Qb, LANES), lambda b, i: (b, i, 0)),
                  pl.BlockSpec((1, nch, LANES), lambda b, i: (b, 0, 0)),
                  pl.BlockSpec((1, nch, LANES), lambda b, i: (b, 0, 0)),
                  pl.BlockSpec((1, T, 4 * LANES), lambda b, i: (b, 0, 0)),
                  pl.BlockSpec((1, T, 2 * LANES), lambda b, i: (b, 0, 0))],
        out_specs=pl.BlockSpec((1, Qb, HD), lambda b, i: (b, i, 0)),
        out_shape=jax.ShapeDtypeStruct((B, T, HD), F32),
        scratch_shapes=[pltpu.VMEM((T, 2 * LANES), BF16), pltpu.VMEM((T, LANES), BF16)],
        compiler_params=_cparams(2),
        name="nsa_attn_prompt",
    )(nq, gate, kc, vc, rows, win)


def _pages_copy(cache_hbm, layer, page, r0, dst, j, sem):
    n = cache_hbm.shape[-1]
    return pltpu.make_async_copy(cache_hbm.at[layer, page, pl.ds(r0, 2)],
                                 dst.at[:, :, pl.ds(pl.multiple_of(j * n, n), n)], sem)


def _pages_start(pt_ref, b, cache_hbm, layer, r0, dst, sem, npages):
    def issue(j, carry):
        _pages_copy(cache_hbm, layer, pt_ref[b, j], r0, dst, j, sem).start()
        return carry
    lax.fori_loop(0, npages, issue, 0)


def _pages_wait(cache_hbm, layer, r0, dst, sem, npages):
    def wait(j, carry):
        _pages_copy(cache_hbm, layer, 0, r0, dst, j, sem).wait()
        return carry
    lax.fori_loop(0, npages, wait, 0)


def _softmax2(s1, s2, mask2):
    s2 = jnp.where(mask2, s2, -2.0 * BIG)
    m = jnp.maximum(jnp.max(s1, axis=-1, keepdims=True), jnp.max(s2, axis=-1, keepdims=True))
    e1 = jnp.exp2(s1 - m)
    e2 = jnp.exp2(s2 - m)
    den = jnp.sum(e1, axis=-1, keepdims=True) + jnp.sum(e2, axis=-1, keepdims=True)
    return e1, e2, 1.0 / den


def _nsa_sample_kernel(pt_ref, q_ref, gate_ref, rows_ref, wt_ref, wnew_ref, cache_hbm,
                       w1_ref, pe_ref, b1_ref, w2_ref, o_ref,
                       cmpbuf, selbuf, xk, xv, kaug, vt, newbuf, wnewbuf, bias_sc, csem, ssem,
                       *, layer, npages, P, Q, wb, nb, n_sel):
    b = pl.program_id(0)
    nseq = pl.num_programs(0)
    page = cache_hbm.shape[-1]
    nch = P // CMP_STRIDE

    @pl.when(b == 0)
    def _():
        _pages_start(pt_ref, 0, cache_hbm, layer, 0, cmpbuf, csem, npages)
        _pages_start(pt_ref, 0, cache_hbm, layer, 2, selbuf, ssem, npages)
        _compress_bias(w1_ref, pe_ref, bias_sc)

        def blocks(c, carry):
            c0 = pl.multiple_of(c * CAST_CHUNK, CAST_CHUNK)
            kaug[LANES:2 * LANES, pl.ds(c0, CAST_CHUNK)] = _block_rows(c0, CAST_CHUNK)
            return carry
        lax.fori_loop(0, P // CAST_CHUNK, blocks, 0)
        newbuf[...] = jnp.zeros(newbuf.shape, F32)
        wnewbuf[...] = jnp.zeros(wnewbuf.shape, F32)

    _pages_wait(cache_hbm, layer, 0, cmpbuf, csem, npages)

    def to_rows(j, carry):
        c0 = pl.multiple_of(j * page, page)
        xk[pl.ds(c0, page), :] = cmpbuf[0, :, pl.ds(c0, page)].T
        xv[pl.ds(c0, page), :] = cmpbuf[1, :, pl.ds(c0, page)].T
        return carry
    lax.fori_loop(0, npages, to_rows, 0)

    @pl.when(b + 1 < nseq)
    def _():
        _pages_start(pt_ref, b + 1, cache_hbm, layer, 0, cmpbuf, csem, npages)

    kc, vc = _compress_x((xk, xv), nch, w1_ref, bias_sc, b1_ref, w2_ref)

    _pages_wait(cache_hbm, layer, 2, selbuf, ssem, npages)

    def pack(c, carry):
        c0 = pl.multiple_of(c * CAST_CHUNK, CAST_CHUNK)
        kaug[0:LANES, pl.ds(c0, CAST_CHUNK)] = selbuf[0, :, pl.ds(c0, CAST_CHUNK)].astype(BF16)
        vt[:, pl.ds(c0, CAST_CHUNK)] = selbuf[1, :, pl.ds(c0, CAST_CHUNK)].astype(BF16)
        return carry
    lax.fori_loop(0, P // CAST_CHUNK, pack, 0)

    @pl.when(b + 1 < nseq)
    def _():
        _pages_start(pt_ref, b + 1, cache_hbm, layer, 2, selbuf, ssem, npages)

    newbuf[0:Q, :] = rows_ref[0, :, 2 * LANES:4 * LANES]
    wnewbuf[0:Q, :] = wnew_ref[0]

    qs, qp1, o_c, sel = _nsa_front(q_ref[0], P, Q, kc, vc, nch - 1, n_sel)
    qpos = _rep_all(qp1)
    lane = lax.broadcasted_iota(jnp.int32, (1, LANES), 1)
    new_pos = P + lane
    is_new = lane < Q

    qaug = jnp.concatenate([qs, _rep_heads(sel[:, 0:LANES] - 1.0, Q).astype(BF16)], axis=1)
    s_past = jnp.dot(qaug, kaug[...], preferred_element_type=F32)
    s_new = _bdot_nt(qs, newbuf[:, 0:LANES])
    new_ok = is_new & (new_pos <= qpos) & (_rep_heads(sel[:, nb:nb + 1], Q) > 0.5)
    e1, e2, inv = _softmax2(s_past, s_new, new_ok)
    o_s = (lax.dot_general(e1.astype(BF16), vt[...], (((1,), (1,)), ((), ())), preferred_element_type=F32)
           + _bdot(e2, newbuf[:, LANES:2 * LANES])) * inv

    dpast = qp1 - ((P - wb) + lax.broadcasted_iota(jnp.int32, (1, wb), 1))
    s_wp = (jnp.dot(qs, wt_ref[0, 0].astype(BF16), preferred_element_type=F32)
            + _rep_all(jnp.where((dpast >= 0) & (dpast <= WINDOW), 0.0, -BIG)))
    s_wn = _bdot_nt(qs, wnewbuf[:, 0:LANES])
    dnew = qpos - new_pos
    e1, e2, inv = _softmax2(s_wp, s_wn, is_new & (dnew >= 0) & (dnew <= WINDOW))
    o_w = (lax.dot_general(e1.astype(BF16), wt_ref[0, 1].astype(BF16), (((1,), (1,)), ((), ())),
                           preferred_element_type=F32)
           + _bdot(e2, wnewbuf[:, LANES:2 * LANES])) * inv

    _nsa_combine(gate_ref[0], o_c, o_s, o_w, Q, o_ref)


def _nsa_sample(page_table, nq, gate, rows, win_t, win_new, cache_t, layer, cw):
    DB, Q, HD = nq.shape
    npages = page_table.shape[1]
    page = cache_t.shape[-1]
    P = npages * page
    wb = win_t.shape[-1]
    nb = P // SEL_BLOCK
    n_sel = -(-(P + Q) // SEL_BLOCK)
    w1, pe, b1, w2 = cw
    assert Q == SUBLANES and P % SEL_BLOCK == 0 and Q <= SEL_BLOCK and nb <= LANES and n_sel >= SEL_TOPK
    assert P % CAST_CHUNK == 0 and (P + Q) // CMP_STRIDE == P // CMP_STRIDE and page == LANES
    bs = lambda shape: pl.BlockSpec((1,) + shape, lambda b, pt: (b,) + (0,) * len(shape))
    cs = lambda shape: pl.BlockSpec(shape, lambda b, pt: (0,) * len(shape), pipeline_mode=pl.Buffered(1))
    grid_spec = pltpu.PrefetchScalarGridSpec(
        num_scalar_prefetch=1, grid=(DB,),
        in_specs=[bs((Q, HD)), bs((Q, LANES)), bs((Q, 4 * LANES)), bs((2, LANES, wb)), bs((Q, 2 * LANES)),
                  pl.BlockSpec(memory_space=pl.ANY),
                  cs(w1.shape), cs(pe.shape), cs(b1.shape), cs(w2.shape)],
        out_specs=bs((Q, HD)),
        scratch_shapes=[pltpu.VMEM((2, LANES, P), F32), pltpu.VMEM((2, LANES, P), F32),
                        pltpu.VMEM((P, LANES), F32), pltpu.VMEM((P, LANES), F32),
                        pltpu.VMEM((2 * LANES, P), BF16), pltpu.VMEM((LANES, P), BF16),
                        pltpu.VMEM((LANES, 2 * LANES), F32), pltpu.VMEM((LANES, 2 * LANES), F32),
                        pltpu.VMEM((2, SUBLANES, 4 * LANES), F32),
                        pltpu.SemaphoreType.DMA(()), pltpu.SemaphoreType.DMA(())])
    return pl.pallas_call(
        functools.partial(_nsa_sample_kernel, layer=layer, npages=npages, P=P, Q=Q, wb=wb, nb=nb, n_sel=n_sel),
        grid_spec=grid_spec,
        out_shape=jax.ShapeDtypeStruct((DB, Q, HD), F32),
        compiler_params=_cparams(1),
        name="nsa_sample",
    )(page_table, nq, gate, rows, win_t, win_new, cache_t, w1, pe, b1, w2)


def _ret_kernel(q_ref, k_ref, v_ref, g_ref, cos_ref, sin_ref, s0_ref, gn_ref, o_ref, snew_ref, s_sc, *, C, nC):
    c = pl.program_id(1)

    @pl.when(c == 0)
    def _():
        s_sc[...] = s0_ref[0]

    cosf = cos_ref[...]
    sinf = sin_ref[...]
    diff = (lax.broadcasted_iota(jnp.int32, (C, C), 0) - lax.broadcasted_iota(jnp.int32, (C, C), 1)).astype(F32)
    ii = lax.broadcasted_iota(jnp.int32, (C, 1), 0).astype(F32)
    half = RET_DK // 2
    for h in range(RET_HEADS):
        lg = math.log(1.0 - 2.0 ** (-5.0 - h))
        hs = slice(h * RET_DK, (h + 1) * RET_DK)
        q = q_ref[0, :, hs]
        k = k_ref[0, :, hs]
        v = v_ref[0, :, hs]
        qr = q * cosf + pltpu.roll(q, half, 1) * sinf
        kr = (k * cosf + pltpu.roll(k, half, 1) * sinf) * (RET_DK ** -0.5)
        decay = jnp.where(diff >= 0, jnp.exp(jnp.maximum(diff, 0.0) * lg), 0.0)
        o_inner = _bdot(_bdot_nt(qr, kr) * decay, v)
        s_old = s_sc[h]
        o_cross = _bdot(qr, s_old) * jnp.exp((ii + 1.0) * lg)
        kw = kr * jnp.exp((C - 1.0 - ii) * lg)
        kv = lax.dot_general(kw.astype(BF16), v.astype(BF16), (((0,), (0,)), ((), ())),
                             preferred_element_type=F32)
        s_sc[h] = math.exp(C * lg) * s_old + kv
        o = o_inner + o_cross
        mu = jnp.mean(o, axis=-1, keepdims=True)
        var = jnp.mean(jnp.square(o - mu), axis=-1, keepdims=True)
        gate = g_ref[0, :, hs]
        o_ref[0, :, hs] = ((o - mu) * lax.rsqrt(var + EPS)) * gn_ref[:, hs] * (gate * _sigmoid(gate))

    @pl.when(c == nC - 1)
    def _():
        snew_ref[0] = s_sc[...]


def _retention(rq, rk, rv, rg, cosf, sinf, s0, gn):
    B, T, W = rq.shape
    C = RET_CHUNK if (T >= RET_CHUNK and T % RET_CHUNK == 0) else T
    nC = T // C
    tok = pl.BlockSpec((1, C, W), lambda b, c: (b, c, 0))
    tab = pl.BlockSpec((C, RET_DK), lambda b, c: (c, 0))
    st = pl.BlockSpec((1,) + s0.shape[1:], lambda b, c: (b, 0, 0, 0))
    return pl.pallas_call(
        functools.partial(_ret_kernel, C=C, nC=nC),
        grid=(B, nC),
        in_specs=[tok, tok, tok, tok, tab, tab, st, _const_spec((1, W))],
        out_specs=[tok, st],
        out_shape=[jax.ShapeDtypeStruct((B, T, W), F32), jax.ShapeDtypeStruct(s0.shape, F32)],
        scratch_shapes=[pltpu.VMEM(s0.shape[1:], F32)],
        compiler_params=_cparams(2),
        name="retention",
    )(rq, rk, rv, rg, cosf, sinf, s0, gn)


def _shift_carry(x, k, tail8):
    r = pltpu.roll(x, k, 0)
    row8 = lax.broadcasted_iota(jnp.int32, (SUBLANES, 1), 0)
    first = jnp.where(row8 >= k, r[:SUBLANES], pltpu.roll(tail8, k, 0))
    return jnp.concatenate([first, r[SUBLANES:]], axis=0)


def _shift_seg(x, k, fill, tpos):
    return jnp.where(tpos >= k, pltpu.roll(x, k, 0), fill)


def _rglru_kernel(x_ref, gate_ref, st_ref, h0_ref, cw_ref, cb_ref, wa_ref, ba_ref, wx_ref, bx_ref, lam_ref,
                  o_ref, h_ref, tail_sc, h_sc, *, tm, seg):
    carry = seg == 0
    x = x_ref[0]
    rows = lax.broadcasted_iota(jnp.int32, (tm, 1), 0)
    if carry:
        @pl.when(pl.program_id(1) == 0)
        def _():
            tail_sc[...] = st_ref[0]
            h_sc[...] = h0_ref[0]
        tail8 = tail_sc[...]
        shifted = [_shift_carry(x, k, tail8) for k in range(1, RG_CONV)]
        tpos = rows
        seglen = tm
        h_in = jnp.where(rows == 0, h_sc[SUBLANES - 1:SUBLANES, :], 0.0)
    else:
        tpos = rows & (seg - 1)
        shifted = [_shift_seg(x, k, st_ref[k - 1], tpos) for k in range(1, RG_CONV)]
        seglen = seg
        h_in = h0_ref[0]
    xc = cb_ref[...] + cw_ref[RG_CONV - 1:RG_CONV, :] * x
    for k in range(1, RG_CONV):
        xc = xc + cw_ref[RG_CONV - 1 - k:RG_CONV - k, :] * shifted[k - 1]
    r = _sigmoid(_bdot(xc, wa_ref[...]) + ba_ref[...])
    i = _sigmoid(_bdot(xc, wx_ref[...]) + bx_ref[...])
    lam = lam_ref[...]
    softplus = jnp.maximum(-lam, 0.0) + jnp.log(1.0 + jnp.exp(-jnp.abs(lam)))
    log_a = (-RG_C * r) * softplus
    a = jnp.exp(log_a)
    bt = jnp.sqrt(1.0 - a * a) * (i * xc)
    bt = bt + a * h_in
    k = 1
    while k < seglen:
        ok = tpos >= k
        a_prev = jnp.where(ok, pltpu.roll(a, k, 0), 1.0)
        b_prev = jnp.where(ok, pltpu.roll(bt, k, 0), 0.0)
        bt = a * b_prev + bt
        a = a * a_prev
        k *= 2
    o_ref[0] = bt * _gelu(gate_ref[0])
    if carry:
        tail_sc[...] = x[tm - SUBLANES:]
        h_sc[...] = bt[tm - SUBLANES:]
        h_ref[0] = bt[tm - SUBLANES:]
    else:
        h_ref[0] = bt


def _rglru(rx, rgate, st, h0, rw, tm, seg):
    G, Tg, W = rx.shape
    tok = pl.BlockSpec((1, tm, W), lambda g, t: (g, t, 0))
    if seg == 0:
        st_spec = pl.BlockSpec((1, SUBLANES, W), lambda g, t: (g, 0, 0))
        h0_spec = pl.BlockSpec((1, SUBLANES, W), lambda g, t: (g, 0, 0))
        h_spec = pl.BlockSpec((1, SUBLANES, W), lambda g, t: (g, 0, 0))
        h_shape = (G, SUBLANES, W)
    else:
        st_spec = pl.BlockSpec(st.shape, lambda g, t: (0, 0, 0))
        h0_spec = tok
        h_spec = tok
        h_shape = (G, Tg, W)
    return pl.pallas_call(
        functools.partial(_rglru_kernel, tm=tm, seg=seg),
        grid=(G, Tg // tm),
        in_specs=[tok, tok, st_spec, h0_spec] + [_const_spec(a.shape) for a in rw],
        out_specs=[tok, h_spec],
        out_shape=[jax.ShapeDtypeStruct((G, Tg, W), F32), jax.ShapeDtypeStruct(h_shape, F32)],
        scratch_shapes=[pltpu.VMEM((SUBLANES, W), F32), pltpu.VMEM((SUBLANES, W), F32)],
        compiler_params=_cparams(2),
        name="rglru",
    )(rx, rgate, st, h0, *rw)


def _merge_kernel(x_ref, oa_ref, or_ref, oc_ref, mg_ref, gt_ref, wa_ref, wb_ref, wc_ref, wo_ref, y_ref, *, D):
    pa = _bdot(oa_ref[0], wa_ref[...])
    pb = _bdot(or_ref[0], wb_ref[...])
    pc = _bdot(oc_ref[0], wc_ref[...])
    merged = (_sigmoid(mg_ref[0, :, 0:D]) * pa + _sigmoid(mg_ref[0, :, D:2 * D]) * pb
              + _sigmoid(mg_ref[0, :, 2 * D:3 * D]) * pc)
    y_ref[0] = x_ref[0] + gt_ref[0] * _bdot(merged, wo_ref[...])


def _merge(x, oa, orr, oc, mg, gt, wa, wb, wc, wo, tm):
    G, Tg, D = x.shape
    tok = lambda w: pl.BlockSpec((1, tm, w), lambda g, t: (g, t, 0))
    return pl.pallas_call(
        functools.partial(_merge_kernel, D=D),
        grid=(G, Tg // tm),
        in_specs=[tok(D), tok(oa.shape[2]), tok(orr.shape[2]), tok(oc.shape[2]), tok(3 * D),
                  _mod_spec(gt.shape[1], tm, D),
                  _const_spec(wa.shape), _const_spec(wb.shape), _const_spec(wc.shape), _const_spec(wo.shape)],
        out_specs=tok(D),
        out_shape=jax.ShapeDtypeStruct((G, Tg, D), F32),
        compiler_params=_cparams(2),
        name="merge_out",
    )(x, oa, orr, oc, mg, gt, wa, wb, wc, wo)


def _ffn_kernel(x_ref, g_ref, sc_ref, sh_ref, gt_ref, st_ref, wup_ref, cw_ref, cb_ref, wdn_ref, fg_ref,
                y_ref, fnew_ref, tail_sc, *, tm, seg, F, wck, final):
    carry = seg == 0
    x = x_ref[0]
    h = _rms_mod(x, g_ref[...], sc_ref[0], sh_ref[0]).astype(BF16)
    rows = lax.broadcasted_iota(jnp.int32, (tm, 1), 0)
    if carry:
        @pl.when(pl.program_id(1) == 0)
        def _():
            tail_sc[...] = st_ref[0]
    else:
        tpos = rows & (seg - 1)
    acc = jnp.zeros(x.shape, F32)
    for c0 in range(0, F, wck):
        cs = slice(c0, c0 + wck)
        gp = jnp.dot(h, wup_ref[:, c0:c0 + wck], preferred_element_type=F32)
        val = jnp.dot(h, wup_ref[:, F + c0:F + c0 + wck], preferred_element_type=F32)
        if carry:
            tail8 = tail_sc[:, cs]
            shifted = [_shift_carry(gp, k, tail8) for k in range(1, FFN_CONV)]
            tail_sc[:, cs] = gp[tm - SUBLANES:]
            fnew_ref[0, :, cs] = gp[tm - SUBLANES:]
        else:
            shifted = [_shift_seg(gp, k, st_ref[k - 1, :, cs], tpos) for k in range(1, FFN_CONV)]
            fnew_ref[0, :, cs] = gp
        gc = cb_ref[:, cs] + cw_ref[FFN_CONV - 1:FFN_CONV, cs] * gp
        for k in range(1, FFN_CONV):
            gc = gc + cw_ref[FFN_CONV - 1 - k:FFN_CONV - k, cs] * shifted[k - 1]
        act = (gc * _sigmoid(gc)) * val
        acc = acc + _bdot(act, wdn_ref[cs, :])
    y = x + gt_ref[0] * acc
    if final:
        y = (y * lax.rsqrt(jnp.mean(y * y, axis=-1, keepdims=True) + EPS)) * fg_ref[...]
    y_ref[0] = y


def _ffn(x, g, sc, sh, gt, st, wup, cw, cb, wdn, fg, tm, seg, final):
    G, Tg, D = x.shape
    F = wdn.shape[0]
    wck = F // 2 if (F // 2) % LANES == 0 else F
    tok = pl.BlockSpec((1, tm, D), lambda g_, t: (g_, t, 0))
    sm = sc.shape[1]
    if seg == 0:
        st_spec = pl.BlockSpec((1, SUBLANES, F), lambda g_, t: (g_, 0, 0))
        fn_spec = pl.BlockSpec((1, SUBLANES, F), lambda g_, t: (g_, 0, 0))
        fn_shape = (G, SUBLANES, F)
    else:
        st_spec = pl.BlockSpec(st.shape, lambda g_, t: (0, 0, 0))
        fn_spec = pl.BlockSpec((1, tm, F), lambda g_, t: (g_, t, 0))
        fn_shape = (G, Tg, F)
    return pl.pallas_call(
        functools.partial(_ffn_kernel, tm=tm, seg=seg, F=F, wck=wck, final=final),
        grid=(G, Tg // tm),
        in_specs=[tok, _const_spec((1, D)), _mod_spec(sm, tm, D), _mod_spec(sm, tm, D), _mod_spec(sm, tm, D),
                  st_spec, _const_spec(wup.shape), _const_spec(cw.shape), _const_spec(cb.shape),
                  _const_spec(wdn.shape), _const_spec((1, D))],
        out_specs=[tok, fn_spec],
        out_shape=[jax.ShapeDtypeStruct((G, Tg, D), F32), jax.ShapeDtypeStruct(fn_shape, F32)],
        scratch_shapes=[pltpu.VMEM((SUBLANES, F), F32)],
        compiler_params=_cparams(2),
        name="conv_ffn",
    )(x, g, sc, sh, gt, st, wup, cw, cb, wdn, fg)


def _head_perm():
    return np.array([(k * NSA_HPG + r) * NSA_HD + d
                     for r in range(NSA_HPG) for k in range(NSA_KV) for d in range(NSA_HD)], np.int32)


def _block_diag(w):
    n, a, b = w.shape[-3:]
    eye = jnp.eye(n, dtype=w.dtype)
    out = jnp.einsum('ij,...iab->...iajb', eye, w)
    return out.reshape(w.shape[:-3] + (n * a, n * b))


def _seg_fill(buf, k, seg):
    B, nb, C = buf.shape
    part = jnp.concatenate([buf[:, nb - k:, :], jnp.zeros((B, seg - k, C), buf.dtype)], axis=1)
    return part.reshape(B * seg, C)


def kernel(x_prompt, x_sample, cache_nsa, cache_nsa_win, state_ret, state_rglru_h, state_rglru_conv,
           state_ffn_conv, page_table, c_prompt, c_sample, norm1_g, norm2_g, w_ada, b_ada, w_in, cmp_pe,
           cmp_w1, cmp_b1, cmp_w2, ret_gn_g, rg_conv_w, rg_conv_b, rg_w_a, rg_b_a, rg_w_x, rg_b_x, rg_lambda,
           w_br_a, w_br_b, w_br_c, w_out, ffn_w_up, ffn_conv_w, ffn_conv_b, ffn_w_down, final_norm_g):
    B, T, D = x_prompt.shape
    DB, Q, _ = x_sample.shape
    L = w_in.shape[0]
    npages = page_table.shape[1]
    page = cache_nsa.shape[2]
    P = npages * page
    NQ = NSA_KV * NSA_HPG * NSA_HD
    NKV = NSA_KV * NSA_HD
    RW = RET_HEADS * RET_DK
    W = rg_conv_w.shape[2]
    F = ffn_w_down.shape[1]
    NS = DB * Q
    wbuf_len = cache_nsa_win.shape[2]
    assert Q == SUBLANES and T >= RG_CONV and P % CMP_STRIDE == 0

    mod = _ada(jnp.concatenate([c_prompt, c_sample], axis=0), w_ada, b_ada)
    cache_t = jnp.transpose(cache_nsa, (0, 1, 3, 4, 5, 2)).reshape(L, cache_nsa.shape[1], 4, NKV, page)
    win_t_all = jnp.transpose(cache_nsa_win, (0, 1, 3, 4, 5, 2)).reshape(L, DB, 2, NKV, wbuf_len)
    perm = _head_perm()

    half = RET_DK // 2
    freq = ROPE_BASE ** (-jnp.arange(half, dtype=F32) / half)

    def rope_tables(pos):
        ang = pos.astype(F32)[:, None] * freq[None, :]
        cos, sin = jnp.cos(ang), jnp.sin(ang)
        return jnp.concatenate([cos, cos], axis=1), jnp.concatenate([-sin, sin], axis=1)

    cos_p, sin_p = rope_tables(jnp.arange(T, dtype=jnp.int32))
    cos_s, sin_s = rope_tables(P + jnp.arange(Q, dtype=jnp.int32))

    widths = (NQ, 4 * NKV, 2 * NKV, LANES, RW, RW, RW, RW, W, W, 3 * D)
    offs = np.cumsum((0, NQ, 6 * NKV, 3 * NSA_KV * NSA_HPG, RW, RW, RW, RW, W, W, 3 * D))
    ngate = 3 * NSA_KV * NSA_HPG

    xp = x_prompt
    xs = x_sample.reshape(1, NS, D)
    outs_p = [[] for _ in range(6)]
    outs_s = [[] for _ in range(6)]
    tm_p = 256 if T % 256 == 0 else T

    for l in range(L):
        wi = w_in[l]
        w_cat = jnp.concatenate([
            wi[:, offs[0]:offs[1]][:, perm],
            wi[:, offs[1]:offs[1] + 4 * NKV],
            wi[:, offs[1] + 4 * NKV:offs[2]],
            jnp.pad(wi[:, offs[2]:offs[3]], ((0, 0), (0, LANES - ngate))),
            wi[:, offs[3]:]], axis=1).astype(BF16)
        w1 = cmp_w1[l]
        bd = lambda w: _block_diag(jnp.broadcast_to(w[:, :, None], w.shape[:2] + (NSA_KV,) + w.shape[2:]))
        cw1 = jnp.concatenate([bd(w1[:, :CMP_STRIDE]), bd(w1[:, CMP_STRIDE:])], axis=-1)
        cw1 = cw1.reshape(2, CMP_STRIDE * NKV, 4 * LANES).astype(BF16)
        cpe = jnp.tile(cmp_pe[l], (1, 1, NSA_KV)).reshape(2, 2, CMP_STRIDE * NKV)
        cb1 = jnp.tile(cmp_b1[l], (1, NSA_KV))[:, None, :]
        cw2 = _block_diag(jnp.broadcast_to(cmp_w2[l][:, None], (2, NSA_KV) + cmp_w2.shape[2:])).astype(BF16)
        cw = (cw1, cpe, cb1, cw2)
        rw = (rg_conv_w[l], rg_conv_b[l][None], _block_diag(rg_w_a[l]).astype(BF16), rg_b_a[l][None],
              _block_diag(rg_w_x[l]).astype(BF16), rg_b_x[l][None], rg_lambda[l][None])
        wa = w_br_a[l][perm].astype(BF16)
        wb = w_br_b[l].astype(BF16)
        wc = w_br_c[l].astype(BF16)
        wo = w_out[l].astype(BF16)
        wup = ffn_w_up[l].astype(BF16)
        wdn = ffn_w_down[l].astype(BF16)
        g1 = norm1_g[l][None]
        g2 = norm2_g[l][None]
        gn = ret_gn_g[l][None]
        fcw, fcb = ffn_conv_w[l], ffn_conv_b[l][None]
        fg = final_norm_g[None]
        final = l == L - 1

        m = [mod[l, :B, i * D:(i + 1) * D][:, None, :] for i in range(6)]
        (nq, rows, win, gate, rq, rk, rv, rg, rx, rgate, mg) = _inproj(xp, g1, m[1], m[0], w_cat, widths, tm_p)
        kc, vc = _compress_prompt(rows, cw)
        o_a = _nsa_prompt(nq, gate, kc, vc, rows, win)
        o_r, s_new = _retention(rq, rk, rv, rg, cos_p, sin_p,
                                jnp.zeros((B, RET_HEADS, RET_DK, RET_DK), F32), gn)
        zs = jnp.zeros((B, SUBLANES, W), F32)
        o_c, h_tail = _rglru(rx, rgate, zs, zs, rw, tm_p, 0)
        x1 = _merge(xp, o_a, o_r, o_c, mg, m[2], wa, wb, wc, wo, tm_p)
        xp, f_tail = _ffn(x1, g2, m[4], m[3], m[5], jnp.zeros((B, SUBLANES, F), F32), wup, fcw, fcb, wdn, fg,
                          tm_p, 0, final)
        wn = min(WINDOW, T)
        outs_p[0].append(rows.reshape(B, T, 4, NSA_KV, NSA_HD))
        outs_p[1].append(win[:, T - wn:].reshape(B, wn, 2, NSA_KV, NSA_HD))
        outs_p[2].append(s_new)
        outs_p[3].append(h_tail[:, SUBLANES - 1])
        outs_p[4].append(rx[:, T - (RG_CONV - 1):])
        outs_p[5].append(f_tail[:, SUBLANES - (FFN_CONV - 1):])

        ms = [jnp.repeat(mod[l, B:, i * D:(i + 1) * D], Q, axis=0)[None] for i in range(6)]
        (nq, rows, win, gate, rq, rk, rv, rg, rx, rgate, mg) = _inproj(xs, g1, ms[1], ms[0], w_cat, widths, NS)
        r3 = lambda a: a.reshape(DB, Q, a.shape[-1])
        o_a = _nsa_sample(page_table, r3(nq), r3(gate), r3(rows), win_t_all[l], r3(win), cache_t, l, cw)
        o_r, s_new = _retention(r3(rq), r3(rk), r3(rv), r3(rg), cos_s, sin_s, state_ret[l].astype(F32), gn)
        cbuf = state_rglru_conv[l]
        st = jnp.stack([_seg_fill(cbuf, k, Q) for k in range(1, RG_CONV)])
        h0 = jnp.pad(state_rglru_h[l].astype(F32)[:, None, :], ((0, 0), (0, Q - 1), (0, 0))).reshape(1, NS, W)
        o_c, h_all = _rglru(rx, rgate, st, h0, rw, NS, Q)
        x1 = _merge(xs, o_a.reshape(1, NS, NQ), o_r.reshape(1, NS, RW), o_c, mg, ms[2], wa, wb, wc, wo, NS)
        fbuf = state_ffn_conv[l]
        fst = jnp.stack([_seg_fill(fbuf, k, Q) for k in range(1, FFN_CONV)])
        xs, g_all = _ffn(x1, g2, ms[4], ms[3], ms[5], fst, wup, fcw, fcb, wdn, fg, NS, Q, final)
        keys = jnp.concatenate([cache_nsa_win[l], r3(win).reshape(DB, Q, 2, NSA_KV, NSA_HD)], axis=1)
        outs_s[0].append(rows.reshape(DB, Q, 4, NSA_KV, NSA_HD))
        outs_s[1].append(keys[:, Q:])
        outs_s[2].append(s_new)
        outs_s[3].append(h_all.reshape(DB, Q, W)[:, Q - 1])
        outs_s[4].append(jnp.concatenate([cbuf, rx.reshape(DB, Q, W)], axis=1)[:, Q:])
        outs_s[5].append(jnp.concatenate([fbuf, g_all.reshape(DB, Q, F)], axis=1)[:, Q:])

    sp = [jnp.stack(a) for a in outs_p]
    ss = [jnp.stack(a) for a in outs_s]
    return (xp, xs.reshape(DB, Q, D), sp[0], ss[0], sp[1], ss[1], sp[2], ss[2],
            sp[3], ss[3], sp[4], ss[4], sp[5], ss[5])
```

```python
import functools
import math

import numpy as np
import jax
import jax.numpy as jnp
from jax import lax
from jax.experimental import pallas as pl
from jax.experimental.pallas import tpu as pltpu

F32 = jnp.float32
BF16 = jnp.bfloat16

NSA_KV = 2
NSA_HPG = 4
NSA_HD = 64
CMP_STRIDE = 16
CMP_BLOCK = 32
SEL_BLOCK = 64
SEL_TOPK = 16
SEL_FORCE = 1e4
WINDOW = 512
QBLOCK = 128
RET_HEADS = 4
RET_DK = 128
RET_CHUNK = 128
ROPE_BASE = 10000.0
RG_CONV = 4
RG_C = 8.0
FFN_CONV = 3
EPS = 1e-6

NEG = -1e30
BIG = float(2 ** 60)
LOG2E = 1.4426950408889634
SUBLANES = 8
LANES = 128
VMEM_LIMIT_V7X = 56 * 1024 * 1024
SEL_CHUNK = 512
CAST_CHUNK = 1024
PAGE_UNROLL = 8


def _cparams(n_grid):
    return pltpu.CompilerParams(dimension_semantics=("arbitrary",) * n_grid,
                                vmem_limit_bytes=VMEM_LIMIT_V7X)


def _bdot(a, b):
    return jnp.dot(a.astype(BF16), b.astype(BF16), preferred_element_type=F32)


def _bdot_nt(a, b):
    return lax.dot_general(a.astype(BF16), b.astype(BF16), (((1,), (1,)), ((), ())),
                           preferred_element_type=F32)


def _split(a):
    hi = a.astype(BF16)
    lo = (a - hi.astype(F32)).astype(BF16)
    return hi, lo


def _dot3(a, b):
    ah, al = _split(a)
    bh, bl = _split(b)
    d = functools.partial(jnp.dot, preferred_element_type=F32)
    return d(ah, bh) + d(al, bh) + d(ah, bl)


def _sigmoid(x):
    return 1.0 / (1.0 + jnp.exp(-x))


def _gelu(x):
    return 0.5 * x * (1.0 + jnp.tanh(0.7978845608028654 * (x + 0.044715 * (x * x * x))))


def _rms_mod(x, g, sc, sh):
    y = x * lax.rsqrt(jnp.mean(x * x, axis=-1, keepdims=True) + EPS)
    return (y * g) * (1.0 + sc) + sh


def _ada_kernel(c_ref, w_ref, b_ref, o_ref):
    c = c_ref[...]
    o_ref[0] = _dot3(c * _sigmoid(c), w_ref[0]) + b_ref[0]


def _ada(c_all, w_ada, b_ada):
    L, D, E = w_ada.shape
    n = c_all.shape[0]
    tn = 1536 if E % 1536 == 0 else E
    return pl.pallas_call(
        _ada_kernel,
        grid=(L, E // tn),
        in_specs=[pl.BlockSpec((n, D), lambda l, j: (0, 0)),
                  pl.BlockSpec((1, D, tn), lambda l, j: (l, 0, j)),
                  pl.BlockSpec((1, 1, tn), lambda l, j: (l, 0, j))],
        out_specs=pl.BlockSpec((1, n, tn), lambda l, j: (l, 0, j)),
        out_shape=jax.ShapeDtypeStruct((L, n, E), F32),
        compiler_params=_cparams(2),
        name="ada_mod",
    )(c_all, w_ada, b_ada.reshape(L, 1, E))


def _mod_spec(sm, tm, d):
    if sm == 1:
        return pl.BlockSpec((1, 1, d), lambda g, t: (g, 0, 0))
    return pl.BlockSpec((1, tm, d), lambda g, t: (g, t, 0))


def _const_spec(shape):
    nd = len(shape)
    return pl.BlockSpec(shape, lambda *a: (0,) * nd, pipeline_mode=pl.Buffered(1))


def _inproj_kernel(x_ref, g_ref, sc_ref, sh_ref, w_ref, *o_refs, segs):
    h = _rms_mod(x_ref[0], g_ref[...], sc_ref[0], sh_ref[0]).astype(BF16)
    for (off, wd), o_ref in zip(segs, o_refs):
        o_ref[0] = jnp.dot(h, w_ref[:, off:off + wd], preferred_element_type=F32)


def _inproj(x, g, sc, sh, w, widths, tm):
    G, Tg, D = x.shape
    segs, off = [], 0
    for wd in widths:
        segs.append((off, wd))
        off += wd
    sm = sc.shape[1]
    return pl.pallas_call(
        functools.partial(_inproj_kernel, segs=tuple(segs)),
        grid=(G, Tg // tm),
        in_specs=[pl.BlockSpec((1, tm, D), lambda g_, t: (g_, t, 0)),
                  _const_spec((1, D)),
                  _mod_spec(sm, tm, D), _mod_spec(sm, tm, D),
                  _const_spec(w.shape)],
        out_specs=[pl.BlockSpec((1, tm, wd), lambda g_, t: (g_, t, 0)) for wd in widths],
        out_shape=[jax.ShapeDtypeStruct((G, Tg, wd), F32) for wd in widths],
        compiler_params=_cparams(2),
        name="in_proj",
    )(x, g, sc, sh, w)


def _compress_bias(w1_ref, pe_ref, bias_sc):
    for c in range(2):
        halves = []
        for h in range(2):
            pe_rows = jnp.broadcast_to(pe_ref[c, h:h + 1, :], (SUBLANES, pe_ref.shape[2]))
            halves.append(_bdot(pe_rows, w1_ref[c, :, h * 2 * LANES:(h + 1) * 2 * LANES]))
        bias_sc[c] = jnp.concatenate(halves, axis=1)


def _compress_x(xrefs, nch, w1_ref, bias_sc, b1_ref, w2_ref):
    last = lax.broadcasted_iota(jnp.int32, (nch, 1), 0) == nch - 1
    outs = []
    for c in range(2):
        lhs = jnp.concatenate([xrefs[c][pl.ds(p, nch, stride=CMP_STRIDE), :].astype(BF16)
                               for p in range(CMP_STRIDE)], axis=1)
        acc = jnp.dot(lhs, w1_ref[c], preferred_element_type=F32) + bias_sc[c, 0:1, :]
        lo = acc[:, :2 * LANES]
        hi = acc[:, 2 * LANES:]
        hi_next = jnp.where(last, 0.0, pltpu.roll(hi, nch - 1, 0))
        hid = _gelu(lo + hi_next + b1_ref[c])
        outs.append(_bdot(hid, w2_ref[c]))
    return outs


def _compress_prompt_kernel(krows_ref, vrows_ref, w1_ref, pe_ref, b1_ref, w2_ref, kc_ref, vc_ref, bias_sc, *, nch):
    @pl.when(pl.program_id(0) == 0)
    def _():
        _compress_bias(w1_ref, pe_ref, bias_sc)

    kc, vc = _compress_x((krows_ref.at[0], vrows_ref.at[0]), nch, w1_ref, bias_sc, b1_ref, w2_ref)
    kc_ref[0] = kc
    vc_ref[0] = vc


def _compress_prompt(rows, cw):
    B, T, _ = rows.shape
    nch = T // CMP_STRIDE
    w1, pe, b1, w2 = cw
    return pl.pallas_call(
        functools.partial(_compress_prompt_kernel, nch=nch),
        grid=(B,),
        in_specs=[pl.BlockSpec((1, T, LANES), lambda b: (b, 0, 0)),
                  pl.BlockSpec((1, T, LANES), lambda b: (b, 0, 1)),
                  _const_spec(w1.shape), _const_spec(pe.shape), _const_spec(b1.shape), _const_spec(w2.shape)],
        out_specs=[pl.BlockSpec((1, nch, LANES), lambda b: (b, 0, 0))] * 2,
        out_shape=[jax.ShapeDtypeStruct((B, nch, LANES), F32)] * 2,
        scratch_shapes=[pltpu.VMEM((2, SUBLANES, 4 * LANES), F32)],
        compiler_params=_cparams(1),
        name="nsa_compress_prompt",
    )(rows, rows, w1, pe, b1, w2)


def _rep_all(a):
    return jnp.concatenate([a] * (NSA_KV * NSA_HPG), axis=0)


def _rep_heads(a, Qb):
    return jnp.concatenate([a[:Qb]] * NSA_HPG + [a[Qb:]] * NSA_HPG, axis=0)


def _nsa_front(qblk, q0, Qb, kc, vc, n_cmp, n_sel):
    R2 = 2 * Qb
    npad = kc.shape[0]
    nspad = -(-n_sel // LANES) * LANES
    lane = lax.broadcasted_iota(jnp.int32, (Qb, LANES), 1)
    lo_half = lane < NSA_HD
    scale = NSA_HD ** -0.5 * LOG2E
    pieces = []
    for k in range(NSA_KV):
        for r in range(NSA_HPG):
            sl = qblk[:, r * LANES:(r + 1) * LANES] * scale
            pieces.append(jnp.where(lo_half if k == 0 else jnp.logical_not(lo_half), sl, 0.0))
    qs = jnp.concatenate(pieces, axis=0).astype(BF16)
    qp1 = q0 + lax.broadcasted_iota(jnp.int32, (Qb, 1), 0)

    n_idx = lax.broadcasted_iota(jnp.int32, (1, npad), 1)
    visible = (n_idx * CMP_STRIDE + (CMP_BLOCK - 1) <= qp1) & (n_idx < n_cmp)
    s = _bdot_nt(qs, kc) + _rep_all(jnp.where(visible, 0.0, -BIG))
    e = jnp.exp2(s - jnp.max(s, axis=-1, keepdims=True))
    any_visible = _rep_all((qp1 >= CMP_BLOCK - 1) & (n_cmp > 0))
    p_c = e * jnp.where(any_visible, 1.0 / jnp.sum(e, axis=-1, keepdims=True), 0.0)
    o_c = _bdot(p_c, vc)

    psum = []
    for k in range(NSA_KV):
        acc = p_c[(k * NSA_HPG) * Qb:(k * NSA_HPG + 1) * Qb]
        for r in range(1, NSA_HPG):
            acc = acc + p_c[(k * NSA_HPG + r) * Qb:(k * NSA_HPG + r + 1) * Qb]
        psum.append(acc)
    psum = jnp.concatenate(psum, axis=0)
    ci = lax.broadcasted_iota(jnp.int32, (npad, nspad), 0) * CMP_STRIDE
    sj = lax.broadcasted_iota(jnp.int32, (npad, nspad), 1) * SEL_BLOCK
    selmap = jnp.where((ci < sj + SEL_BLOCK) & (ci + CMP_BLOCK > sj), 1.0, 0.0).astype(BF16)
    ph, plo = _split(psum)
    imp = (jnp.dot(ph, selmap, preferred_element_type=F32)
           + jnp.dot(plo, selmap, preferred_element_type=F32))
    j = lax.broadcasted_iota(jnp.int32, (R2, nspad), 1)
    jf = j.astype(F32)
    qpos2 = jnp.concatenate([qp1] * NSA_KV, axis=0)
    cur = qpos2 >> 6
    forced = (j == 0) | (j == cur) | (j == cur - 1)
    imp = jnp.where(forced, SEL_FORCE, imp)
    imp = jnp.where(j * SEL_BLOCK <= qpos2, imp, -SEL_FORCE)
    imp = jnp.where(j < n_sel, imp, NEG)

    def pick(_, carry):
        imp_c, sel_c = carry
        m = jnp.max(imp_c, axis=-1, keepdims=True)
        first = jnp.min(jnp.where(imp_c == m, jf, float(nspad)), axis=-1, keepdims=True)
        hit = jf == first
        return jnp.where(hit, NEG, imp_c), jnp.where(hit, 1.0, sel_c)

    _, sel = lax.fori_loop(0, min(SEL_TOPK, n_sel), pick, (imp, jnp.zeros((R2, nspad), F32)))
    return qs, qp1, o_c, sel


def _nsa_combine(gate, o_c, o_s, o_w, Qb, o_ref):
    lo_half = lax.broadcasted_iota(jnp.int32, (Qb, LANES), 1) < NSA_HD
    g = _sigmoid(gate)
    for r in range(NSA_HPG):
        halves = []
        for k in range(NSA_KV):
            rs = slice((k * NSA_HPG + r) * Qb, (k * NSA_HPG + r + 1) * Qb)
            c = (k * NSA_HPG + r) * 3
            halves.append(g[:, c:c + 1] * o_c[rs] + g[:, c + 1:c + 2] * o_s[rs] + g[:, c + 2:c + 3] * o_w[rs])
        o_ref[0, :, r * LANES:(r + 1) * LANES] = jnp.where(lo_half, halves[0], halves[1])


def _block_columns(k0, n):
    key = k0 + lax.broadcasted_iota(jnp.int32, (n, LANES), 0)
    blk = lax.broadcasted_iota(jnp.int32, (n, LANES), 1)
    return jnp.where((key >> 6) == blk, BIG, 0.0).astype(BF16)


def _block_rows(k0, n):
    key = k0 + lax.broadcasted_iota(jnp.int32, (LANES, n), 1)
    blk = lax.broadcasted_iota(jnp.int32, (LANES, n), 0)
    return jnp.where((key >> 6) == blk, BIG, 0.0).astype(BF16)


def _lanes_all(a):
    return jnp.concatenate([a] * (NSA_KV * NSA_HPG), axis=1)


def _lanes_heads(a, Qb):
    return jnp.concatenate([a[:, :Qb]] * NSA_HPG + [a[:, Qb:]] * NSA_HPG, axis=1)


def _softmax_cols(s):
    e = jnp.exp2(s - jnp.max(s, axis=0, keepdims=True))
    return e, 1.0 / jnp.sum(e, axis=0, keepdims=True)


def _nsa_prompt_kernel(q_ref, gate_ref, kc_ref, vc_ref, rows_ref, win_ref, o_ref, kaug, vt,
                       *, T, Qb, n_cmp, n_sel, wl):
    i = pl.program_id(1)
    q0 = i * Qb
    R = NSA_KV * NSA_HPG * Qb
    R2 = NSA_KV * Qb

    @pl.when(i == 0)
    def _():
        def pack(c, carry):
            r0 = pl.multiple_of(c * SEL_CHUNK, SEL_CHUNK)
            kaug[pl.ds(r0, SEL_CHUNK), 0:LANES] = rows_ref[0, pl.ds(r0, SEL_CHUNK), 2 * LANES:3 * LANES].astype(BF16)
            kaug[pl.ds(r0, SEL_CHUNK), LANES:2 * LANES] = _block_columns(r0, SEL_CHUNK)
            vt[:, pl.ds(r0, SEL_CHUNK)] = rows_ref[0, pl.ds(r0, SEL_CHUNK), 3 * LANES:4 * LANES].T.astype(BF16)
            return carry
        lax.fori_loop(0, T // SEL_CHUNK, pack, 0)

    top = lax.broadcasted_iota(jnp.int32, (LANES, Qb), 0) < NSA_HD
    scale = NSA_HD ** -0.5 * LOG2E
    qblk = q_ref[0]
    q_t = [(qblk[:, r * LANES:(r + 1) * LANES] * scale).T for r in range(NSA_HPG)]
    qs = jnp.concatenate([jnp.where(top if k == 0 else jnp.logical_not(top), q_t[r], 0.0)
                          for k in range(NSA_KV) for r in range(NSA_HPG)], axis=1).astype(BF16)
    qlane = q0 + lax.broadcasted_iota(jnp.int32, (1, Qb), 1)
    qpos = _lanes_all(qlane)

    kc = kc_ref[0]
    npad = kc.shape[0]
    n_idx = lax.broadcasted_iota(jnp.int32, (npad, 1), 0)
    visible = (n_idx * CMP_STRIDE + (CMP_BLOCK - 1) <= qlane) & (n_idx < n_cmp)
    e_c, inv_c = _softmax_cols(_bdot(kc, qs) + _lanes_all(jnp.where(visible, 0.0, -BIG)))
    any_visible = _lanes_all((qlane >= CMP_BLOCK - 1) & (n_cmp > 0))
    p_c = e_c * jnp.where(any_visible, inv_c, 0.0)
    o_c = _bdot(vc_ref[0].T, p_c)

    psum = []
    for k in range(NSA_KV):
        acc = p_c[:, (k * NSA_HPG) * Qb:(k * NSA_HPG + 1) * Qb]
        for r in range(1, NSA_HPG):
            acc = acc + p_c[:, (k * NSA_HPG + r) * Qb:(k * NSA_HPG + r + 1) * Qb]
        psum.append(acc)
    psum = jnp.concatenate(psum, axis=1)
    nsr = -(-n_sel // SUBLANES) * SUBLANES
    sj = lax.broadcasted_iota(jnp.int32, (nsr, npad), 0) * SEL_BLOCK
    ci = lax.broadcasted_iota(jnp.int32, (nsr, npad), 1) * CMP_STRIDE
    selmap = jnp.where((ci < sj + SEL_BLOCK) & (ci + CMP_BLOCK > sj), 1.0, 0.0).astype(BF16)
    ph, plo = _split(psum)
    imp = (jnp.dot(selmap, ph, preferred_element_type=F32)
           + jnp.dot(selmap, plo, preferred_element_type=F32))
    j = lax.broadcasted_iota(jnp.int32, (nsr, R2), 0)
    jf = j.astype(F32)
    qpos2 = jnp.concatenate([qlane] * NSA_KV, axis=1)
    cur = qpos2 >> 6
    forced = (j == 0) | (j == cur) | (j == cur - 1)
    imp = jnp.where(forced, SEL_FORCE, imp)
    imp = jnp.where(j * SEL_BLOCK <= qpos2, imp, -SEL_FORCE)
    imp = jnp.where(j < n_sel, imp, NEG)

    def pick(_, carry):
        imp_c, sel_c = carry
        m = jnp.max(imp_c, axis=0, keepdims=True)
        first = jnp.min(jnp.where(imp_c == m, jf, float(nsr)), axis=0, keepdims=True)
        hit = jf == first
        return jnp.where(hit, NEG, imp_c), jnp.where(hit, 1.0, sel_c)

    _, sel = lax.fori_loop(0, min(SEL_TOPK, n_sel), pick, (imp, jnp.zeros((nsr, R2), F32)))
    selm = jnp.concatenate([sel - 1.0, jnp.zeros((LANES - nsr, R2), F32)], axis=0) if nsr < LANES else sel - 1.0
    qaug = jnp.concatenate([qs, _lanes_heads(selm, Qb).astype(BF16)], axis=0)

    def update(s, vcols, carry):
        m, l, acc = carry
        m_new = jnp.maximum(m, jnp.max(s, axis=0, keepdims=True))
        alpha = jnp.exp2(m - m_new)
        p = jnp.exp2(s - m_new)
        l = alpha * l + jnp.sum(p, axis=0, keepdims=True)
        acc = alpha * acc + jnp.dot(vcols, p.astype(BF16), preferred_element_type=F32)
        return m_new, l, acc

    def scores(k0):
        return jnp.dot(kaug[pl.ds(k0, SEL_CHUNK), :], qaug, preferred_element_type=F32)

    def full_chunk(c, carry):
        k0 = pl.multiple_of(c * SEL_CHUNK, SEL_CHUNK)
        return update(scores(k0), vt[:, pl.ds(k0, SEL_CHUNK)], carry)

    n_full = q0 // SEL_CHUNK
    init = (jnp.full((1, R), -4.0 * BIG, F32), jnp.zeros((1, R), F32), jnp.zeros((LANES, R), F32))
    carry = lax.fori_loop(0, n_full, full_chunk, init)
    k0 = pl.multiple_of(n_full * SEL_CHUNK, SEL_CHUNK)
    kpos = k0 + lax.broadcasted_iota(jnp.int32, (SEL_CHUNK, 1), 0)
    s_diag = jnp.where(kpos <= qpos, scores(k0), -2.0 * BIG)
    _, l_s, acc_s = update(s_diag, vt[:, pl.ds(k0, SEL_CHUNK)], carry)
    o_s = acc_s * (1.0 / l_s)

    ws = pl.multiple_of(jnp.maximum(q0 - WINDOW, 0), LANES)
    wk = win_ref[0, pl.ds(ws, wl), 0:LANES]
    wv = win_ref[0, pl.ds(ws, wl), LANES:2 * LANES]
    dpos = qlane - (ws + lax.broadcasted_iota(jnp.int32, (wl, 1), 0))
    e_w, inv_w = _softmax_cols(_bdot(wk, qs) + _lanes_all(jnp.where((dpos >= 0) & (dpos <= WINDOW), 0.0, -BIG)))
    o_w = _bdot(wv.T, e_w) * inv_w

    g = _sigmoid(gate_ref[0]).T
    for r in range(NSA_HPG):
        halves = []
        for k in range(NSA_KV):
            cs = slice((k * NSA_HPG + r) * Qb, (k * NSA_HPG + r + 1) * Qb)
            c = (k * NSA_HPG + r) * 3
            halves.append(g[c:c + 1, :] * o_c[:, cs] + g[c + 1:c + 2, :] * o_s[:, cs] + g[c + 2:c + 3, :] * o_w[:, cs])
        o_ref[0, :, r * LANES:(r + 1) * LANES] = jnp.where(top, halves[0], halves[1]).T


def _nsa_prompt(nq, gate, kc, vc, rows, win):
    B, T, HD = nq.shape
    Qb = QBLOCK
    nch = kc.shape[1]
    n_sel = -(-T // SEL_BLOCK)
    wl = WINDOW + Qb
    assert T % SEL_CHUNK == 0 and T >= wl and SEL_TOPK <= n_sel <= LANES
    return pl.pallas_call(
        functools.partial(_nsa_prompt_kernel, T=T, Qb=Qb, n_cmp=nch - 1, n_sel=n_sel, wl=wl),
        grid=(B, T // Qb),
        in_specs=[pl.BlockSpec((1, Qb, HD), lambda b, i: (b, i, 0)),
                  pl.BlockSpec((1, Qb, LANES), lambda b, i: (b, i, 0)),
                  pl.BlockSpec((1, nch, LANES), lambda b, i: (b, 0, 0)),
                  pl.BlockSpec((1, nch, LANES), lambda b, i: (b, 0, 0)),
                  pl.BlockSpec((1, T, 4 * LANES), lambda b, i: (b, 0, 0)),
                  pl.BlockSpec((1, T, 2 * LANES), lambda b, i: (b, 0, 0))],
        out_specs=pl.BlockSpec((1, Qb, HD), lambda b, i: (b, i, 0)),
        out_shape=jax.ShapeDtypeStruct((B, T, HD), F32),
        scratch_shapes=[pltpu.VMEM((T, 2 * LANES), BF16), pltpu.VMEM((LANES, T), BF16)],
        compiler_params=_cparams(2),
        name="nsa_attn_prompt",
    )(nq, gate, kc, vc, rows, win)


def _pages_copy(cache_hbm, layer, page, r0, dst, j, sem):
    n = cache_hbm.shape[-1]
    return pltpu.make_async_copy(cache_hbm.at[layer, page, pl.ds(r0, 2)],
                                 dst.at[:, :, pl.ds(pl.multiple_of(j * n, n), n)], sem)


def _pages_start(pt_ref, b, cache_hbm, layer, r0, dst, sem, npages):
    def issue(j, carry):
        _pages_copy(cache_hbm, layer, pt_ref[b, j], r0, dst, j, sem).start()
        return carry
    lax.fori_loop(0, npages, issue, 0, unroll=PAGE_UNROLL)


def _pages_wait(cache_hbm, layer, r0, dst, sem, npages):
    def wait(j, carry):
        _pages_copy(cache_hbm, layer, 0, r0, dst, j, sem).wait()
        return carry
    lax.fori_loop(0, npages, wait, 0, unroll=PAGE_UNROLL)


def _softmax2(s1, s2, mask2):
    s2 = jnp.where(mask2, s2, -2.0 * BIG)
    m = jnp.maximum(jnp.max(s1, axis=-1, keepdims=True), jnp.max(s2, axis=-1, keepdims=True))
    e1 = jnp.exp2(s1 - m)
    e2 = jnp.exp2(s2 - m)
    den = jnp.sum(e1, axis=-1, keepdims=True) + jnp.sum(e2, axis=-1, keepdims=True)
    return e1, e2, 1.0 / den


def _nsa_sample_kernel(pt_ref, q_ref, gate_ref, rows_ref, wt_ref, wnew_ref, cache_hbm,
                       w1_ref, pe_ref, b1_ref, w2_ref, o_ref,
                       cmpbuf, selbuf, xk, xv, kaug, vt, newbuf, wnewbuf, bias_sc, csem, ssem,
                       *, layer, npages, P, Q, wb, nb, n_sel):
    b = pl.program_id(0)
    nseq = pl.num_programs(0)
    page = cache_hbm.shape[-1]
    nch = P // CMP_STRIDE

    @pl.when(b == 0)
    def _():
        _pages_start(pt_ref, 0, cache_hbm, layer, 0, cmpbuf, csem, npages)
        _pages_start(pt_ref, 0, cache_hbm, layer, 2, selbuf, ssem, npages)
        _compress_bias(w1_ref, pe_ref, bias_sc)

        def blocks(c, carry):
            c0 = pl.multiple_of(c * CAST_CHUNK, CAST_CHUNK)
            kaug[LANES:2 * LANES, pl.ds(c0, CAST_CHUNK)] = _block_rows(c0, CAST_CHUNK)
            return carry
        lax.fori_loop(0, P // CAST_CHUNK, blocks, 0)
        newbuf[...] = jnp.zeros(newbuf.shape, F32)
        wnewbuf[...] = jnp.zeros(wnewbuf.shape, F32)

    _pages_wait(cache_hbm, layer, 0, cmpbuf, csem, npages)

    def to_rows(j, carry):
        c0 = pl.multiple_of(j * page, page)
        xk[pl.ds(c0, page), :] = cmpbuf[0, :, pl.ds(c0, page)].T
        xv[pl.ds(c0, page), :] = cmpbuf[1, :, pl.ds(c0, page)].T
        return carry
    lax.fori_loop(0, npages, to_rows, 0, unroll=PAGE_UNROLL)

    @pl.when(b + 1 < nseq)
    def _():
        _pages_start(pt_ref, b + 1, cache_hbm, layer, 0, cmpbuf, csem, npages)

    kc, vc = _compress_x((xk, xv), nch, w1_ref, bias_sc, b1_ref, w2_ref)

    _pages_wait(cache_hbm, layer, 2, selbuf, ssem, npages)

    def pack(c, carry):
        c0 = pl.multiple_of(c * CAST_CHUNK, CAST_CHUNK)
        kaug[0:LANES, pl.ds(c0, CAST_CHUNK)] = selbuf[0, :, pl.ds(c0, CAST_CHUNK)].astype(BF16)
        vt[:, pl.ds(c0, CAST_CHUNK)] = selbuf[1, :, pl.ds(c0, CAST_CHUNK)].astype(BF16)
        return carry
    lax.fori_loop(0, P // CAST_CHUNK, pack, 0)

    @pl.when(b + 1 < nseq)
    def _():
        _pages_start(pt_ref, b + 1, cache_hbm, layer, 2, selbuf, ssem, npages)

    newbuf[0:Q, :] = rows_ref[0, :, 2 * LANES:4 * LANES]
    wnewbuf[0:Q, :] = wnew_ref[0]

    qs, qp1, o_c, sel = _nsa_front(q_ref[0], P, Q, kc, vc, nch - 1, n_sel)
    qpos = _rep_all(qp1)
    lane = lax.broadcasted_iota(jnp.int32, (1, LANES), 1)
    new_pos = P + lane
    is_new = lane < Q

    qaug = jnp.concatenate([qs, _rep_heads(sel[:, 0:LANES] - 1.0, Q).astype(BF16)], axis=1)
    s_past = jnp.dot(qaug, kaug[...], preferred_element_type=F32)
    s_new = _bdot_nt(qs, newbuf[:, 0:LANES])
    new_ok = is_new & (new_pos <= qpos) & (_rep_heads(sel[:, nb:nb + 1], Q) > 0.5)
    e1, e2, inv = _softmax2(s_past, s_new, new_ok)
    o_s = (lax.dot_general(e1.astype(BF16), vt[...], (((1,), (1,)), ((), ())), preferred_element_type=F32)
           + _bdot(e2, newbuf[:, LANES:2 * LANES])) * inv

    dpast = qp1 - ((P - wb) + lax.broadcasted_iota(jnp.int32, (1, wb), 1))
    s_wp = (jnp.dot(qs, wt_ref[0, 0].astype(BF16), preferred_element_type=F32)
            + _rep_all(jnp.where((dpast >= 0) & (dpast <= WINDOW), 0.0, -BIG)))
    s_wn = _bdot_nt(qs, wnewbuf[:, 0:LANES])
    dnew = qpos - new_pos
    e1, e2, inv = _softmax2(s_wp, s_wn, is_new & (dnew >= 0) & (dnew <= WINDOW))
    o_w = (lax.dot_general(e1.astype(BF16), wt_ref[0, 1].astype(BF16), (((1,), (1,)), ((), ())),
                           preferred_element_type=F32)
           + _bdot(e2, wnewbuf[:, LANES:2 * LANES])) * inv

    _nsa_combine(gate_ref[0], o_c, o_s, o_w, Q, o_ref)


def _nsa_sample(page_table, nq, gate, rows, win_t, win_new, cache_t, layer, cw):
    DB, Q, HD = nq.shape
    npages = page_table.shape[1]
    page = cache_t.shape[-1]
    P = npages * page
    wb = win_t.shape[-1]
    nb = P // SEL_BLOCK
    n_sel = -(-(P + Q) // SEL_BLOCK)
    w1, pe, b1, w2 = cw
    assert Q == SUBLANES and P % SEL_BLOCK == 0 and Q <= SEL_BLOCK and nb <= LANES and n_sel >= SEL_TOPK
    assert P % CAST_CHUNK == 0 and (P + Q) // CMP_STRIDE == P // CMP_STRIDE and page == LANES
    bs = lambda shape: pl.BlockSpec((1,) + shape, lambda b, pt: (b,) + (0,) * len(shape))
    cs = lambda shape: pl.BlockSpec(shape, lambda b, pt: (0,) * len(shape), pipeline_mode=pl.Buffered(1))
    grid_spec = pltpu.PrefetchScalarGridSpec(
        num_scalar_prefetch=1, grid=(DB,),
        in_specs=[bs((Q, HD)), bs((Q, LANES)), bs((Q, 4 * LANES)), bs((2, LANES, wb)), bs((Q, 2 * LANES)),
                  pl.BlockSpec(memory_space=pl.ANY),
                  cs(w1.shape), cs(pe.shape), cs(b1.shape), cs(w2.shape)],
        out_specs=bs((Q, HD)),
        scratch_shapes=[pltpu.VMEM((2, LANES, P), F32), pltpu.VMEM((2, LANES, P), F32),
                        pltpu.VMEM((P, LANES), F32), pltpu.VMEM((P, LANES), F32),
                        pltpu.VMEM((2 * LANES, P), BF16), pltpu.VMEM((LANES, P), BF16),
                        pltpu.VMEM((LANES, 2 * LANES), F32), pltpu.VMEM((LANES, 2 * LANES), F32),
                        pltpu.VMEM((2, SUBLANES, 4 * LANES), F32),
                        pltpu.SemaphoreType.DMA(()), pltpu.SemaphoreType.DMA(())])
    return pl.pallas_call(
        functools.partial(_nsa_sample_kernel, layer=layer, npages=npages, P=P, Q=Q, wb=wb, nb=nb, n_sel=n_sel),
        grid_spec=grid_spec,
        out_shape=jax.ShapeDtypeStruct((DB, Q, HD), F32),
        compiler_params=_cparams(1),
        name="nsa_sample",
    )(page_table, nq, gate, rows, win_t, win_new, cache_t, w1, pe, b1, w2)


def _ret_kernel(q_ref, k_ref, v_ref, g_ref, cos_ref, sin_ref, s0_ref, gn_ref, o_ref, snew_ref, s_sc, *, C, nC):
    c = pl.program_id(1)

    @pl.when(c == 0)
    def _():
        s_sc[...] = s0_ref[0]

    cosf = cos_ref[...]
    sinf = sin_ref[...]
    diff = (lax.broadcasted_iota(jnp.int32, (C, C), 0) - lax.broadcasted_iota(jnp.int32, (C, C), 1)).astype(F32)
    ii = lax.broadcasted_iota(jnp.int32, (C, 1), 0).astype(F32)
    half = RET_DK // 2
    for h in range(RET_HEADS):
        lg = math.log(1.0 - 2.0 ** (-5.0 - h))
        hs = slice(h * RET_DK, (h + 1) * RET_DK)
        q = q_ref[0, :, hs]
        k = k_ref[0, :, hs]
        v = v_ref[0, :, hs]
        qr = q * cosf + pltpu.roll(q, half, 1) * sinf
        kr = (k * cosf + pltpu.roll(k, half, 1) * sinf) * (RET_DK ** -0.5)
        decay = jnp.where(diff >= 0, jnp.exp(jnp.maximum(diff, 0.0) * lg), 0.0)
        o_inner = _bdot(_bdot_nt(qr, kr) * decay, v)
        s_old = s_sc[h]
        o_cross = _bdot(qr, s_old) * jnp.exp((ii + 1.0) * lg)
        kw = kr * jnp.exp((C - 1.0 - ii) * lg)
        kv = lax.dot_general(kw.astype(BF16), v.astype(BF16), (((0,), (0,)), ((), ())),
                             preferred_element_type=F32)
        s_sc[h] = math.exp(C * lg) * s_old + kv
        o = o_inner + o_cross
        mu = jnp.mean(o, axis=-1, keepdims=True)
        var = jnp.mean(jnp.square(o - mu), axis=-1, keepdims=True)
        gate = g_ref[0, :, hs]
        o_ref[0, :, hs] = ((o - mu) * lax.rsqrt(var + EPS)) * gn_ref[:, hs] * (gate * _sigmoid(gate))

    @pl.when(c == nC - 1)
    def _():
        snew_ref[0] = s_sc[...]


def _retention(rq, rk, rv, rg, cosf, sinf, s0, gn):
    B, T, W = rq.shape
    C = RET_CHUNK if (T >= RET_CHUNK and T % RET_CHUNK == 0) else T
    nC = T // C
    tok = pl.BlockSpec((1, C, W), lambda b, c: (b, c, 0))
    tab = pl.BlockSpec((C, RET_DK), lambda b, c: (c, 0))
    st = pl.BlockSpec((1,) + s0.shape[1:], lambda b, c: (b, 0, 0, 0))
    return pl.pallas_call(
        functools.partial(_ret_kernel, C=C, nC=nC),
        grid=(B, nC),
        in_specs=[tok, tok, tok, tok, tab, tab, st, _const_spec((1, W))],
        out_specs=[tok, st],
        out_shape=[jax.ShapeDtypeStruct((B, T, W), F32), jax.ShapeDtypeStruct(s0.shape, F32)],
        scratch_shapes=[pltpu.VMEM(s0.shape[1:], F32)],
        compiler_params=_cparams(2),
        name="retention",
    )(rq, rk, rv, rg, cosf, sinf, s0, gn)


def _shift_carry(x, k, tail8):
    r = pltpu.roll(x, k, 0)
    row8 = lax.broadcasted_iota(jnp.int32, (SUBLANES, 1), 0)
    first = jnp.where(row8 >= k, r[:SUBLANES], pltpu.roll(tail8, k, 0))
    return jnp.concatenate([first, r[SUBLANES:]], axis=0)


def _shift_seg(x, k, fill, tpos):
    return jnp.where(tpos >= k, pltpu.roll(x, k, 0), fill)


def _rglru_kernel(x_ref, gate_ref, st_ref, h0_ref, cw_ref, cb_ref, wa_ref, ba_ref, wx_ref, bx_ref, lam_ref,
                  o_ref, h_ref, tail_sc, h_sc, *, tm, seg):
    carry = seg == 0
    x = x_ref[0]
    rows = lax.broadcasted_iota(jnp.int32, (tm, 1), 0)
    if carry:
        @pl.when(pl.program_id(1) == 0)
        def _():
            tail_sc[...] = st_ref[0]
            h_sc[...] = h0_ref[0]
        tail8 = tail_sc[...]
        shifted = [_shift_carry(x, k, tail8) for k in range(1, RG_CONV)]
        tpos = rows
        seglen = tm
        h_in = jnp.where(rows == 0, h_sc[SUBLANES - 1:SUBLANES, :], 0.0)
    else:
        tpos = rows & (seg - 1)
        shifted = [_shift_seg(x, k, st_ref[k - 1], tpos) for k in range(1, RG_CONV)]
        seglen = seg
        h_in = h0_ref[0]
    xc = cb_ref[...] + cw_ref[RG_CONV - 1:RG_CONV, :] * x
    for k in range(1, RG_CONV):
        xc = xc + cw_ref[RG_CONV - 1 - k:RG_CONV - k, :] * shifted[k - 1]
    r = _sigmoid(_bdot(xc, wa_ref[...]) + ba_ref[...])
    i = _sigmoid(_bdot(xc, wx_ref[...]) + bx_ref[...])
    lam = lam_ref[...]
    softplus = jnp.maximum(-lam, 0.0) + jnp.log(1.0 + jnp.exp(-jnp.abs(lam)))
    log_a = (-RG_C * r) * softplus
    a = jnp.exp(log_a)
    bt = jnp.sqrt(1.0 - a * a) * (i * xc)
    bt = bt + a * h_in
    k = 1
    while k < seglen:
        ok = tpos >= k
        a_prev = jnp.where(ok, pltpu.roll(a, k, 0), 1.0)
        b_prev = jnp.where(ok, pltpu.roll(bt, k, 0), 0.0)
        bt = a * b_prev + bt
        a = a * a_prev
        k *= 2
    o_ref[0] = bt * _gelu(gate_ref[0])
    if carry:
        tail_sc[...] = x[tm - SUBLANES:]
        h_sc[...] = bt[tm - SUBLANES:]
        h_ref[0] = bt[tm - SUBLANES:]
    else:
        h_ref[0] = bt


def _rglru(rx, rgate, st, h0, rw, tm, seg):
    G, Tg, W = rx.shape
    tok = pl.BlockSpec((1, tm, W), lambda g, t: (g, t, 0))
    if seg == 0:
        st_spec = pl.BlockSpec((1, SUBLANES, W), lambda g, t: (g, 0, 0))
        h0_spec = pl.BlockSpec((1, SUBLANES, W), lambda g, t: (g, 0, 0))
        h_spec = pl.BlockSpec((1, SUBLANES, W), lambda g, t: (g, 0, 0))
        h_shape = (G, SUBLANES, W)
    else:
        st_spec = pl.BlockSpec(st.shape, lambda g, t: (0, 0, 0))
        h0_spec = tok
        h_spec = tok
        h_shape = (G, Tg, W)
    return pl.pallas_call(
        functools.partial(_rglru_kernel, tm=tm, seg=seg),
        grid=(G, Tg // tm),
        in_specs=[tok, tok, st_spec, h0_spec] + [_const_spec(a.shape) for a in rw],
        out_specs=[tok, h_spec],
        out_shape=[jax.ShapeDtypeStruct((G, Tg, W), F32), jax.ShapeDtypeStruct(h_shape, F32)],
        scratch_shapes=[pltpu.VMEM((SUBLANES, W), F32), pltpu.VMEM((SUBLANES, W), F32)],
        compiler_params=_cparams(2),
        name="rglru",
    )(rx, rgate, st, h0, *rw)


def _merge_kernel(x_ref, oa_ref, or_ref, oc_ref, mg_ref, gt_ref, wa_ref, wb_ref, wc_ref, wo_ref, y_ref, *, D):
    pa = _bdot(oa_ref[0], wa_ref[...])
    pb = _bdot(or_ref[0], wb_ref[...])
    pc = _bdot(oc_ref[0], wc_ref[...])
    merged = (_sigmoid(mg_ref[0, :, 0:D]) * pa + _sigmoid(mg_ref[0, :, D:2 * D]) * pb
              + _sigmoid(mg_ref[0, :, 2 * D:3 * D]) * pc)
    y_ref[0] = x_ref[0] + gt_ref[0] * _bdot(merged, wo_ref[...])


def _merge(x, oa, orr, oc, mg, gt, wa, wb, wc, wo, tm):
    G, Tg, D = x.shape
    tok = lambda w: pl.BlockSpec((1, tm, w), lambda g, t: (g, t, 0))
    return pl.pallas_call(
        functools.partial(_merge_kernel, D=D),
        grid=(G, Tg // tm),
        in_specs=[tok(D), tok(oa.shape[2]), tok(orr.shape[2]), tok(oc.shape[2]), tok(3 * D),
                  _mod_spec(gt.shape[1], tm, D),
                  _const_spec(wa.shape), _const_spec(wb.shape), _const_spec(wc.shape), _const_spec(wo.shape)],
        out_specs=tok(D),
        out_shape=jax.ShapeDtypeStruct((G, Tg, D), F32),
        compiler_params=_cparams(2),
        name="merge_out",
    )(x, oa, orr, oc, mg, gt, wa, wb, wc, wo)


def _ffn_kernel(x_ref, g_ref, sc_ref, sh_ref, gt_ref, st_ref, wup_ref, cw_ref, cb_ref, wdn_ref, fg_ref,
                y_ref, fnew_ref, tail_sc, *, tm, seg, F, wck, final):
    carry = seg == 0
    x = x_ref[0]
    h = _rms_mod(x, g_ref[...], sc_ref[0], sh_ref[0]).astype(BF16)
    rows = lax.broadcasted_iota(jnp.int32, (tm, 1), 0)
    if carry:
        @pl.when(pl.program_id(1) == 0)
        def _():
            tail_sc[...] = st_ref[0]
    else:
        tpos = rows & (seg - 1)
    acc = jnp.zeros(x.shape, F32)
    for c0 in range(0, F, wck):
        cs = slice(c0, c0 + wck)
        gp = jnp.dot(h, wup_ref[:, c0:c0 + wck], preferred_element_type=F32)
        val = jnp.dot(h, wup_ref[:, F + c0:F + c0 + wck], preferred_element_type=F32)
        if carry:
            tail8 = tail_sc[:, cs]
            shifted = [_shift_carry(gp, k, tail8) for k in range(1, FFN_CONV)]
            tail_sc[:, cs] = gp[tm - SUBLANES:]
            fnew_ref[0, :, cs] = gp[tm - SUBLANES:]
        else:
            shifted = [_shift_seg(gp, k, st_ref[k - 1, :, cs], tpos) for k in range(1, FFN_CONV)]
            fnew_ref[0, :, cs] = gp
        gc = cb_ref[:, cs] + cw_ref[FFN_CONV - 1:FFN_CONV, cs] * gp
        for k in range(1, FFN_CONV):
            gc = gc + cw_ref[FFN_CONV - 1 - k:FFN_CONV - k, cs] * shifted[k - 1]
        act = (gc * _sigmoid(gc)) * val
        acc = acc + _bdot(act, wdn_ref[cs, :])
    y = x + gt_ref[0] * acc
    if final:
        y = (y * lax.rsqrt(jnp.mean(y * y, axis=-1, keepdims=True) + EPS)) * fg_ref[...]
    y_ref[0] = y


def _ffn(x, g, sc, sh, gt, st, wup, cw, cb, wdn, fg, tm, seg, final):
    G, Tg, D = x.shape
    F = wdn.shape[0]
    wck = F // 2 if (F // 2) % LANES == 0 else F
    tok = pl.BlockSpec((1, tm, D), lambda g_, t: (g_, t, 0))
    sm = sc.shape[1]
    if seg == 0:
        st_spec = pl.BlockSpec((1, SUBLANES, F), lambda g_, t: (g_, 0, 0))
        fn_spec = pl.BlockSpec((1, SUBLANES, F), lambda g_, t: (g_, 0, 0))
        fn_shape = (G, SUBLANES, F)
    else:
        st_spec = pl.BlockSpec(st.shape, lambda g_, t: (0, 0, 0))
        fn_spec = pl.BlockSpec((1, tm, F), lambda g_, t: (g_, t, 0))
        fn_shape = (G, Tg, F)
    return pl.pallas_call(
        functools.partial(_ffn_kernel, tm=tm, seg=seg, F=F, wck=wck, final=final),
        grid=(G, Tg // tm),
        in_specs=[tok, _const_spec((1, D)), _mod_spec(sm, tm, D), _mod_spec(sm, tm, D), _mod_spec(sm, tm, D),
                  st_spec, _const_spec(wup.shape), _const_spec(cw.shape), _const_spec(cb.shape),
                  _const_spec(wdn.shape), _const_spec((1, D))],
        out_specs=[tok, fn_spec],
        out_shape=[jax.ShapeDtypeStruct((G, Tg, D), F32), jax.ShapeDtypeStruct(fn_shape, F32)],
        scratch_shapes=[pltpu.VMEM((SUBLANES, F), F32)],
        compiler_params=_cparams(2),
        name="conv_ffn",
    )(x, g, sc, sh, gt, st, wup, cw, cb, wdn, fg)


def _head_perm():
    return np.array([(k * NSA_HPG + r) * NSA_HD + d
                     for r in range(NSA_HPG) for k in range(NSA_KV) for d in range(NSA_HD)], np.int32)


def _block_diag(w):
    n, a, b = w.shape[-3:]
    eye = jnp.eye(n, dtype=w.dtype)
    out = jnp.einsum('ij,...iab->...iajb', eye, w)
    return out.reshape(w.shape[:-3] + (n * a, n * b))


def _seg_fill(buf, k, seg):
    B, nb, C = buf.shape
    part = jnp.concatenate([buf[:, nb - k:, :], jnp.zeros((B, seg - k, C), buf.dtype)], axis=1)
    return part.reshape(B * seg, C)


def kernel(x_prompt, x_sample, cache_nsa, cache_nsa_win, state_ret, state_rglru_h, state_rglru_conv,
           state_ffn_conv, page_table, c_prompt, c_sample, norm1_g, norm2_g, w_ada, b_ada, w_in, cmp_pe,
           cmp_w1, cmp_b1, cmp_w2, ret_gn_g, rg_conv_w, rg_conv_b, rg_w_a, rg_b_a, rg_w_x, rg_b_x, rg_lambda,
           w_br_a, w_br_b, w_br_c, w_out, ffn_w_up, ffn_conv_w, ffn_conv_b, ffn_w_down, final_norm_g):
    B, T, D = x_prompt.shape
    DB, Q, _ = x_sample.shape
    L = w_in.shape[0]
    npages = page_table.shape[1]
    page = cache_nsa.shape[2]
    P = npages * page
    NQ = NSA_KV * NSA_HPG * NSA_HD
    NKV = NSA_KV * NSA_HD
    RW = RET_HEADS * RET_DK
    W = rg_conv_w.shape[2]
    F = ffn_w_down.shape[1]
    NS = DB * Q
    wbuf_len = cache_nsa_win.shape[2]
    assert Q == SUBLANES and T >= RG_CONV and P % CMP_STRIDE == 0

    mod = _ada(jnp.concatenate([c_prompt, c_sample], axis=0), w_ada, b_ada)
    cache_t = jnp.transpose(cache_nsa, (0, 1, 3, 4, 5, 2)).reshape(L, cache_nsa.shape[1], 4, NKV, page)
    win_t_all = jnp.transpose(cache_nsa_win, (0, 1, 3, 4, 5, 2)).reshape(L, DB, 2, NKV, wbuf_len)
    perm = _head_perm()

    half = RET_DK // 2
    freq = ROPE_BASE ** (-jnp.arange(half, dtype=F32) / half)

    def rope_tables(pos):
        ang = pos.astype(F32)[:, None] * freq[None, :]
        cos, sin = jnp.cos(ang), jnp.sin(ang)
        return jnp.concatenate([cos, cos], axis=1), jnp.concatenate([-sin, sin], axis=1)

    cos_p, sin_p = rope_tables(jnp.arange(T, dtype=jnp.int32))
    cos_s, sin_s = rope_tables(P + jnp.arange(Q, dtype=jnp.int32))

    widths = (NQ, 4 * NKV, 2 * NKV, LANES, RW, RW, RW, RW, W, W, 3 * D)
    offs = np.cumsum((0, NQ, 6 * NKV, 3 * NSA_KV * NSA_HPG, RW, RW, RW, RW, W, W, 3 * D))
    ngate = 3 * NSA_KV * NSA_HPG

    xp = x_prompt
    xs = x_sample.reshape(1, NS, D)
    outs_p = [[] for _ in range(6)]
    outs_s = [[] for _ in range(6)]
    tm_p = 256 if T % 256 == 0 else T

    for l in range(L):
        wi = w_in[l]
        w_cat = jnp.concatenate([
            wi[:, offs[0]:offs[1]][:, perm],
            wi[:, offs[1]:offs[1] + 4 * NKV],
            wi[:, offs[1] + 4 * NKV:offs[2]],
            jnp.pad(wi[:, offs[2]:offs[3]], ((0, 0), (0, LANES - ngate))),
            wi[:, offs[3]:]], axis=1).astype(BF16)
        w1 = cmp_w1[l]
        bd = lambda w: _block_diag(jnp.broadcast_to(w[:, :, None], w.shape[:2] + (NSA_KV,) + w.shape[2:]))
        cw1 = jnp.concatenate([bd(w1[:, :CMP_STRIDE]), bd(w1[:, CMP_STRIDE:])], axis=-1)
        cw1 = cw1.reshape(2, CMP_STRIDE * NKV, 4 * LANES).astype(BF16)
        cpe = jnp.tile(cmp_pe[l], (1, 1, NSA_KV)).reshape(2, 2, CMP_STRIDE * NKV)
        cb1 = jnp.tile(cmp_b1[l], (1, NSA_KV))[:, None, :]
        cw2 = _block_diag(jnp.broadcast_to(cmp_w2[l][:, None], (2, NSA_KV) + cmp_w2.shape[2:])).astype(BF16)
        cw = (cw1, cpe, cb1, cw2)
        rw = (rg_conv_w[l], rg_conv_b[l][None], _block_diag(rg_w_a[l]).astype(BF16), rg_b_a[l][None],
              _block_diag(rg_w_x[l]).astype(BF16), rg_b_x[l][None], rg_lambda[l][None])
        wa = w_br_a[l][perm].astype(BF16)
        wb = w_br_b[l].astype(BF16)
        wc = w_br_c[l].astype(BF16)
        wo = w_out[l].astype(BF16)
        wup = ffn_w_up[l].astype(BF16)
        wdn = ffn_w_down[l].astype(BF16)
        g1 = norm1_g[l][None]
        g2 = norm2_g[l][None]
        gn = ret_gn_g[l][None]
        fcw, fcb = ffn_conv_w[l], ffn_conv_b[l][None]
        fg = final_norm_g[None]
        final = l == L - 1

        m = [mod[l, :B, i * D:(i + 1) * D][:, None, :] for i in range(6)]
        (nq, rows, win, gate, rq, rk, rv, rg, rx, rgate, mg) = _inproj(xp, g1, m[1], m[0], w_cat, widths, tm_p)
        kc, vc = _compress_prompt(rows, cw)
        o_a = _nsa_prompt(nq, gate, kc, vc, rows, win)
        o_r, s_new = _retention(rq, rk, rv, rg, cos_p, sin_p,
                                jnp.zeros((B, RET_HEADS, RET_DK, RET_DK), F32), gn)
        zs = jnp.zeros((B, SUBLANES, W), F32)
        o_c, h_tail = _rglru(rx, rgate, zs, zs, rw, tm_p, 0)
        x1 = _merge(xp, o_a, o_r, o_c, mg, m[2], wa, wb, wc, wo, tm_p)
        xp, f_tail = _ffn(x1, g2, m[4], m[3], m[5], jnp.zeros((B, SUBLANES, F), F32), wup, fcw, fcb, wdn, fg,
                          tm_p, 0, final)
        wn = min(WINDOW, T)
        outs_p[0].append(rows.reshape(B, T, 4, NSA_KV, NSA_HD))
        outs_p[1].append(win[:, T - wn:].reshape(B, wn, 2, NSA_KV, NSA_HD))
        outs_p[2].append(s_new)
        outs_p[3].append(h_tail[:, SUBLANES - 1])
        outs_p[4].append(rx[:, T - (RG_CONV - 1):])
        outs_p[5].append(f_tail[:, SUBLANES - (FFN_CONV - 1):])

        ms = [jnp.repeat(mod[l, B:, i * D:(i + 1) * D], Q, axis=0)[None] for i in range(6)]
        (nq, rows, win, gate, rq, rk, rv, rg, rx, rgate, mg) = _inproj(xs, g1, ms[1], ms[0], w_cat, widths, NS)
        r3 = lambda a: a.reshape(DB, Q, a.shape[-1])
        o_a = _nsa_sample(page_table, r3(nq), r3(gate), r3(rows), win_t_all[l], r3(win), cache_t, l, cw)
        o_r, s_new = _retention(r3(rq), r3(rk), r3(rv), r3(rg), cos_s, sin_s, state_ret[l].astype(F32), gn)
        cbuf = state_rglru_conv[l]
        st = jnp.stack([_seg_fill(cbuf, k, Q) for k in range(1, RG_CONV)])
        h0 = jnp.pad(state_rglru_h[l].astype(F32)[:, None, :], ((0, 0), (0, Q - 1), (0, 0))).reshape(1, NS, W)
        o_c, h_all = _rglru(rx, rgate, st, h0, rw, NS, Q)
        x1 = _merge(xs, o_a.reshape(1, NS, NQ), o_r.reshape(1, NS, RW), o_c, mg, ms[2], wa, wb, wc, wo, NS)
        fbuf = state_ffn_conv[l]
        fst = jnp.stack([_seg_fill(fbuf, k, Q) for k in range(1, FFN_CONV)])
        xs, g_all = _ffn(x1, g2, ms[4], ms[3], ms[5], fst, wup, fcw, fcb, wdn, fg, NS, Q, final)
        keys = jnp.concatenate([cache_nsa_win[l], r3(win).reshape(DB, Q, 2, NSA_KV, NSA_HD)], axis=1)
        outs_s[0].append(rows.reshape(DB, Q, 4, NSA_KV, NSA_HD))
        outs_s[1].append(keys[:, Q:])
        outs_s[2].append(s_new)
        outs_s[3].append(h_all.reshape(DB, Q, W)[:, Q - 1])
        outs_s[4].append(jnp.concatenate([cbuf, rx.reshape(DB, Q, W)], axis=1)[:, Q:])
        outs_s[5].append(jnp.concatenate([fbuf, g_all.reshape(DB, Q, F)], axis=1)[:, Q:])

    sp = [jnp.stack(a) for a in outs_p]
    ss = [jnp.stack(a) for a in outs_s]
    return (xp, xs.reshape(DB, Q, D), sp[0], ss[0], sp[1], ss[1], sp[2], ss[2],
            sp[3], ss[3], sp[4], ss[4], sp[5], ss[5])
```

```python
import functools
import math

import numpy as np
import jax
import jax.numpy as jnp
from jax import lax
from jax.experimental import pallas as pl
from jax.experimental.pallas import tpu as pltpu

F32 = jnp.float32
BF16 = jnp.bfloat16

NSA_KV = 2
NSA_HPG = 4
NSA_HD = 64
CMP_STRIDE = 16
CMP_BLOCK = 32
SEL_BLOCK = 64
SEL_TOPK = 16
SEL_FORCE = 1e4
WINDOW = 512
QBLOCK = 128
RET_HEADS = 4
RET_DK = 128
RET_CHUNK = 128
ROPE_BASE = 10000.0
RG_CONV = 4
RG_C = 8.0
FFN_CONV = 3
EPS = 1e-6

NEG = -1e30
BIG = float(2 ** 60)
LOG2E = 1.4426950408889634
SUBLANES = 8
LANES = 128
VMEM_LIMIT_V7X = 56 * 1024 * 1024
SEL_CHUNK = 512
CAST_CHUNK = 1024
PAGE_UNROLL = 8


def _cparams(n_grid):
    return pltpu.CompilerParams(dimension_semantics=("arbitrary",) * n_grid,
                                vmem_limit_bytes=VMEM_LIMIT_V7X)


def _bdot(a, b):
    return jnp.dot(a.astype(BF16), b.astype(BF16), preferred_element_type=F32)


def _bdot_nt(a, b):
    return lax.dot_general(a.astype(BF16), b.astype(BF16), (((1,), (1,)), ((), ())),
                           preferred_element_type=F32)


def _split(a):
    hi = a.astype(BF16)
    lo = (a - hi.astype(F32)).astype(BF16)
    return hi, lo


def _dot3(a, b):
    ah, al = _split(a)
    bh, bl = _split(b)
    d = functools.partial(jnp.dot, preferred_element_type=F32)
    return d(ah, bh) + d(al, bh) + d(ah, bl)


def _sigmoid(x):
    return 1.0 / (1.0 + jnp.exp(-x))


def _gelu(x):
    return 0.5 * x * (1.0 + jnp.tanh(0.7978845608028654 * (x + 0.044715 * (x * x * x))))


def _rms_mod(x, g, sc, sh):
    y = x * lax.rsqrt(jnp.mean(x * x, axis=-1, keepdims=True) + EPS)
    return (y * g) * (1.0 + sc) + sh


def _ada_kernel(c_ref, w_ref, b_ref, o_ref):
    c = c_ref[...]
    o_ref[0] = _dot3(c * _sigmoid(c), w_ref[0]) + b_ref[0]


def _ada(c_all, w_ada, b_ada):
    L, D, E = w_ada.shape
    n = c_all.shape[0]
    tn = 1536 if E % 1536 == 0 else E
    return pl.pallas_call(
        _ada_kernel,
        grid=(L, E // tn),
        in_specs=[pl.BlockSpec((n, D), lambda l, j: (0, 0)),
                  pl.BlockSpec((1, D, tn), lambda l, j: (l, 0, j)),
                  pl.BlockSpec((1, 1, tn), lambda l, j: (l, 0, j))],
        out_specs=pl.BlockSpec((1, n, tn), lambda l, j: (l, 0, j)),
        out_shape=jax.ShapeDtypeStruct((L, n, E), F32),
        compiler_params=_cparams(2),
        name="ada_mod",
    )(c_all, w_ada, b_ada.reshape(L, 1, E))


def _mod_spec(sm, tm, d):
    if sm == 1:
        return pl.BlockSpec((1, 1, d), lambda g, t: (g, 0, 0))
    return pl.BlockSpec((1, tm, d), lambda g, t: (g, t, 0))


def _const_spec(shape):
    nd = len(shape)
    return pl.BlockSpec(shape, lambda *a: (0,) * nd, pipeline_mode=pl.Buffered(1))


def _inproj_kernel(x_ref, g_ref, sc_ref, sh_ref, w_ref, *o_refs, segs):
    h = _rms_mod(x_ref[0], g_ref[...], sc_ref[0], sh_ref[0]).astype(BF16)
    for (off, wd), o_ref in zip(segs, o_refs):
        o_ref[0] = jnp.dot(h, w_ref[:, off:off + wd], preferred_element_type=F32).astype(o_ref.dtype)


def _inproj(x, g, sc, sh, w, widths, dtypes, tm):
    G, Tg, D = x.shape
    segs, off = [], 0
    for wd in widths:
        segs.append((off, wd))
        off += wd
    sm = sc.shape[1]
    return pl.pallas_call(
        functools.partial(_inproj_kernel, segs=tuple(segs)),
        grid=(G, Tg // tm),
        in_specs=[pl.BlockSpec((1, tm, D), lambda g_, t: (g_, t, 0)),
                  _const_spec((1, D)),
                  _mod_spec(sm, tm, D), _mod_spec(sm, tm, D),
                  _const_spec(w.shape)],
        out_specs=[pl.BlockSpec((1, tm, wd), lambda g_, t: (g_, t, 0)) for wd in widths],
        out_shape=[jax.ShapeDtypeStruct((G, Tg, wd), dt) for wd, dt in zip(widths, dtypes)],
        compiler_params=_cparams(2),
        name="in_proj",
    )(x, g, sc, sh, w)


def _compress_bias(w1_ref, pe_ref, bias_sc):
    for c in range(2):
        halves = []
        for h in range(2):
            pe_rows = jnp.broadcast_to(pe_ref[c, h:h + 1, :], (SUBLANES, pe_ref.shape[2]))
            halves.append(_bdot(pe_rows, w1_ref[c, :, h * 2 * LANES:(h + 1) * 2 * LANES]))
        bias_sc[c] = jnp.concatenate(halves, axis=1)


def _compress_x(xrefs, nch, w1_ref, bias_sc, b1_ref, w2_ref):
    last = lax.broadcasted_iota(jnp.int32, (nch, 1), 0) == nch - 1
    outs = []
    for c in range(2):
        lhs = jnp.concatenate([xrefs[c][pl.ds(p, nch, stride=CMP_STRIDE), :].astype(BF16)
                               for p in range(CMP_STRIDE)], axis=1)
        acc = jnp.dot(lhs, w1_ref[c], preferred_element_type=F32) + bias_sc[c, 0:1, :]
        lo = acc[:, :2 * LANES]
        hi = acc[:, 2 * LANES:]
        hi_next = jnp.where(last, 0.0, pltpu.roll(hi, nch - 1, 0))
        hid = _gelu(lo + hi_next + b1_ref[c])
        outs.append(_bdot(hid, w2_ref[c]))
    return outs


def _compress_prompt_kernel(krows_ref, vrows_ref, w1_ref, pe_ref, b1_ref, w2_ref, kc_ref, vc_ref, bias_sc, *, nch):
    @pl.when(pl.program_id(0) == 0)
    def _():
        _compress_bias(w1_ref, pe_ref, bias_sc)

    kc, vc = _compress_x((krows_ref.at[0], vrows_ref.at[0]), nch, w1_ref, bias_sc, b1_ref, w2_ref)
    kc_ref[0] = kc
    vc_ref[0] = vc


def _compress_prompt(rows, cw):
    B, T, _ = rows.shape
    nch = T // CMP_STRIDE
    w1, pe, b1, w2 = cw
    return pl.pallas_call(
        functools.partial(_compress_prompt_kernel, nch=nch),
        grid=(B,),
        in_specs=[pl.BlockSpec((1, T, LANES), lambda b: (b, 0, 0)),
                  pl.BlockSpec((1, T, LANES), lambda b: (b, 0, 1)),
                  _const_spec(w1.shape), _const_spec(pe.shape), _const_spec(b1.shape), _const_spec(w2.shape)],
        out_specs=[pl.BlockSpec((1, nch, LANES), lambda b: (b, 0, 0))] * 2,
        out_shape=[jax.ShapeDtypeStruct((B, nch, LANES), F32)] * 2,
        scratch_shapes=[pltpu.VMEM((2, SUBLANES, 4 * LANES), F32)],
        compiler_params=_cparams(1),
        name="nsa_compress_prompt",
    )(rows, rows, w1, pe, b1, w2)


def _rep_all(a):
    return jnp.concatenate([a] * (NSA_KV * NSA_HPG), axis=0)


def _rep_heads(a, Qb):
    return jnp.concatenate([a[:Qb]] * NSA_HPG + [a[Qb:]] * NSA_HPG, axis=0)


def _nsa_front(qblk, q0, Qb, kc, vc, n_cmp, n_sel):
    R2 = 2 * Qb
    npad = kc.shape[0]
    lane = lax.broadcasted_iota(jnp.int32, (Qb, LANES), 1)
    lo_half = lane < NSA_HD
    scale = NSA_HD ** -0.5 * LOG2E
    pieces = []
    for k in range(NSA_KV):
        for r in range(NSA_HPG):
            sl = qblk[:, r * LANES:(r + 1) * LANES] * scale
            pieces.append(jnp.where(lo_half if k == 0 else jnp.logical_not(lo_half), sl, 0.0))
    qs = jnp.concatenate(pieces, axis=0).astype(BF16)
    qp1 = q0 + lax.broadcasted_iota(jnp.int32, (Qb, 1), 0)

    n_idx = lax.broadcasted_iota(jnp.int32, (1, npad), 1)
    visible = (n_idx * CMP_STRIDE + (CMP_BLOCK - 1) <= qp1) & (n_idx < n_cmp)
    s = _bdot_nt(qs, kc) + _rep_all(jnp.where(visible, 0.0, -BIG))
    e = jnp.exp2(s - jnp.max(s, axis=-1, keepdims=True))
    any_visible = _rep_all((qp1 >= CMP_BLOCK - 1) & (n_cmp > 0))
    p_c = e * jnp.where(any_visible, 1.0 / jnp.sum(e, axis=-1, keepdims=True), 0.0)
    o_c = _bdot(p_c, vc)

    psum = []
    for k in range(NSA_KV):
        acc = p_c[(k * NSA_HPG) * Qb:(k * NSA_HPG + 1) * Qb]
        for r in range(1, NSA_HPG):
            acc = acc + p_c[(k * NSA_HPG + r) * Qb:(k * NSA_HPG + r + 1) * Qb]
        psum.append(acc)
    psum = jnp.concatenate(psum + [jnp.zeros((LANES - R2, npad), F32)], axis=0)
    nsr = -(-n_sel // SUBLANES) * SUBLANES
    sj = lax.broadcasted_iota(jnp.int32, (nsr, npad), 0) * SEL_BLOCK
    ci = lax.broadcasted_iota(jnp.int32, (nsr, npad), 1) * CMP_STRIDE
    selmap = jnp.where((ci < sj + SEL_BLOCK) & (ci + CMP_BLOCK > sj), 1.0, 0.0).astype(BF16)
    ph, plo = _split(psum)
    imp = _bdot_nt(selmap, ph) + _bdot_nt(selmap, plo)
    col_pos = q0 + (lax.broadcasted_iota(jnp.int32, (1, LANES), 1) & (Qb - 1))
    return qs, qp1, o_c, _select_blocks(imp, col_pos, n_sel)


def _select_blocks(imp, qpos, n_sel):
    nsr, ncol = imp.shape
    j = lax.broadcasted_iota(jnp.int32, (nsr, ncol), 0)
    jf = j.astype(F32)
    cur = qpos >> 6
    forced = (j == 0) | (j == cur) | (j == cur - 1)
    imp = jnp.where(forced, SEL_FORCE, imp)
    imp = jnp.where(j * SEL_BLOCK <= qpos, imp, -SEL_FORCE)
    imp = jnp.where(j < n_sel, imp, NEG)

    def pick(_, carry):
        imp_c, sel_c = carry
        m = jnp.max(imp_c, axis=0, keepdims=True)
        first = jnp.min(jnp.where(imp_c == m, jf, float(nsr)), axis=0, keepdims=True)
        hit = jf == first
        return jnp.where(hit, NEG, imp_c), jnp.where(hit, 1.0, sel_c)

    _, sel = lax.fori_loop(0, min(SEL_TOPK, n_sel), pick, (imp, jnp.zeros((nsr, ncol), F32)))
    return sel


def _nsa_combine(gate, o_c, o_s, o_w, Qb, o_ref):
    lo_half = lax.broadcasted_iota(jnp.int32, (Qb, LANES), 1) < NSA_HD
    g = _sigmoid(gate)
    for r in range(NSA_HPG):
        halves = []
        for k in range(NSA_KV):
            rs = slice((k * NSA_HPG + r) * Qb, (k * NSA_HPG + r + 1) * Qb)
            c = (k * NSA_HPG + r) * 3
            halves.append(g[:, c:c + 1] * o_c[rs] + g[:, c + 1:c + 2] * o_s[rs] + g[:, c + 2:c + 3] * o_w[rs])
        o_ref[0, :, r * LANES:(r + 1) * LANES] = jnp.where(lo_half, halves[0], halves[1])


def _block_columns(k0, n):
    key = k0 + lax.broadcasted_iota(jnp.int32, (n, LANES), 0)
    blk = lax.broadcasted_iota(jnp.int32, (n, LANES), 1)
    return jnp.where((key >> 6) == blk, BIG, 0.0).astype(BF16)


def _block_rows(k0, n):
    key = k0 + lax.broadcasted_iota(jnp.int32, (LANES, n), 1)
    blk = lax.broadcasted_iota(jnp.int32, (LANES, n), 0)
    return jnp.where((key >> 6) == blk, BIG, 0.0).astype(BF16)


def _lanes_all(a):
    return jnp.concatenate([a] * (NSA_KV * NSA_HPG), axis=1)


def _lanes_heads(a, Qb):
    return jnp.concatenate([a[:, :Qb]] * NSA_HPG + [a[:, Qb:]] * NSA_HPG, axis=1)


def _softmax_cols(s):
    e = jnp.exp2(s - jnp.max(s, axis=0, keepdims=True))
    return e, 1.0 / jnp.sum(e, axis=0, keepdims=True)


def _nsa_prompt_kernel(q_ref, gate_ref, kc_ref, vc_ref, rows_ref, win_ref, o_ref, kaug, vt,
                       *, T, Qb, n_cmp, n_sel, wl):
    i = pl.program_id(1)
    q0 = i * Qb
    R = NSA_KV * NSA_HPG * Qb
    R2 = NSA_KV * Qb

    @pl.when(i == 0)
    def _():
        def pack(c, carry):
            r0 = pl.multiple_of(c * SEL_CHUNK, SEL_CHUNK)
            kaug[pl.ds(r0, SEL_CHUNK), 0:LANES] = rows_ref[0, pl.ds(r0, SEL_CHUNK), 2 * LANES:3 * LANES].astype(BF16)
            kaug[pl.ds(r0, SEL_CHUNK), LANES:2 * LANES] = _block_columns(r0, SEL_CHUNK)
            vt[:, pl.ds(r0, SEL_CHUNK)] = rows_ref[0, pl.ds(r0, SEL_CHUNK), 3 * LANES:4 * LANES].T.astype(BF16)
            return carry
        lax.fori_loop(0, T // SEL_CHUNK, pack, 0)

    top = lax.broadcasted_iota(jnp.int32, (LANES, Qb), 0) < NSA_HD
    scale = NSA_HD ** -0.5 * LOG2E
    qblk = q_ref[0]
    q_t = [(qblk[:, r * LANES:(r + 1) * LANES] * scale).T for r in range(NSA_HPG)]
    qs = jnp.concatenate([jnp.where(top if k == 0 else jnp.logical_not(top), q_t[r], 0.0)
                          for k in range(NSA_KV) for r in range(NSA_HPG)], axis=1).astype(BF16)
    qlane = q0 + lax.broadcasted_iota(jnp.int32, (1, Qb), 1)
    qpos = _lanes_all(qlane)

    kc = kc_ref[0]
    npad = kc.shape[0]
    n_idx = lax.broadcasted_iota(jnp.int32, (npad, 1), 0)
    visible = (n_idx * CMP_STRIDE + (CMP_BLOCK - 1) <= qlane) & (n_idx < n_cmp)
    e_c, inv_c = _softmax_cols(_bdot(kc, qs) + _lanes_all(jnp.where(visible, 0.0, -BIG)))
    any_visible = _lanes_all((qlane >= CMP_BLOCK - 1) & (n_cmp > 0))
    p_c = e_c * jnp.where(any_visible, inv_c, 0.0)
    o_c = _bdot(vc_ref[0].T, p_c)

    psum = []
    for k in range(NSA_KV):
        acc = p_c[:, (k * NSA_HPG) * Qb:(k * NSA_HPG + 1) * Qb]
        for r in range(1, NSA_HPG):
            acc = acc + p_c[:, (k * NSA_HPG + r) * Qb:(k * NSA_HPG + r + 1) * Qb]
        psum.append(acc)
    psum = jnp.concatenate(psum, axis=1)
    nsr = -(-n_sel // SUBLANES) * SUBLANES
    sj = lax.broadcasted_iota(jnp.int32, (nsr, npad), 0) * SEL_BLOCK
    ci = lax.broadcasted_iota(jnp.int32, (nsr, npad), 1) * CMP_STRIDE
    selmap = jnp.where((ci < sj + SEL_BLOCK) & (ci + CMP_BLOCK > sj), 1.0, 0.0).astype(BF16)
    ph, plo = _split(psum)
    imp = (jnp.dot(selmap, ph, preferred_element_type=F32)
           + jnp.dot(selmap, plo, preferred_element_type=F32))
    sel = _select_blocks(imp, jnp.concatenate([qlane] * NSA_KV, axis=1), n_sel)
    selm =jnp.concatenate([sel - 1.0, jnp.zeros((LANES - nsr, R2), F32)], axis=0) if nsr < LANES else sel - 1.0
    qaug = jnp.concatenate([qs, _lanes_heads(selm, Qb).astype(BF16)], axis=0)

    def update(s, vcols, carry):
        m, l, acc = carry
        m_new = jnp.maximum(m, jnp.max(s, axis=0, keepdims=True))
        alpha = jnp.exp2(m - m_new)
        p = jnp.exp2(s - m_new)
        l = alpha * l + jnp.sum(p, axis=0, keepdims=True)
        acc = alpha * acc + jnp.dot(vcols, p.astype(BF16), preferred_element_type=F32)
        return m_new, l, acc

    def scores(k0):
        return jnp.dot(kaug[pl.ds(k0, SEL_CHUNK), :], qaug, preferred_element_type=F32)

    def full_chunk(c, carry):
        k0 = pl.multiple_of(c * SEL_CHUNK, SEL_CHUNK)
        return update(scores(k0), vt[:, pl.ds(k0, SEL_CHUNK)], carry)

    n_full = q0 // SEL_CHUNK
    init = (jnp.full((1, R), -4.0 * BIG, F32), jnp.zeros((1, R), F32), jnp.zeros((LANES, R), F32))
    carry = lax.fori_loop(0, n_full, full_chunk, init)
    k0 = pl.multiple_of(n_full * SEL_CHUNK, SEL_CHUNK)
    kpos = k0 + lax.broadcasted_iota(jnp.int32, (SEL_CHUNK, 1), 0)
    s_diag = jnp.where(kpos <= qpos, scores(k0), -2.0 * BIG)
    _, l_s, acc_s = update(s_diag, vt[:, pl.ds(k0, SEL_CHUNK)], carry)
    o_s = acc_s * (1.0 / l_s)

    ws = pl.multiple_of(jnp.maximum(q0 - WINDOW, 0), LANES)
    wk = win_ref[0, pl.ds(ws, wl), 0:LANES]
    wv = win_ref[0, pl.ds(ws, wl), LANES:2 * LANES]
    dpos = qlane - (ws + lax.broadcasted_iota(jnp.int32, (wl, 1), 0))
    e_w, inv_w = _softmax_cols(_bdot(wk, qs) + _lanes_all(jnp.where((dpos >= 0) & (dpos <= WINDOW), 0.0, -BIG)))
    o_w = _bdot(wv.T, e_w) * inv_w

    g = _sigmoid(gate_ref[0]).T
    for r in range(NSA_HPG):
        halves = []
        for k in range(NSA_KV):
            cs = slice((k * NSA_HPG + r) * Qb, (k * NSA_HPG + r + 1) * Qb)
            c = (k * NSA_HPG + r) * 3
            halves.append(g[c:c + 1, :] * o_c[:, cs] + g[c + 1:c + 2, :] * o_s[:, cs] + g[c + 2:c + 3, :] * o_w[:, cs])
        o_ref[0, :, r * LANES:(r + 1) * LANES] = jnp.where(top, halves[0], halves[1]).T.astype(o_ref.dtype)


def _nsa_prompt(nq, gate, kc, vc, rows, win):
    B, T, HD = nq.shape
    Qb = QBLOCK
    nch = kc.shape[1]
    n_sel = -(-T // SEL_BLOCK)
    wl = WINDOW + Qb
    assert T % SEL_CHUNK == 0 and T >= wl and SEL_TOPK <= n_sel <= LANES
    return pl.pallas_call(
        functools.partial(_nsa_prompt_kernel, T=T, Qb=Qb, n_cmp=nch - 1, n_sel=n_sel, wl=wl),
        grid=(B, T // Qb),
        in_specs=[pl.BlockSpec((1, Qb, HD), lambda b, i: (b, i, 0)),
                  pl.BlockSpec((1, Qb, LANES), lambda b, i: (b, i, 0)),
                  pl.BlockSpec((1, nch, LANES), lambda b, i: (b, 0, 0)),
                  pl.BlockSpec((1, nch, LANES), lambda b, i: (b, 0, 0)),
                  pl.BlockSpec((1, T, 4 * LANES), lambda b, i: (b, 0, 0)),
                  pl.BlockSpec((1, T, 2 * LANES), lambda b, i: (b, 0, 0))],
        out_specs=pl.BlockSpec((1, Qb, HD), lambda b, i: (b, i, 0)),
        out_shape=jax.ShapeDtypeStruct((B, T, HD), BF16),
        scratch_shapes=[pltpu.VMEM((T, 2 * LANES), BF16), pltpu.VMEM((LANES, T), BF16)],
        compiler_params=_cparams(2),
        name="nsa_attn_prompt",
    )(nq, gate, kc, vc, rows, win)


def _pages_copy(cache_hbm, layer, page, r0, dst, j, sem):
    n = cache_hbm.shape[-1]
    return pltpu.make_async_copy(cache_hbm.at[layer, page, pl.ds(r0, 2)],
                                 dst.at[:, :, pl.ds(pl.multiple_of(j * n, n), n)], sem)


def _pages_start(pt_ref, b, cache_hbm, layer, r0, dst, sem, npages):
    def issue(j, carry):
        _pages_copy(cache_hbm, layer, pt_ref[b, j], r0, dst, j, sem).start()
        return carry
    lax.fori_loop(0, npages, issue, 0, unroll=PAGE_UNROLL)


def _pages_wait(cache_hbm, layer, r0, dst, sem, npages):
    def wait(j, carry):
        _pages_copy(cache_hbm, layer, 0, r0, dst, j, sem).wait()
        return carry
    lax.fori_loop(0, npages, wait, 0, unroll=PAGE_UNROLL)


def _softmax2(s1, s2, mask2):
    s2 = jnp.where(mask2, s2, -2.0 * BIG)
    m = jnp.maximum(jnp.max(s1, axis=-1, keepdims=True), jnp.max(s2, axis=-1, keepdims=True))
    e1 = jnp.exp2(s1 - m)
    e2 = jnp.exp2(s2 - m)
    den = jnp.sum(e1, axis=-1, keepdims=True) + jnp.sum(e2, axis=-1, keepdims=True)
    return e1, e2, 1.0 / den


def _nsa_sample_kernel(pt_ref, q_ref, gate_ref, rows_ref, wt_ref, wnew_ref, cache_hbm,
                       w1_ref, pe_ref, b1_ref, w2_ref, o_ref,
                       cmpbuf, selbuf, xk, xv, kaug, vt, newbuf, wnewbuf, bias_sc, csem, ssem,
                       *, layer, npages, P, Q, wb, nb, n_sel):
    b = pl.program_id(0)
    nseq = pl.num_programs(0)
    page = cache_hbm.shape[-1]
    nch = P // CMP_STRIDE

    @pl.when(b == 0)
    def _():
        _pages_start(pt_ref, 0, cache_hbm, layer, 0, cmpbuf, csem, npages)
        _pages_start(pt_ref, 0, cache_hbm, layer, 2, selbuf, ssem, npages)
        _compress_bias(w1_ref, pe_ref, bias_sc)

        def blocks(c, carry):
            c0 = pl.multiple_of(c * CAST_CHUNK, CAST_CHUNK)
            kaug[LANES:2 * LANES, pl.ds(c0, CAST_CHUNK)] = _block_rows(c0, CAST_CHUNK)
            return carry
        lax.fori_loop(0, P // CAST_CHUNK, blocks, 0)
        newbuf[...] = jnp.zeros(newbuf.shape, F32)
        wnewbuf[...] = jnp.zeros(wnewbuf.shape, F32)

    _pages_wait(cache_hbm, layer, 0, cmpbuf, csem, npages)

    for c, dst in enumerate((xk, xv)):
        for j in range(npages):
            dst[j * page:(j + 1) * page, :] = cmpbuf[c, :, j * page:(j + 1) * page].T
    kc, vc = _compress_x((xk, xv), nch, w1_ref, bias_sc, b1_ref, w2_ref)

    @pl.when(b + 1 < nseq)
    def _():
        _pages_start(pt_ref, b + 1, cache_hbm, layer, 0, cmpbuf, csem, npages)

    _pages_wait(cache_hbm, layer, 2, selbuf, ssem, npages)

    def pack(c, carry):
        c0 = pl.multiple_of(c * CAST_CHUNK, CAST_CHUNK)
        kaug[0:LANES, pl.ds(c0, CAST_CHUNK)] = selbuf[0, :, pl.ds(c0, CAST_CHUNK)].astype(BF16)
        vt[:, pl.ds(c0, CAST_CHUNK)] = selbuf[1, :, pl.ds(c0, CAST_CHUNK)].astype(BF16)
        return carry
    lax.fori_loop(0, P // CAST_CHUNK, pack, 0)

    @pl.when(b + 1 < nseq)
    def _():
        _pages_start(pt_ref, b + 1, cache_hbm, layer, 2, selbuf, ssem, npages)

    newbuf[0:Q, :] = rows_ref[0, :, 2 * LANES:4 * LANES]
    wnewbuf[0:Q, :] = wnew_ref[0]

    qs, qp1, o_c, sel_t = _nsa_front(q_ref[0], P, Q, kc, vc, nch - 1, n_sel)
    qpos = _rep_all(qp1)
    lane = lax.broadcasted_iota(jnp.int32, (1, LANES), 1)
    new_pos = P + lane
    is_new = lane < Q

    def as_rows(blk):
        if blk.shape[0] < LANES:
            blk = jnp.concatenate([blk, jnp.zeros((LANES - blk.shape[0], LANES), F32)], axis=0)
        return blk.T[0:NSA_KV * Q]
    sel_rows = as_rows(sel_t[0:min(LANES, sel_t.shape[0])])
    sel_new = as_rows(sel_t[nb:nb + SUBLANES])[:, 0:1]

    qaug = jnp.concatenate([qs, _rep_heads(sel_rows - 1.0, Q).astype(BF16)], axis=1)
    s_past = jnp.dot(qaug, kaug[...], preferred_element_type=F32)
    s_new = _bdot_nt(qs, newbuf[:, 0:LANES])
    new_ok = is_new & (new_pos <= qpos) & (_rep_heads(sel_new, Q) > 0.5)
    e1, e2, inv = _softmax2(s_past, s_new, new_ok)
    o_s = (lax.dot_general(e1.astype(BF16), vt[...], (((1,), (1,)), ((), ())), preferred_element_type=F32)
           + _bdot(e2, newbuf[:, LANES:2 * LANES])) * inv

    dpast = qp1 - ((P - wb) + lax.broadcasted_iota(jnp.int32, (1, wb), 1))
    s_wp = (jnp.dot(qs, wt_ref[0, 0].astype(BF16), preferred_element_type=F32)
            + _rep_all(jnp.where((dpast >= 0) & (dpast <= WINDOW), 0.0, -BIG)))
    s_wn = _bdot_nt(qs, wnewbuf[:, 0:LANES])
    dnew = qpos - new_pos
    e1, e2, inv = _softmax2(s_wp, s_wn, is_new & (dnew >= 0) & (dnew <= WINDOW))
    o_w = (lax.dot_general(e1.astype(BF16), wt_ref[0, 1].astype(BF16), (((1,), (1,)), ((), ())),
                           preferred_element_type=F32)
           + _bdot(e2, wnewbuf[:, LANES:2 * LANES])) * inv

    _nsa_combine(gate_ref[0], o_c, o_s, o_w, Q, o_ref)


def _nsa_sample(page_table, nq, gate, rows, win_t, win_new, cache_t, layer, cw):
    DB, Q, HD = nq.shape
    npages = page_table.shape[1]
    page = cache_t.shape[-1]
    P = npages * page
    wb = win_t.shape[-1]
    nb = P // SEL_BLOCK
    n_sel = -(-(P + Q) // SEL_BLOCK)
    w1, pe, b1, w2 = cw
    assert Q == SUBLANES and P % SEL_BLOCK == 0 and Q <= SEL_BLOCK and nb <= LANES and n_sel >= SEL_TOPK
    assert P % CAST_CHUNK == 0 and (P + Q) // CMP_STRIDE == P // CMP_STRIDE and page == LANES
    bs = lambda shape: pl.BlockSpec((1,) + shape, lambda b, pt: (b,) + (0,) * len(shape))
    cs = lambda shape: pl.BlockSpec(shape, lambda b, pt: (0,) * len(shape), pipeline_mode=pl.Buffered(1))
    grid_spec = pltpu.PrefetchScalarGridSpec(
        num_scalar_prefetch=1, grid=(DB,),
        in_specs=[bs((Q, HD)), bs((Q, LANES)), bs((Q, 4 * LANES)), bs((2, LANES, wb)), bs((Q, 2 * LANES)),
                  pl.BlockSpec(memory_space=pl.ANY),
                  cs(w1.shape), cs(pe.shape), cs(b1.shape), cs(w2.shape)],
        out_specs=bs((Q, HD)),
        scratch_shapes=[pltpu.VMEM((2, LANES, P), F32), pltpu.VMEM((2, LANES, P), F32),
                        pltpu.VMEM((P, LANES), F32), pltpu.VMEM((P, LANES), F32),
                        pltpu.VMEM((2 * LANES, P), BF16), pltpu.VMEM((LANES, P), BF16),
                        pltpu.VMEM((LANES, 2 * LANES), F32), pltpu.VMEM((LANES, 2 * LANES), F32),
                        pltpu.VMEM((2, SUBLANES, 4 * LANES), F32),
                        pltpu.SemaphoreType.DMA(()), pltpu.SemaphoreType.DMA(())])
    return pl.pallas_call(
        functools.partial(_nsa_sample_kernel, layer=layer, npages=npages, P=P, Q=Q, wb=wb, nb=nb, n_sel=n_sel),
        grid_spec=grid_spec,
        out_shape=jax.ShapeDtypeStruct((DB, Q, HD), F32),
        compiler_params=_cparams(1),
        name="nsa_sample",
    )(page_table, nq, gate, rows, win_t, win_new, cache_t, w1, pe, b1, w2)


def _ret_kernel(q_ref, k_ref, v_ref, g_ref, cos_ref, sin_ref, s0_ref, gn_ref, o_ref, snew_ref, s_sc,
                *, C, nC, nseq):
    c = pl.program_id(1)

    @pl.when(c == 0)
    def _():
        s_sc[...] = s0_ref[...]

    cosf = cos_ref[...]
    sinf = sin_ref[...]
    diff = (lax.broadcasted_iota(jnp.int32, (C, C), 0) - lax.broadcasted_iota(jnp.int32, (C, C), 1)).astype(F32)
    ii = lax.broadcasted_iota(jnp.int32, (C, 1), 0).astype(F32)
    half = RET_DK // 2
    for h in range(RET_HEADS):
        lg = math.log(1.0 - 2.0 ** (-5.0 - h))
        hs = slice(h * RET_DK, (h + 1) * RET_DK)
        decay = jnp.where(diff >= 0, jnp.exp(jnp.maximum(diff, 0.0) * lg), 0.0)
        cross = jnp.exp((ii + 1.0) * lg)
        kweight = jnp.exp((C - 1.0 - ii) * lg)
        for b in range(nseq):
            q = q_ref[b, :, hs]
            k = k_ref[b, :, hs]
            v = v_ref[b, :, hs]
            qr = q * cosf + pltpu.roll(q, half, 1) * sinf
            kr = (k * cosf + pltpu.roll(k, half, 1) * sinf) * (RET_DK ** -0.5)
            o_inner = _bdot(_bdot_nt(qr, kr) * decay, v)
            s_old = s_sc[b, h]
            o_cross = _bdot(qr, s_old) * cross
            kv = lax.dot_general((kr * kweight).astype(BF16), v.astype(BF16), (((0,), (0,)), ((), ())),
                                 preferred_element_type=F32)
            s_sc[b, h] = math.exp(C * lg) * s_old + kv
            o = o_inner + o_cross
            mu = jnp.mean(o, axis=-1, keepdims=True)
            var = jnp.mean(jnp.square(o - mu), axis=-1, keepdims=True)
            gate = g_ref[b, :, hs]
            o_ref[b, :, hs] = (((o - mu) * lax.rsqrt(var + EPS)) * gn_ref[:, hs]
                               * (gate * _sigmoid(gate))).astype(o_ref.dtype)

    @pl.when(c == nC - 1)
    def _():
        snew_ref[...] = s_sc[...]


def _retention(rq, rk, rv, rg, cosf, sinf, s0, gn, nseq, out_dtype):
    B, T, W = rq.shape
    C = RET_CHUNK if (T >= RET_CHUNK and T % RET_CHUNK == 0) else T
    nC = T // C
    tok = pl.BlockSpec((nseq, C, W), lambda b, c: (b, c, 0))
    tab = pl.BlockSpec((C, RET_DK), lambda b, c: (c, 0))
    st = pl.BlockSpec((nseq,) + s0.shape[1:], lambda b, c: (b, 0, 0, 0))
    return pl.pallas_call(
        functools.partial(_ret_kernel, C=C, nC=nC, nseq=nseq),
        grid=(B // nseq, nC),
        in_specs=[tok, tok, tok, tok, tab, tab, st, _const_spec((1, W))],
        out_specs=[tok, st],
        out_shape=[jax.ShapeDtypeStruct((B, T, W), out_dtype), jax.ShapeDtypeStruct(s0.shape, F32)],
        scratch_shapes=[pltpu.VMEM((nseq,) + s0.shape[1:], F32)],
        compiler_params=_cparams(2),
        name="retention",
    )(rq, rk, rv, rg, cosf, sinf, s0, gn)


def _shift_carry(x, k, tail8):
    r = pltpu.roll(x, k, 0)
    row8 = lax.broadcasted_iota(jnp.int32, (SUBLANES, 1), 0)
    first = jnp.where(row8 >= k, r[:SUBLANES], pltpu.roll(tail8, k, 0))
    return jnp.concatenate([first, r[SUBLANES:]], axis=0)


def _shift_seg(x, k, fill, tpos):
    return jnp.where(tpos >= k, pltpu.roll(x, k, 0), fill)


def _rglru_kernel(x_ref, gate_ref, st_ref, h0_ref, cw_ref, cb_ref, wa_ref, ba_ref, wx_ref, bx_ref, lam_ref,
                  o_ref, h_ref, tail_sc, h_sc, *, tm, seg):
    carry = seg == 0
    x = x_ref[0]
    rows = lax.broadcasted_iota(jnp.int32, (tm, 1), 0)
    if carry:
        @pl.when(pl.program_id(1) == 0)
        def _():
            tail_sc[...] = st_ref[0]
            h_sc[...] = h0_ref[0]
        tail8 = tail_sc[...]
        shifted = [_shift_carry(x, k, tail8) for k in range(1, RG_CONV)]
        tpos = rows
        seglen = tm
        h_in = jnp.where(rows == 0, h_sc[SUBLANES - 1:SUBLANES, :], 0.0)
    else:
        tpos = rows & (seg - 1)
        shifted = [_shift_seg(x, k, st_ref[k - 1], tpos) for k in range(1, RG_CONV)]
        seglen = seg
        h_in = h0_ref[0]
    xc = cb_ref[...] + cw_ref[RG_CONV - 1:RG_CONV, :] * x
    for k in range(1, RG_CONV):
        xc = xc + cw_ref[RG_CONV - 1 - k:RG_CONV - k, :] * shifted[k - 1]
    r = _sigmoid(_bdot(xc, wa_ref[...]) + ba_ref[...])
    i = _sigmoid(_bdot(xc, wx_ref[...]) + bx_ref[...])
    lam = lam_ref[...]
    softplus = jnp.maximum(-lam, 0.0) + jnp.log(1.0 + jnp.exp(-jnp.abs(lam)))
    log_a = (-RG_C * r) * softplus
    a = jnp.exp(log_a)
    bt = jnp.sqrt(1.0 - a * a) * (i * xc)
    bt = bt + a * h_in
    k = 1
    while k < seglen:
        ok = tpos >= k
        a_prev = jnp.where(ok, pltpu.roll(a, k, 0), 1.0)
        b_prev = jnp.where(ok, pltpu.roll(bt, k, 0), 0.0)
        bt = a * b_prev + bt
        a = a * a_prev
        k *= 2
    o_ref[0] = (bt * _gelu(gate_ref[0].astype(F32))).astype(o_ref.dtype)
    if carry:
        tail_sc[...] = x[tm - SUBLANES:]
        h_sc[...] = bt[tm - SUBLANES:]
        h_ref[0] = bt[tm - SUBLANES:]
    else:
        h_ref[0] = bt


def _rglru(rx, rgate, st, h0, rw, tm, seg):
    G, Tg, W = rx.shape
    tok = pl.BlockSpec((1, tm, W), lambda g, t: (g, t, 0))
    if seg == 0:
        st_spec = pl.BlockSpec((1, SUBLANES, W), lambda g, t: (g, 0, 0))
        h0_spec = pl.BlockSpec((1, SUBLANES, W), lambda g, t: (g, 0, 0))
        h_spec = pl.BlockSpec((1, SUBLANES, W), lambda g, t: (g, 0, 0))
        h_shape = (G, SUBLANES, W)
    else:
        st_spec = pl.BlockSpec(st.shape, lambda g, t: (0, 0, 0))
        h0_spec = tok
        h_spec = tok
        h_shape = (G, Tg, W)
    return pl.pallas_call(
        functools.partial(_rglru_kernel, tm=tm, seg=seg),
        grid=(G, Tg // tm),
        in_specs=[tok, tok, st_spec, h0_spec] + [_const_spec(a.shape) for a in rw],
        out_specs=[tok, h_spec],
        out_shape=[jax.ShapeDtypeStruct((G, Tg, W), rgate.dtype), jax.ShapeDtypeStruct(h_shape, F32)],
        scratch_shapes=[pltpu.VMEM((SUBLANES, W), F32), pltpu.VMEM((SUBLANES, W), F32)],
        compiler_params=_cparams(2),
        name="rglru",
    )(rx, rgate, st, h0, *rw)


def _merge_kernel(x_ref, oa_ref, or_ref, oc_ref, mg_ref, gt_ref, wa_ref, wb_ref, wc_ref, wo_ref, y_ref, *, D):
    pa = _bdot(oa_ref[0], wa_ref[...])
    pb = _bdot(or_ref[0], wb_ref[...])
    pc = _bdot(oc_ref[0], wc_ref[...])
    gate = lambda i: _sigmoid(mg_ref[0, :, i * D:(i + 1) * D].astype(F32))
    merged = gate(0) * pa + gate(1) * pb + gate(2) * pc
    y_ref[0] = x_ref[0] + gt_ref[0] * _bdot(merged, wo_ref[...])


def _merge(x, oa, orr, oc, mg, gt, wa, wb, wc, wo, tm):
    G, Tg, D = x.shape
    tok = lambda w: pl.BlockSpec((1, tm, w), lambda g, t: (g, t, 0))
    return pl.pallas_call(
        functools.partial(_merge_kernel, D=D),
        grid=(G, Tg // tm),
        in_specs=[tok(D), tok(oa.shape[2]), tok(orr.shape[2]), tok(oc.shape[2]), tok(3 * D),
                  _mod_spec(gt.shape[1], tm, D),
                  _const_spec(wa.shape), _const_spec(wb.shape), _const_spec(wc.shape), _const_spec(wo.shape)],
        out_specs=tok(D),
        out_shape=jax.ShapeDtypeStruct((G, Tg, D), F32),
        compiler_params=_cparams(2),
        name="merge_out",
    )(x, oa, orr, oc, mg, gt, wa, wb, wc, wo)


def _ffn_kernel(x_ref, g_ref, sc_ref, sh_ref, gt_ref, st_ref, wup_ref, cw_ref, cb_ref, wdn_ref, fg_ref,
                y_ref, fnew_ref, tail_sc, *, tm, seg, F, wck, final):
    carry = seg == 0
    x = x_ref[0]
    h = _rms_mod(x, g_ref[...], sc_ref[0], sh_ref[0]).astype(BF16)
    rows = lax.broadcasted_iota(jnp.int32, (tm, 1), 0)
    if carry:
        @pl.when(pl.program_id(1) == 0)
        def _():
            tail_sc[...] = st_ref[0]
    else:
        tpos = rows & (seg - 1)
    acc = jnp.zeros(x.shape, F32)
    for c0 in range(0, F, wck):
        cs = slice(c0, c0 + wck)
        gp = jnp.dot(h, wup_ref[:, c0:c0 + wck], preferred_element_type=F32)
        val = jnp.dot(h, wup_ref[:, F + c0:F + c0 + wck], preferred_element_type=F32)
        if carry:
            tail8 = tail_sc[:, cs]
            shifted = [_shift_carry(gp, k, tail8) for k in range(1, FFN_CONV)]
            tail_sc[:, cs] = gp[tm - SUBLANES:]
            fnew_ref[0, :, cs] = gp[tm - SUBLANES:]
        else:
            shifted = [_shift_seg(gp, k, st_ref[k - 1, :, cs], tpos) for k in range(1, FFN_CONV)]
            fnew_ref[0, :, cs] = gp
        gc = cb_ref[:, cs] + cw_ref[FFN_CONV - 1:FFN_CONV, cs] * gp
        for k in range(1, FFN_CONV):
            gc = gc + cw_ref[FFN_CONV - 1 - k:FFN_CONV - k, cs] * shifted[k - 1]
        act = (gc * _sigmoid(gc)) * val
        acc = acc + _bdot(act, wdn_ref[cs, :])
    y = x + gt_ref[0] * acc
    if final:
        y = (y * lax.rsqrt(jnp.mean(y * y, axis=-1, keepdims=True) + EPS)) * fg_ref[...]
    y_ref[0] = y


def _ffn(x, g, sc, sh, gt, st, wup, cw, cb, wdn, fg, tm, seg, final):
    G, Tg, D = x.shape
    F = wdn.shape[0]
    wck = F // 2 if (F // 2) % LANES == 0 else F
    tok = pl.BlockSpec((1, tm, D), lambda g_, t: (g_, t, 0))
    sm = sc.shape[1]
    if seg == 0:
        st_spec = pl.BlockSpec((1, SUBLANES, F), lambda g_, t: (g_, 0, 0))
        fn_spec = pl.BlockSpec((1, SUBLANES, F), lambda g_, t: (g_, 0, 0))
        fn_shape = (G, SUBLANES, F)
    else:
        st_spec = pl.BlockSpec(st.shape, lambda g_, t: (0, 0, 0))
        fn_spec = pl.BlockSpec((1, tm, F), lambda g_, t: (g_, t, 0))
        fn_shape = (G, Tg, F)
    return pl.pallas_call(
        functools.partial(_ffn_kernel, tm=tm, seg=seg, F=F, wck=wck, final=final),
        grid=(G, Tg // tm),
        in_specs=[tok, _const_spec((1, D)), _mod_spec(sm, tm, D), _mod_spec(sm, tm, D), _mod_spec(sm, tm, D),
                  st_spec, _const_spec(wup.shape), _const_spec(cw.shape), _const_spec(cb.shape),
                  _const_spec(wdn.shape), _const_spec((1, D))],
        out_specs=[tok, fn_spec],
        out_shape=[jax.ShapeDtypeStruct((G, Tg, D), F32), jax.ShapeDtypeStruct(fn_shape, F32)],
        scratch_shapes=[pltpu.VMEM((SUBLANES, F), F32)],
        compiler_params=_cparams(2),
        name="conv_ffn",
    )(x, g, sc, sh, gt, st, wup, cw, cb, wdn, fg)


def _head_perm():
    return np.array([(k * NSA_HPG + r) * NSA_HD + d
                     for r in range(NSA_HPG) for k in range(NSA_KV) for d in range(NSA_HD)], np.int32)


def _block_diag(w):
    n, a, b = w.shape[-3:]
    eye = jnp.eye(n, dtype=w.dtype)
    out = jnp.einsum('ij,...iab->...iajb', eye, w)
    return out.reshape(w.shape[:-3] + (n * a, n * b))


def _seg_fill(buf, k, seg):
    B, nb, C = buf.shape
    part = jnp.concatenate([buf[:, nb - k:, :], jnp.zeros((B, seg - k, C), buf.dtype)], axis=1)
    return part.reshape(B * seg, C)


def kernel(x_prompt, x_sample, cache_nsa, cache_nsa_win, state_ret, state_rglru_h, state_rglru_conv,
           state_ffn_conv, page_table, c_prompt, c_sample, norm1_g, norm2_g, w_ada, b_ada, w_in, cmp_pe,
           cmp_w1, cmp_b1, cmp_w2, ret_gn_g, rg_conv_w, rg_conv_b, rg_w_a, rg_b_a, rg_w_x, rg_b_x, rg_lambda,
           w_br_a, w_br_b, w_br_c, w_out, ffn_w_up, ffn_conv_w, ffn_conv_b, ffn_w_down, final_norm_g):
    B, T, D = x_prompt.shape
    DB, Q, _ = x_sample.shape
    L = w_in.shape[0]
    npages = page_table.shape[1]
    page = cache_nsa.shape[2]
    P = npages * page
    NQ = NSA_KV * NSA_HPG * NSA_HD
    NKV = NSA_KV * NSA_HD
    RW = RET_HEADS * RET_DK
    W = rg_conv_w.shape[2]
    F = ffn_w_down.shape[1]
    NS = DB * Q
    wbuf_len = cache_nsa_win.shape[2]
    assert Q == SUBLANES and T >= RG_CONV and P % CMP_STRIDE == 0

    mod = _ada(jnp.concatenate([c_prompt, c_sample], axis=0), w_ada, b_ada)
    cache_t = jnp.transpose(cache_nsa, (0, 1, 3, 4, 5, 2)).reshape(L, cache_nsa.shape[1], 4, NKV, page)
    win_t_all = jnp.transpose(cache_nsa_win, (0, 1, 3, 4, 5, 2)).reshape(L, DB, 2, NKV, wbuf_len)
    perm = _head_perm()

    half = RET_DK // 2
    freq = ROPE_BASE ** (-jnp.arange(half, dtype=F32) / half)

    def rope_tables(pos):
        ang = pos.astype(F32)[:, None] * freq[None, :]
        cos, sin = jnp.cos(ang), jnp.sin(ang)
        return jnp.concatenate([cos, cos], axis=1), jnp.concatenate([-sin, sin], axis=1)

    cos_p, sin_p = rope_tables(jnp.arange(T, dtype=jnp.int32))
    cos_s, sin_s = rope_tables(P + jnp.arange(Q, dtype=jnp.int32))

    widths = (NQ, 4 * NKV, 2 * NKV, LANES, RW, RW, RW, RW, W, W, 3 * D)
    dt_prompt = (F32,) * 9 + (BF16, BF16)
    dt_sample = (F32,) * 11
    offs = np.cumsum((0, NQ, 6 * NKV, 3 * NSA_KV * NSA_HPG, RW, RW, RW, RW, W, W, 3 * D))
    ngate = 3 * NSA_KV * NSA_HPG

    xp = x_prompt
    xs = x_sample.reshape(1, NS, D)
    outs_p = [[] for _ in range(6)]
    outs_s = [[] for _ in range(6)]
    tm_p = 256 if T % 256 == 0 else T

    for l in range(L):
        wi = w_in[l]
        w_cat = jnp.concatenate([
            wi[:, offs[0]:offs[1]][:, perm],
            wi[:, offs[1]:offs[1] + 4 * NKV],
            wi[:, offs[1] + 4 * NKV:offs[2]],
            jnp.pad(wi[:, offs[2]:offs[3]], ((0, 0), (0, LANES - ngate))),
            wi[:, offs[3]:]], axis=1).astype(BF16)
        w1 = cmp_w1[l]
        bd = lambda w: _block_diag(jnp.broadcast_to(w[:, :, None], w.shape[:2] + (NSA_KV,) + w.shape[2:]))
        cw1 = jnp.concatenate([bd(w1[:, :CMP_STRIDE]), bd(w1[:, CMP_STRIDE:])], axis=-1)
        cw1 = cw1.reshape(2, CMP_STRIDE * NKV, 4 * LANES).astype(BF16)
        cpe = jnp.tile(cmp_pe[l], (1, 1, NSA_KV)).reshape(2, 2, CMP_STRIDE * NKV)
        cb1 = jnp.tile(cmp_b1[l], (1, NSA_KV))[:, None, :]
        cw2 = _block_diag(jnp.broadcast_to(cmp_w2[l][:, None], (2, NSA_KV) + cmp_w2.shape[2:])).astype(BF16)
        cw = (cw1, cpe, cb1, cw2)
        rw = (rg_conv_w[l], rg_conv_b[l][None], _block_diag(rg_w_a[l]).astype(BF16), rg_b_a[l][None],
              _block_diag(rg_w_x[l]).astype(BF16), rg_b_x[l][None], rg_lambda[l][None])
        wa = w_br_a[l][perm].astype(BF16)
        wb = w_br_b[l].astype(BF16)
        wc = w_br_c[l].astype(BF16)
        wo = w_out[l].astype(BF16)
        wup = ffn_w_up[l].astype(BF16)
        wdn = ffn_w_down[l].astype(BF16)
        g1 = norm1_g[l][None]
        g2 = norm2_g[l][None]
        gn = ret_gn_g[l][None]
        fcw, fcb = ffn_conv_w[l], ffn_conv_b[l][None]
        fg = final_norm_g[None]
        final = l == L - 1

        m = [mod[l, :B, i * D:(i + 1) * D][:, None, :] for i in range(6)]
        (nq, rows, win, gate, rq, rk, rv, rg, rx, rgate, mg) = _inproj(xp, g1, m[1], m[0], w_cat, widths,
                                                                        dt_prompt, tm_p)
        kc, vc = _compress_prompt(rows, cw)
        o_a = _nsa_prompt(nq, gate, kc, vc, rows, win)
        o_r, s_new = _retention(rq, rk, rv, rg, cos_p, sin_p,
                                jnp.zeros((B, RET_HEADS, RET_DK, RET_DK), F32), gn, B, BF16)
        zs = jnp.zeros((B, SUBLANES, W), F32)
        o_c, h_tail = _rglru(rx, rgate, zs, zs, rw, tm_p, 0)
        x1 = _merge(xp, o_a, o_r, o_c, mg, m[2], wa, wb, wc, wo, tm_p)
        xp, f_tail = _ffn(x1, g2, m[4], m[3], m[5], jnp.zeros((B, SUBLANES, F), F32), wup, fcw, fcb, wdn, fg,
                          tm_p, 0, final)
        wn = min(WINDOW, T)
        outs_p[0].append(rows.reshape(B, T, 4, NSA_KV, NSA_HD))
        outs_p[1].append(win[:, T - wn:].reshape(B, wn, 2, NSA_KV, NSA_HD))
        outs_p[2].append(s_new)
        outs_p[3].append(h_tail[:, SUBLANES - 1])
        outs_p[4].append(rx[:, T - (RG_CONV - 1):])
        outs_p[5].append(f_tail[:, SUBLANES - (FFN_CONV - 1):])

        ms = [jnp.repeat(mod[l, B:, i * D:(i + 1) * D], Q, axis=0)[None] for i in range(6)]
        (nq, rows, win, gate, rq, rk, rv, rg, rx, rgate, mg) = _inproj(xs, g1, ms[1], ms[0], w_cat, widths,
                                                                        dt_sample, NS)
        r3 = lambda a: a.reshape(DB, Q, a.shape[-1])
        o_a = _nsa_sample(page_table, r3(nq), r3(gate), r3(rows), win_t_all[l], r3(win), cache_t, l, cw)
        o_r, s_new = _retention(r3(rq), r3(rk), r3(rv), r3(rg), cos_s, sin_s, state_ret[l].astype(F32), gn,
                                math.gcd(DB, SUBLANES), F32)
        cbuf = state_rglru_conv[l]
        st = jnp.stack([_seg_fill(cbuf, k, Q) for k in range(1, RG_CONV)])
        h0 = jnp.pad(state_rglru_h[l].astype(F32)[:, None, :], ((0, 0), (0, Q - 1), (0, 0))).reshape(1, NS, W)
        o_c, h_all = _rglru(rx, rgate, st, h0, rw, NS, Q)
        x1 = _merge(xs, o_a.reshape(1, NS, NQ), o_r.reshape(1, NS, RW), o_c, mg, ms[2], wa, wb, wc, wo, NS)
        fbuf = state_ffn_conv[l]
        fst = jnp.stack([_seg_fill(fbuf, k, Q) for k in range(1, FFN_CONV)])
        xs, g_all = _ffn(x1, g2, ms[4], ms[3], ms[5], fst, wup, fcw, fcb, wdn, fg, NS, Q, final)
        keys = jnp.concatenate([cache_nsa_win[l], r3(win).reshape(DB, Q, 2, NSA_KV, NSA_HD)], axis=1)
        outs_s[0].append(rows.reshape(DB, Q, 4, NSA_KV, NSA_HD))
        outs_s[1].append(keys[:, Q:])
        outs_s[2].append(s_new)
        outs_s[3].append(h_all.reshape(DB, Q, W)[:, Q - 1])
        outs_s[4].append(jnp.concatenate([cbuf, rx.reshape(DB, Q, W)], axis=1)[:, Q:])
        outs_s[5].append(jnp.concatenate([fbuf, g_all.reshape(DB, Q, F)], axis=1)[:, Q:])

    sp = [jnp.stack(a) for a in outs_p]
    ss = [jnp.stack(a) for a in outs_s]
    return (xp, xs.reshape(DB, Q, D), sp[0], ss[0], sp[1], ss[1], sp[2], ss[2],
            sp[3], ss[3], sp[4], ss[4], sp[5], ss[5])
```

```python
import functools
import math

import numpy as np
import jax
import jax.numpy as jnp
from jax import lax
from jax.experimental import pallas as pl
from jax.experimental.pallas import tpu as pltpu

F32 = jnp.float32
BF16 = jnp.bfloat16

NSA_KV = 2
NSA_HPG = 4
NSA_HD = 64
CMP_STRIDE = 16
CMP_BLOCK = 32
SEL_BLOCK = 64
SEL_TOPK = 16
SEL_FORCE = 1e4
WINDOW = 512
QBLOCK = 256
RET_HEADS = 4
RET_DK = 128
RET_CHUNK = 128
ROPE_BASE = 10000.0
RG_CONV = 4
RG_C = 8.0
FFN_CONV = 3
EPS = 1e-6

NEG = -1e30
BIG = float(2 ** 60)
LOG2E = 1.4426950408889634
SUBLANES = 8
LANES = 128
VMEM_LIMIT_V7X = 56 * 1024 * 1024
SEL_CHUNK = 512
CAST_CHUNK = 1024
PAGE_UNROLL = 8


def _cparams(n_grid):
    return pltpu.CompilerParams(dimension_semantics=("arbitrary",) * n_grid,
                                vmem_limit_bytes=VMEM_LIMIT_V7X)


def _bdot(a, b):
    return jnp.dot(a.astype(BF16), b.astype(BF16), preferred_element_type=F32)


def _bdot_nt(a, b):
    return lax.dot_general(a.astype(BF16), b.astype(BF16), (((1,), (1,)), ((), ())),
                           preferred_element_type=F32)


def _split(a):
    hi = a.astype(BF16)
    lo = (a - hi.astype(F32)).astype(BF16)
    return hi, lo


def _dot3(a, b):
    ah, al = _split(a)
    bh, bl = _split(b)
    d = functools.partial(jnp.dot, preferred_element_type=F32)
    return d(ah, bh) + d(al, bh) + d(ah, bl)


def _sigmoid(x):
    return 1.0 / (1.0 + jnp.exp(-x))


def _gelu(x):
    return 0.5 * x * (1.0 + jnp.tanh(0.7978845608028654 * (x + 0.044715 * (x * x * x))))


def _rms_mod(x, g, sc, sh):
    y = x * lax.rsqrt(jnp.mean(x * x, axis=-1, keepdims=True) + EPS)
    return (y * g) * (1.0 + sc) + sh


def _ada_kernel(c_ref, w_ref, b_ref, o_ref):
    c = c_ref[...]
    o_ref[0] = _dot3(c * _sigmoid(c), w_ref[0]) + b_ref[0]


def _ada(c_all, w_ada, b_ada):
    L, D, E = w_ada.shape
    n = c_all.shape[0]
    tn = 1536 if E % 1536 == 0 else E
    return pl.pallas_call(
        _ada_kernel,
        grid=(L, E // tn),
        in_specs=[pl.BlockSpec((n, D), lambda l, j: (0, 0)),
                  pl.BlockSpec((1, D, tn), lambda l, j: (l, 0, j)),
                  pl.BlockSpec((1, 1, tn), lambda l, j: (l, 0, j))],
        out_specs=pl.BlockSpec((1, n, tn), lambda l, j: (l, 0, j)),
        out_shape=jax.ShapeDtypeStruct((L, n, E), F32),
        compiler_params=_cparams(2),
        name="ada_mod",
    )(c_all, w_ada, b_ada.reshape(L, 1, E))


class _LayerOf:
    def __init__(self, arr, layer):
        self.arr, self.layer, self.shape = arr, layer, arr.shape[1:]


class _ModOf:
    def __init__(self, arr, layer, idx):
        self.arr, self.layer, self.idx = arr, layer, idx


def _mod_spec(m, tm, d):
    l, i = m.layer, m.idx
    if m.arr.ndim == 4:
        return pl.BlockSpec((None, None, 1, d), lambda g, t: (l, g, 0, i))
    return pl.BlockSpec((None, tm, d), lambda g, t: (l, t, i))


def _pspec(p):
    if isinstance(p, _LayerOf):
        nd, l = len(p.shape), p.layer
        return pl.BlockSpec((None,) + tuple(p.shape), lambda *a: (l,) + (0,) * nd, pipeline_mode=pl.Buffered(1))
    nd = p.ndim
    return pl.BlockSpec(p.shape, lambda *a: (0,) * nd, pipeline_mode=pl.Buffered(1))


def _parg(p):
    return p.arr if isinstance(p, _LayerOf) else p


def _inproj_kernel(x_ref, g_ref, sc_ref, sh_ref, w_ref, *o_refs, segs):
    h = _rms_mod(x_ref[0], g_ref[...], sc_ref[...], sh_ref[...]).astype(BF16)
    for (off, wd), o_ref in zip(segs, o_refs):
        o_ref[0] = jnp.dot(h, w_ref[:, off:off + wd], preferred_element_type=F32).astype(o_ref.dtype)


def _inproj(x, g, sc, sh, w, widths, dtypes, tm):
    G, Tg, D = x.shape
    segs, off = [], 0
    for wd in widths:
        segs.append((off, wd))
        off += wd
    return pl.pallas_call(
        functools.partial(_inproj_kernel, segs=tuple(segs)),
        grid=(G, Tg // tm),
        in_specs=[pl.BlockSpec((1, tm, D), lambda g_, t: (g_, t, 0)),
                  _pspec(g), _mod_spec(sc, tm, D), _mod_spec(sh, tm, D), _pspec(w)],
        out_specs=[pl.BlockSpec((1, tm, wd), lambda g_, t: (g_, t, 0)) for wd in widths],
        out_shape=[jax.ShapeDtypeStruct((G, Tg, wd), dt) for wd, dt in zip(widths, dtypes)],
        compiler_params=_cparams(2),
        name="in_proj",
    )(x, _parg(g), sc.arr, sh.arr, _parg(w))


def _compress_bias(w1_ref, pe_ref, bias_sc):
    for c in range(2):
        halves = []
        for h in range(2):
            pe_rows = jnp.broadcast_to(pe_ref[c, h:h + 1, :], (SUBLANES, pe_ref.shape[2]))
            halves.append(_bdot(pe_rows, w1_ref[c, :, h * 2 * LANES:(h + 1) * 2 * LANES]))
        bias_sc[c] = jnp.concatenate(halves, axis=1)


def _compress_x(xrefs, nch, w1_ref, bias_sc, b1_ref, w2_ref):
    last = lax.broadcasted_iota(jnp.int32, (nch, 1), 0) == nch - 1
    outs = []
    for c in range(2):
        lhs = jnp.concatenate([xrefs[c][pl.ds(p, nch, stride=CMP_STRIDE), :].astype(BF16)
                               for p in range(CMP_STRIDE)], axis=1)
        acc = jnp.dot(lhs, w1_ref[c], preferred_element_type=F32) + bias_sc[c, 0:1, :]
        lo = acc[:, :2 * LANES]
        hi = acc[:, 2 * LANES:]
        hi_next = jnp.where(last, 0.0, pltpu.roll(hi, nch - 1, 0))
        hid = _gelu(lo + hi_next + b1_ref[c])
        outs.append(_bdot(hid, w2_ref[c]))
    return outs


def _compress_prompt_kernel(krows_ref, vrows_ref, w1_ref, pe_ref, b1_ref, w2_ref, kc_ref, vc_ref, bias_sc, *, nch):
    @pl.when(pl.program_id(0) == 0)
    def _():
        _compress_bias(w1_ref, pe_ref, bias_sc)

    kc, vc = _compress_x((krows_ref.at[0], vrows_ref.at[0]), nch, w1_ref, bias_sc, b1_ref, w2_ref)
    kc_ref[0] = kc
    vc_ref[0] = vc


def _compress_prompt(rows, cw):
    B, T, _ = rows.shape
    nch = T // CMP_STRIDE
    w1, pe, b1, w2 = cw
    return pl.pallas_call(
        functools.partial(_compress_prompt_kernel, nch=nch),
        grid=(B,),
        in_specs=[pl.BlockSpec((1, T, LANES), lambda b: (b, 0, 0)),
                  pl.BlockSpec((1, T, LANES), lambda b: (b, 0, 1)),
                  _pspec(w1), _pspec(pe), _pspec(b1), _pspec(w2)],
        out_specs=[pl.BlockSpec((1, nch, LANES), lambda b: (b, 0, 0))] * 2,
        out_shape=[jax.ShapeDtypeStruct((B, nch, LANES), F32)] * 2,
        scratch_shapes=[pltpu.VMEM((2, SUBLANES, 4 * LANES), F32)],
        compiler_params=_cparams(1),
        name="nsa_compress_prompt",
    )(rows, rows, _parg(w1), _parg(pe), _parg(b1), _parg(w2))


def _rep_all(a):
    return jnp.concatenate([a] * (NSA_KV * NSA_HPG), axis=0)


def _rep_heads(a, Qb):
    return jnp.concatenate([a[:Qb]] * NSA_HPG + [a[Qb:]] * NSA_HPG, axis=0)


def _nsa_front(qblk, q0, Qb, kc, vc, n_cmp, n_sel):
    R2 = 2 * Qb
    npad = kc.shape[0]
    lane = lax.broadcasted_iota(jnp.int32, (Qb, LANES), 1)
    lo_half = lane < NSA_HD
    scale = NSA_HD ** -0.5 * LOG2E
    pieces = []
    for k in range(NSA_KV):
        for r in range(NSA_HPG):
            sl = qblk[:, r * LANES:(r + 1) * LANES] * scale
            pieces.append(jnp.where(lo_half if k == 0 else jnp.logical_not(lo_half), sl, 0.0))
    qs = jnp.concatenate(pieces, axis=0).astype(BF16)
    qp1 = q0 + lax.broadcasted_iota(jnp.int32, (Qb, 1), 0)

    n_idx = lax.broadcasted_iota(jnp.int32, (1, npad), 1)
    visible = (n_idx * CMP_STRIDE + (CMP_BLOCK - 1) <= qp1) & (n_idx < n_cmp)
    s = _bdot_nt(qs, kc) + _rep_all(jnp.where(visible, 0.0, -BIG))
    e = jnp.exp2(s - jnp.max(s, axis=-1, keepdims=True))
    any_visible = _rep_all((qp1 >= CMP_BLOCK - 1) & (n_cmp > 0))
    p_c = e * jnp.where(any_visible, 1.0 / jnp.sum(e, axis=-1, keepdims=True), 0.0)
    o_c = _bdot(p_c, vc)

    psum = []
    for k in range(NSA_KV):
        acc = p_c[(k * NSA_HPG) * Qb:(k * NSA_HPG + 1) * Qb]
        for r in range(1, NSA_HPG):
            acc = acc + p_c[(k * NSA_HPG + r) * Qb:(k * NSA_HPG + r + 1) * Qb]
        psum.append(acc)
    psum = jnp.concatenate(psum + [jnp.zeros((LANES - R2, npad), F32)], axis=0)
    nsr = -(-n_sel // SUBLANES) * SUBLANES
    sj = lax.broadcasted_iota(jnp.int32, (nsr, npad), 0) * SEL_BLOCK
    ci = lax.broadcasted_iota(jnp.int32, (nsr, npad), 1) * CMP_STRIDE
    selmap = jnp.where((ci < sj + SEL_BLOCK) & (ci + CMP_BLOCK > sj), 1.0, 0.0).astype(BF16)
    ph, plo = _split(psum)
    imp = _bdot_nt(selmap, ph) + _bdot_nt(selmap, plo)
    col_pos = q0 + (lax.broadcasted_iota(jnp.int32, (1, LANES), 1) & (Qb - 1))
    return qs, qp1, o_c, _select_blocks(imp, col_pos, n_sel)


def _select_blocks(imp, qpos, n_sel):
    nsr, ncol = imp.shape
    j = lax.broadcasted_iota(jnp.int32, (nsr, ncol), 0)
    jf = j.astype(F32)
    cur = qpos >> 6
    forced = (j == 0) | (j == cur) | (j == cur - 1)
    imp = jnp.where(forced, SEL_FORCE, imp)
    imp = jnp.where(j * SEL_BLOCK <= qpos, imp, -SEL_FORCE)
    imp = jnp.where(j < n_sel, imp, NEG)

    def pick(_, carry):
        imp_c, sel_c = carry
        m = jnp.max(imp_c, axis=0, keepdims=True)
        first = jnp.min(jnp.where(imp_c == m, jf, float(nsr)), axis=0, keepdims=True)
        hit = jf == first
        return jnp.where(hit, NEG, imp_c), jnp.where(hit, 1.0, sel_c)

    _, sel = lax.fori_loop(0, min(SEL_TOPK, n_sel), pick, (imp, jnp.zeros((nsr, ncol), F32)))
    return sel


def _nsa_combine(gate, o_c, o_s, o_w, Qb, o_ref):
    lo_half = lax.broadcasted_iota(jnp.int32, (Qb, LANES), 1) < NSA_HD
    g = _sigmoid(gate)
    for r in range(NSA_HPG):
        halves = []
        for k in range(NSA_KV):
            rs = slice((k * NSA_HPG + r) * Qb, (k * NSA_HPG + r + 1) * Qb)
            c = (k * NSA_HPG + r) * 3
            halves.append(g[:, c:c + 1] * o_c[rs] + g[:, c + 1:c + 2] * o_s[rs] + g[:, c + 2:c + 3] * o_w[rs])
        o_ref[0, :, r * LANES:(r + 1) * LANES] = jnp.where(lo_half, halves[0], halves[1])


def _block_columns(k0, n):
    key = k0 + lax.broadcasted_iota(jnp.int32, (n, LANES), 0)
    blk = lax.broadcasted_iota(jnp.int32, (n, LANES), 1)
    return jnp.where((key >> 6) == blk, BIG, 0.0).astype(BF16)


def _block_rows(k0, n):
    key = k0 + lax.broadcasted_iota(jnp.int32, (LANES, n), 1)
    blk = lax.broadcasted_iota(jnp.int32, (LANES, n), 0)
    return jnp.where((key >> 6) == blk, BIG, 0.0).astype(BF16)


def _lanes_all(a):
    return jnp.concatenate([a] * (NSA_KV * NSA_HPG), axis=1)


def _lanes_heads(a, Qb):
    return jnp.concatenate([a[:, :Qb]] * NSA_HPG + [a[:, Qb:]] * NSA_HPG, axis=1)


def _softmax_cols(s):
    e = jnp.exp2(s - jnp.max(s, axis=0, keepdims=True))
    return e, 1.0 / jnp.sum(e, axis=0, keepdims=True)


def _nsa_prompt_kernel(q_ref, gate_ref, kc_ref, vc_ref, rows_ref, win_ref, o_ref, kaug, vt,
                       *, T, Qb, n_cmp, n_sel, wl):
    i = pl.program_id(1)
    q0 = i * Qb
    R = NSA_KV * NSA_HPG * Qb
    R2 = NSA_KV * Qb

    @pl.when(i == 0)
    def _():
        def pack(c, carry):
            r0 = pl.multiple_of(c * SEL_CHUNK, SEL_CHUNK)
            kaug[pl.ds(r0, SEL_CHUNK), 0:LANES] = rows_ref[0, pl.ds(r0, SEL_CHUNK), 2 * LANES:3 * LANES].astype(BF16)
            kaug[pl.ds(r0, SEL_CHUNK), LANES:2 * LANES] = _block_columns(r0, SEL_CHUNK)
            vt[:, pl.ds(r0, SEL_CHUNK)] = rows_ref[0, pl.ds(r0, SEL_CHUNK), 3 * LANES:4 * LANES].T.astype(BF16)
            return carry
        lax.fori_loop(0, T // SEL_CHUNK, pack, 0)

    top = lax.broadcasted_iota(jnp.int32, (LANES, Qb), 0) < NSA_HD
    scale = NSA_HD ** -0.5 * LOG2E
    qblk = q_ref[0]
    q_t = [(qblk[:, r * LANES:(r + 1) * LANES] * scale).T for r in range(NSA_HPG)]
    qs = jnp.concatenate([jnp.where(top if k == 0 else jnp.logical_not(top), q_t[r], 0.0)
                          for k in range(NSA_KV) for r in range(NSA_HPG)], axis=1).astype(BF16)
    qlane = q0 + lax.broadcasted_iota(jnp.int32, (1, Qb), 1)
    qpos = _lanes_all(qlane)

    kc = kc_ref[0]
    npad = kc.shape[0]
    n_idx = lax.broadcasted_iota(jnp.int32, (npad, 1), 0)
    visible = (n_idx * CMP_STRIDE + (CMP_BLOCK - 1) <= qlane) & (n_idx < n_cmp)
    e_c, inv_c = _softmax_cols(_bdot(kc, qs) + _lanes_all(jnp.where(visible, 0.0, -BIG)))
    any_visible = _lanes_all((qlane >= CMP_BLOCK - 1) & (n_cmp > 0))
    p_c = e_c * jnp.where(any_visible, inv_c, 0.0)
    o_c = _bdot(vc_ref[0].T, p_c)

    psum = []
    for k in range(NSA_KV):
        acc = p_c[:, (k * NSA_HPG) * Qb:(k * NSA_HPG + 1) * Qb]
        for r in range(1, NSA_HPG):
            acc = acc + p_c[:, (k * NSA_HPG + r) * Qb:(k * NSA_HPG + r + 1) * Qb]
        psum.append(acc)
    psum = jnp.concatenate(psum, axis=1)
    nsr = -(-n_sel // SUBLANES) * SUBLANES
    sj = lax.broadcasted_iota(jnp.int32, (nsr, npad), 0) * SEL_BLOCK
    ci = lax.broadcasted_iota(jnp.int32, (nsr, npad), 1) * CMP_STRIDE
    selmap = jnp.where((ci < sj + SEL_BLOCK) & (ci + CMP_BLOCK > sj), 1.0, 0.0).astype(BF16)
    ph, plo = _split(psum)
    imp = (jnp.dot(selmap, ph, preferred_element_type=F32)
           + jnp.dot(selmap, plo, preferred_element_type=F32))
    sel = _select_blocks(imp, jnp.concatenate([qlane] * NSA_KV, axis=1), n_sel)
    selm =jnp.concatenate([sel - 1.0, jnp.zeros((LANES - nsr, R2), F32)], axis=0) if nsr < LANES else sel - 1.0
    qaug = jnp.concatenate([qs, _lanes_heads(selm, Qb).astype(BF16)], axis=0)

    def update(s, vcols, carry):
        m, l, acc = carry
        m_new = jnp.maximum(m, jnp.max(s, axis=0, keepdims=True))
        alpha = jnp.exp2(m - m_new)
        p = jnp.exp2(s - m_new)
        l = alpha * l + jnp.sum(p, axis=0, keepdims=True)
        acc = alpha * acc + jnp.dot(vcols, p.astype(BF16), preferred_element_type=F32)
        return m_new, l, acc

    def scores(k0):
        return jnp.dot(kaug[pl.ds(k0, SEL_CHUNK), :], qaug, preferred_element_type=F32)

    def full_chunk(c, carry):
        k0 = pl.multiple_of(c * SEL_CHUNK, SEL_CHUNK)
        return update(scores(k0), vt[:, pl.ds(k0, SEL_CHUNK)], carry)

    n_full = q0 // SEL_CHUNK
    init = (jnp.full((1, R), -4.0 * BIG, F32), jnp.zeros((1, R), F32), jnp.zeros((LANES, R), F32))
    carry = lax.fori_loop(0, n_full, full_chunk, init)
    k0 = pl.multiple_of(n_full * SEL_CHUNK, SEL_CHUNK)
    kpos = k0 + lax.broadcasted_iota(jnp.int32, (SEL_CHUNK, 1), 0)
    s_diag = jnp.where(kpos <= qpos, scores(k0), -2.0 * BIG)
    _, l_s, acc_s = update(s_diag, vt[:, pl.ds(k0, SEL_CHUNK)], carry)
    o_s = acc_s * (1.0 / l_s)

    ws = pl.multiple_of(jnp.maximum(q0 - WINDOW, 0), LANES)
    wk = win_ref[0, pl.ds(ws, wl), 0:LANES]
    wv = win_ref[0, pl.ds(ws, wl), LANES:2 * LANES]
    dpos = qlane - (ws + lax.broadcasted_iota(jnp.int32, (wl, 1), 0))
    e_w, inv_w = _softmax_cols(_bdot(wk, qs) + _lanes_all(jnp.where((dpos >= 0) & (dpos <= WINDOW), 0.0, -BIG)))
    o_w = _bdot(wv.T, e_w) * inv_w

    g = _sigmoid(gate_ref[0]).T
    for r in range(NSA_HPG):
        halves = []
        for k in range(NSA_KV):
            cs = slice((k * NSA_HPG + r) * Qb, (k * NSA_HPG + r + 1) * Qb)
            c = (k * NSA_HPG + r) * 3
            halves.append(g[c:c + 1, :] * o_c[:, cs] + g[c + 1:c + 2, :] * o_s[:, cs] + g[c + 2:c + 3, :] * o_w[:, cs])
        o_ref[0, :, r * LANES:(r + 1) * LANES] = jnp.where(top, halves[0], halves[1]).T.astype(o_ref.dtype)


def _nsa_prompt(nq, gate, kc, vc, rows, win):
    B, T, HD = nq.shape
    Qb = QBLOCK
    nch = kc.shape[1]
    n_sel = -(-T // SEL_BLOCK)
    wl = WINDOW + Qb
    assert T % SEL_CHUNK == 0 and T >= wl and SEL_TOPK <= n_sel <= LANES
    return pl.pallas_call(
        functools.partial(_nsa_prompt_kernel, T=T, Qb=Qb, n_cmp=nch - 1, n_sel=n_sel, wl=wl),
        grid=(B, T // Qb),
        in_specs=[pl.BlockSpec((1, Qb, HD), lambda b, i: (b, i, 0)),
                  pl.BlockSpec((1, Qb, LANES), lambda b, i: (b, i, 0)),
                  pl.BlockSpec((1, nch, LANES), lambda b, i: (b, 0, 0)),
                  pl.BlockSpec((1, nch, LANES), lambda b, i: (b, 0, 0)),
                  pl.BlockSpec((1, T, 4 * LANES), lambda b, i: (b, 0, 0)),
                  pl.BlockSpec((1, T, 2 * LANES), lambda b, i: (b, 0, 0))],
        out_specs=pl.BlockSpec((1, Qb, HD), lambda b, i: (b, i, 0)),
        out_shape=jax.ShapeDtypeStruct((B, T, HD), BF16),
        scratch_shapes=[pltpu.VMEM((T, 2 * LANES), BF16), pltpu.VMEM((LANES, T), BF16)],
        compiler_params=_cparams(2),
        name="nsa_attn_prompt",
    )(nq, gate, kc, vc, rows, win)


def _pages_copy(cache_hbm, layer, page, r0, dst, j, sem):
    n = cache_hbm.shape[-1]
    return pltpu.make_async_copy(cache_hbm.at[layer, page, pl.ds(r0, 2)],
                                 dst.at[:, :, pl.ds(pl.multiple_of(j * n, n), n)], sem)


def _pages_start(pt_ref, b, cache_hbm, layer, r0, dst, sem, npages):
    def issue(j, carry):
        _pages_copy(cache_hbm, layer, pt_ref[b, j], r0, dst, j, sem).start()
        return carry
    lax.fori_loop(0, npages, issue, 0, unroll=PAGE_UNROLL)


def _pages_wait(cache_hbm, layer, r0, dst, sem, npages):
    def wait(j, carry):
        _pages_copy(cache_hbm, layer, 0, r0, dst, j, sem).wait()
        return carry
    lax.fori_loop(0, npages, wait, 0, unroll=PAGE_UNROLL)


def _softmax2(s1, s2, mask2):
    s2 = jnp.where(mask2, s2, -2.0 * BIG)
    m = jnp.maximum(jnp.max(s1, axis=-1, keepdims=True), jnp.max(s2, axis=-1, keepdims=True))
    e1 = jnp.exp2(s1 - m)
    e2 = jnp.exp2(s2 - m)
    den = jnp.sum(e1, axis=-1, keepdims=True) + jnp.sum(e2, axis=-1, keepdims=True)
    return e1, e2, 1.0 / den


def _nsa_sample_kernel(pt_ref, q_ref, gate_ref, rows_ref, wt_ref, wnew_ref, cache_hbm,
                       w1_ref, pe_ref, b1_ref, w2_ref, o_ref,
                       cmpbuf, selbuf, xk, xv, kaug, vt, newbuf, wnewbuf, bias_sc, csem, ssem,
                       *, layer, npages, P, Q, wb, nb, n_sel):
    b = pl.program_id(0)
    nseq = pl.num_programs(0)
    page = cache_hbm.shape[-1]
    nch = P // CMP_STRIDE

    @pl.when(b == 0)
    def _():
        _pages_start(pt_ref, 0, cache_hbm, layer, 0, cmpbuf, csem, npages)
        _pages_start(pt_ref, 0, cache_hbm, layer, 2, selbuf, ssem, npages)
        _compress_bias(w1_ref, pe_ref, bias_sc)

        def blocks(c, carry):
            c0 = pl.multiple_of(c * CAST_CHUNK, CAST_CHUNK)
            kaug[LANES:2 * LANES, pl.ds(c0, CAST_CHUNK)] = _block_rows(c0, CAST_CHUNK)
            return carry
        lax.fori_loop(0, P // CAST_CHUNK, blocks, 0)
        newbuf[...] = jnp.zeros(newbuf.shape, F32)
        wnewbuf[...] = jnp.zeros(wnewbuf.shape, F32)

    _pages_wait(cache_hbm, layer, 0, cmpbuf, csem, npages)

    for c, dst in enumerate((xk, xv)):
        for j in range(npages):
            dst[j * page:(j + 1) * page, :] = cmpbuf[c, :, j * page:(j + 1) * page].T
    kc, vc = _compress_x((xk, xv), nch, w1_ref, bias_sc, b1_ref, w2_ref)

    @pl.when(b + 1 < nseq)
    def _():
        _pages_start(pt_ref, b + 1, cache_hbm, layer, 0, cmpbuf, csem, npages)

    _pages_wait(cache_hbm, layer, 2, selbuf, ssem, npages)

    def pack(c, carry):
        c0 = pl.multiple_of(c * CAST_CHUNK, CAST_CHUNK)
        kaug[0:LANES, pl.ds(c0, CAST_CHUNK)] = selbuf[0, :, pl.ds(c0, CAST_CHUNK)].astype(BF16)
        vt[:, pl.ds(c0, CAST_CHUNK)] = selbuf[1, :, pl.ds(c0, CAST_CHUNK)].astype(BF16)
        return carry
    lax.fori_loop(0, P // CAST_CHUNK, pack, 0)

    @pl.when(b + 1 < nseq)
    def _():
        _pages_start(pt_ref, b + 1, cache_hbm, layer, 2, selbuf, ssem, npages)

    newbuf[0:Q, :] = rows_ref[0, :, 2 * LANES:4 * LANES]
    wnewbuf[0:Q, :] = wnew_ref[0]

    qs, qp1, o_c, sel_t = _nsa_front(q_ref[0], P, Q, kc, vc, nch - 1, n_sel)
    qpos = _rep_all(qp1)
    lane = lax.broadcasted_iota(jnp.int32, (1, LANES), 1)
    new_pos = P + lane
    is_new = lane < Q

    def as_rows(blk):
        if blk.shape[0] < LANES:
            blk = jnp.concatenate([blk, jnp.zeros((LANES - blk.shape[0], LANES), F32)], axis=0)
        return blk.T[0:NSA_KV * Q]
    sel_rows = as_rows(sel_t[0:min(LANES, sel_t.shape[0])])
    sel_new = as_rows(sel_t[nb:nb + SUBLANES])[:, 0:1]

    qaug = jnp.concatenate([qs, _rep_heads(sel_rows - 1.0, Q).astype(BF16)], axis=1)
    s_past = jnp.dot(qaug, kaug[...], preferred_element_type=F32)
    s_new = _bdot_nt(qs, newbuf[:, 0:LANES])
    new_ok = is_new & (new_pos <= qpos) & (_rep_heads(sel_new, Q) > 0.5)
    e1, e2, inv = _softmax2(s_past, s_new, new_ok)
    o_s = (lax.dot_general(e1.astype(BF16), vt[...], (((1,), (1,)), ((), ())), preferred_element_type=F32)
           + _bdot(e2, newbuf[:, LANES:2 * LANES])) * inv

    dpast = qp1 - ((P - wb) + lax.broadcasted_iota(jnp.int32, (1, wb), 1))
    s_wp = (jnp.dot(qs, wt_ref[0, 0].astype(BF16), preferred_element_type=F32)
            + _rep_all(jnp.where((dpast >= 0) & (dpast <= WINDOW), 0.0, -BIG)))
    s_wn = _bdot_nt(qs, wnewbuf[:, 0:LANES])
    dnew = qpos - new_pos
    e1, e2, inv = _softmax2(s_wp, s_wn, is_new & (dnew >= 0) & (dnew <= WINDOW))
    o_w = (lax.dot_general(e1.astype(BF16), wt_ref[0, 1].astype(BF16), (((1,), (1,)), ((), ())),
                           preferred_element_type=F32)
           + _bdot(e2, wnewbuf[:, LANES:2 * LANES])) * inv

    _nsa_combine(gate_ref[0], o_c, o_s, o_w, Q, o_ref)


def _nsa_sample(page_table, nq, gate, rows, win_t, win_new, cache_t, layer, cw):
    DB, Q, HD = nq.shape
    npages = page_table.shape[1]
    page = cache_t.shape[-1]
    P = npages * page
    wb = win_t.shape[-1]
    nb = P // SEL_BLOCK
    n_sel = -(-(P + Q) // SEL_BLOCK)
    w1, pe, b1, w2 = cw
    assert Q == SUBLANES and P % SEL_BLOCK == 0 and Q <= SEL_BLOCK and nb <= LANES and n_sel >= SEL_TOPK
    assert P % CAST_CHUNK == 0 and (P + Q) // CMP_STRIDE == P // CMP_STRIDE and page == LANES
    bs = lambda shape: pl.BlockSpec((1,) + shape, lambda b, pt: (b,) + (0,) * len(shape))
    win_spec = pl.BlockSpec((None, 1, 2, LANES, wb), lambda b, pt: (layer, b, 0, 0, 0))
    grid_spec = pltpu.PrefetchScalarGridSpec(
        num_scalar_prefetch=1, grid=(DB,),
        in_specs=[bs((Q, HD)), bs((Q, LANES)), bs((Q, 4 * LANES)), win_spec, bs((Q, 2 * LANES)),
                  pl.BlockSpec(memory_space=pl.ANY),
                  _pspec(w1), _pspec(pe), _pspec(b1), _pspec(w2)],
        out_specs=bs((Q, HD)),
        scratch_shapes=[pltpu.VMEM((2, LANES, P), F32), pltpu.VMEM((2, LANES, P), F32),
                        pltpu.VMEM((P, LANES), F32), pltpu.VMEM((P, LANES), F32),
                        pltpu.VMEM((2 * LANES, P), BF16), pltpu.VMEM((LANES, P), BF16),
                        pltpu.VMEM((LANES, 2 * LANES), F32), pltpu.VMEM((LANES, 2 * LANES), F32),
                        pltpu.VMEM((2, SUBLANES, 4 * LANES), F32),
                        pltpu.SemaphoreType.DMA(()), pltpu.SemaphoreType.DMA(())])
    return pl.pallas_call(
        functools.partial(_nsa_sample_kernel, layer=layer, npages=npages, P=P, Q=Q, wb=wb, nb=nb, n_sel=n_sel),
        grid_spec=grid_spec,
        out_shape=jax.ShapeDtypeStruct((DB, Q, HD), F32),
        compiler_params=_cparams(1),
        name="nsa_sample",
    )(page_table, nq, gate, rows, win_t, win_new, cache_t, _parg(w1), _parg(pe), _parg(b1), _parg(w2))


def _ret_kernel(q_ref, k_ref, v_ref, g_ref, cos_ref, sin_ref, s0_ref, gn_ref, o_ref, snew_ref, s_sc,
                *, C, nC, nseq):
    c = pl.program_id(1)

    @pl.when(c == 0)
    def _():
        s_sc[...] = s0_ref[...]

    cosf = cos_ref[...]
    sinf = sin_ref[...]
    diff = (lax.broadcasted_iota(jnp.int32, (C, C), 0) - lax.broadcasted_iota(jnp.int32, (C, C), 1)).astype(F32)
    ii = lax.broadcasted_iota(jnp.int32, (C, 1), 0).astype(F32)
    half = RET_DK // 2
    for h in range(RET_HEADS):
        lg = math.log(1.0 - 2.0 ** (-5.0 - h))
        hs = slice(h * RET_DK, (h + 1) * RET_DK)
        decay = jnp.where(diff >= 0, jnp.exp(jnp.maximum(diff, 0.0) * lg), 0.0)
        cross = jnp.exp((ii + 1.0) * lg)
        kweight = jnp.exp((C - 1.0 - ii) * lg)
        for b in range(nseq):
            q = q_ref[b, :, hs]
            k = k_ref[b, :, hs]
            v = v_ref[b, :, hs]
            qr = q * cosf + pltpu.roll(q, half, 1) * sinf
            kr = (k * cosf + pltpu.roll(k, half, 1) * sinf) * (RET_DK ** -0.5)
            o_inner = _bdot(_bdot_nt(qr, kr) * decay, v)
            s_old = s_sc[b, h]
            o_cross = _bdot(qr, s_old) * cross
            kv = lax.dot_general((kr * kweight).astype(BF16), v.astype(BF16), (((0,), (0,)), ((), ())),
                                 preferred_element_type=F32)
            s_sc[b, h] = math.exp(C * lg) * s_old + kv
            o = o_inner + o_cross
            mu = jnp.mean(o, axis=-1, keepdims=True)
            var = jnp.mean(jnp.square(o - mu), axis=-1, keepdims=True)
            gate = g_ref[b, :, hs]
            o_ref[b, :, hs] = (((o - mu) * lax.rsqrt(var + EPS)) * gn_ref[:, hs]
                               * (gate * _sigmoid(gate))).astype(o_ref.dtype)

    @pl.when(c == nC - 1)
    def _():
        snew_ref[...] = s_sc[...]


def _retention(rq, rk, rv, rg, cosf, sinf, s0, gn, nseq, out_dtype):
    B, T, W = rq.shape
    C = RET_CHUNK if (T >= RET_CHUNK and T % RET_CHUNK == 0) else T
    nC = T // C
    tok = pl.BlockSpec((nseq, C, W), lambda b, c: (b, c, 0))
    tab = pl.BlockSpec((C, RET_DK), lambda b, c: (c, 0))
    st = pl.BlockSpec((nseq,) + s0.shape[1:], lambda b, c: (b, 0, 0, 0))
    return pl.pallas_call(
        functools.partial(_ret_kernel, C=C, nC=nC, nseq=nseq),
        grid=(B // nseq, nC),
        in_specs=[tok, tok, tok, tok, tab, tab, st, _pspec(gn)],
        out_specs=[tok, st],
        out_shape=[jax.ShapeDtypeStruct((B, T, W), out_dtype), jax.ShapeDtypeStruct(s0.shape, F32)],
        scratch_shapes=[pltpu.VMEM((nseq,) + s0.shape[1:], F32)],
        compiler_params=_cparams(2),
        name="retention",
    )(rq, rk, rv, rg, cosf, sinf, s0, _parg(gn))


def _shift_carry(x, k, tail8):
    r = pltpu.roll(x, k, 0)
    row8 = lax.broadcasted_iota(jnp.int32, (SUBLANES, 1), 0)
    first = jnp.where(row8 >= k, r[:SUBLANES], pltpu.roll(tail8, k, 0))
    return jnp.concatenate([first, r[SUBLANES:]], axis=0)


def _shift_seg(x, k, fill, tpos):
    return jnp.where(tpos >= k, pltpu.roll(x, k, 0), fill)


def _rglru_kernel(x_ref, gate_ref, st_ref, h0_ref, cw_ref, cb_ref, wa_ref, ba_ref, wx_ref, bx_ref, lam_ref,
                  o_ref, h_ref, tail_sc, h_sc, *, tm, seg):
    carry = seg == 0
    x = x_ref[0]
    rows = lax.broadcasted_iota(jnp.int32, (tm, 1), 0)
    if carry:
        @pl.when(pl.program_id(1) == 0)
        def _():
            tail_sc[...] = st_ref[0]
            h_sc[...] = h0_ref[0]
        tail8 = tail_sc[...]
        shifted = [_shift_carry(x, k, tail8) for k in range(1, RG_CONV)]
        tpos = rows
        seglen = tm
        h_in = jnp.where(rows == 0, h_sc[SUBLANES - 1:SUBLANES, :], 0.0)
    else:
        tpos = rows & (seg - 1)
        shifted = [_shift_seg(x, k, st_ref[k - 1], tpos) for k in range(1, RG_CONV)]
        seglen = seg
        h_in = h0_ref[0]
    xc = cb_ref[...] + cw_ref[RG_CONV - 1:RG_CONV, :] * x
    for k in range(1, RG_CONV):
        xc = xc + cw_ref[RG_CONV - 1 - k:RG_CONV - k, :] * shifted[k - 1]
    r = _sigmoid(_bdot(xc, wa_ref[...]) + ba_ref[...])
    i = _sigmoid(_bdot(xc, wx_ref[...]) + bx_ref[...])
    lam = lam_ref[...]
    softplus = jnp.maximum(-lam, 0.0) + jnp.log(1.0 + jnp.exp(-jnp.abs(lam)))
    log_a = (-RG_C * r) * softplus
    a = jnp.exp(log_a)
    bt = jnp.sqrt(1.0 - a * a) * (i * xc)
    bt = bt + a * h_in
    k = 1
    while k < seglen:
        ok = tpos >= k
        a_prev = jnp.where(ok, pltpu.roll(a, k, 0), 1.0)
        b_prev = jnp.where(ok, pltpu.roll(bt, k, 0), 0.0)
        bt = a * b_prev + bt
        a = a * a_prev
        k *= 2
    o_ref[0] = (bt * _gelu(gate_ref[0].astype(F32))).astype(o_ref.dtype)
    if carry:
        tail_sc[...] = x[tm - SUBLANES:]
        h_sc[...] = bt[tm - SUBLANES:]
        h_ref[0] = bt[tm - SUBLANES:]
    else:
        h_ref[0] = bt


def _rglru(rx, rgate, st, h0, rw, tm, seg):
    G, Tg, W = rx.shape
    tok = pl.BlockSpec((1, tm, W), lambda g, t: (g, t, 0))
    if seg == 0:
        st_spec = pl.BlockSpec((1, SUBLANES, W), lambda g, t: (g, 0, 0))
        h0_spec = pl.BlockSpec((1, SUBLANES, W), lambda g, t: (g, 0, 0))
        h_spec = pl.BlockSpec((1, SUBLANES, W), lambda g, t: (g, 0, 0))
        h_shape = (G, SUBLANES, W)
    else:
        st_spec = pl.BlockSpec(st.shape, lambda g, t: (0, 0, 0))
        h0_spec = tok
        h_spec = tok
        h_shape = (G, Tg, W)
    return pl.pallas_call(
        functools.partial(_rglru_kernel, tm=tm, seg=seg),
        grid=(G, Tg // tm),
        in_specs=[tok, tok, st_spec, h0_spec] + [_pspec(a) for a in rw],
        out_specs=[tok, h_spec],
        out_shape=[jax.ShapeDtypeStruct((G, Tg, W), rgate.dtype), jax.ShapeDtypeStruct(h_shape, F32)],
        scratch_shapes=[pltpu.VMEM((SUBLANES, W), F32), pltpu.VMEM((SUBLANES, W), F32)],
        compiler_params=_cparams(2),
        name="rglru",
    )(rx, rgate, st, h0, *[_parg(a) for a in rw])


def _merge_kernel(x_ref, oa_ref, or_ref, oc_ref, mg_ref, gt_ref, wa_ref, wb_ref, wc_ref, wo_ref, y_ref, *, D):
    pa = _bdot(oa_ref[0], wa_ref[...])
    pb = _bdot(or_ref[0], wb_ref[...])
    pc = _bdot(oc_ref[0], wc_ref[...])
    gate = lambda i: _sigmoid(mg_ref[0, :, i * D:(i + 1) * D].astype(F32))
    merged = gate(0) * pa + gate(1) * pb + gate(2) * pc
    y_ref[0] = x_ref[0] + gt_ref[...] * _bdot(merged, wo_ref[...])


def _merge(x, oa, orr, oc, mg, gt, wa, wb, wc, wo, tm):
    G, Tg, D = x.shape
    tok = lambda w: pl.BlockSpec((1, tm, w), lambda g, t: (g, t, 0))
    return pl.pallas_call(
        functools.partial(_merge_kernel, D=D),
        grid=(G, Tg // tm),
        in_specs=[tok(D), tok(oa.shape[2]), tok(orr.shape[2]), tok(oc.shape[2]), tok(3 * D),
                  _mod_spec(gt, tm, D), _pspec(wa), _pspec(wb), _pspec(wc), _pspec(wo)],
        out_specs=tok(D),
        out_shape=jax.ShapeDtypeStruct((G, Tg, D), F32),
        compiler_params=_cparams(2),
        name="merge_out",
    )(x, oa, orr, oc, mg, gt.arr, _parg(wa), _parg(wb), _parg(wc), _parg(wo))


def _ffn_kernel(x_ref, g_ref, sc_ref, sh_ref, gt_ref, st_ref, wup_ref, cw_ref, cb_ref, wdn_ref, fg_ref,
                y_ref, fnew_ref, tail_sc, *, tm, seg, F, wck, final):
    carry = seg == 0
    x = x_ref[0]
    h = _rms_mod(x, g_ref[...], sc_ref[...], sh_ref[...]).astype(BF16)
    rows = lax.broadcasted_iota(jnp.int32, (tm, 1), 0)
    if carry:
        @pl.when(pl.program_id(1) == 0)
        def _():
            tail_sc[...] = st_ref[0]
    else:
        tpos = rows & (seg - 1)
    acc = jnp.zeros(x.shape, F32)
    for c0 in range(0, F, wck):
        cs = slice(c0, c0 + wck)
        gp = jnp.dot(h, wup_ref[:, c0:c0 + wck], preferred_element_type=F32)
        val = jnp.dot(h, wup_ref[:, F + c0:F + c0 + wck], preferred_element_type=F32)
        if carry:
            tail8 = tail_sc[:, cs]
            shifted = [_shift_carry(gp, k, tail8) for k in range(1, FFN_CONV)]
            tail_sc[:, cs] = gp[tm - SUBLANES:]
            fnew_ref[0, :, cs] = gp[tm - SUBLANES:]
        else:
            shifted = [_shift_seg(gp, k, st_ref[k - 1, :, cs], tpos) for k in range(1, FFN_CONV)]
            fnew_ref[0, :, cs] = gp
        gc = cb_ref[:, cs] + cw_ref[FFN_CONV - 1:FFN_CONV, cs] * gp
        for k in range(1, FFN_CONV):
            gc = gc + cw_ref[FFN_CONV - 1 - k:FFN_CONV - k, cs] * shifted[k - 1]
        act = (gc * _sigmoid(gc)) * val
        acc = acc + _bdot(act, wdn_ref[cs, :])
    y = x + gt_ref[...] * acc
    if final:
        y = (y * lax.rsqrt(jnp.mean(y * y, axis=-1, keepdims=True) + EPS)) * fg_ref[...]
    y_ref[0] = y


def _ffn(x, g, sc, sh, gt, st, wup, cw, cb, wdn, fg, tm, seg, final):
    G, Tg, D = x.shape
    F = wdn.shape[0]
    wck = F // 2 if (F // 2) % LANES == 0 else F
    tok = pl.BlockSpec((1, tm, D), lambda g_, t: (g_, t, 0))
    if seg == 0:
        st_spec = pl.BlockSpec((1, SUBLANES, F), lambda g_, t: (g_, 0, 0))
        fn_spec = pl.BlockSpec((1, SUBLANES, F), lambda g_, t: (g_, 0, 0))
        fn_shape = (G, SUBLANES, F)
    else:
        st_spec = pl.BlockSpec(st.shape, lambda g_, t: (0, 0, 0))
        fn_spec = pl.BlockSpec((1, tm, F), lambda g_, t: (g_, t, 0))
        fn_shape = (G, Tg, F)
    return pl.pallas_call(
        functools.partial(_ffn_kernel, tm=tm, seg=seg, F=F, wck=wck, final=final),
        grid=(G, Tg // tm),
        in_specs=[tok, _pspec(g), _mod_spec(sc, tm, D), _mod_spec(sh, tm, D), _mod_spec(gt, tm, D),
                  st_spec, _pspec(wup), _pspec(cw), _pspec(cb), _pspec(wdn), _pspec(fg)],
        out_specs=[tok, fn_spec],
        out_shape=[jax.ShapeDtypeStruct((G, Tg, D), F32), jax.ShapeDtypeStruct(fn_shape, F32)],
        scratch_shapes=[pltpu.VMEM((SUBLANES, F), F32)],
        compiler_params=_cparams(2),
        name="conv_ffn",
    )(x, _parg(g), sc.arr, sh.arr, gt.arr, st, _parg(wup), _parg(cw), _parg(cb), _parg(wdn), fg)


def _head_perm():
    return np.array([(k * NSA_HPG + r) * NSA_HD + d
                     for r in range(NSA_HPG) for k in range(NSA_KV) for d in range(NSA_HD)], np.int32)


def _block_diag(w):
    n, a, b = w.shape[-3:]
    eye = jnp.eye(n, dtype=w.dtype)
    out = jnp.einsum('ij,...iab->...iajb', eye, w)
    return out.reshape(w.shape[:-3] + (n * a, n * b))


def _seg_fill(buf, k, seg):
    B, nb, C = buf.shape
    part = jnp.concatenate([buf[:, nb - k:, :], jnp.zeros((B, seg - k, C), buf.dtype)], axis=1)
    return part.reshape(B * seg, C)


def kernel(x_prompt, x_sample, cache_nsa, cache_nsa_win, state_ret, state_rglru_h, state_rglru_conv,
           state_ffn_conv, page_table, c_prompt, c_sample, norm1_g, norm2_g, w_ada, b_ada, w_in, cmp_pe,
           cmp_w1, cmp_b1, cmp_w2, ret_gn_g, rg_conv_w, rg_conv_b, rg_w_a, rg_b_a, rg_w_x, rg_b_x, rg_lambda,
           w_br_a, w_br_b, w_br_c, w_out, ffn_w_up, ffn_conv_w, ffn_conv_b, ffn_w_down, final_norm_g):
    B, T, D = x_prompt.shape
    DB, Q, _ = x_sample.shape
    L = w_in.shape[0]
    npages = page_table.shape[1]
    page = cache_nsa.shape[2]
    P = npages * page
    NQ = NSA_KV * NSA_HPG * NSA_HD
    NKV = NSA_KV * NSA_HD
    RW = RET_HEADS * RET_DK
    W = rg_conv_w.shape[2]
    F = ffn_w_down.shape[1]
    NS = DB * Q
    wbuf_len = cache_nsa_win.shape[2]
    assert Q == SUBLANES and T >= RG_CONV and P % CMP_STRIDE == 0

    mod = _ada(jnp.concatenate([c_prompt, c_sample], axis=0), w_ada, b_ada)
    cache_t = jnp.transpose(cache_nsa, (0, 1, 3, 4, 5, 2)).reshape(L, cache_nsa.shape[1], 4, NKV, page)
    win_t_all = jnp.transpose(cache_nsa_win, (0, 1, 3, 4, 5, 2)).reshape(L, DB, 2, NKV, wbuf_len)
    perm = _head_perm()

    half = RET_DK // 2
    freq = ROPE_BASE ** (-jnp.arange(half, dtype=F32) / half)

    def rope_tables(pos):
        ang = pos.astype(F32)[:, None] * freq[None, :]
        cos, sin = jnp.cos(ang), jnp.sin(ang)
        return jnp.concatenate([cos, cos], axis=1), jnp.concatenate([-sin, sin], axis=1)

    cos_p, sin_p = rope_tables(jnp.arange(T, dtype=jnp.int32))
    cos_s, sin_s = rope_tables(P + jnp.arange(Q, dtype=jnp.int32))

    widths = (NQ, 4 * NKV, 2 * NKV, LANES, RW, RW, RW, RW, W, W, 3 * D)
    dt_prompt = (F32,) * 9 + (BF16, BF16)
    dt_sample = (F32,) * 11
    offs = np.cumsum((0, NQ, 6 * NKV, 3 * NSA_KV * NSA_HPG, RW, RW, RW, RW, W, W, 3 * D))
    ngate = 3 * NSA_KV * NSA_HPG

    xp = x_prompt
    xs = x_sample.reshape(1, NS, D)
    outs_p = [[] for _ in range(6)]
    outs_s = [[] for _ in range(6)]
    tm_p = 256 if T % 256 == 0 else T

    wi = w_in.astype(BF16)
    w_cat_all = jnp.concatenate([
        wi[:, :, offs[0]:offs[1]][:, :, perm],
        wi[:, :, offs[1]:offs[2]],
        jnp.pad(wi[:, :, offs[2]:offs[3]], ((0, 0), (0, 0), (0, LANES - ngate))),
        wi[:, :, offs[3]:]], axis=2)
    grouped = lambda w: _block_diag(jnp.broadcast_to(w[..., None, :, :], w.shape[:-2] + (NSA_KV,) + w.shape[-2:]))
    cw1_all = jnp.concatenate([grouped(cmp_w1[:, :, :CMP_STRIDE]), grouped(cmp_w1[:, :, CMP_STRIDE:])], axis=-1)
    cw1_all = cw1_all.reshape(L, 2, CMP_STRIDE * NKV, 4 * LANES).astype(BF16)
    cpe_all = jnp.tile(cmp_pe, (1, 1, 1, NSA_KV)).reshape(L, 2, 2, CMP_STRIDE * NKV)
    cb1_all = jnp.tile(cmp_b1, (1, 1, NSA_KV))[:, :, None, :]
    cw2_all = grouped(cmp_w2).astype(BF16)
    row = lambda a: a[:, None, :]
    rw_all = (rg_conv_w, row(rg_conv_b), _block_diag(rg_w_a).astype(BF16), row(rg_b_a),
              _block_diag(rg_w_x).astype(BF16), row(rg_b_x), row(rg_lambda))
    wa_all = w_br_a[:, perm].astype(BF16)
    wb_all = w_br_b.astype(BF16)
    wc_all = w_br_c.astype(BF16)
    wo_all = w_out.astype(BF16)
    wup_all = ffn_w_up.astype(BF16)
    wdn_all = ffn_w_down.astype(BF16)
    g1_all, g2_all, gn_all, fcb_all = row(norm1_g), row(norm2_g), row(ret_gn_g), row(ffn_conv_b)
    fg = final_norm_g[None]
    mod_p = mod[:, :B].reshape(L, B, 1, 6 * D)
    mod_s = jnp.repeat(mod[:, B:], Q, axis=1)

    for l in range(L):
        lay = lambda a: _LayerOf(a, l)
        w_cat = lay(w_cat_all)
        cw = (lay(cw1_all), lay(cpe_all), lay(cb1_all), lay(cw2_all))
        rw = tuple(lay(a) for a in rw_all)
        wa, wb, wc, wo, wup, wdn = (lay(a) for a in (wa_all, wb_all, wc_all, wo_all, wup_all, wdn_all))
        g1, g2, gn, fcw, fcb = (lay(a) for a in (g1_all, g2_all, gn_all, ffn_conv_w, fcb_all))
        final = l == L - 1

        m = [_ModOf(mod_p, l, i) for i in range(6)]
        (nq, rows, win, gate, rq, rk, rv, rg, rx, rgate, mg) = _inproj(xp, g1, m[1], m[0], w_cat, widths,
                                                                        dt_prompt, tm_p)
        kc, vc = _compress_prompt(rows, cw)
        o_a = _nsa_prompt(nq, gate, kc, vc, rows, win)
        o_r, s_new = _retention(rq, rk, rv, rg, cos_p, sin_p,
                                jnp.zeros((B, RET_HEADS, RET_DK, RET_DK), F32), gn, B, BF16)
        zs = jnp.zeros((B, SUBLANES, W), F32)
        o_c, h_tail = _rglru(rx, rgate, zs, zs, rw, tm_p, 0)
        x1 = _merge(xp, o_a, o_r, o_c, mg, m[2], wa, wb, wc, wo, tm_p)
        xp, f_tail = _ffn(x1, g2, m[4], m[3], m[5], jnp.zeros((B, SUBLANES, F), F32), wup, fcw, fcb, wdn, fg,
                          tm_p, 0, final)
        wn = min(WINDOW, T)
        outs_p[0].append(rows.reshape(B, T, 4, NSA_KV, NSA_HD))
        outs_p[1].append(win[:, T - wn:].reshape(B, wn, 2, NSA_KV, NSA_HD))
        outs_p[2].append(s_new)
        outs_p[3].append(h_tail[:, SUBLANES - 1])
        outs_p[4].append(rx[:, T - (RG_CONV - 1):])
        outs_p[5].append(f_tail[:, SUBLANES - (FFN_CONV - 1):])

        ms = [_ModOf(mod_s, l, i) for i in range(6)]
        (nq, rows, win, gate, rq, rk, rv, rg, rx, rgate, mg) = _inproj(xs, g1, ms[1], ms[0], w_cat, widths,
                                                                        dt_sample, NS)
        r3 = lambda a: a.reshape(DB, Q, a.shape[-1])
        o_a = _nsa_sample(page_table, r3(nq), r3(gate), r3(rows), win_t_all, r3(win), cache_t, l, cw)
        o_r, s_new = _retention(r3(rq), r3(rk), r3(rv), r3(rg), cos_s, sin_s, state_ret[l].astype(F32), gn,
                                math.gcd(DB, SUBLANES), F32)
        cbuf = state_rglru_conv[l]
        st = jnp.stack([_seg_fill(cbuf, k, Q) for k in range(1, RG_CONV)])
        h0 = jnp.pad(state_rglru_h[l].astype(F32)[:, None, :], ((0, 0), (0, Q - 1), (0, 0))).reshape(1, NS, W)
        o_c, h_all = _rglru(rx, rgate, st, h0, rw, NS, Q)
        x1 = _merge(xs, o_a.reshape(1, NS, NQ), o_r.reshape(1, NS, RW), o_c, mg, ms[2], wa, wb, wc, wo, NS)
        fbuf = state_ffn_conv[l]
        fst = jnp.stack([_seg_fill(fbuf, k, Q) for k in range(1, FFN_CONV)])
        xs, g_all = _ffn(x1, g2, ms[4], ms[3], ms[5], fst, wup, fcw, fcb, wdn, fg, NS, Q, final)
        keys = jnp.concatenate([cache_nsa_win[l], r3(win).reshape(DB, Q, 2, NSA_KV, NSA_HD)], axis=1)
        outs_s[0].append(rows.reshape(DB, Q, 4, NSA_KV, NSA_HD))
        outs_s[1].append(keys[:, Q:])
        outs_s[2].append(s_new)
        outs_s[3].append(h_all.reshape(DB, Q, W)[:, Q - 1])
        outs_s[4].append(jnp.concatenate([cbuf, rx.reshape(DB, Q, W)], axis=1)[:, Q:])
        outs_s[5].append(jnp.concatenate([fbuf, g_all.reshape(DB, Q, F)], axis=1)[:, Q:])

    sp = [jnp.stack(a) for a in outs_p]
    ss = [jnp.stack(a) for a in outs_s]
    return (xp, xs.reshape(DB, Q, D), sp[0], ss[0], sp[1], ss[1], sp[2], ss[2],
            sp[3], ss[3], sp[4], ss[4], sp[5], ss[5])
```

```python
import functools
import math

import numpy as np
import jax
import jax.numpy as jnp
from jax import lax
from jax.experimental import pallas as pl
from jax.experimental.pallas import tpu as pltpu

F32 = jnp.float32
BF16 = jnp.bfloat16

NSA_KV = 2
NSA_HPG = 4
NSA_HD = 64
CMP_STRIDE = 16
CMP_BLOCK = 32
SEL_BLOCK = 64
SEL_TOPK = 16
SEL_FORCE = 1e4
WINDOW = 512
QBLOCK = 256
RET_HEADS = 4
RET_DK = 128
RET_CHUNK = 128
ROPE_BASE = 10000.0
RG_CONV = 4
RG_C = 8.0
FFN_CONV = 3
EPS = 1e-6

NEG = -1e30
BIG = float(2 ** 60)
LOG2E = 1.4426950408889634
SUBLANES = 8
LANES = 128
VMEM_LIMIT_V7X = 56 * 1024 * 1024
SEL_CHUNK = 512
CAST_CHUNK = 1024
PAGE_UNROLL = 8
MXU_DEPTH_V7X = 256


def _cparams(n_grid):
    return pltpu.CompilerParams(dimension_semantics=("arbitrary",) * n_grid,
                                vmem_limit_bytes=VMEM_LIMIT_V7X)


def _bdot(a, b):
    return jnp.dot(a.astype(BF16), b.astype(BF16), preferred_element_type=F32)


def _bdot_nt(a, b):
    return lax.dot_general(a.astype(BF16), b.astype(BF16), (((1,), (1,)), ((), ())),
                           preferred_element_type=F32)


def _split(a):
    hi = a.astype(BF16)
    lo = (a - hi.astype(F32)).astype(BF16)
    return hi, lo


def _dot3(a, b):
    ah, al = _split(a)
    bh, bl = _split(b)
    d = functools.partial(jnp.dot, preferred_element_type=F32)
    return d(ah, bh) + d(al, bh) + d(ah, bl)


def _sigmoid(x):
    return 0.5 * jnp.tanh(0.5 * x) + 0.5


def _gelu(x):
    return 0.5 * x * (1.0 + jnp.tanh(0.7978845608028654 * (x + 0.044715 * (x * x * x))))


def _rms_mod(x, g, sc, sh):
    y = x * lax.rsqrt(jnp.mean(x * x, axis=-1, keepdims=True) + EPS)
    return (y * g) * (1.0 + sc) + sh


def _ada_kernel(c_ref, w_ref, b_ref, o_ref):
    c = c_ref[...]
    o_ref[0] = _dot3(c * _sigmoid(c), w_ref[0]) + b_ref[0]


def _ada(c_all, w_ada, b_ada):
    L, D, E = w_ada.shape
    n = c_all.shape[0]
    tn = 1536 if E % 1536 == 0 else E
    return pl.pallas_call(
        _ada_kernel,
        grid=(L, E // tn),
        in_specs=[pl.BlockSpec((n, D), lambda l, j: (0, 0)),
                  pl.BlockSpec((1, D, tn), lambda l, j: (l, 0, j)),
                  pl.BlockSpec((1, 1, tn), lambda l, j: (l, 0, j))],
        out_specs=pl.BlockSpec((1, n, tn), lambda l, j: (l, 0, j)),
        out_shape=jax.ShapeDtypeStruct((L, n, E), F32),
        compiler_params=_cparams(2),
        name="ada_mod",
    )(c_all, w_ada, b_ada.reshape(L, 1, E))


class _LayerOf:
    def __init__(self, arr, layer):
        self.arr, self.layer, self.shape = arr, layer, arr.shape[1:]


class _ModOf:
    def __init__(self, arr, layer, idx):
        self.arr, self.layer, self.idx = arr, layer, idx


def _mod_spec(m, tm, d):
    l, i = m.layer, m.idx
    if m.arr.ndim == 4:
        return pl.BlockSpec((None, None, 1, d), lambda g, t: (l, g, 0, i))
    return pl.BlockSpec((None, tm, d), lambda g, t: (l, t, i))


def _pspec(p):
    if isinstance(p, _LayerOf):
        nd, l = len(p.shape), p.layer
        return pl.BlockSpec((None,) + tuple(p.shape), lambda *a: (l,) + (0,) * nd, pipeline_mode=pl.Buffered(1))
    nd = p.ndim
    return pl.BlockSpec(p.shape, lambda *a: (0,) * nd, pipeline_mode=pl.Buffered(1))


def _parg(p):
    return p.arr if isinstance(p, _LayerOf) else p


def _inproj_kernel(x_ref, g_ref, sc_ref, sh_ref, w_ref, *o_refs, segs):
    h = _rms_mod(x_ref[0], g_ref[...], sc_ref[...], sh_ref[...]).astype(BF16)
    for (off, wd), o_ref in zip(segs, o_refs):
        o_ref[0] = jnp.dot(h, w_ref[:, off:off + wd], preferred_element_type=F32).astype(o_ref.dtype)


def _inproj(x, g, sc, sh, w, widths, dtypes, tm):
    G, Tg, D = x.shape
    segs, off = [], 0
    for wd in widths:
        segs.append((off, wd))
        off += wd
    return pl.pallas_call(
        functools.partial(_inproj_kernel, segs=tuple(segs)),
        grid=(G, Tg // tm),
        in_specs=[pl.BlockSpec((1, tm, D), lambda g_, t: (g_, t, 0)),
                  _pspec(g), _mod_spec(sc, tm, D), _mod_spec(sh, tm, D), _pspec(w)],
        out_specs=[pl.BlockSpec((1, tm, wd), lambda g_, t: (g_, t, 0)) for wd in widths],
        out_shape=[jax.ShapeDtypeStruct((G, Tg, wd), dt) for wd, dt in zip(widths, dtypes)],
        compiler_params=_cparams(2),
        name="in_proj",
    )(x, _parg(g), sc.arr, sh.arr, _parg(w))


def _compress_bias(w1_ref, pe_ref, bias_sc):
    for c in range(2):
        halves = []
        for h in range(2):
            pe_rows = jnp.broadcast_to(pe_ref[c, h:h + 1, :], (SUBLANES, pe_ref.shape[2]))
            halves.append(_bdot(pe_rows, w1_ref[c, :, h * 2 * LANES:(h + 1) * 2 * LANES]))
        bias_sc[c] = jnp.concatenate(halves, axis=1)


def _compress_x(xrefs, nch, w1_ref, bias_sc, b1_ref, w2_ref):
    last = lax.broadcasted_iota(jnp.int32, (nch, 1), 0) == nch - 1
    outs = []
    for c in range(2):
        lhs = jnp.concatenate([xrefs[c][pl.ds(p, nch, stride=CMP_STRIDE), :].astype(BF16)
                               for p in range(CMP_STRIDE)], axis=1)
        acc = jnp.dot(lhs, w1_ref[c], preferred_element_type=F32) + bias_sc[c, 0:1, :]
        lo = acc[:, :2 * LANES]
        hi = acc[:, 2 * LANES:]
        hi_next = jnp.where(last, 0.0, pltpu.roll(hi, nch - 1, 0))
        hid = _gelu(lo + hi_next + b1_ref[c])
        outs.append(_bdot(hid, w2_ref[c]))
    return outs


def _compress_prompt_kernel(krows_ref, vrows_ref, w1_ref, pe_ref, b1_ref, w2_ref, kc_ref, vc_ref, bias_sc, *, nch):
    @pl.when(pl.program_id(0) == 0)
    def _():
        _compress_bias(w1_ref, pe_ref, bias_sc)

    kc, vc = _compress_x((krows_ref.at[0], vrows_ref.at[0]), nch, w1_ref, bias_sc, b1_ref, w2_ref)
    kc_ref[0] = kc
    vc_ref[0] = vc


def _compress_prompt(rows, cw):
    B, T, _ = rows.shape
    nch = T // CMP_STRIDE
    w1, pe, b1, w2 = cw
    return pl.pallas_call(
        functools.partial(_compress_prompt_kernel, nch=nch),
        grid=(B,),
        in_specs=[pl.BlockSpec((1, T, LANES), lambda b: (b, 0, 0)),
                  pl.BlockSpec((1, T, LANES), lambda b: (b, 0, 1)),
                  _pspec(w1), _pspec(pe), _pspec(b1), _pspec(w2)],
        out_specs=[pl.BlockSpec((1, nch, LANES), lambda b: (b, 0, 0))] * 2,
        out_shape=[jax.ShapeDtypeStruct((B, nch, LANES), F32)] * 2,
        scratch_shapes=[pltpu.VMEM((2, SUBLANES, 4 * LANES), F32)],
        compiler_params=_cparams(1),
        name="nsa_compress_prompt",
    )(rows, rows, _parg(w1), _parg(pe), _parg(b1), _parg(w2))


def _rep_all(a):
    return jnp.concatenate([a] * (NSA_KV * NSA_HPG), axis=0)


def _rep_heads(a, Qb):
    return jnp.concatenate([a[:Qb]] * NSA_HPG + [a[Qb:]] * NSA_HPG, axis=0)


def _nsa_front(qblk, q0, Qb, kc, vc, n_cmp, n_sel):
    R2 = 2 * Qb
    npad = kc.shape[0]
    lane = lax.broadcasted_iota(jnp.int32, (Qb, LANES), 1)
    lo_half = lane < NSA_HD
    scale = NSA_HD ** -0.5 * LOG2E
    pieces = []
    for k in range(NSA_KV):
        for r in range(NSA_HPG):
            sl = qblk[:, r * LANES:(r + 1) * LANES] * scale
            pieces.append(jnp.where(lo_half if k == 0 else jnp.logical_not(lo_half), sl, 0.0))
    qs = jnp.concatenate(pieces, axis=0).astype(BF16)
    qp1 = q0 + lax.broadcasted_iota(jnp.int32, (Qb, 1), 0)

    n_idx = lax.broadcasted_iota(jnp.int32, (1, npad), 1)
    visible = (n_idx * CMP_STRIDE + (CMP_BLOCK - 1) <= qp1) & (n_idx < n_cmp)
    s = _bdot_nt(qs, kc) + _rep_all(jnp.where(visible, 0.0, -BIG))
    e = jnp.exp2(s - jnp.max(s, axis=-1, keepdims=True))
    any_visible = _rep_all((qp1 >= CMP_BLOCK - 1) & (n_cmp > 0))
    p_c = e * jnp.where(any_visible, 1.0 / jnp.sum(e, axis=-1, keepdims=True), 0.0)
    o_c = _bdot(p_c, vc)

    psum = []
    for k in range(NSA_KV):
        acc = p_c[(k * NSA_HPG) * Qb:(k * NSA_HPG + 1) * Qb]
        for r in range(1, NSA_HPG):
            acc = acc + p_c[(k * NSA_HPG + r) * Qb:(k * NSA_HPG + r + 1) * Qb]
        psum.append(acc)
    psum = jnp.concatenate(psum + [jnp.zeros((LANES - R2, npad), F32)], axis=0)
    nsr = -(-n_sel // SUBLANES) * SUBLANES
    sj = lax.broadcasted_iota(jnp.int32, (nsr, npad), 0) * SEL_BLOCK
    ci = lax.broadcasted_iota(jnp.int32, (nsr, npad), 1) * CMP_STRIDE
    selmap = jnp.where((ci < sj + SEL_BLOCK) & (ci + CMP_BLOCK > sj), 1.0, 0.0).astype(BF16)
    ph, plo = _split(psum)
    imp = _bdot_nt(selmap, ph) + _bdot_nt(selmap, plo)
    col_pos = q0 + (lax.broadcasted_iota(jnp.int32, (1, LANES), 1) & (Qb - 1))
    return qs, qp1, o_c, _select_blocks(imp, col_pos, n_sel)


def _select_blocks(imp, qpos, n_sel):
    nsr, ncol = imp.shape
    j = lax.broadcasted_iota(jnp.int32, (nsr, ncol), 0)
    jf = j.astype(F32)
    cur = qpos >> 6
    forced = (j == 0) | (j == cur) | (j == cur - 1)
    imp = jnp.where(forced, SEL_FORCE, imp)
    imp = jnp.where(j * SEL_BLOCK <= qpos, imp, -SEL_FORCE)
    imp = jnp.where(j < n_sel, imp, NEG)

    def pick(_, carry):
        imp_c, sel_c = carry
        m = jnp.max(imp_c, axis=0, keepdims=True)
        first = jnp.min(jnp.where(imp_c == m, jf, float(nsr)), axis=0, keepdims=True)
        hit = jf == first
        return jnp.where(hit, NEG, imp_c), jnp.where(hit, 1.0, sel_c)

    _, sel = lax.fori_loop(0, min(SEL_TOPK, n_sel), pick, (imp, jnp.zeros((nsr, ncol), F32)), unroll=True)
    return sel


def _nsa_combine(gate, o_c, o_s, o_w, Qb, o_ref):
    lo_half = lax.broadcasted_iota(jnp.int32, (Qb, LANES), 1) < NSA_HD
    g = _sigmoid(gate)
    for r in range(NSA_HPG):
        halves = []
        for k in range(NSA_KV):
            rs = slice((k * NSA_HPG + r) * Qb, (k * NSA_HPG + r + 1) * Qb)
            c = (k * NSA_HPG + r) * 3
            halves.append(g[:, c:c + 1] * o_c[rs] + g[:, c + 1:c + 2] * o_s[rs] + g[:, c + 2:c + 3] * o_w[rs])
        o_ref[0, :, r * LANES:(r + 1) * LANES] = jnp.where(lo_half, halves[0], halves[1])


def _block_columns(k0, n):
    key = k0 + lax.broadcasted_iota(jnp.int32, (n, LANES), 0)
    blk = lax.broadcasted_iota(jnp.int32, (n, LANES), 1)
    return jnp.where((key >> 6) == blk, BIG, 0.0).astype(BF16)


def _block_rows(k0, n):
    key = k0 + lax.broadcasted_iota(jnp.int32, (LANES, n), 1)
    blk = lax.broadcasted_iota(jnp.int32, (LANES, n), 0)
    return jnp.where((key >> 6) == blk, BIG, 0.0).astype(BF16)


def _lanes_all(a):
    return jnp.concatenate([a] * (NSA_KV * NSA_HPG), axis=1)


def _lanes_heads(a, Qb):
    return jnp.concatenate([a[:, :Qb]] * NSA_HPG + [a[:, Qb:]] * NSA_HPG, axis=1)


def _softmax_cols(s):
    e = jnp.exp2(s - jnp.max(s, axis=0, keepdims=True))
    return e, 1.0 / jnp.sum(e, axis=0, keepdims=True)


def _nsa_prompt_kernel(q_ref, gate_ref, kc_ref, vc_ref, rows_ref, win_ref, o_ref, kaug, vt,
                       *, T, Qb, n_cmp, n_sel, wl):
    i = pl.program_id(1)
    q0 = i * Qb
    R = NSA_KV * NSA_HPG * Qb
    R2 = NSA_KV * Qb

    @pl.when(i == 0)
    def _():
        def pack(c, carry):
            r0 = pl.multiple_of(c * SEL_CHUNK, SEL_CHUNK)
            kaug[pl.ds(r0, SEL_CHUNK), 0:LANES] = rows_ref[0, pl.ds(r0, SEL_CHUNK), 2 * LANES:3 * LANES].astype(BF16)
            kaug[pl.ds(r0, SEL_CHUNK), LANES:2 * LANES] = _block_columns(r0, SEL_CHUNK)
            vt[:, pl.ds(r0, SEL_CHUNK)] = rows_ref[0, pl.ds(r0, SEL_CHUNK), 3 * LANES:4 * LANES].T.astype(BF16)
            return carry
        lax.fori_loop(0, T // SEL_CHUNK, pack, 0)

    top = lax.broadcasted_iota(jnp.int32, (LANES, Qb), 0) < NSA_HD
    scale = NSA_HD ** -0.5 * LOG2E
    qblk = q_ref[0]
    q_t = [(qblk[:, r * LANES:(r + 1) * LANES] * scale).T for r in range(NSA_HPG)]
    qs = jnp.concatenate([jnp.where(top if k == 0 else jnp.logical_not(top), q_t[r], 0.0)
                          for k in range(NSA_KV) for r in range(NSA_HPG)], axis=1).astype(BF16)
    qlane = q0 + lax.broadcasted_iota(jnp.int32, (1, Qb), 1)
    qpos = _lanes_all(qlane)

    kc = kc_ref[0]
    npad = kc.shape[0]
    n_idx = lax.broadcasted_iota(jnp.int32, (npad, 1), 0)
    visible = (n_idx * CMP_STRIDE + (CMP_BLOCK - 1) <= qlane) & (n_idx < n_cmp)
    e_c, inv_c = _softmax_cols(_bdot(kc, qs) + _lanes_all(jnp.where(visible, 0.0, -BIG)))
    any_visible = _lanes_all((qlane >= CMP_BLOCK - 1) & (n_cmp > 0))
    p_c = e_c * jnp.where(any_visible, inv_c, 0.0)
    o_c = _bdot(vc_ref[0].T, p_c)

    psum = []
    for k in range(NSA_KV):
        acc = p_c[:, (k * NSA_HPG) * Qb:(k * NSA_HPG + 1) * Qb]
        for r in range(1, NSA_HPG):
            acc = acc + p_c[:, (k * NSA_HPG + r) * Qb:(k * NSA_HPG + r + 1) * Qb]
        psum.append(acc)
    psum = jnp.concatenate(psum, axis=1)
    nsr = -(-n_sel // SUBLANES) * SUBLANES
    sj = lax.broadcasted_iota(jnp.int32, (nsr, npad), 0) * SEL_BLOCK
    ci = lax.broadcasted_iota(jnp.int32, (nsr, npad), 1) * CMP_STRIDE
    selmap = jnp.where((ci < sj + SEL_BLOCK) & (ci + CMP_BLOCK > sj), 1.0, 0.0).astype(BF16)
    ph, plo = _split(psum)
    imp = (jnp.dot(selmap, ph, preferred_element_type=F32)
           + jnp.dot(selmap, plo, preferred_element_type=F32))
    sel = _select_blocks(imp, jnp.concatenate([qlane] * NSA_KV, axis=1), n_sel)
    selm =jnp.concatenate([sel - 1.0, jnp.zeros((LANES - nsr, R2), F32)], axis=0) if nsr < LANES else sel - 1.0
    qaug = jnp.concatenate([qs, _lanes_heads(selm, Qb).astype(BF16)], axis=0)

    def update(s, vcols, carry):
        m, l, acc = carry
        m_new = jnp.maximum(m, jnp.max(s, axis=0, keepdims=True))
        alpha = jnp.exp2(m - m_new)
        p = jnp.exp2(s - m_new)
        l = alpha * l + jnp.sum(p, axis=0, keepdims=True)
        acc = alpha * acc + jnp.dot(vcols, p.astype(BF16), preferred_element_type=F32)
        return m_new, l, acc

    def scores(k0):
        return jnp.dot(kaug[pl.ds(k0, SEL_CHUNK), :], qaug, preferred_element_type=F32)

    def full_chunk(c, carry):
        k0 = pl.multiple_of(c * SEL_CHUNK, SEL_CHUNK)
        return update(scores(k0), vt[:, pl.ds(k0, SEL_CHUNK)], carry)

    n_full = q0 // SEL_CHUNK
    init = (jnp.full((1, R), -4.0 * BIG, F32), jnp.zeros((1, R), F32), jnp.zeros((LANES, R), F32))
    carry = lax.fori_loop(0, n_full, full_chunk, init)
    k0 = pl.multiple_of(n_full * SEL_CHUNK, SEL_CHUNK)
    kpos = k0 + lax.broadcasted_iota(jnp.int32, (SEL_CHUNK, 1), 0)
    s_diag = jnp.where(kpos <= qpos, scores(k0), -2.0 * BIG)
    _, l_s, acc_s = update(s_diag, vt[:, pl.ds(k0, SEL_CHUNK)], carry)
    o_s = acc_s * (1.0 / l_s)

    ws = pl.multiple_of(jnp.maximum(q0 - WINDOW, 0), LANES)
    wk = win_ref[0, pl.ds(ws, wl), 0:LANES]
    wv = win_ref[0, pl.ds(ws, wl), LANES:2 * LANES]
    dpos = qlane - (ws + lax.broadcasted_iota(jnp.int32, (wl, 1), 0))
    e_w, inv_w = _softmax_cols(_bdot(wk, qs) + _lanes_all(jnp.where((dpos >= 0) & (dpos <= WINDOW), 0.0, -BIG)))
    o_w = _bdot(wv.T, e_w) * inv_w

    g = _sigmoid(gate_ref[0]).T
    for r in range(NSA_HPG):
        halves = []
        for k in range(NSA_KV):
            cs = slice((k * NSA_HPG + r) * Qb, (k * NSA_HPG + r + 1) * Qb)
            c = (k * NSA_HPG + r) * 3
            halves.append(g[c:c + 1, :] * o_c[:, cs] + g[c + 1:c + 2, :] * o_s[:, cs] + g[c + 2:c + 3, :] * o_w[:, cs])
        o_ref[0, :, r * LANES:(r + 1) * LANES] = jnp.where(top, halves[0], halves[1]).T.astype(o_ref.dtype)


def _nsa_prompt(nq, gate, kc, vc, rows, win):
    B, T, HD = nq.shape
    Qb = QBLOCK
    nch = kc.shape[1]
    n_sel = -(-T // SEL_BLOCK)
    wl = WINDOW + Qb
    assert T % SEL_CHUNK == 0 and T >= wl and SEL_TOPK <= n_sel <= LANES
    return pl.pallas_call(
        functools.partial(_nsa_prompt_kernel, T=T, Qb=Qb, n_cmp=nch - 1, n_sel=n_sel, wl=wl),
        grid=(B, T // Qb),
        in_specs=[pl.BlockSpec((1, Qb, HD), lambda b, i: (b, i, 0)),
                  pl.BlockSpec((1, Qb, LANES), lambda b, i: (b, i, 0)),
                  pl.BlockSpec((1, nch, LANES), lambda b, i: (b, 0, 0)),
                  pl.BlockSpec((1, nch, LANES), lambda b, i: (b, 0, 0)),
                  pl.BlockSpec((1, T, 4 * LANES), lambda b, i: (b, 0, 0)),
                  pl.BlockSpec((1, T, 2 * LANES), lambda b, i: (b, 0, 0))],
        out_specs=pl.BlockSpec((1, Qb, HD), lambda b, i: (b, i, 0)),
        out_shape=jax.ShapeDtypeStruct((B, T, HD), BF16),
        scratch_shapes=[pltpu.VMEM((T, 2 * LANES), BF16), pltpu.VMEM((LANES, T), BF16)],
        compiler_params=_cparams(2),
        name="nsa_attn_prompt",
    )(nq, gate, kc, vc, rows, win)


def _pages_copy(cache_hbm, layer, page, r0, dst, j, sem):
    n = cache_hbm.shape[-1]
    return pltpu.make_async_copy(cache_hbm.at[layer, page, pl.ds(r0, 2)],
                                 dst.at[:, :, pl.ds(pl.multiple_of(j * n, n), n)], sem)


def _pages_start(pt_ref, b, cache_hbm, layer, r0, dst, sem, npages):
    def issue(j, carry):
        _pages_copy(cache_hbm, layer, pt_ref[b, j], r0, dst, j, sem).start()
        return carry
    lax.fori_loop(0, npages, issue, 0, unroll=PAGE_UNROLL)


def _pages_wait(cache_hbm, layer, r0, dst, sem, npages):
    def wait(j, carry):
        _pages_copy(cache_hbm, layer, 0, r0, dst, j, sem).wait()
        return carry
    lax.fori_loop(0, npages, wait, 0, unroll=PAGE_UNROLL)


def _softmax2(s1, s2, mask2):
    s2 = jnp.where(mask2, s2, -2.0 * BIG)
    m = jnp.maximum(jnp.max(s1, axis=-1, keepdims=True), jnp.max(s2, axis=-1, keepdims=True))
    e1 = jnp.exp2(s1 - m)
    e2 = jnp.exp2(s2 - m)
    den = jnp.sum(e1, axis=-1, keepdims=True) + jnp.sum(e2, axis=-1, keepdims=True)
    return e1, e2, 1.0 / den


def _nsa_sample_kernel(pt_ref, q_ref, gate_ref, rows_ref, wt_ref, wnew_ref, cache_hbm,
                       w1_ref, pe_ref, b1_ref, w2_ref, o_ref,
                       cmpbuf, selbuf, xk, xv, kaug, vt, newbuf, wnewbuf, bias_sc, csem, ssem,
                       *, layer, npages, P, Q, wb, nb, n_sel):
    b = pl.program_id(0)
    nseq = pl.num_programs(0)
    page = cache_hbm.shape[-1]
    nch = P // CMP_STRIDE

    @pl.when(b == 0)
    def _():
        _pages_start(pt_ref, 0, cache_hbm, layer, 0, cmpbuf, csem, npages)
        _pages_start(pt_ref, 0, cache_hbm, layer, 2, selbuf, ssem, npages)
        _compress_bias(w1_ref, pe_ref, bias_sc)

        def blocks(c, carry):
            c0 = pl.multiple_of(c * CAST_CHUNK, CAST_CHUNK)
            kaug[LANES:2 * LANES, pl.ds(c0, CAST_CHUNK)] = _block_rows(c0, CAST_CHUNK)
            return carry
        lax.fori_loop(0, P // CAST_CHUNK, blocks, 0)
        newbuf[...] = jnp.zeros(newbuf.shape, F32)
        wnewbuf[...] = jnp.zeros(wnewbuf.shape, F32)

    _pages_wait(cache_hbm, layer, 0, cmpbuf, csem, npages)

    for c, dst in enumerate((xk, xv)):
        for j in range(npages):
            dst[j * page:(j + 1) * page, :] = cmpbuf[c, :, j * page:(j + 1) * page].T
    kc, vc = _compress_x((xk, xv), nch, w1_ref, bias_sc, b1_ref, w2_ref)

    @pl.when(b + 1 < nseq)
    def _():
        _pages_start(pt_ref, b + 1, cache_hbm, layer, 0, cmpbuf, csem, npages)

    _pages_wait(cache_hbm, layer, 2, selbuf, ssem, npages)

    def pack(c, carry):
        c0 = pl.multiple_of(c * CAST_CHUNK, CAST_CHUNK)
        kaug[0:LANES, pl.ds(c0, CAST_CHUNK)] = selbuf[0, :, pl.ds(c0, CAST_CHUNK)].astype(BF16)
        vt[:, pl.ds(c0, CAST_CHUNK)] = selbuf[1, :, pl.ds(c0, CAST_CHUNK)].astype(BF16)
        return carry
    lax.fori_loop(0, P // CAST_CHUNK, pack, 0)

    @pl.when(b + 1 < nseq)
    def _():
        _pages_start(pt_ref, b + 1, cache_hbm, layer, 2, selbuf, ssem, npages)

    newbuf[0:Q, :] = rows_ref[0, :, 2 * LANES:4 * LANES]
    wnewbuf[0:Q, :] = wnew_ref[0]

    qs, qp1, o_c, sel_t = _nsa_front(q_ref[0], P, Q, kc, vc, nch - 1, n_sel)
    qpos = _rep_all(qp1)
    lane = lax.broadcasted_iota(jnp.int32, (1, LANES), 1)
    new_pos = P + lane
    is_new = lane < Q

    def as_rows(blk):
        if blk.shape[0] < LANES:
            blk = jnp.concatenate([blk, jnp.zeros((LANES - blk.shape[0], LANES), F32)], axis=0)
        return blk.T[0:NSA_KV * Q]
    sel_rows = as_rows(sel_t[0:min(LANES, sel_t.shape[0])])
    sel_new = as_rows(sel_t[nb:nb + SUBLANES])[:, 0:1]

    qaug = jnp.concatenate([qs, _rep_heads(sel_rows - 1.0, Q).astype(BF16)], axis=1)
    s_past = jnp.dot(qaug, kaug[...], preferred_element_type=F32)
    s_new = _bdot_nt(qs, newbuf[:, 0:LANES])
    new_ok = is_new & (new_pos <= qpos) & (_rep_heads(sel_new, Q) > 0.5)
    e1, e2, inv = _softmax2(s_past, s_new, new_ok)
    o_s = (lax.dot_general(e1.astype(BF16), vt[...], (((1,), (1,)), ((), ())), preferred_element_type=F32)
           + _bdot(e2, newbuf[:, LANES:2 * LANES])) * inv

    dpast = qp1 - ((P - wb) + lax.broadcasted_iota(jnp.int32, (1, wb), 1))
    s_wp = (jnp.dot(qs, wt_ref[0, 0].astype(BF16), preferred_element_type=F32)
            + _rep_all(jnp.where((dpast >= 0) & (dpast <= WINDOW), 0.0, -BIG)))
    s_wn = _bdot_nt(qs, wnewbuf[:, 0:LANES])
    dnew = qpos - new_pos
    e1, e2, inv = _softmax2(s_wp, s_wn, is_new & (dnew >= 0) & (dnew <= WINDOW))
    o_w = (lax.dot_general(e1.astype(BF16), wt_ref[0, 1].astype(BF16), (((1,), (1,)), ((), ())),
                           preferred_element_type=F32)
           + _bdot(e2, wnewbuf[:, LANES:2 * LANES])) * inv

    _nsa_combine(gate_ref[0], o_c, o_s, o_w, Q, o_ref)


def _nsa_sample(page_table, nq, gate, rows, win_t, win_new, cache_t, layer, cw):
    DB, Q, HD = nq.shape
    npages = page_table.shape[1]
    page = cache_t.shape[-1]
    P = npages * page
    wb = win_t.shape[-1]
    nb = P // SEL_BLOCK
    n_sel = -(-(P + Q) // SEL_BLOCK)
    w1, pe, b1, w2 = cw
    assert Q == SUBLANES and P % SEL_BLOCK == 0 and Q <= SEL_BLOCK and nb <= LANES and n_sel >= SEL_TOPK
    assert P % CAST_CHUNK == 0 and (P + Q) // CMP_STRIDE == P // CMP_STRIDE and page == LANES
    bs = lambda shape: pl.BlockSpec((1,) + shape, lambda b, pt: (b,) + (0,) * len(shape))
    win_spec = pl.BlockSpec((None, 1, 2, LANES, wb), lambda b, pt: (layer, b, 0, 0, 0))
    grid_spec = pltpu.PrefetchScalarGridSpec(
        num_scalar_prefetch=1, grid=(DB,),
        in_specs=[bs((Q, HD)), bs((Q, LANES)), bs((Q, 4 * LANES)), win_spec, bs((Q, 2 * LANES)),
                  pl.BlockSpec(memory_space=pl.ANY),
                  _pspec(w1), _pspec(pe), _pspec(b1), _pspec(w2)],
        out_specs=bs((Q, HD)),
        scratch_shapes=[pltpu.VMEM((2, LANES, P), F32), pltpu.VMEM((2, LANES, P), F32),
                        pltpu.VMEM((P, LANES), F32), pltpu.VMEM((P, LANES), F32),
                        pltpu.VMEM((2 * LANES, P), BF16), pltpu.VMEM((LANES, P), BF16),
                        pltpu.VMEM((LANES, 2 * LANES), F32), pltpu.VMEM((LANES, 2 * LANES), F32),
                        pltpu.VMEM((2, SUBLANES, 4 * LANES), F32),
                        pltpu.SemaphoreType.DMA(()), pltpu.SemaphoreType.DMA(())])
    return pl.pallas_call(
        functools.partial(_nsa_sample_kernel, layer=layer, npages=npages, P=P, Q=Q, wb=wb, nb=nb, n_sel=n_sel),
        grid_spec=grid_spec,
        out_shape=jax.ShapeDtypeStruct((DB, Q, HD), F32),
        compiler_params=_cparams(1),
        name="nsa_sample",
    )(page_table, nq, gate, rows, win_t, win_new, cache_t, _parg(w1), _parg(pe), _parg(b1), _parg(w2))


def _ret_kernel(q_ref, k_ref, v_ref, g_ref, cos_ref, sin_ref, s0_ref, gn_ref, o_ref, snew_ref, s_sc,
                *, C, nC, nseq):
    c = pl.program_id(1)

    @pl.when(c == 0)
    def _():
        s_sc[...] = s0_ref[...]

    cosf = cos_ref[...]
    sinf = sin_ref[...]
    diff = (lax.broadcasted_iota(jnp.int32, (C, C), 0) - lax.broadcasted_iota(jnp.int32, (C, C), 1)).astype(F32)
    ii = lax.broadcasted_iota(jnp.int32, (C, 1), 0).astype(F32)
    half = RET_DK // 2
    for h in range(RET_HEADS):
        lg = math.log(1.0 - 2.0 ** (-5.0 - h))
        hs = slice(h * RET_DK, (h + 1) * RET_DK)
        decay = jnp.where(diff >= 0, jnp.exp(jnp.maximum(diff, 0.0) * lg), 0.0)
        cross = jnp.exp((ii + 1.0) * lg)
        kweight = jnp.exp((C - 1.0 - ii) * lg)
        for b in range(nseq):
            q = q_ref[b, :, hs]
            k = k_ref[b, :, hs]
            v = v_ref[b, :, hs]
            qr = q * cosf + pltpu.roll(q, half, 1) * sinf
            kr = (k * cosf + pltpu.roll(k, half, 1) * sinf) * (RET_DK ** -0.5)
            o_inner = _bdot(_bdot_nt(qr, kr) * decay, v)
            s_old = s_sc[b, h]
            o_cross = _bdot(qr, s_old) * cross
            kv = lax.dot_general((kr * kweight).astype(BF16), v.astype(BF16), (((0,), (0,)), ((), ())),
                                 preferred_element_type=F32)
            s_sc[b, h] = math.exp(C * lg) * s_old + kv
            o = o_inner + o_cross
            mu = jnp.mean(o, axis=-1, keepdims=True)
            var = jnp.mean(jnp.square(o - mu), axis=-1, keepdims=True)
            gate = g_ref[b, :, hs]
            o_ref[b, :, hs] = (((o - mu) * lax.rsqrt(var + EPS)) * gn_ref[:, hs]
                               * (gate * _sigmoid(gate))).astype(o_ref.dtype)

    @pl.when(c == nC - 1)
    def _():
        snew_ref[...] = s_sc[...]


def _retention(rq, rk, rv, rg, cosf, sinf, s0, gn, nseq, out_dtype):
    B, T, W = rq.shape
    C = RET_CHUNK if (T >= RET_CHUNK and T % RET_CHUNK == 0) else T
    nC = T // C
    tok = pl.BlockSpec((nseq, C, W), lambda b, c: (b, c, 0))
    tab = pl.BlockSpec((C, RET_DK), lambda b, c: (c, 0))
    st = pl.BlockSpec((nseq,) + s0.shape[1:], lambda b, c: (b, 0, 0, 0))
    return pl.pallas_call(
        functools.partial(_ret_kernel, C=C, nC=nC, nseq=nseq),
        grid=(B // nseq, nC),
        in_specs=[tok, tok, tok, tok, tab, tab, st, _pspec(gn)],
        out_specs=[tok, st],
        out_shape=[jax.ShapeDtypeStruct((B, T, W), out_dtype), jax.ShapeDtypeStruct(s0.shape, F32)],
        scratch_shapes=[pltpu.VMEM((nseq,) + s0.shape[1:], F32)],
        compiler_params=_cparams(2),
        name="retention",
    )(rq, rk, rv, rg, cosf, sinf, s0, _parg(gn))


def _shift_carry(x, k, tail8):
    r = pltpu.roll(x, k, 0)
    row8 = lax.broadcasted_iota(jnp.int32, (SUBLANES, 1), 0)
    first = jnp.where(row8 >= k, r[:SUBLANES], pltpu.roll(tail8, k, 0))
    return jnp.concatenate([first, r[SUBLANES:]], axis=0)


def _shift_seg(x, k, fill, tpos):
    return jnp.where(tpos >= k, pltpu.roll(x, k, 0), fill)


def _rglru_kernel(x_ref, gate_ref, st_ref, h0_ref, cw_ref, cb_ref, wa_ref, ba_ref, wx_ref, bx_ref, lam_ref,
                  o_ref, h_ref, tail_sc, h_sc, *, tm, seg):
    carry = seg == 0
    x = x_ref[0]
    rows = lax.broadcasted_iota(jnp.int32, (tm, 1), 0)
    if carry:
        @pl.when(pl.program_id(1) == 0)
        def _():
            tail_sc[...] = st_ref[0]
            h_sc[...] = h0_ref[0]
        tail8 = tail_sc[...]
        shifted = [_shift_carry(x, k, tail8) for k in range(1, RG_CONV)]
        tpos = rows & (SUBLANES - 1)
    else:
        tpos = rows & (seg - 1)
        shifted = [_shift_seg(x, k, st_ref[k - 1], tpos) for k in range(1, RG_CONV)]
    xc = cb_ref[...] + cw_ref[RG_CONV - 1:RG_CONV, :] * x
    for k in range(1, RG_CONV):
        xc = xc + cw_ref[RG_CONV - 1 - k:RG_CONV - k, :] * shifted[k - 1]
    r = _sigmoid(_bdot(xc, wa_ref[...]) + ba_ref[...])
    i = _sigmoid(_bdot(xc, wx_ref[...]) + bx_ref[...])
    lam = lam_ref[...]
    softplus = jnp.maximum(-lam, 0.0) + jnp.log(1.0 + jnp.exp(-jnp.abs(lam)))
    log_a = (-RG_C * r) * softplus
    a = jnp.exp(log_a)
    gap = 1.0 - a * a
    bt = jnp.where(gap > 0, gap * lax.rsqrt(gap), 0.0) * (i * xc)
    if not carry:
        bt = bt + a * h0_ref[0]
    k = 1
    while k < SUBLANES:
        ok = tpos >= k
        a_prev = jnp.where(ok, pltpu.roll(a, k, 0), 1.0)
        b_prev = jnp.where(ok, pltpu.roll(bt, k, 0), 0.0)
        bt = a * b_prev + bt
        a = a * a_prev
        k *= 2
    if carry:
        h_prev = h_sc[SUBLANES - 1:SUBLANES, :]
        groups = []
        for g0 in range(0, tm, SUBLANES):
            h_g = bt[g0:g0 + SUBLANES] + a[g0:g0 + SUBLANES] * h_prev
            groups.append(h_g)
            h_prev = h_g[SUBLANES - 1:SUBLANES, :]
        bt = jnp.concatenate(groups, axis=0)
    o_ref[0] = (bt * _gelu(gate_ref[0].astype(F32))).astype(o_ref.dtype)
    if carry:
        tail_sc[...] = x[tm - SUBLANES:]
        h_sc[...] = bt[tm - SUBLANES:]
        h_ref[0] = bt[tm - SUBLANES:]
    else:
        h_ref[0] = bt


def _rglru(rx, rgate, st, h0, rw, tm, seg):
    G, Tg, W = rx.shape
    tok = pl.BlockSpec((1, tm, W), lambda g, t: (g, t, 0))
    if seg == 0:
        st_spec = pl.BlockSpec((1, SUBLANES, W), lambda g, t: (g, 0, 0))
        h0_spec = pl.BlockSpec((1, SUBLANES, W), lambda g, t: (g, 0, 0))
        h_spec = pl.BlockSpec((1, SUBLANES, W), lambda g, t: (g, 0, 0))
        h_shape = (G, SUBLANES, W)
    else:
        st_spec = pl.BlockSpec(st.shape, lambda g, t: (0, 0, 0))
        h0_spec = tok
        h_spec = tok
        h_shape = (G, Tg, W)
    return pl.pallas_call(
        functools.partial(_rglru_kernel, tm=tm, seg=seg),
        grid=(G, Tg // tm),
        in_specs=[tok, tok, st_spec, h0_spec] + [_pspec(a) for a in rw],
        out_specs=[tok, h_spec],
        out_shape=[jax.ShapeDtypeStruct((G, Tg, W), rgate.dtype), jax.ShapeDtypeStruct(h_shape, F32)],
        scratch_shapes=[pltpu.VMEM((SUBLANES, W), F32), pltpu.VMEM((SUBLANES, W), F32)],
        compiler_params=_cparams(2),
        name="rglru",
    )(rx, rgate, st, h0, *[_parg(a) for a in rw])


def _mix_ffn_kernel(x_ref, oa_ref, or_ref, oc_ref, mg_ref, gt1_ref, wa_ref, wb_ref, wc_ref, wo_ref,
                    g_ref, sc_ref, sh_ref, gt_ref, st_ref, wup_ref, cw_ref, cb_ref, wdn_ref, fg_ref,
                    y_ref, fnew_ref, tail_sc, *, tm, seg, F, chunks, final):
    D = x_ref.shape[2]
    pa = _bdot(oa_ref[0], wa_ref[...])
    pb = _bdot(or_ref[0], wb_ref[...])
    pc = _bdot(oc_ref[0], wc_ref[...])
    gate = lambda i: _sigmoid(mg_ref[0, :, i * D:(i + 1) * D].astype(F32))
    merged = gate(0) * pa + gate(1) * pb + gate(2) * pc
    x = x_ref[0] + gt1_ref[...] * _bdot(merged, wo_ref[...])

    carry = seg == 0
    h = _rms_mod(x, g_ref[...], sc_ref[...], sh_ref[...]).astype(BF16)
    rows = lax.broadcasted_iota(jnp.int32, (tm, 1), 0)
    if carry:
        @pl.when(pl.program_id(1) == 0)
        def _():
            tail_sc[...] = st_ref[0]
    else:
        tpos = rows & (seg - 1)
    acc = jnp.zeros(x.shape, F32)
    for c0, wck in chunks:
        cs = slice(c0, c0 + wck)
        gp = jnp.dot(h, wup_ref[:, c0:c0 + wck], preferred_element_type=F32)
        val = jnp.dot(h, wup_ref[:, F + c0:F + c0 + wck], preferred_element_type=F32)
        if carry:
            tail8 = tail_sc[:, cs]
            shifted = [_shift_carry(gp, k, tail8) for k in range(1, FFN_CONV)]
            tail_sc[:, cs] = gp[tm - SUBLANES:]
            fnew_ref[0, :, cs] = gp[tm - SUBLANES:]
        else:
            shifted = [_shift_seg(gp, k, st_ref[k - 1, :, cs], tpos) for k in range(1, FFN_CONV)]
            fnew_ref[0, :, cs] = gp
        gc = cb_ref[:, cs] + cw_ref[FFN_CONV - 1:FFN_CONV, cs] * gp
        for k in range(1, FFN_CONV):
            gc = gc + cw_ref[FFN_CONV - 1 - k:FFN_CONV - k, cs] * shifted[k - 1]
        act = (gc * _sigmoid(gc)) * val
        acc = acc + _bdot(act, wdn_ref[cs, :])
    y = x + gt_ref[...] * acc
    if final:
        y = (y * lax.rsqrt(jnp.mean(y * y, axis=-1, keepdims=True) + EPS)) * fg_ref[...]
    y_ref[0] = y


def _ffn_chunks(F):
    half = -(-(F // 2) // MXU_DEPTH_V7X) * MXU_DEPTH_V7X
    return ((0, half), (half, F - half)) if 0 < half < F else ((0, F),)


def _mix_ffn(x, oa, orr, oc, mg, gt1, wa, wb, wc, wo, g, sc, sh, gt, st, wup, cw, cb, wdn, fg, tm, seg, final):
    G, Tg, D = x.shape
    F = wdn.shape[0]
    tokw = lambda w: pl.BlockSpec((1, tm, w), lambda g_, t: (g_, t, 0))
    tok = tokw(D)
    if seg == 0:
        st_spec = pl.BlockSpec((1, SUBLANES, F), lambda g_, t: (g_, 0, 0))
        fn_spec = pl.BlockSpec((1, SUBLANES, F), lambda g_, t: (g_, 0, 0))
        fn_shape = (G, SUBLANES, F)
    else:
        st_spec = pl.BlockSpec(st.shape, lambda g_, t: (0, 0, 0))
        fn_spec = pl.BlockSpec((1, tm, F), lambda g_, t: (g_, t, 0))
        fn_shape = (G, Tg, F)
    return pl.pallas_call(
        functools.partial(_mix_ffn_kernel, tm=tm, seg=seg, F=F, chunks=_ffn_chunks(F), final=final),
        grid=(G, Tg // tm),
        in_specs=[tok, tokw(oa.shape[2]), tokw(orr.shape[2]), tokw(oc.shape[2]), tokw(3 * D),
                  _mod_spec(gt1, tm, D), _pspec(wa), _pspec(wb), _pspec(wc), _pspec(wo),
                  _pspec(g), _mod_spec(sc, tm, D), _mod_spec(sh, tm, D), _mod_spec(gt, tm, D),
                  st_spec, _pspec(wup), _pspec(cw), _pspec(cb), _pspec(wdn), _pspec(fg)],
        out_specs=[tok, fn_spec],
        out_shape=[jax.ShapeDtypeStruct((G, Tg, D), F32), jax.ShapeDtypeStruct(fn_shape, F32)],
        scratch_shapes=[pltpu.VMEM((SUBLANES, F), F32)],
        compiler_params=_cparams(2),
        name="mix_ffn",
    )(x, oa, orr, oc, mg, gt1.arr, _parg(wa), _parg(wb), _parg(wc), _parg(wo),
      _parg(g), sc.arr, sh.arr, gt.arr, st, _parg(wup), _parg(cw), _parg(cb), _parg(wdn), fg)


def _head_perm():
    return np.array([(k * NSA_HPG + r) * NSA_HD + d
                     for r in range(NSA_HPG) for k in range(NSA_KV) for d in range(NSA_HD)], np.int32)


def _block_diag(w):
    n, a, b = w.shape[-3:]
    eye = jnp.eye(n, dtype=w.dtype)
    out = jnp.einsum('ij,...iab->...iajb', eye, w)
    return out.reshape(w.shape[:-3] + (n * a, n * b))


def _seg_fill(buf, k, seg):
    B, nb, C = buf.shape
    part = jnp.concatenate([buf[:, nb - k:, :], jnp.zeros((B, seg - k, C), buf.dtype)], axis=1)
    return part.reshape(B * seg, C)


def kernel(x_prompt, x_sample, cache_nsa, cache_nsa_win, state_ret, state_rglru_h, state_rglru_conv,
           state_ffn_conv, page_table, c_prompt, c_sample, norm1_g, norm2_g, w_ada, b_ada, w_in, cmp_pe,
           cmp_w1, cmp_b1, cmp_w2, ret_gn_g, rg_conv_w, rg_conv_b, rg_w_a, rg_b_a, rg_w_x, rg_b_x, rg_lambda,
           w_br_a, w_br_b, w_br_c, w_out, ffn_w_up, ffn_conv_w, ffn_conv_b, ffn_w_down, final_norm_g):
    B, T, D = x_prompt.shape
    DB, Q, _ = x_sample.shape
    L = w_in.shape[0]
    npages = page_table.shape[1]
    page = cache_nsa.shape[2]
    P = npages * page
    NQ = NSA_KV * NSA_HPG * NSA_HD
    NKV = NSA_KV * NSA_HD
    RW = RET_HEADS * RET_DK
    W = rg_conv_w.shape[2]
    F = ffn_w_down.shape[1]
    NS = DB * Q
    wbuf_len = cache_nsa_win.shape[2]
    assert Q == SUBLANES and T >= RG_CONV and P % CMP_STRIDE == 0

    mod = _ada(jnp.concatenate([c_prompt, c_sample], axis=0), w_ada, b_ada)
    cache_t = jnp.transpose(cache_nsa, (0, 1, 3, 4, 5, 2)).reshape(L, cache_nsa.shape[1], 4, NKV, page)
    win_t_all = jnp.transpose(cache_nsa_win, (0, 1, 3, 4, 5, 2)).reshape(L, DB, 2, NKV, wbuf_len)
    perm = _head_perm()

    half = RET_DK // 2
    freq = ROPE_BASE ** (-jnp.arange(half, dtype=F32) / half)

    def rope_tables(pos):
        ang = pos.astype(F32)[:, None] * freq[None, :]
        cos, sin = jnp.cos(ang), jnp.sin(ang)
        return jnp.concatenate([cos, cos], axis=1), jnp.concatenate([-sin, sin], axis=1)

    cos_p, sin_p = rope_tables(jnp.arange(T, dtype=jnp.int32))
    cos_s, sin_s = rope_tables(P + jnp.arange(Q, dtype=jnp.int32))

    widths = (NQ, 4 * NKV, 2 * NKV, LANES, RW, RW, RW, RW, W, W, 3 * D)
    dt_prompt = (F32,) * 9 + (BF16, BF16)
    dt_sample = (F32,) * 11
    offs = np.cumsum((0, NQ, 6 * NKV, 3 * NSA_KV * NSA_HPG, RW, RW, RW, RW, W, W, 3 * D))
    ngate = 3 * NSA_KV * NSA_HPG

    xp = x_prompt
    xs = x_sample.reshape(1, NS, D)
    outs_p = [[] for _ in range(6)]
    outs_s = [[] for _ in range(6)]
    tm_p = 256 if T % 256 == 0 else T

    wi = w_in.astype(BF16)
    w_cat_all = jnp.concatenate([
        wi[:, :, offs[0]:offs[1]][:, :, perm],
        wi[:, :, offs[1]:offs[2]],
        jnp.pad(wi[:, :, offs[2]:offs[3]], ((0, 0), (0, 0), (0, LANES - ngate))),
        wi[:, :, offs[3]:]], axis=2)
    grouped = lambda w: _block_diag(jnp.broadcast_to(w[..., None, :, :], w.shape[:-2] + (NSA_KV,) + w.shape[-2:]))
    cw1_all = jnp.concatenate([grouped(cmp_w1[:, :, :CMP_STRIDE]), grouped(cmp_w1[:, :, CMP_STRIDE:])], axis=-1)
    cw1_all = cw1_all.reshape(L, 2, CMP_STRIDE * NKV, 4 * LANES).astype(BF16)
    cpe_all = jnp.tile(cmp_pe, (1, 1, 1, NSA_KV)).reshape(L, 2, 2, CMP_STRIDE * NKV)
    cb1_all = jnp.tile(cmp_b1, (1, 1, NSA_KV))[:, :, None, :]
    cw2_all = grouped(cmp_w2).astype(BF16)
    row = lambda a: a[:, None, :]
    rw_all = (rg_conv_w, row(rg_conv_b), _block_diag(rg_w_a).astype(BF16), row(rg_b_a),
              _block_diag(rg_w_x).astype(BF16), row(rg_b_x), row(rg_lambda))
    wa_all = w_br_a[:, perm].astype(BF16)
    wb_all = w_br_b.astype(BF16)
    wc_all = w_br_c.astype(BF16)
    wo_all = w_out.astype(BF16)
    wup_all = ffn_w_up.astype(BF16)
    wdn_all = ffn_w_down.astype(BF16)
    g1_all, g2_all, gn_all, fcb_all = row(norm1_g), row(norm2_g), row(ret_gn_g), row(ffn_conv_b)
    fg = final_norm_g[None]
    mod_p = mod[:, :B].reshape(L, B, 1, 6 * D)
    mod_s = jnp.repeat(mod[:, B:], Q, axis=1)

    for l in range(L):
        lay = lambda a: _LayerOf(a, l)
        w_cat = lay(w_cat_all)
        cw = (lay(cw1_all), lay(cpe_all), lay(cb1_all), lay(cw2_all))
        rw = tuple(lay(a) for a in rw_all)
        wa, wb, wc, wo, wup, wdn = (lay(a) for a in (wa_all, wb_all, wc_all, wo_all, wup_all, wdn_all))
        g1, g2, gn, fcw, fcb = (lay(a) for a in (g1_all, g2_all, gn_all, ffn_conv_w, fcb_all))
        final = l == L - 1

        m = [_ModOf(mod_p, l, i) for i in range(6)]
        (nq, rows, win, gate, rq, rk, rv, rg, rx, rgate, mg) = _inproj(xp, g1, m[1], m[0], w_cat, widths,
                                                                        dt_prompt, tm_p)
        kc, vc = _compress_prompt(rows, cw)
        o_a = _nsa_prompt(nq, gate, kc, vc, rows, win)
        o_r, s_new = _retention(rq, rk, rv, rg, cos_p, sin_p,
                                jnp.zeros((B, RET_HEADS, RET_DK, RET_DK), F32), gn, B, BF16)
        zs = jnp.zeros((B, SUBLANES, W), F32)
        o_c, h_tail = _rglru(rx, rgate, zs, zs, rw, tm_p, 0)
        xp, f_tail = _mix_ffn(xp, o_a, o_r, o_c, mg, m[2], wa, wb, wc, wo, g2, m[4], m[3], m[5],
                              jnp.zeros((B, SUBLANES, F), F32), wup, fcw, fcb, wdn, fg, tm_p, 0, final)
        wn = min(WINDOW, T)
        outs_p[0].append(rows.reshape(B, T, 4, NSA_KV, NSA_HD))
        outs_p[1].append(win[:, T - wn:].reshape(B, wn, 2, NSA_KV, NSA_HD))
        outs_p[2].append(s_new)
        outs_p[3].append(h_tail[:, SUBLANES - 1])
        outs_p[4].append(rx[:, T - (RG_CONV - 1):])
        outs_p[5].append(f_tail[:, SUBLANES - (FFN_CONV - 1):])

        ms = [_ModOf(mod_s, l, i) for i in range(6)]
        (nq, rows, win, gate, rq, rk, rv, rg, rx, rgate, mg) = _inproj(xs, g1, ms[1], ms[0], w_cat, widths,
                                                                        dt_sample, NS)
        r3 = lambda a: a.reshape(DB, Q, a.shape[-1])
        o_a = _nsa_sample(page_table, r3(nq), r3(gate), r3(rows), win_t_all, r3(win), cache_t, l, cw)
        o_r, s_new = _retention(r3(rq), r3(rk), r3(rv), r3(rg), cos_s, sin_s, state_ret[l].astype(F32), gn,
                                math.gcd(DB, SUBLANES), F32)
        cbuf = state_rglru_conv[l]
        st = jnp.stack([_seg_fill(cbuf, k, Q) for k in range(1, RG_CONV)])
        h0 = jnp.pad(state_rglru_h[l].astype(F32)[:, None, :], ((0, 0), (0, Q - 1), (0, 0))).reshape(1, NS, W)
        o_c, h_all = _rglru(rx, rgate, st, h0, rw, NS, Q)
        fbuf = state_ffn_conv[l]
        fst = jnp.stack([_seg_fill(fbuf, k, Q) for k in range(1, FFN_CONV)])
        xs, g_all = _mix_ffn(xs, o_a.reshape(1, NS, NQ), o_r.reshape(1, NS, RW), o_c, mg, ms[2], wa, wb, wc, wo,
                             g2, ms[4], ms[3], ms[5], fst, wup, fcw, fcb, wdn, fg, NS, Q, final)
        keys = jnp.concatenate([cache_nsa_win[l], r3(win).reshape(DB, Q, 2, NSA_KV, NSA_HD)], axis=1)
        outs_s[0].append(rows.reshape(DB, Q, 4, NSA_KV, NSA_HD))
        outs_s[1].append(keys[:, Q:])
        outs_s[2].append(s_new)
        outs_s[3].append(h_all.reshape(DB, Q, W)[:, Q - 1])
        outs_s[4].append(jnp.concatenate([cbuf, rx.reshape(DB, Q, W)], axis=1)[:, Q:])
        outs_s[5].append(jnp.concatenate([fbuf, g_all.reshape(DB, Q, F)], axis=1)[:, Q:])

    sp = [jnp.stack(a) for a in outs_p]
    ss = [jnp.stack(a) for a in outs_s]
    return (xp, xs.reshape(DB, Q, D), sp[0], ss[0], sp[1], ss[1], sp[2], ss[2],
            sp[3], ss[3], sp[4], ss[4], sp[5], ss[5])
```

```python
import functools
import math

import numpy as np
import jax
import jax.numpy as jnp
from jax import lax
from jax.experimental import pallas as pl
from jax.experimental.pallas import tpu as pltpu

F32 = jnp.float32
BF16 = jnp.bfloat16

NSA_KV = 2
NSA_HPG = 4
NSA_HD = 64
CMP_STRIDE = 16
CMP_BLOCK = 32
SEL_BLOCK = 64
SEL_TOPK = 16
SEL_FORCE = 1e4
WINDOW = 512
QBLOCK = 256
RET_HEADS = 4
RET_DK = 128
RET_CHUNK = 128
ROPE_BASE = 10000.0
RG_CONV = 4
RG_C = 8.0
FFN_CONV = 3
EPS = 1e-6

NEG = -1e30
BIG = float(2 ** 60)
LOG2E = 1.4426950408889634
SUBLANES = 8
LANES = 128
VMEM_LIMIT_V7X = 56 * 1024 * 1024
SEL_CHUNK = 512
CAST_CHUNK = 1024
PAGE_UNROLL = 8
MXU_DEPTH_V7X = 256


def _cparams(n_grid):
    return pltpu.CompilerParams(dimension_semantics=("arbitrary",) * n_grid,
                                vmem_limit_bytes=VMEM_LIMIT_V7X)


def _bdot(a, b):
    return jnp.dot(a.astype(BF16), b.astype(BF16), preferred_element_type=F32)


def _bdot_nt(a, b):
    return lax.dot_general(a.astype(BF16), b.astype(BF16), (((1,), (1,)), ((), ())),
                           preferred_element_type=F32)


def _split(a):
    hi = a.astype(BF16)
    lo = (a - hi.astype(F32)).astype(BF16)
    return hi, lo


def _dot3(a, b):
    ah, al = _split(a)
    bh, bl = _split(b)
    d = functools.partial(jnp.dot, preferred_element_type=F32)
    return d(ah, bh) + d(al, bh) + d(ah, bl)


def _sigmoid(x):
    return 0.5 * jnp.tanh(0.5 * x) + 0.5


def _gelu(x):
    return 0.5 * x * (1.0 + jnp.tanh(0.7978845608028654 * (x + 0.044715 * (x * x * x))))


def _rms_mod(x, g, sc, sh):
    y = x * lax.rsqrt(jnp.mean(x * x, axis=-1, keepdims=True) + EPS)
    return (y * g) * (1.0 + sc) + sh


def _ada_kernel(c_ref, w_ref, b_ref, o_ref):
    c = c_ref[...]
    o_ref[0] = _dot3(c * _sigmoid(c), w_ref[0]) + b_ref[0]


def _ada(c_all, w_ada, b_ada):
    L, D, E = w_ada.shape
    n = c_all.shape[0]
    tn = 1536 if E % 1536 == 0 else E
    return pl.pallas_call(
        _ada_kernel,
        grid=(L, E // tn),
        in_specs=[pl.BlockSpec((n, D), lambda l, j: (0, 0)),
                  pl.BlockSpec((1, D, tn), lambda l, j: (l, 0, j)),
                  pl.BlockSpec((1, 1, tn), lambda l, j: (l, 0, j))],
        out_specs=pl.BlockSpec((1, n, tn), lambda l, j: (l, 0, j)),
        out_shape=jax.ShapeDtypeStruct((L, n, E), F32),
        compiler_params=_cparams(2),
        name="ada_mod",
    )(c_all, w_ada, b_ada.reshape(L, 1, E))


class _LayerOf:
    def __init__(self, arr, layer):
        self.arr, self.layer, self.shape = arr, layer, arr.shape[1:]


class _ModOf:
    def __init__(self, arr, layer, idx):
        self.arr, self.layer, self.idx = arr, layer, idx


def _mod_spec(m, tm, d):
    l, i = m.layer, m.idx
    if m.arr.ndim == 4:
        return pl.BlockSpec((None, None, 1, d), lambda g, t: (l, g, 0, i))
    return pl.BlockSpec((None, tm, d), lambda g, t: (l, t, i))


def _pspec(p):
    if isinstance(p, _LayerOf):
        nd, l = len(p.shape), p.layer
        return pl.BlockSpec((None,) + tuple(p.shape), lambda *a: (l,) + (0,) * nd, pipeline_mode=pl.Buffered(1))
    nd = p.ndim
    return pl.BlockSpec(p.shape, lambda *a: (0,) * nd, pipeline_mode=pl.Buffered(1))


def _parg(p):
    return p.arr if isinstance(p, _LayerOf) else p


def _inproj_kernel(x_ref, g_ref, sc_ref, sh_ref, w_ref, *o_refs, segs):
    h = _rms_mod(x_ref[0], g_ref[...], sc_ref[...], sh_ref[...]).astype(BF16)
    for (off, wd), o_ref in zip(segs, o_refs):
        o_ref[0] = jnp.dot(h, w_ref[:, off:off + wd], preferred_element_type=F32).astype(o_ref.dtype)


def _inproj(x, g, sc, sh, w, widths, dtypes, tm):
    G, Tg, D = x.shape
    segs, off = [], 0
    for wd in widths:
        segs.append((off, wd))
        off += wd
    return pl.pallas_call(
        functools.partial(_inproj_kernel, segs=tuple(segs)),
        grid=(G, Tg // tm),
        in_specs=[pl.BlockSpec((1, tm, D), lambda g_, t: (g_, t, 0)),
                  _pspec(g), _mod_spec(sc, tm, D), _mod_spec(sh, tm, D), _pspec(w)],
        out_specs=[pl.BlockSpec((1, tm, wd), lambda g_, t: (g_, t, 0)) for wd in widths],
        out_shape=[jax.ShapeDtypeStruct((G, Tg, wd), dt) for wd, dt in zip(widths, dtypes)],
        compiler_params=_cparams(2),
        name="in_proj",
    )(x, _parg(g), sc.arr, sh.arr, _parg(w))


def _compress_bias(w1_ref, pe_ref, bias_sc):
    for c in range(2):
        halves = []
        for h in range(2):
            pe_rows = jnp.broadcast_to(pe_ref[c, h:h + 1, :], (SUBLANES, pe_ref.shape[2]))
            halves.append(_bdot(pe_rows, w1_ref[c, :, h * 2 * LANES:(h + 1) * 2 * LANES]))
        bias_sc[c] = jnp.concatenate(halves, axis=1)


def _compress_x(xrefs, nch, w1_ref, bias_sc, b1_ref, w2_ref):
    outs = []
    for c in range(2):
        lhs = jnp.concatenate([xrefs[c][pl.ds(p, nch, stride=CMP_STRIDE), :].astype(BF16)
                               for p in range(CMP_STRIDE)], axis=1)
        outs.append(_compress_one(c, lhs, nch, w1_ref, bias_sc, b1_ref, w2_ref))
    return outs


def _compress_one(c, lhs, nch, w1_ref, bias_sc, b1_ref, w2_ref):
    last = lax.broadcasted_iota(jnp.int32, (nch, 1), 0) == nch - 1
    acc = jnp.dot(lhs, w1_ref[c], preferred_element_type=F32) + bias_sc[c, 0:1, :]
    lo = acc[:, :2 * LANES]
    hi = acc[:, 2 * LANES:]
    hi_next = jnp.where(last, 0.0, pltpu.roll(hi, nch - 1, 0))
    hid = _gelu(lo + hi_next + b1_ref[c])
    return _bdot(hid, w2_ref[c])


def _compress_prompt_kernel(krows_ref, vrows_ref, w1_ref, pe_ref, b1_ref, w2_ref, kc_ref, vc_ref, bias_sc, *, nch):
    @pl.when(pl.program_id(0) == 0)
    def _():
        _compress_bias(w1_ref, pe_ref, bias_sc)

    kc, vc = _compress_x((krows_ref.at[0], vrows_ref.at[0]), nch, w1_ref, bias_sc, b1_ref, w2_ref)
    kc_ref[0] = kc
    vc_ref[0] = vc


def _compress_prompt(rows, cw):
    B, T, _ = rows.shape
    nch = T // CMP_STRIDE
    w1, pe, b1, w2 = cw
    return pl.pallas_call(
        functools.partial(_compress_prompt_kernel, nch=nch),
        grid=(B,),
        in_specs=[pl.BlockSpec((1, T, LANES), lambda b: (b, 0, 0)),
                  pl.BlockSpec((1, T, LANES), lambda b: (b, 0, 1)),
                  _pspec(w1), _pspec(pe), _pspec(b1), _pspec(w2)],
        out_specs=[pl.BlockSpec((1, nch, LANES), lambda b: (b, 0, 0))] * 2,
        out_shape=[jax.ShapeDtypeStruct((B, nch, LANES), F32)] * 2,
        scratch_shapes=[pltpu.VMEM((2, SUBLANES, 4 * LANES), F32)],
        compiler_params=_cparams(1),
        name="nsa_compress_prompt",
    )(rows, rows, _parg(w1), _parg(pe), _parg(b1), _parg(w2))


def _rep_all(a):
    return jnp.concatenate([a] * (NSA_KV * NSA_HPG), axis=0)


def _rep_heads(a, Qb):
    return jnp.concatenate([a[:Qb]] * NSA_HPG + [a[Qb:]] * NSA_HPG, axis=0)


def _nsa_front(qblk, q0, Qb, kc, vc, n_cmp, n_sel):
    R2 = 2 * Qb
    npad = kc.shape[0]
    lane = lax.broadcasted_iota(jnp.int32, (Qb, LANES), 1)
    lo_half = lane < NSA_HD
    scale = NSA_HD ** -0.5 * LOG2E
    pieces = []
    for k in range(NSA_KV):
        for r in range(NSA_HPG):
            sl = qblk[:, r * LANES:(r + 1) * LANES] * scale
            pieces.append(jnp.where(lo_half if k == 0 else jnp.logical_not(lo_half), sl, 0.0))
    qs = jnp.concatenate(pieces, axis=0).astype(BF16)
    qp1 = q0 + lax.broadcasted_iota(jnp.int32, (Qb, 1), 0)

    n_idx = lax.broadcasted_iota(jnp.int32, (1, npad), 1)
    visible = (n_idx * CMP_STRIDE + (CMP_BLOCK - 1) <= qp1) & (n_idx < n_cmp)
    s = _bdot_nt(qs, kc) + _rep_all(jnp.where(visible, 0.0, -BIG))
    e = jnp.exp2(s - jnp.max(s, axis=-1, keepdims=True))
    any_visible = _rep_all((qp1 >= CMP_BLOCK - 1) & (n_cmp > 0))
    p_c = e * jnp.where(any_visible, 1.0 / jnp.sum(e, axis=-1, keepdims=True), 0.0)
    o_c = _bdot(p_c, vc)

    psum = []
    for k in range(NSA_KV):
        acc = p_c[(k * NSA_HPG) * Qb:(k * NSA_HPG + 1) * Qb]
        for r in range(1, NSA_HPG):
            acc = acc + p_c[(k * NSA_HPG + r) * Qb:(k * NSA_HPG + r + 1) * Qb]
        psum.append(acc)
    psum = jnp.concatenate(psum + [jnp.zeros((LANES - R2, npad), F32)], axis=0)
    nsr = -(-n_sel // SUBLANES) * SUBLANES
    sj = lax.broadcasted_iota(jnp.int32, (nsr, npad), 0) * SEL_BLOCK
    ci = lax.broadcasted_iota(jnp.int32, (nsr, npad), 1) * CMP_STRIDE
    selmap = jnp.where((ci < sj + SEL_BLOCK) & (ci + CMP_BLOCK > sj), 1.0, 0.0).astype(BF16)
    ph, plo = _split(psum)
    imp = _bdot_nt(selmap, ph) + _bdot_nt(selmap, plo)
    col_pos = q0 + (lax.broadcasted_iota(jnp.int32, (1, LANES), 1) & (Qb - 1))
    return qs, qp1, o_c, _select_blocks(imp, col_pos, n_sel)


def _select_blocks(imp, qpos, n_sel):
    nsr, ncol = imp.shape
    j = lax.broadcasted_iota(jnp.int32, (nsr, ncol), 0)
    jf = j.astype(F32)
    cur = qpos >> 6
    forced = (j == 0) | (j == cur) | (j == cur - 1)
    imp = jnp.where(forced, SEL_FORCE, imp)
    imp = jnp.where(j * SEL_BLOCK <= qpos, imp, -SEL_FORCE)
    imp = jnp.where(j < n_sel, imp, NEG)

    def pick(_, carry):
        imp_c, sel_c = carry
        m = jnp.max(imp_c, axis=0, keepdims=True)
        first = jnp.min(jnp.where(imp_c == m, jf, float(nsr)), axis=0, keepdims=True)
        hit = jf == first
        return jnp.where(hit, NEG, imp_c), jnp.where(hit, 1.0, sel_c)

    _, sel = lax.fori_loop(0, min(SEL_TOPK, n_sel), pick, (imp, jnp.zeros((nsr, ncol), F32)), unroll=True)
    return sel


def _nsa_combine(gate, o_c, o_s, o_w, Qb, o_ref):
    lo_half = lax.broadcasted_iota(jnp.int32, (Qb, LANES), 1) < NSA_HD
    g = _sigmoid(gate)
    for r in range(NSA_HPG):
        halves = []
        for k in range(NSA_KV):
            rs = slice((k * NSA_HPG + r) * Qb, (k * NSA_HPG + r + 1) * Qb)
            c = (k * NSA_HPG + r) * 3
            halves.append(g[:, c:c + 1] * o_c[rs] + g[:, c + 1:c + 2] * o_s[rs] + g[:, c + 2:c + 3] * o_w[rs])
        o_ref[0, :, r * LANES:(r + 1) * LANES] = jnp.where(lo_half, halves[0], halves[1])


def _block_columns(k0, n):
    key = k0 + lax.broadcasted_iota(jnp.int32, (n, LANES), 0)
    blk = lax.broadcasted_iota(jnp.int32, (n, LANES), 1)
    return jnp.where((key >> 6) == blk, BIG, 0.0).astype(BF16)


def _block_rows(k0, n):
    key = k0 + lax.broadcasted_iota(jnp.int32, (LANES, n), 1)
    blk = lax.broadcasted_iota(jnp.int32, (LANES, n), 0)
    return jnp.where((key >> 6) == blk, BIG, 0.0).astype(BF16)


def _lanes_all(a):
    return jnp.concatenate([a] * (NSA_KV * NSA_HPG), axis=1)


def _lanes_heads(a, Qb):
    return jnp.concatenate([a[:, :Qb]] * NSA_HPG + [a[:, Qb:]] * NSA_HPG, axis=1)


def _softmax_cols(s):
    e = jnp.exp2(s - jnp.max(s, axis=0, keepdims=True))
    return e, 1.0 / jnp.sum(e, axis=0, keepdims=True)


def _nsa_prompt_kernel(q_ref, gate_ref, kc_ref, vc_ref, rows_ref, win_ref, o_ref, kaug, vt,
                       *, T, Qb, n_cmp, n_sel, wl):
    i = pl.program_id(1)
    q0 = i * Qb
    R = NSA_KV * NSA_HPG * Qb
    R2 = NSA_KV * Qb

    @pl.when(i == 0)
    def _():
        def pack(c, carry):
            r0 = pl.multiple_of(c * SEL_CHUNK, SEL_CHUNK)
            kaug[pl.ds(r0, SEL_CHUNK), 0:LANES] = rows_ref[0, pl.ds(r0, SEL_CHUNK), 2 * LANES:3 * LANES].astype(BF16)
            kaug[pl.ds(r0, SEL_CHUNK), LANES:2 * LANES] = _block_columns(r0, SEL_CHUNK)
            vt[:, pl.ds(r0, SEL_CHUNK)] = rows_ref[0, pl.ds(r0, SEL_CHUNK), 3 * LANES:4 * LANES].T.astype(BF16)
            return carry
        lax.fori_loop(0, T // SEL_CHUNK, pack, 0)

    top = lax.broadcasted_iota(jnp.int32, (LANES, Qb), 0) < NSA_HD
    scale = NSA_HD ** -0.5 * LOG2E
    qblk = q_ref[0]
    q_t = [(qblk[:, r * LANES:(r + 1) * LANES] * scale).T for r in range(NSA_HPG)]
    qs = jnp.concatenate([jnp.where(top if k == 0 else jnp.logical_not(top), q_t[r], 0.0)
                          for k in range(NSA_KV) for r in range(NSA_HPG)], axis=1).astype(BF16)
    qlane = q0 + lax.broadcasted_iota(jnp.int32, (1, Qb), 1)
    qpos = _lanes_all(qlane)

    kc = kc_ref[0]
    npad = kc.shape[0]
    n_idx = lax.broadcasted_iota(jnp.int32, (npad, 1), 0)
    visible = (n_idx * CMP_STRIDE + (CMP_BLOCK - 1) <= qlane) & (n_idx < n_cmp)
    e_c, inv_c = _softmax_cols(_bdot(kc, qs) + _lanes_all(jnp.where(visible, 0.0, -BIG)))
    any_visible = _lanes_all((qlane >= CMP_BLOCK - 1) & (n_cmp > 0))
    p_c = e_c * jnp.where(any_visible, inv_c, 0.0)
    o_c = _bdot(vc_ref[0].T, p_c)

    psum = []
    for k in range(NSA_KV):
        acc = p_c[:, (k * NSA_HPG) * Qb:(k * NSA_HPG + 1) * Qb]
        for r in range(1, NSA_HPG):
            acc = acc + p_c[:, (k * NSA_HPG + r) * Qb:(k * NSA_HPG + r + 1) * Qb]
        psum.append(acc)
    psum = jnp.concatenate(psum, axis=1)
    nsr = -(-n_sel // SUBLANES) * SUBLANES
    sj = lax.broadcasted_iota(jnp.int32, (nsr, npad), 0) * SEL_BLOCK
    ci = lax.broadcasted_iota(jnp.int32, (nsr, npad), 1) * CMP_STRIDE
    selmap = jnp.where((ci < sj + SEL_BLOCK) & (ci + CMP_BLOCK > sj), 1.0, 0.0).astype(BF16)
    ph, plo = _split(psum)
    imp = (jnp.dot(selmap, ph, preferred_element_type=F32)
           + jnp.dot(selmap, plo, preferred_element_type=F32))
    sel = _select_blocks(imp, jnp.concatenate([qlane] * NSA_KV, axis=1), n_sel)
    selm =jnp.concatenate([sel - 1.0, jnp.zeros((LANES - nsr, R2), F32)], axis=0) if nsr < LANES else sel - 1.0
    qaug = jnp.concatenate([qs, _lanes_heads(selm, Qb).astype(BF16)], axis=0)

    def update(s, vcols, carry):
        m, l, acc = carry
        m_new = jnp.maximum(m, jnp.max(s, axis=0, keepdims=True))
        alpha = jnp.exp2(m - m_new)
        p = jnp.exp2(s - m_new)
        l = alpha * l + jnp.sum(p, axis=0, keepdims=True)
        acc = alpha * acc + jnp.dot(vcols, p.astype(BF16), preferred_element_type=F32)
        return m_new, l, acc

    def scores(k0):
        return jnp.dot(kaug[pl.ds(k0, SEL_CHUNK), :], qaug, preferred_element_type=F32)

    def full_chunk(c, carry):
        k0 = pl.multiple_of(c * SEL_CHUNK, SEL_CHUNK)
        return update(scores(k0), vt[:, pl.ds(k0, SEL_CHUNK)], carry)

    n_full = q0 // SEL_CHUNK
    init = (jnp.full((1, R), -4.0 * BIG, F32), jnp.zeros((1, R), F32), jnp.zeros((LANES, R), F32))
    carry = lax.fori_loop(0, n_full, full_chunk, init)
    k0 = pl.multiple_of(n_full * SEL_CHUNK, SEL_CHUNK)
    kpos = k0 + lax.broadcasted_iota(jnp.int32, (SEL_CHUNK, 1), 0)
    s_diag = jnp.where(kpos <= qpos, scores(k0), -2.0 * BIG)
    _, l_s, acc_s = update(s_diag, vt[:, pl.ds(k0, SEL_CHUNK)], carry)
    o_s = acc_s * (1.0 / l_s)

    ws = pl.multiple_of(jnp.maximum(q0 - WINDOW, 0), LANES)
    wk = win_ref[0, pl.ds(ws, wl), 0:LANES]
    wv = win_ref[0, pl.ds(ws, wl), LANES:2 * LANES]
    dpos = qlane - (ws + lax.broadcasted_iota(jnp.int32, (wl, 1), 0))
    e_w, inv_w = _softmax_cols(_bdot(wk, qs) + _lanes_all(jnp.where((dpos >= 0) & (dpos <= WINDOW), 0.0, -BIG)))
    o_w = _bdot(wv.T, e_w) * inv_w

    g = _sigmoid(gate_ref[0]).T
    for r in range(NSA_HPG):
        halves = []
        for k in range(NSA_KV):
            cs = slice((k * NSA_HPG + r) * Qb, (k * NSA_HPG + r + 1) * Qb)
            c = (k * NSA_HPG + r) * 3
            halves.append(g[c:c + 1, :] * o_c[:, cs] + g[c + 1:c + 2, :] * o_s[:, cs] + g[c + 2:c + 3, :] * o_w[:, cs])
        o_ref[0, :, r * LANES:(r + 1) * LANES] = jnp.where(top, halves[0], halves[1]).T.astype(o_ref.dtype)


def _nsa_prompt(nq, gate, kc, vc, rows, win):
    B, T, HD = nq.shape
    Qb = QBLOCK
    nch = kc.shape[1]
    n_sel = -(-T // SEL_BLOCK)
    wl = WINDOW + Qb
    assert T % SEL_CHUNK == 0 and T >= wl and SEL_TOPK <= n_sel <= LANES
    return pl.pallas_call(
        functools.partial(_nsa_prompt_kernel, T=T, Qb=Qb, n_cmp=nch - 1, n_sel=n_sel, wl=wl),
        grid=(B, T // Qb),
        in_specs=[pl.BlockSpec((1, Qb, HD), lambda b, i: (b, i, 0)),
                  pl.BlockSpec((1, Qb, LANES), lambda b, i: (b, i, 0)),
                  pl.BlockSpec((1, nch, LANES), lambda b, i: (b, 0, 0)),
                  pl.BlockSpec((1, nch, LANES), lambda b, i: (b, 0, 0)),
                  pl.BlockSpec((1, T, 4 * LANES), lambda b, i: (b, 0, 0)),
                  pl.BlockSpec((1, T, 2 * LANES), lambda b, i: (b, 0, 0))],
        out_specs=pl.BlockSpec((1, Qb, HD), lambda b, i: (b, i, 0)),
        out_shape=jax.ShapeDtypeStruct((B, T, HD), BF16),
        scratch_shapes=[pltpu.VMEM((T, 2 * LANES), BF16), pltpu.VMEM((LANES, T), BF16)],
        compiler_params=_cparams(2),
        name="nsa_attn_prompt",
    )(nq, gate, kc, vc, rows, win)


def _pages_copy(cache_hbm, layer, page, r0, dst, j, sem):
    n = cache_hbm.shape[-1]
    return pltpu.make_async_copy(cache_hbm.at[layer, page, pl.ds(r0, 2)],
                                 dst.at[:, :, pl.ds(pl.multiple_of(j * n, n), n)], sem)


def _pages_start(pt_ref, b, cache_hbm, layer, r0, dst, sem, npages):
    def issue(j, carry):
        _pages_copy(cache_hbm, layer, pt_ref[b, j], r0, dst, j, sem).start()
        return carry
    lax.fori_loop(0, npages, issue, 0, unroll=PAGE_UNROLL)


def _pages_wait(cache_hbm, layer, r0, dst, sem, npages):
    def wait(j, carry):
        _pages_copy(cache_hbm, layer, 0, r0, dst, j, sem).wait()
        return carry
    lax.fori_loop(0, npages, wait, 0, unroll=PAGE_UNROLL)


def _softmax2(s1, s2, mask2):
    s2 = jnp.where(mask2, s2, -2.0 * BIG)
    m = jnp.maximum(jnp.max(s1, axis=-1, keepdims=True), jnp.max(s2, axis=-1, keepdims=True))
    e1 = jnp.exp2(s1 - m)
    e2 = jnp.exp2(s2 - m)
    den = jnp.sum(e1, axis=-1, keepdims=True) + jnp.sum(e2, axis=-1, keepdims=True)
    return e1, e2, 1.0 / den


def _nsa_sample_kernel(pt_ref, q_ref, gate_ref, rows_ref, wt_ref, wnew_ref, cache_hbm,
                       w1_ref, pe_ref, b1_ref, w2_ref, o_ref,
                       cmpbuf, selbuf, xk, xv, kaug, vt, newbuf, wnewbuf, bias_sc, perm_sc, csem, ssem,
                       *, layer, npages, P, Q, wb, nb, n_sel):
    b = pl.program_id(0)
    nseq = pl.num_programs(0)
    page = cache_hbm.shape[-1]
    nch = P // CMP_STRIDE

    @pl.when(b == 0)
    def _():
        _pages_start(pt_ref, 0, cache_hbm, layer, 0, cmpbuf, csem, npages)
        _pages_start(pt_ref, 0, cache_hbm, layer, 2, selbuf, ssem, npages)
        _compress_bias(w1_ref, pe_ref, bias_sc)

        def blocks(c, carry):
            c0 = pl.multiple_of(c * CAST_CHUNK, CAST_CHUNK)
            kaug[LANES:2 * LANES, pl.ds(c0, CAST_CHUNK)] = _block_rows(c0, CAST_CHUNK)
            return carry
        lax.fori_loop(0, P // CAST_CHUNK, blocks, 0)
        newbuf[...] = jnp.zeros(newbuf.shape, F32)
        wnewbuf[...] = jnp.zeros(wnewbuf.shape, F32)
        src = lax.broadcasted_iota(jnp.int32, (2 * page, 2 * page), 0)
        dst_col = lax.broadcasted_iota(jnp.int32, (2 * page, 2 * page), 1)
        t = src & (page - 1)
        want = (src - t) + (t % CMP_STRIDE) * (page // CMP_STRIDE) + t // CMP_STRIDE
        perm_sc[...] = jnp.where(dst_col == want, 1.0, 0.0).astype(BF16)

    _pages_wait(cache_hbm, layer, 0, cmpbuf, csem, npages)
    per_page = page // CMP_STRIDE
    for c, dst in enumerate((xk, xv)):
        for jp in range(npages // 2):
            cols = cmpbuf[c, :, 2 * jp * page:2 * (jp + 1) * page].astype(BF16)
            regrouped = jnp.dot(cols, perm_sc[...], preferred_element_type=F32)
            for half in range(2):
                tok = regrouped[:, half * page:(half + 1) * page].T
                r0 = (2 * jp + half) * per_page
                for p in range(CMP_STRIDE):
                    dst[r0:r0 + per_page, p * LANES:(p + 1) * LANES] = tok[p * per_page:(p + 1) * per_page, :]
    kc = _compress_one(0, xk[...].astype(BF16), nch, w1_ref, bias_sc, b1_ref, w2_ref)
    vc = _compress_one(1, xv[...].astype(BF16), nch, w1_ref, bias_sc, b1_ref, w2_ref)

    @pl.when(b + 1 < nseq)
    def _():
        _pages_start(pt_ref, b + 1, cache_hbm, layer, 0, cmpbuf, csem, npages)

    _pages_wait(cache_hbm, layer, 2, selbuf, ssem, npages)

    def pack(c, carry):
        c0 = pl.multiple_of(c * CAST_CHUNK, CAST_CHUNK)
        kaug[0:LANES, pl.ds(c0, CAST_CHUNK)] = selbuf[0, :, pl.ds(c0, CAST_CHUNK)].astype(BF16)
        vt[:, pl.ds(c0, CAST_CHUNK)] = selbuf[1, :, pl.ds(c0, CAST_CHUNK)].astype(BF16)
        return carry
    lax.fori_loop(0, P // CAST_CHUNK, pack, 0)

    @pl.when(b + 1 < nseq)
    def _():
        _pages_start(pt_ref, b + 1, cache_hbm, layer, 2, selbuf, ssem, npages)

    newbuf[0:Q, :] = rows_ref[0, :, 2 * LANES:4 * LANES]
    wnewbuf[0:Q, :] = wnew_ref[0]

    qs, qp1, o_c, sel_t = _nsa_front(q_ref[0], P, Q, kc, vc, nch - 1, n_sel)
    qpos = _rep_all(qp1)
    lane = lax.broadcasted_iota(jnp.int32, (1, LANES), 1)
    new_pos = P + lane
    is_new = lane < Q

    def as_rows(blk):
        if blk.shape[0] < LANES:
            blk = jnp.concatenate([blk, jnp.zeros((LANES - blk.shape[0], LANES), F32)], axis=0)
        return blk.T[0:NSA_KV * Q]
    sel_rows = as_rows(sel_t[0:min(LANES, sel_t.shape[0])])
    sel_new = as_rows(sel_t[nb:nb + SUBLANES])[:, 0:1]

    qaug = jnp.concatenate([qs, _rep_heads(sel_rows - 1.0, Q).astype(BF16)], axis=1)
    s_past = jnp.dot(qaug, kaug[...], preferred_element_type=F32)
    s_new = _bdot_nt(qs, newbuf[:, 0:LANES])
    new_ok = is_new & (new_pos <= qpos) & (_rep_heads(sel_new, Q) > 0.5)
    e1, e2, inv = _softmax2(s_past, s_new, new_ok)
    o_s = (lax.dot_general(e1.astype(BF16), vt[...], (((1,), (1,)), ((), ())), preferred_element_type=F32)
           + _bdot(e2, newbuf[:, LANES:2 * LANES])) * inv

    dpast = qp1 - ((P - wb) + lax.broadcasted_iota(jnp.int32, (1, wb), 1))
    s_wp = (jnp.dot(qs, wt_ref[0, 0].astype(BF16), preferred_element_type=F32)
            + _rep_all(jnp.where((dpast >= 0) & (dpast <= WINDOW), 0.0, -BIG)))
    s_wn = _bdot_nt(qs, wnewbuf[:, 0:LANES])
    dnew = qpos - new_pos
    e1, e2, inv = _softmax2(s_wp, s_wn, is_new & (dnew >= 0) & (dnew <= WINDOW))
    o_w = (lax.dot_general(e1.astype(BF16), wt_ref[0, 1].astype(BF16), (((1,), (1,)), ((), ())),
                           preferred_element_type=F32)
           + _bdot(e2, wnewbuf[:, LANES:2 * LANES])) * inv

    _nsa_combine(gate_ref[0], o_c, o_s, o_w, Q, o_ref)


def _nsa_sample(page_table, nq, gate, rows, win_t, win_new, cache_t, layer, cw):
    DB, Q, HD = nq.shape
    npages = page_table.shape[1]
    page = cache_t.shape[-1]
    P = npages * page
    wb = win_t.shape[-1]
    nb = P // SEL_BLOCK
    n_sel = -(-(P + Q) // SEL_BLOCK)
    w1, pe, b1, w2 = cw
    assert Q == SUBLANES and P % SEL_BLOCK == 0 and Q <= SEL_BLOCK and nb <= LANES and n_sel >= SEL_TOPK
    assert P % CAST_CHUNK == 0 and (P + Q) // CMP_STRIDE == P // CMP_STRIDE and page == LANES
    assert npages % 2 == 0 and page // CMP_STRIDE == SUBLANES
    bs = lambda shape: pl.BlockSpec((1,) + shape, lambda b, pt: (b,) + (0,) * len(shape))
    win_spec = pl.BlockSpec((None, 1, 2, LANES, wb), lambda b, pt: (layer, b, 0, 0, 0))
    grid_spec = pltpu.PrefetchScalarGridSpec(
        num_scalar_prefetch=1, grid=(DB,),
        in_specs=[bs((Q, HD)), bs((Q, LANES)), bs((Q, 4 * LANES)), win_spec, bs((Q, 2 * LANES)),
                  pl.BlockSpec(memory_space=pl.ANY),
                  _pspec(w1), _pspec(pe), _pspec(b1), _pspec(w2)],
        out_specs=bs((Q, HD)),
        scratch_shapes=[pltpu.VMEM((2, LANES, P), F32), pltpu.VMEM((2, LANES, P), F32),
                        pltpu.VMEM((P // CMP_STRIDE, CMP_STRIDE * LANES), F32),
                        pltpu.VMEM((P // CMP_STRIDE, CMP_STRIDE * LANES), F32),
                        pltpu.VMEM((2 * LANES, P), BF16), pltpu.VMEM((LANES, P), BF16),
                        pltpu.VMEM((LANES, 2 * LANES), F32), pltpu.VMEM((LANES, 2 * LANES), F32),
                        pltpu.VMEM((2, SUBLANES, 4 * LANES), F32), pltpu.VMEM((2 * page, 2 * page), BF16),
                        pltpu.SemaphoreType.DMA(()), pltpu.SemaphoreType.DMA(())])
    return pl.pallas_call(
        functools.partial(_nsa_sample_kernel, layer=layer, npages=npages, P=P, Q=Q, wb=wb, nb=nb, n_sel=n_sel),
        grid_spec=grid_spec,
        out_shape=jax.ShapeDtypeStruct((DB, Q, HD), F32),
        compiler_params=_cparams(1),
        name="nsa_sample",
    )(page_table, nq, gate, rows, win_t, win_new, cache_t, _parg(w1), _parg(pe), _parg(b1), _parg(w2))


def _ret_kernel(q_ref, k_ref, v_ref, g_ref, cos_ref, sin_ref, s0_ref, gn_ref, o_ref, snew_ref, s_sc,
                *, C, nC, nseq):
    c = pl.program_id(1)

    @pl.when(c == 0)
    def _():
        s_sc[...] = s0_ref[...]

    cosf = cos_ref[...]
    sinf = sin_ref[...]
    diff = (lax.broadcasted_iota(jnp.int32, (C, C), 0) - lax.broadcasted_iota(jnp.int32, (C, C), 1)).astype(F32)
    ii = lax.broadcasted_iota(jnp.int32, (C, 1), 0).astype(F32)
    half = RET_DK // 2
    for h in range(RET_HEADS):
        lg = math.log(1.0 - 2.0 ** (-5.0 - h))
        hs = slice(h * RET_DK, (h + 1) * RET_DK)
        decay = jnp.where(diff >= 0, jnp.exp(jnp.maximum(diff, 0.0) * lg), 0.0)
        cross = jnp.exp((ii + 1.0) * lg)
        kweight = jnp.exp((C - 1.0 - ii) * lg)
        for b in range(nseq):
            q = q_ref[b, :, hs]
            k = k_ref[b, :, hs]
            v = v_ref[b, :, hs]
            qr = q * cosf + pltpu.roll(q, half, 1) * sinf
            kr = (k * cosf + pltpu.roll(k, half, 1) * sinf) * (RET_DK ** -0.5)
            o_inner = _bdot(_bdot_nt(qr, kr) * decay, v)
            s_old = s_sc[b, h]
            o_cross = _bdot(qr, s_old) * cross
            kv = lax.dot_general((kr * kweight).astype(BF16), v.astype(BF16), (((0,), (0,)), ((), ())),
                                 preferred_element_type=F32)
            s_sc[b, h] = math.exp(C * lg) * s_old + kv
            o = o_inner + o_cross
            mu = jnp.mean(o, axis=-1, keepdims=True)
            var = jnp.mean(jnp.square(o - mu), axis=-1, keepdims=True)
            gate = g_ref[b, :, hs]
            o_ref[b, :, hs] = (((o - mu) * lax.rsqrt(var + EPS)) * gn_ref[:, hs]
                               * (gate * _sigmoid(gate))).astype(o_ref.dtype)

    @pl.when(c == nC - 1)
    def _():
        snew_ref[...] = s_sc[...]


def _retention(rq, rk, rv, rg, cosf, sinf, s0, gn, nseq, out_dtype):
    B, T, W = rq.shape
    C = RET_CHUNK if (T >= RET_CHUNK and T % RET_CHUNK == 0) else T
    nC = T // C
    tok = pl.BlockSpec((nseq, C, W), lambda b, c: (b, c, 0))
    tab = pl.BlockSpec((C, RET_DK), lambda b, c: (c, 0))
    st = pl.BlockSpec((nseq,) + s0.shape[1:], lambda b, c: (b, 0, 0, 0))
    return pl.pallas_call(
        functools.partial(_ret_kernel, C=C, nC=nC, nseq=nseq),
        grid=(B // nseq, nC),
        in_specs=[tok, tok, tok, tok, tab, tab, st, _pspec(gn)],
        out_specs=[tok, st],
        out_shape=[jax.ShapeDtypeStruct((B, T, W), out_dtype), jax.ShapeDtypeStruct(s0.shape, F32)],
        scratch_shapes=[pltpu.VMEM((nseq,) + s0.shape[1:], F32)],
        compiler_params=_cparams(2),
        name="retention",
    )(rq, rk, rv, rg, cosf, sinf, s0, _parg(gn))


def _shift_carry(x, k, tail8):
    r = pltpu.roll(x, k, 0)
    row8 = lax.broadcasted_iota(jnp.int32, (SUBLANES, 1), 0)
    first = jnp.where(row8 >= k, r[:SUBLANES], pltpu.roll(tail8, k, 0))
    return jnp.concatenate([first, r[SUBLANES:]], axis=0)


def _shift_seg(x, k, fill, tpos):
    return jnp.where(tpos >= k, pltpu.roll(x, k, 0), fill)


def _rglru_kernel(x_ref, gate_ref, st_ref, h0_ref, cw_ref, cb_ref, wa_ref, ba_ref, wx_ref, bx_ref, lam_ref,
                  o_ref, h_ref, tail_sc, h_sc, *, tm, seg):
    carry = seg == 0
    x = x_ref[0]
    rows = lax.broadcasted_iota(jnp.int32, (tm, 1), 0)
    if carry:
        @pl.when(pl.program_id(1) == 0)
        def _():
            tail_sc[...] = st_ref[0]
            h_sc[...] = h0_ref[0]
        tail8 = tail_sc[...]
        shifted = [_shift_carry(x, k, tail8) for k in range(1, RG_CONV)]
        tpos = rows & (SUBLANES - 1)
    else:
        tpos = rows & (seg - 1)
        shifted = [_shift_seg(x, k, st_ref[k - 1], tpos) for k in range(1, RG_CONV)]
    xc = cb_ref[...] + cw_ref[RG_CONV - 1:RG_CONV, :] * x
    for k in range(1, RG_CONV):
        xc = xc + cw_ref[RG_CONV - 1 - k:RG_CONV - k, :] * shifted[k - 1]
    r = _sigmoid(_bdot(xc, wa_ref[...]) + ba_ref[...])
    i = _sigmoid(_bdot(xc, wx_ref[...]) + bx_ref[...])
    lam = lam_ref[...]
    softplus = jnp.maximum(-lam, 0.0) + jnp.log(1.0 + jnp.exp(-jnp.abs(lam)))
    log_a = (-RG_C * r) * softplus
    a = jnp.exp(log_a)
    gap = 1.0 - a * a
    bt = jnp.where(gap > 0, gap * lax.rsqrt(gap), 0.0) * (i * xc)
    if not carry:
        bt = bt + a * h0_ref[0]
    k = 1
    while k < SUBLANES:
        ok = tpos >= k
        a_prev = jnp.where(ok, pltpu.roll(a, k, 0), 1.0)
        b_prev = jnp.where(ok, pltpu.roll(bt, k, 0), 0.0)
        bt = a * b_prev + bt
        a = a * a_prev
        k *= 2
    if carry:
        h_prev = h_sc[SUBLANES - 1:SUBLANES, :]
        groups = []
        for g0 in range(0, tm, SUBLANES):
            h_g = bt[g0:g0 + SUBLANES] + a[g0:g0 + SUBLANES] * h_prev
            groups.append(h_g)
            h_prev = h_g[SUBLANES - 1:SUBLANES, :]
        bt = jnp.concatenate(groups, axis=0)
    o_ref[0] = (bt * _gelu(gate_ref[0].astype(F32))).astype(o_ref.dtype)
    if carry:
        tail_sc[...] = x[tm - SUBLANES:]
        h_sc[...] = bt[tm - SUBLANES:]
        h_ref[0] = bt[tm - SUBLANES:]
    else:
        h_ref[0] = bt


def _rglru(rx, rgate, st, h0, rw, tm, seg):
    G, Tg, W = rx.shape
    tok = pl.BlockSpec((1, tm, W), lambda g, t: (g, t, 0))
    if seg == 0:
        st_spec = pl.BlockSpec((1, SUBLANES, W), lambda g, t: (g, 0, 0))
        h0_spec = pl.BlockSpec((1, SUBLANES, W), lambda g, t: (g, 0, 0))
        h_spec = pl.BlockSpec((1, SUBLANES, W), lambda g, t: (g, 0, 0))
        h_shape = (G, SUBLANES, W)
    else:
        st_spec = pl.BlockSpec(st.shape, lambda g, t: (0, 0, 0))
        h0_spec = tok
        h_spec = tok
        h_shape = (G, Tg, W)
    return pl.pallas_call(
        functools.partial(_rglru_kernel, tm=tm, seg=seg),
        grid=(G, Tg // tm),
        in_specs=[tok, tok, st_spec, h0_spec] + [_pspec(a) for a in rw],
        out_specs=[tok, h_spec],
        out_shape=[jax.ShapeDtypeStruct((G, Tg, W), rgate.dtype), jax.ShapeDtypeStruct(h_shape, F32)],
        scratch_shapes=[pltpu.VMEM((SUBLANES, W), F32), pltpu.VMEM((SUBLANES, W), F32)],
        compiler_params=_cparams(2),
        name="rglru",
    )(rx, rgate, st, h0, *[_parg(a) for a in rw])


def _mix_ffn_kernel(x_ref, oa_ref, or_ref, oc_ref, mg_ref, gt1_ref, wa_ref, wb_ref, wc_ref, wo_ref,
                    g_ref, sc_ref, sh_ref, gt_ref, st_ref, wup_ref, cw_ref, cb_ref, wdn_ref, fg_ref,
                    y_ref, fnew_ref, tail_sc, *, tm, seg, F, chunks, final):
    D = x_ref.shape[2]
    pa = _bdot(oa_ref[0], wa_ref[...])
    pb = _bdot(or_ref[0], wb_ref[...])
    pc = _bdot(oc_ref[0], wc_ref[...])
    gate = lambda i: _sigmoid(mg_ref[0, :, i * D:(i + 1) * D].astype(F32))
    merged = gate(0) * pa + gate(1) * pb + gate(2) * pc
    x = x_ref[0] + gt1_ref[...] * _bdot(merged, wo_ref[...])

    carry = seg == 0
    h = _rms_mod(x, g_ref[...], sc_ref[...], sh_ref[...]).astype(BF16)
    rows = lax.broadcasted_iota(jnp.int32, (tm, 1), 0)
    if carry:
        @pl.when(pl.program_id(1) == 0)
        def _():
            tail_sc[...] = st_ref[0]
    else:
        tpos = rows & (seg - 1)
    acc = jnp.zeros(x.shape, F32)
    for c0, wck in chunks:
        cs = slice(c0, c0 + wck)
        gp = jnp.dot(h, wup_ref[:, c0:c0 + wck], preferred_element_type=F32)
        val = jnp.dot(h, wup_ref[:, F + c0:F + c0 + wck], preferred_element_type=F32)
        if carry:
            tail8 = tail_sc[:, cs]
            shifted = [_shift_carry(gp, k, tail8) for k in range(1, FFN_CONV)]
            tail_sc[:, cs] = gp[tm - SUBLANES:]
            fnew_ref[0, :, cs] = gp[tm - SUBLANES:]
        else:
            shifted = [_shift_seg(gp, k, st_ref[k - 1, :, cs], tpos) for k in range(1, FFN_CONV)]
            fnew_ref[0, :, cs] = gp
        gc = cb_ref[:, cs] + cw_ref[FFN_CONV - 1:FFN_CONV, cs] * gp
        for k in range(1, FFN_CONV):
            gc = gc + cw_ref[FFN_CONV - 1 - k:FFN_CONV - k, cs] * shifted[k - 1]
        act = (gc * _sigmoid(gc)) * val
        acc = acc + _bdot(act, wdn_ref[cs, :])
    y = x + gt_ref[...] * acc
    if final:
        y = (y * lax.rsqrt(jnp.mean(y * y, axis=-1, keepdims=True) + EPS)) * fg_ref[...]
    y_ref[0] = y


def _ffn_chunks(F):
    half = -(-(F // 2) // MXU_DEPTH_V7X) * MXU_DEPTH_V7X
    return ((0, half), (half, F - half)) if 0 < half < F else ((0, F),)


def _mix_ffn(x, oa, orr, oc, mg, gt1, wa, wb, wc, wo, g, sc, sh, gt, st, wup, cw, cb, wdn, fg, tm, seg, final):
    G, Tg, D = x.shape
    F = wdn.shape[0]
    tokw = lambda w: pl.BlockSpec((1, tm, w), lambda g_, t: (g_, t, 0))
    tok = tokw(D)
    if seg == 0:
        st_spec = pl.BlockSpec((1, SUBLANES, F), lambda g_, t: (g_, 0, 0))
        fn_spec = pl.BlockSpec((1, SUBLANES, F), lambda g_, t: (g_, 0, 0))
        fn_shape = (G, SUBLANES, F)
    else:
        st_spec = pl.BlockSpec(st.shape, lambda g_, t: (0, 0, 0))
        fn_spec = pl.BlockSpec((1, tm, F), lambda g_, t: (g_, t, 0))
        fn_shape = (G, Tg, F)
    return pl.pallas_call(
        functools.partial(_mix_ffn_kernel, tm=tm, seg=seg, F=F, chunks=_ffn_chunks(F), final=final),
        grid=(G, Tg // tm),
        in_specs=[tok, tokw(oa.shape[2]), tokw(orr.shape[2]), tokw(oc.shape[2]), tokw(3 * D),
                  _mod_spec(gt1, tm, D), _pspec(wa), _pspec(wb), _pspec(wc), _pspec(wo),
                  _pspec(g), _mod_spec(sc, tm, D), _mod_spec(sh, tm, D), _mod_spec(gt, tm, D),
                  st_spec, _pspec(wup), _pspec(cw), _pspec(cb), _pspec(wdn), _pspec(fg)],
        out_specs=[tok, fn_spec],
        out_shape=[jax.ShapeDtypeStruct((G, Tg, D), F32), jax.ShapeDtypeStruct(fn_shape, F32)],
        scratch_shapes=[pltpu.VMEM((SUBLANES, F), F32)],
        compiler_params=_cparams(2),
        name="mix_ffn",
    )(x, oa, orr, oc, mg, gt1.arr, _parg(wa), _parg(wb), _parg(wc), _parg(wo),
      _parg(g), sc.arr, sh.arr, gt.arr, st, _parg(wup), _parg(cw), _parg(cb), _parg(wdn), fg)


def _head_perm():
    return np.array([(k * NSA_HPG + r) * NSA_HD + d
                     for r in range(NSA_HPG) for k in range(NSA_KV) for d in range(NSA_HD)], np.int32)


def _block_diag(w):
    n, a, b = w.shape[-3:]
    eye = jnp.eye(n, dtype=w.dtype)
    out = jnp.einsum('ij,...iab->...iajb', eye, w)
    return out.reshape(w.shape[:-3] + (n * a, n * b))


def _seg_fill(buf, k, seg):
    B, nb, C = buf.shape
    part = jnp.concatenate([buf[:, nb - k:, :], jnp.zeros((B, seg - k, C), buf.dtype)], axis=1)
    return part.reshape(B * seg, C)


def kernel(x_prompt, x_sample, cache_nsa, cache_nsa_win, state_ret, state_rglru_h, state_rglru_conv,
           state_ffn_conv, page_table, c_prompt, c_sample, norm1_g, norm2_g, w_ada, b_ada, w_in, cmp_pe,
           cmp_w1, cmp_b1, cmp_w2, ret_gn_g, rg_conv_w, rg_conv_b, rg_w_a, rg_b_a, rg_w_x, rg_b_x, rg_lambda,
           w_br_a, w_br_b, w_br_c, w_out, ffn_w_up, ffn_conv_w, ffn_conv_b, ffn_w_down, final_norm_g):
    B, T, D = x_prompt.shape
    DB, Q, _ = x_sample.shape
    L = w_in.shape[0]
    npages = page_table.shape[1]
    page = cache_nsa.shape[2]
    P = npages * page
    NQ = NSA_KV * NSA_HPG * NSA_HD
    NKV = NSA_KV * NSA_HD
    RW = RET_HEADS * RET_DK
    W = rg_conv_w.shape[2]
    F = ffn_w_down.shape[1]
    NS = DB * Q
    wbuf_len = cache_nsa_win.shape[2]
    assert Q == SUBLANES and T >= RG_CONV and P % CMP_STRIDE == 0

    mod = _ada(jnp.concatenate([c_prompt, c_sample], axis=0), w_ada, b_ada)
    cache_t = jnp.transpose(cache_nsa, (0, 1, 3, 4, 5, 2)).reshape(L, cache_nsa.shape[1], 4, NKV, page)
    win_t_all = jnp.transpose(cache_nsa_win, (0, 1, 3, 4, 5, 2)).reshape(L, DB, 2, NKV, wbuf_len)
    perm = _head_perm()

    half = RET_DK // 2
    freq = ROPE_BASE ** (-jnp.arange(half, dtype=F32) / half)

    def rope_tables(pos):
        ang = pos.astype(F32)[:, None] * freq[None, :]
        cos, sin = jnp.cos(ang), jnp.sin(ang)
        return jnp.concatenate([cos, cos], axis=1), jnp.concatenate([-sin, sin], axis=1)

    cos_p, sin_p = rope_tables(jnp.arange(T, dtype=jnp.int32))
    cos_s, sin_s = rope_tables(P + jnp.arange(Q, dtype=jnp.int32))

    widths = (NQ, 4 * NKV, 2 * NKV, LANES, RW, RW, RW, RW, W, W, 3 * D)
    dt_prompt = (F32,) * 9 + (BF16, BF16)
    dt_sample = (F32,) * 11
    offs = np.cumsum((0, NQ, 6 * NKV, 3 * NSA_KV * NSA_HPG, RW, RW, RW, RW, W, W, 3 * D))
    ngate = 3 * NSA_KV * NSA_HPG

    xp = x_prompt
    xs = x_sample.reshape(1, NS, D)
    outs_p = [[] for _ in range(6)]
    outs_s = [[] for _ in range(6)]
    tm_p = 256 if T % 256 == 0 else T

    cols = np.concatenate([perm, np.arange(offs[1], offs[3]), np.full(LANES - ngate, offs[3] - 1),
                           np.arange(offs[3], offs[-1])]).astype(np.int32)
    w_cat_all = jnp.take(w_in.astype(BF16), cols, axis=2)
    grouped = lambda w: _block_diag(jnp.broadcast_to(w[..., None, :, :], w.shape[:-2] + (NSA_KV,) + w.shape[-2:]))
    cw1_all = jnp.concatenate([grouped(cmp_w1[:, :, :CMP_STRIDE]), grouped(cmp_w1[:, :, CMP_STRIDE:])], axis=-1)
    cw1_all = cw1_all.reshape(L, 2, CMP_STRIDE * NKV, 4 * LANES).astype(BF16)
    cpe_all = jnp.tile(cmp_pe, (1, 1, 1, NSA_KV)).reshape(L, 2, 2, CMP_STRIDE * NKV)
    cb1_all = jnp.tile(cmp_b1, (1, 1, NSA_KV))[:, :, None, :]
    cw2_all = grouped(cmp_w2).astype(BF16)
    row = lambda a: a[:, None, :]
    rw_all = (rg_conv_w, row(rg_conv_b), _block_diag(rg_w_a).astype(BF16), row(rg_b_a),
              _block_diag(rg_w_x).astype(BF16), row(rg_b_x), row(rg_lambda))
    wa_all = w_br_a[:, perm].astype(BF16)
    wb_all = w_br_b.astype(BF16)
    wc_all = w_br_c.astype(BF16)
    wo_all = w_out.astype(BF16)
    wup_all = ffn_w_up.astype(BF16)
    wdn_all = ffn_w_down.astype(BF16)
    g1_all, g2_all, gn_all, fcb_all = row(norm1_g), row(norm2_g), row(ret_gn_g), row(ffn_conv_b)
    fg = final_norm_g[None]
    mod_p = mod[:, :B].reshape(L, B, 1, 6 * D)
    mod_s = jnp.repeat(mod[:, B:], Q, axis=1)

    for l in range(L):
        lay = lambda a: _LayerOf(a, l)
        w_cat = lay(w_cat_all)
        cw = (lay(cw1_all), lay(cpe_all), lay(cb1_all), lay(cw2_all))
        rw = tuple(lay(a) for a in rw_all)
        wa, wb, wc, wo, wup, wdn = (lay(a) for a in (wa_all, wb_all, wc_all, wo_all, wup_all, wdn_all))
        g1, g2, gn, fcw, fcb = (lay(a) for a in (g1_all, g2_all, gn_all, ffn_conv_w, fcb_all))
        final = l == L - 1

        m = [_ModOf(mod_p, l, i) for i in range(6)]
        (nq, rows, win, gate, rq, rk, rv, rg, rx, rgate, mg) = _inproj(xp, g1, m[1], m[0], w_cat, widths,
                                                                        dt_prompt, tm_p)
        kc, vc = _compress_prompt(rows, cw)
        o_a = _nsa_prompt(nq, gate, kc, vc, rows, win)
        o_r, s_new = _retention(rq, rk, rv, rg, cos_p, sin_p,
                                jnp.zeros((B, RET_HEADS, RET_DK, RET_DK), F32), gn, B, BF16)
        zs = jnp.zeros((B, SUBLANES, W), F32)
        o_c, h_tail = _rglru(rx, rgate, zs, zs, rw, tm_p, 0)
        xp, f_tail = _mix_ffn(xp, o_a, o_r, o_c, mg, m[2], wa, wb, wc, wo, g2, m[4], m[3], m[5],
                              jnp.zeros((B, SUBLANES, F), F32), wup, fcw, fcb, wdn, fg, tm_p, 0, final)
        wn = min(WINDOW, T)
        outs_p[0].append(rows.reshape(B, T, 4, NSA_KV, NSA_HD))
        outs_p[1].append(win[:, T - wn:].reshape(B, wn, 2, NSA_KV, NSA_HD))
        outs_p[2].append(s_new)
        outs_p[3].append(h_tail[:, SUBLANES - 1])
        outs_p[4].append(rx[:, T - (RG_CONV - 1):])
        outs_p[5].append(f_tail[:, SUBLANES - (FFN_CONV - 1):])

        ms = [_ModOf(mod_s, l, i) for i in range(6)]
        (nq, rows, win, gate, rq, rk, rv, rg, rx, rgate, mg) = _inproj(xs, g1, ms[1], ms[0], w_cat, widths,
                                                                        dt_sample, NS)
        r3 = lambda a: a.reshape(DB, Q, a.shape[-1])
        o_a = _nsa_sample(page_table, r3(nq), r3(gate), r3(rows), win_t_all, r3(win), cache_t, l, cw)
        o_r, s_new = _retention(r3(rq), r3(rk), r3(rv), r3(rg), cos_s, sin_s, state_ret[l].astype(F32), gn,
                                math.gcd(DB, SUBLANES), F32)
        cbuf = state_rglru_conv[l]
        st = jnp.stack([_seg_fill(cbuf, k, Q) for k in range(1, RG_CONV)])
        h0 = jnp.pad(state_rglru_h[l].astype(F32)[:, None, :], ((0, 0), (0, Q - 1), (0, 0))).reshape(1, NS, W)
        o_c, h_all = _rglru(rx, rgate, st, h0, rw, NS, Q)
        fbuf = state_ffn_conv[l]
        fst = jnp.stack([_seg_fill(fbuf, k, Q) for k in range(1, FFN_CONV)])
        xs, g_all = _mix_ffn(xs, o_a.reshape(1, NS, NQ), o_r.reshape(1, NS, RW), o_c, mg, ms[2], wa, wb, wc, wo,
                             g2, ms[4], ms[3], ms[5], fst, wup, fcw, fcb, wdn, fg, NS, Q, final)
        outs_s[0].append(rows.reshape(DB, Q, 4, NSA_KV, NSA_HD))
        outs_s[1].append(r3(win).reshape(DB, Q, 2, NSA_KV, NSA_HD))
        outs_s[2].append(s_new)
        outs_s[3].append(h_all.reshape(DB, Q, W)[:, Q - 1])
        outs_s[4].append(jnp.concatenate([cbuf, rx.reshape(DB, Q, W)], axis=1)[:, Q:])
        outs_s[5].append(jnp.concatenate([fbuf, g_all.reshape(DB, Q, F)], axis=1)[:, Q:])

    sp = [jnp.stack(a) for a in outs_p]
    ss = [jnp.stack(a) for a in outs_s]
    keep = max(wbuf_len - Q, 0)
    ss[1] = jnp.concatenate([cache_nsa_win[:, :, wbuf_len - keep:], ss[1]], axis=2)[:, :, -wbuf_len:]
    return (xp, xs.reshape(DB, Q, D), sp[0], ss[0], sp[1], ss[1], sp[2], ss[2],
            sp[3], ss[3], sp[4], ss[4], sp[5], ss[5])
```

```python
import functools
import math

import numpy as np
import jax
import jax.numpy as jnp
from jax import lax
from jax.experimental import pallas as pl
from jax.experimental.pallas import tpu as pltpu

F32 = jnp.float32
BF16 = jnp.bfloat16

NSA_KV = 2
NSA_HPG = 4
NSA_HD = 64
CMP_STRIDE = 16
CMP_BLOCK = 32
SEL_BLOCK = 64
SEL_TOPK = 16
SEL_FORCE = 1e4
WINDOW = 512
QBLOCK = 256
RET_HEADS = 4
RET_DK = 128
RET_CHUNK = 128
ROPE_BASE = 10000.0
RG_CONV = 4
RG_C = 8.0
FFN_CONV = 3
EPS = 1e-6

NEG = -1e30
BIG = float(2 ** 60)
LOG2E = 1.4426950408889634
SUBLANES = 8
LANES = 128
VMEM_LIMIT_V7X = 56 * 1024 * 1024
SEL_CHUNK = 512
CAST_CHUNK = 1024
PAGE_UNROLL = 8
MXU_DEPTH_V7X = 256


def _cparams(n_grid):
    return pltpu.CompilerParams(dimension_semantics=("arbitrary",) * n_grid,
                                vmem_limit_bytes=VMEM_LIMIT_V7X)


def _bdot(a, b):
    return jnp.dot(a.astype(BF16), b.astype(BF16), preferred_element_type=F32)


def _bdot_nt(a, b):
    return lax.dot_general(a.astype(BF16), b.astype(BF16), (((1,), (1,)), ((), ())),
                           preferred_element_type=F32)


def _split(a):
    hi = a.astype(BF16)
    lo = (a - hi.astype(F32)).astype(BF16)
    return hi, lo


def _dot3(a, b):
    ah, al = _split(a)
    bh, bl = _split(b)
    d = functools.partial(jnp.dot, preferred_element_type=F32)
    return d(ah, bh) + d(al, bh) + d(ah, bl)


def _sigmoid(x):
    return 0.5 * jnp.tanh(0.5 * x) + 0.5


def _gelu(x):
    return 0.5 * x * (1.0 + jnp.tanh(0.7978845608028654 * (x + 0.044715 * (x * x * x))))


def _rms_mod(x, g, sc, sh):
    y = x * lax.rsqrt(jnp.mean(x * x, axis=-1, keepdims=True) + EPS)
    return (y * g) * (1.0 + sc) + sh


def _ada_kernel(c_ref, w_ref, b_ref, o_ref):
    c = c_ref[...]
    o_ref[0] = _dot3(c * _sigmoid(c), w_ref[0]) + b_ref[0]


def _ada(c_all, w_ada, b_ada):
    L, D, E = w_ada.shape
    n = c_all.shape[0]
    tn = 1536 if E % 1536 == 0 else E
    return pl.pallas_call(
        _ada_kernel,
        grid=(L, E // tn),
        in_specs=[pl.BlockSpec((n, D), lambda l, j: (0, 0)),
                  pl.BlockSpec((1, D, tn), lambda l, j: (l, 0, j)),
                  pl.BlockSpec((1, 1, tn), lambda l, j: (l, 0, j))],
        out_specs=pl.BlockSpec((1, n, tn), lambda l, j: (l, 0, j)),
        out_shape=jax.ShapeDtypeStruct((L, n, E), F32),
        compiler_params=_cparams(2),
        name="ada_mod",
    )(c_all, w_ada, b_ada.reshape(L, 1, E))


class _LayerOf:
    def __init__(self, arr, layer):
        self.arr, self.layer, self.shape = arr, layer, arr.shape[1:]


class _ModOf:
    def __init__(self, arr, layer, idx):
        self.arr, self.layer, self.idx = arr, layer, idx


def _mod_spec(m, tm, d):
    l, i = m.layer, m.idx
    if m.arr.ndim == 4:
        return pl.BlockSpec((None, None, 1, d), lambda g, t: (l, g, 0, i))
    return pl.BlockSpec((None, tm, d), lambda g, t: (l, t, i))


def _pspec(p):
    if isinstance(p, _LayerOf):
        nd, l = len(p.shape), p.layer
        return pl.BlockSpec((None,) + tuple(p.shape), lambda *a: (l,) + (0,) * nd, pipeline_mode=pl.Buffered(1))
    nd = p.ndim
    return pl.BlockSpec(p.shape, lambda *a: (0,) * nd, pipeline_mode=pl.Buffered(1))


def _parg(p):
    return p.arr if isinstance(p, _LayerOf) else p


def _inproj_kernel(x_ref, g_ref, sc_ref, sh_ref, w_ref, *o_refs, segs):
    h = _rms_mod(x_ref[0], g_ref[...], sc_ref[...], sh_ref[...]).astype(BF16)
    for (off, wd), o_ref in zip(segs, o_refs):
        o_ref[0] = jnp.dot(h, w_ref[:, off:off + wd], preferred_element_type=F32).astype(o_ref.dtype)


def _inproj(x, g, sc, sh, w, widths, dtypes, tm):
    G, Tg, D = x.shape
    segs, off = [], 0
    for wd in widths:
        segs.append((off, wd))
        off += wd
    return pl.pallas_call(
        functools.partial(_inproj_kernel, segs=tuple(segs)),
        grid=(G, Tg // tm),
        in_specs=[pl.BlockSpec((1, tm, D), lambda g_, t: (g_, t, 0)),
                  _pspec(g), _mod_spec(sc, tm, D), _mod_spec(sh, tm, D), _pspec(w)],
        out_specs=[pl.BlockSpec((1, tm, wd), lambda g_, t: (g_, t, 0)) for wd in widths],
        out_shape=[jax.ShapeDtypeStruct((G, Tg, wd), dt) for wd, dt in zip(widths, dtypes)],
        compiler_params=_cparams(2),
        name="in_proj",
    )(x, _parg(g), sc.arr, sh.arr, _parg(w))


def _compress_bias(w1_ref, pe_ref, bias_sc):
    for c in range(2):
        halves = []
        for h in range(2):
            pe_rows = jnp.broadcast_to(pe_ref[c, h:h + 1, :], (SUBLANES, pe_ref.shape[2]))
            halves.append(_bdot(pe_rows, w1_ref[c, :, h * 2 * LANES:(h + 1) * 2 * LANES]))
        bias_sc[c] = jnp.concatenate(halves, axis=1)


def _compress_x(xrefs, nch, w1_ref, bias_sc, b1_ref, w2_ref):
    outs = []
    for c in range(2):
        lhs = jnp.concatenate([xrefs[c][pl.ds(p, nch, stride=CMP_STRIDE), :].astype(BF16)
                               for p in range(CMP_STRIDE)], axis=1)
        outs.append(_compress_one(c, lhs, nch, w1_ref, bias_sc, b1_ref, w2_ref))
    return outs


def _compress_one(c, lhs, nch, w1_ref, bias_sc, b1_ref, w2_ref):
    last = lax.broadcasted_iota(jnp.int32, (nch, 1), 0) == nch - 1
    acc = jnp.dot(lhs, w1_ref[c], preferred_element_type=F32) + bias_sc[c, 0:1, :]
    lo = acc[:, :2 * LANES]
    hi = acc[:, 2 * LANES:]
    hi_next = jnp.where(last, 0.0, pltpu.roll(hi, nch - 1, 0))
    hid = _gelu(lo + hi_next + b1_ref[c])
    return _bdot(hid, w2_ref[c])


def _compress_prompt_kernel(krows_ref, vrows_ref, w1_ref, pe_ref, b1_ref, w2_ref, kc_ref, vc_ref, bias_sc, *, nch):
    @pl.when(pl.program_id(0) == 0)
    def _():
        _compress_bias(w1_ref, pe_ref, bias_sc)

    kc, vc = _compress_x((krows_ref.at[0], vrows_ref.at[0]), nch, w1_ref, bias_sc, b1_ref, w2_ref)
    kc_ref[0] = kc
    vc_ref[0] = vc


def _compress_prompt(rows, cw):
    B, T, _ = rows.shape
    nch = T // CMP_STRIDE
    w1, pe, b1, w2 = cw
    return pl.pallas_call(
        functools.partial(_compress_prompt_kernel, nch=nch),
        grid=(B,),
        in_specs=[pl.BlockSpec((1, T, LANES), lambda b: (b, 0, 0)),
                  pl.BlockSpec((1, T, LANES), lambda b: (b, 0, 1)),
                  _pspec(w1), _pspec(pe), _pspec(b1), _pspec(w2)],
        out_specs=[pl.BlockSpec((1, nch, LANES), lambda b: (b, 0, 0))] * 2,
        out_shape=[jax.ShapeDtypeStruct((B, nch, LANES), F32)] * 2,
        scratch_shapes=[pltpu.VMEM((2, SUBLANES, 4 * LANES), F32)],
        compiler_params=_cparams(1),
        name="nsa_compress_prompt",
    )(rows, rows, _parg(w1), _parg(pe), _parg(b1), _parg(w2))


def _rep_all(a):
    return jnp.concatenate([a] * (NSA_KV * NSA_HPG), axis=0)


def _rep_heads(a, Qb):
    return jnp.concatenate([a[:Qb]] * NSA_HPG + [a[Qb:]] * NSA_HPG, axis=0)


def _nsa_front(qblk, q0, Qb, kc, vc, n_cmp, n_sel):
    R2 = 2 * Qb
    npad = kc.shape[0]
    lane = lax.broadcasted_iota(jnp.int32, (Qb, LANES), 1)
    lo_half = lane < NSA_HD
    scale = NSA_HD ** -0.5 * LOG2E
    pieces = []
    for k in range(NSA_KV):
        for r in range(NSA_HPG):
            sl = qblk[:, r * LANES:(r + 1) * LANES] * scale
            pieces.append(jnp.where(lo_half if k == 0 else jnp.logical_not(lo_half), sl, 0.0))
    qs = jnp.concatenate(pieces, axis=0).astype(BF16)
    qp1 = q0 + lax.broadcasted_iota(jnp.int32, (Qb, 1), 0)

    n_idx = lax.broadcasted_iota(jnp.int32, (1, npad), 1)
    visible = (n_idx * CMP_STRIDE + (CMP_BLOCK - 1) <= qp1) & (n_idx < n_cmp)
    s = _bdot_nt(qs, kc) + _rep_all(jnp.where(visible, 0.0, -BIG))
    e = jnp.exp2(s - jnp.max(s, axis=-1, keepdims=True))
    any_visible = _rep_all((qp1 >= CMP_BLOCK - 1) & (n_cmp > 0))
    p_c = e * jnp.where(any_visible, 1.0 / jnp.sum(e, axis=-1, keepdims=True), 0.0)
    o_c = _bdot(p_c, vc)

    psum = []
    for k in range(NSA_KV):
        acc = p_c[(k * NSA_HPG) * Qb:(k * NSA_HPG + 1) * Qb]
        for r in range(1, NSA_HPG):
            acc = acc + p_c[(k * NSA_HPG + r) * Qb:(k * NSA_HPG + r + 1) * Qb]
        psum.append(acc)
    psum = jnp.concatenate(psum + [jnp.zeros((LANES - R2, npad), F32)], axis=0)
    nsr = -(-n_sel // SUBLANES) * SUBLANES
    sj = lax.broadcasted_iota(jnp.int32, (nsr, npad), 0) * SEL_BLOCK
    ci = lax.broadcasted_iota(jnp.int32, (nsr, npad), 1) * CMP_STRIDE
    selmap = jnp.where((ci < sj + SEL_BLOCK) & (ci + CMP_BLOCK > sj), 1.0, 0.0).astype(BF16)
    ph, plo = _split(psum)
    imp = _bdot_nt(selmap, ph) + _bdot_nt(selmap, plo)
    col_pos = q0 + (lax.broadcasted_iota(jnp.int32, (1, LANES), 1) & (Qb - 1))
    return qs, qp1, o_c, _select_blocks(imp, col_pos, n_sel)


def _select_blocks(imp, qpos, n_sel):
    nsr, ncol = imp.shape
    j = lax.broadcasted_iota(jnp.int32, (nsr, ncol), 0)
    jf = j.astype(F32)
    cur = qpos >> 6
    forced = (j == 0) | (j == cur) | (j == cur - 1)
    imp = jnp.where(forced, SEL_FORCE, imp)
    imp = jnp.where(j * SEL_BLOCK <= qpos, imp, -SEL_FORCE)
    imp = jnp.where(j < n_sel, imp, NEG)

    def pick(_, carry):
        imp_c, sel_c = carry
        m = jnp.max(imp_c, axis=0, keepdims=True)
        first = jnp.min(jnp.where(imp_c == m, jf, float(nsr)), axis=0, keepdims=True)
        hit = jf == first
        return jnp.where(hit, NEG, imp_c), jnp.where(hit, 1.0, sel_c)

    _, sel = lax.fori_loop(0, min(SEL_TOPK, n_sel), pick, (imp, jnp.zeros((nsr, ncol), F32)), unroll=True)
    return sel


def _nsa_combine(gate, o_c, o_s, o_w, Qb, o_ref):
    lo_half = lax.broadcasted_iota(jnp.int32, (Qb, LANES), 1) < NSA_HD
    g = _sigmoid(gate)
    for r in range(NSA_HPG):
        halves = []
        for k in range(NSA_KV):
            rs = slice((k * NSA_HPG + r) * Qb, (k * NSA_HPG + r + 1) * Qb)
            c = (k * NSA_HPG + r) * 3
            halves.append(g[:, c:c + 1] * o_c[rs] + g[:, c + 1:c + 2] * o_s[rs] + g[:, c + 2:c + 3] * o_w[rs])
        o_ref[0, :, r * LANES:(r + 1) * LANES] = jnp.where(lo_half, halves[0], halves[1])


def _block_columns(k0, n):
    key = k0 + lax.broadcasted_iota(jnp.int32, (n, LANES), 0)
    blk = lax.broadcasted_iota(jnp.int32, (n, LANES), 1)
    return jnp.where((key >> 6) == blk, BIG, 0.0).astype(BF16)


def _block_rows(k0, n):
    key = k0 + lax.broadcasted_iota(jnp.int32, (LANES, n), 1)
    blk = lax.broadcasted_iota(jnp.int32, (LANES, n), 0)
    return jnp.where((key >> 6) == blk, BIG, 0.0).astype(BF16)


def _lanes_all(a):
    return jnp.concatenate([a] * (NSA_KV * NSA_HPG), axis=1)


def _lanes_heads(a, Qb):
    return jnp.concatenate([a[:, :Qb]] * NSA_HPG + [a[:, Qb:]] * NSA_HPG, axis=1)


def _softmax_cols(s):
    e = jnp.exp2(s - jnp.max(s, axis=0, keepdims=True))
    return e, 1.0 / jnp.sum(e, axis=0, keepdims=True)


def _nsa_prompt_kernel(q_ref, gate_ref, kc_ref, vc_ref, rows_ref, win_ref, o_ref, kaug, vt,
                       *, T, Qb, n_cmp, n_sel, wl):
    i = pl.program_id(1)
    q0 = i * Qb
    R = NSA_KV * NSA_HPG * Qb
    R2 = NSA_KV * Qb

    @pl.when(i == 0)
    def _():
        def pack(c, carry):
            r0 = pl.multiple_of(c * SEL_CHUNK, SEL_CHUNK)
            kaug[pl.ds(r0, SEL_CHUNK), 0:LANES] = rows_ref[0, pl.ds(r0, SEL_CHUNK), 2 * LANES:3 * LANES].astype(BF16)
            kaug[pl.ds(r0, SEL_CHUNK), LANES:2 * LANES] = _block_columns(r0, SEL_CHUNK)
            vt[:, pl.ds(r0, SEL_CHUNK)] = rows_ref[0, pl.ds(r0, SEL_CHUNK), 3 * LANES:4 * LANES].T.astype(BF16)
            return carry
        lax.fori_loop(0, T // SEL_CHUNK, pack, 0)

    top = lax.broadcasted_iota(jnp.int32, (LANES, Qb), 0) < NSA_HD
    scale = NSA_HD ** -0.5 * LOG2E
    qblk = q_ref[0]
    q_t = [(qblk[:, r * LANES:(r + 1) * LANES] * scale).T for r in range(NSA_HPG)]
    qs = jnp.concatenate([jnp.where(top if k == 0 else jnp.logical_not(top), q_t[r], 0.0)
                          for k in range(NSA_KV) for r in range(NSA_HPG)], axis=1).astype(BF16)
    qlane = q0 + lax.broadcasted_iota(jnp.int32, (1, Qb), 1)
    qpos = _lanes_all(qlane)

    kc = kc_ref[0]
    npad = kc.shape[0]
    n_idx = lax.broadcasted_iota(jnp.int32, (npad, 1), 0)
    visible = (n_idx * CMP_STRIDE + (CMP_BLOCK - 1) <= qlane) & (n_idx < n_cmp)
    e_c, inv_c = _softmax_cols(_bdot(kc, qs) + _lanes_all(jnp.where(visible, 0.0, -BIG)))
    any_visible = _lanes_all((qlane >= CMP_BLOCK - 1) & (n_cmp > 0))
    p_c = e_c * jnp.where(any_visible, inv_c, 0.0)
    o_c = _bdot(vc_ref[0].T, p_c)

    psum = []
    for k in range(NSA_KV):
        acc = p_c[:, (k * NSA_HPG) * Qb:(k * NSA_HPG + 1) * Qb]
        for r in range(1, NSA_HPG):
            acc = acc + p_c[:, (k * NSA_HPG + r) * Qb:(k * NSA_HPG + r + 1) * Qb]
        psum.append(acc)
    psum = jnp.concatenate(psum, axis=1)
    nsr = -(-n_sel // SUBLANES) * SUBLANES
    sj = lax.broadcasted_iota(jnp.int32, (nsr, npad), 0) * SEL_BLOCK
    ci = lax.broadcasted_iota(jnp.int32, (nsr, npad), 1) * CMP_STRIDE
    selmap = jnp.where((ci < sj + SEL_BLOCK) & (ci + CMP_BLOCK > sj), 1.0, 0.0).astype(BF16)
    ph, plo = _split(psum)
    imp = (jnp.dot(selmap, ph, preferred_element_type=F32)
           + jnp.dot(selmap, plo, preferred_element_type=F32))
    sel = _select_blocks(imp, jnp.concatenate([qlane] * NSA_KV, axis=1), n_sel)
    selm =jnp.concatenate([sel - 1.0, jnp.zeros((LANES - nsr, R2), F32)], axis=0) if nsr < LANES else sel - 1.0
    qaug = jnp.concatenate([qs, _lanes_heads(selm, Qb).astype(BF16)], axis=0)

    def update(s, vcols, carry):
        m, l, acc = carry
        m_new = jnp.maximum(m, jnp.max(s, axis=0, keepdims=True))
        alpha = jnp.exp2(m - m_new)
        p = jnp.exp2(s - m_new)
        l = alpha * l + jnp.sum(p, axis=0, keepdims=True)
        acc = alpha * acc + jnp.dot(vcols, p.astype(BF16), preferred_element_type=F32)
        return m_new, l, acc

    def scores(k0):
        return jnp.dot(kaug[pl.ds(k0, SEL_CHUNK), :], qaug, preferred_element_type=F32)

    def full_chunk(c, carry):
        k0 = pl.multiple_of(c * SEL_CHUNK, SEL_CHUNK)
        return update(scores(k0), vt[:, pl.ds(k0, SEL_CHUNK)], carry)

    n_full = q0 // SEL_CHUNK
    init = (jnp.full((1, R), -4.0 * BIG, F32), jnp.zeros((1, R), F32), jnp.zeros((LANES, R), F32))
    carry = lax.fori_loop(0, n_full, full_chunk, init)
    k0 = pl.multiple_of(n_full * SEL_CHUNK, SEL_CHUNK)
    kpos = k0 + lax.broadcasted_iota(jnp.int32, (SEL_CHUNK, 1), 0)
    s_diag = jnp.where(kpos <= qpos, scores(k0), -2.0 * BIG)
    _, l_s, acc_s = update(s_diag, vt[:, pl.ds(k0, SEL_CHUNK)], carry)
    o_s = acc_s * (1.0 / l_s)

    ws = pl.multiple_of(jnp.maximum(q0 - WINDOW, 0), LANES)
    wk = win_ref[0, pl.ds(ws, wl), 0:LANES]
    wv = win_ref[0, pl.ds(ws, wl), LANES:2 * LANES]
    dpos = qlane - (ws + lax.broadcasted_iota(jnp.int32, (wl, 1), 0))
    e_w, inv_w = _softmax_cols(_bdot(wk, qs) + _lanes_all(jnp.where((dpos >= 0) & (dpos <= WINDOW), 0.0, -BIG)))
    o_w = _bdot(wv.T, e_w) * inv_w

    g = _sigmoid(gate_ref[0]).T
    for r in range(NSA_HPG):
        halves = []
        for k in range(NSA_KV):
            cs = slice((k * NSA_HPG + r) * Qb, (k * NSA_HPG + r + 1) * Qb)
            c = (k * NSA_HPG + r) * 3
            halves.append(g[c:c + 1, :] * o_c[:, cs] + g[c + 1:c + 2, :] * o_s[:, cs] + g[c + 2:c + 3, :] * o_w[:, cs])
        o_ref[0, :, r * LANES:(r + 1) * LANES] = jnp.where(top, halves[0], halves[1]).T.astype(o_ref.dtype)


def _nsa_prompt(nq, gate, kc, vc, rows, win):
    B, T, HD = nq.shape
    Qb = QBLOCK
    nch = kc.shape[1]
    n_sel = -(-T // SEL_BLOCK)
    wl = WINDOW + Qb
    assert T % SEL_CHUNK == 0 and T >= wl and SEL_TOPK <= n_sel <= LANES
    return pl.pallas_call(
        functools.partial(_nsa_prompt_kernel, T=T, Qb=Qb, n_cmp=nch - 1, n_sel=n_sel, wl=wl),
        grid=(B, T // Qb),
        in_specs=[pl.BlockSpec((1, Qb, HD), lambda b, i: (b, i, 0)),
                  pl.BlockSpec((1, Qb, LANES), lambda b, i: (b, i, 0)),
                  pl.BlockSpec((1, nch, LANES), lambda b, i: (b, 0, 0)),
                  pl.BlockSpec((1, nch, LANES), lambda b, i: (b, 0, 0)),
                  pl.BlockSpec((1, T, 4 * LANES), lambda b, i: (b, 0, 0)),
                  pl.BlockSpec((1, T, 2 * LANES), lambda b, i: (b, 0, 0))],
        out_specs=pl.BlockSpec((1, Qb, HD), lambda b, i: (b, i, 0)),
        out_shape=jax.ShapeDtypeStruct((B, T, HD), BF16),
        scratch_shapes=[pltpu.VMEM((T, 2 * LANES), BF16), pltpu.VMEM((LANES, T), BF16)],
        compiler_params=_cparams(2),
        name="nsa_attn_prompt",
    )(nq, gate, kc, vc, rows, win)


def _pages_copy(cache_hbm, layer, page, r0, dst, j, sem):
    n = cache_hbm.shape[-1]
    return pltpu.make_async_copy(cache_hbm.at[layer, page, pl.ds(r0, 2)],
                                 dst.at[:, :, pl.ds(pl.multiple_of(j * n, n), n)], sem)


def _pages_start(pt_ref, b, cache_hbm, layer, r0, dst, sem, npages):
    def issue(j, carry):
        _pages_copy(cache_hbm, layer, pt_ref[b, j], r0, dst, j, sem).start()
        return carry
    lax.fori_loop(0, npages, issue, 0, unroll=PAGE_UNROLL)


def _pages_wait(cache_hbm, layer, r0, dst, sem, npages):
    def wait(j, carry):
        _pages_copy(cache_hbm, layer, 0, r0, dst, j, sem).wait()
        return carry
    lax.fori_loop(0, npages, wait, 0, unroll=PAGE_UNROLL)


def _softmax2(s1, s2, mask2):
    s2 = jnp.where(mask2, s2, -2.0 * BIG)
    m = jnp.maximum(jnp.max(s1, axis=-1, keepdims=True), jnp.max(s2, axis=-1, keepdims=True))
    e1 = jnp.exp2(s1 - m)
    e2 = jnp.exp2(s2 - m)
    den = jnp.sum(e1, axis=-1, keepdims=True) + jnp.sum(e2, axis=-1, keepdims=True)
    return e1, e2, 1.0 / den


def _nsa_sample_kernel(pt_ref, q_ref, gate_ref, rows_ref, wt_ref, wnew_ref, cache_hbm,
                       w1_ref, pe_ref, b1_ref, w2_ref, o_ref,
                       cmpbuf, selbuf, xk, xv, kaug, vt, newbuf, wnewbuf, bias_sc, perm_sc, csem, ssem,
                       *, layer, npages, P, Q, wb, nb, n_sel):
    b = pl.program_id(0)
    nseq = pl.num_programs(0)
    page = cache_hbm.shape[-1]
    nch = P // CMP_STRIDE

    @pl.when(b == 0)
    def _():
        _pages_start(pt_ref, 0, cache_hbm, layer, 0, cmpbuf, csem, npages)
        _pages_start(pt_ref, 0, cache_hbm, layer, 2, selbuf, ssem, npages)
        _compress_bias(w1_ref, pe_ref, bias_sc)

        def blocks(c, carry):
            c0 = pl.multiple_of(c * CAST_CHUNK, CAST_CHUNK)
            kaug[LANES:2 * LANES, pl.ds(c0, CAST_CHUNK)] = _block_rows(c0, CAST_CHUNK)
            return carry
        lax.fori_loop(0, P // CAST_CHUNK, blocks, 0)
        newbuf[...] = jnp.zeros(newbuf.shape, F32)
        wnewbuf[...] = jnp.zeros(wnewbuf.shape, F32)
        src = lax.broadcasted_iota(jnp.int32, (2 * page, 2 * page), 0)
        dst_col = lax.broadcasted_iota(jnp.int32, (2 * page, 2 * page), 1)
        t = src & (page - 1)
        want = (src - t) + (t % CMP_STRIDE) * (page // CMP_STRIDE) + t // CMP_STRIDE
        perm_sc[...] = jnp.where(dst_col == want, 1.0, 0.0).astype(BF16)

    _pages_wait(cache_hbm, layer, 0, cmpbuf, csem, npages)
    per_page = page // CMP_STRIDE
    for c, dst in enumerate((xk, xv)):
        for jp in range(npages // 2):
            cols = cmpbuf[c, :, 2 * jp * page:2 * (jp + 1) * page].astype(BF16)
            regrouped = jnp.dot(cols, perm_sc[...], preferred_element_type=F32)
            for half in range(2):
                tok = regrouped[:, half * page:(half + 1) * page].T
                r0 = (2 * jp + half) * per_page
                for p in range(CMP_STRIDE):
                    dst[r0:r0 + per_page, p * LANES:(p + 1) * LANES] = tok[p * per_page:(p + 1) * per_page, :]
    kc = _compress_one(0, xk[...].astype(BF16), nch, w1_ref, bias_sc, b1_ref, w2_ref)
    vc = _compress_one(1, xv[...].astype(BF16), nch, w1_ref, bias_sc, b1_ref, w2_ref)

    @pl.when(b + 1 < nseq)
    def _():
        _pages_start(pt_ref, b + 1, cache_hbm, layer, 0, cmpbuf, csem, npages)

    _pages_wait(cache_hbm, layer, 2, selbuf, ssem, npages)

    def pack(c, carry):
        c0 = pl.multiple_of(c * CAST_CHUNK, CAST_CHUNK)
        kaug[0:LANES, pl.ds(c0, CAST_CHUNK)] = selbuf[0, :, pl.ds(c0, CAST_CHUNK)].astype(BF16)
        vt[:, pl.ds(c0, CAST_CHUNK)] = selbuf[1, :, pl.ds(c0, CAST_CHUNK)].astype(BF16)
        return carry
    lax.fori_loop(0, P // CAST_CHUNK, pack, 0)

    @pl.when(b + 1 < nseq)
    def _():
        _pages_start(pt_ref, b + 1, cache_hbm, layer, 2, selbuf, ssem, npages)

    newbuf[0:Q, :] = rows_ref[0, :, 2 * LANES:4 * LANES]
    wnewbuf[0:Q, :] = wnew_ref[0]

    qs, qp1, o_c, sel_t = _nsa_front(q_ref[0], P, Q, kc, vc, nch - 1, n_sel)
    qpos = _rep_all(qp1)
    lane = lax.broadcasted_iota(jnp.int32, (1, LANES), 1)
    new_pos = P + lane
    is_new = lane < Q

    def as_rows(blk):
        if blk.shape[0] < LANES:
            blk = jnp.concatenate([blk, jnp.zeros((LANES - blk.shape[0], LANES), F32)], axis=0)
        return blk.T[0:NSA_KV * Q]
    sel_rows = as_rows(sel_t[0:min(LANES, sel_t.shape[0])])
    sel_new = as_rows(sel_t[nb:nb + SUBLANES])[:, 0:1]

    qaug = jnp.concatenate([qs, _rep_heads(sel_rows - 1.0, Q).astype(BF16)], axis=1)
    s_past = jnp.dot(qaug, kaug[...], preferred_element_type=F32)
    s_new = _bdot_nt(qs, newbuf[:, 0:LANES])
    new_ok = is_new & (new_pos <= qpos) & (_rep_heads(sel_new, Q) > 0.5)
    e1, e2, inv = _softmax2(s_past, s_new, new_ok)
    o_s = (lax.dot_general(e1.astype(BF16), vt[...], (((1,), (1,)), ((), ())), preferred_element_type=F32)
           + _bdot(e2, newbuf[:, LANES:2 * LANES])) * inv

    dpast = qp1 - ((P - wb) + lax.broadcasted_iota(jnp.int32, (1, wb), 1))
    s_wp = (jnp.dot(qs, wt_ref[0, 0].astype(BF16), preferred_element_type=F32)
            + _rep_all(jnp.where((dpast >= 0) & (dpast <= WINDOW), 0.0, -BIG)))
    s_wn = _bdot_nt(qs, wnewbuf[:, 0:LANES])
    dnew = qpos - new_pos
    e1, e2, inv = _softmax2(s_wp, s_wn, is_new & (dnew >= 0) & (dnew <= WINDOW))
    o_w = (lax.dot_general(e1.astype(BF16), wt_ref[0, 1].astype(BF16), (((1,), (1,)), ((), ())),
                           preferred_element_type=F32)
           + _bdot(e2, wnewbuf[:, LANES:2 * LANES])) * inv

    _nsa_combine(gate_ref[0], o_c, o_s, o_w, Q, o_ref)


def _nsa_sample(page_table, nq, gate, rows, win_t, win_new, cache_t, layer, cw):
    DB, Q, HD = nq.shape
    npages = page_table.shape[1]
    page = cache_t.shape[-1]
    P = npages * page
    wb = win_t.shape[-1]
    nb = P // SEL_BLOCK
    n_sel = -(-(P + Q) // SEL_BLOCK)
    w1, pe, b1, w2 = cw
    assert Q == SUBLANES and P % SEL_BLOCK == 0 and Q <= SEL_BLOCK and nb <= LANES and n_sel >= SEL_TOPK
    assert P % CAST_CHUNK == 0 and (P + Q) // CMP_STRIDE == P // CMP_STRIDE and page == LANES
    assert npages % 2 == 0 and page // CMP_STRIDE == SUBLANES
    bs = lambda shape: pl.BlockSpec((1,) + shape, lambda b, pt: (b,) + (0,) * len(shape))
    win_spec = pl.BlockSpec((None, 1, 2, LANES, wb), lambda b, pt: (layer, b, 0, 0, 0))
    grid_spec = pltpu.PrefetchScalarGridSpec(
        num_scalar_prefetch=1, grid=(DB,),
        in_specs=[bs((Q, HD)), bs((Q, LANES)), bs((Q, 4 * LANES)), win_spec, bs((Q, 2 * LANES)),
                  pl.BlockSpec(memory_space=pl.ANY),
                  _pspec(w1), _pspec(pe), _pspec(b1), _pspec(w2)],
        out_specs=bs((Q, HD)),
        scratch_shapes=[pltpu.VMEM((2, LANES, P), F32), pltpu.VMEM((2, LANES, P), F32),
                        pltpu.VMEM((P // CMP_STRIDE, CMP_STRIDE * LANES), F32),
                        pltpu.VMEM((P // CMP_STRIDE, CMP_STRIDE * LANES), F32),
                        pltpu.VMEM((2 * LANES, P), BF16), pltpu.VMEM((LANES, P), BF16),
                        pltpu.VMEM((LANES, 2 * LANES), F32), pltpu.VMEM((LANES, 2 * LANES), F32),
                        pltpu.VMEM((2, SUBLANES, 4 * LANES), F32), pltpu.VMEM((2 * page, 2 * page), BF16),
                        pltpu.SemaphoreType.DMA(()), pltpu.SemaphoreType.DMA(())])
    return pl.pallas_call(
        functools.partial(_nsa_sample_kernel, layer=layer, npages=npages, P=P, Q=Q, wb=wb, nb=nb, n_sel=n_sel),
        grid_spec=grid_spec,
        out_shape=jax.ShapeDtypeStruct((DB, Q, HD), F32),
        compiler_params=_cparams(1),
        name="nsa_sample",
    )(page_table, nq, gate, rows, win_t, win_new, cache_t, _parg(w1), _parg(pe), _parg(b1), _parg(w2))


def _ret_kernel(q_ref, k_ref, v_ref, g_ref, cos_ref, sin_ref, s0_ref, gn_ref, o_ref, snew_ref, s_sc,
                *, C, nC, nseq):
    c = pl.program_id(1)

    @pl.when(c == 0)
    def _():
        s_sc[...] = s0_ref[...]

    cosf = cos_ref[...]
    sinf = sin_ref[...]
    diff = (lax.broadcasted_iota(jnp.int32, (C, C), 0) - lax.broadcasted_iota(jnp.int32, (C, C), 1)).astype(F32)
    ii = lax.broadcasted_iota(jnp.int32, (C, 1), 0).astype(F32)
    half = RET_DK // 2
    for h in range(RET_HEADS):
        lg = math.log(1.0 - 2.0 ** (-5.0 - h))
        hs = slice(h * RET_DK, (h + 1) * RET_DK)
        decay = jnp.where(diff >= 0, jnp.exp(jnp.maximum(diff, 0.0) * lg), 0.0)
        cross = jnp.exp((ii + 1.0) * lg)
        kweight = jnp.exp((C - 1.0 - ii) * lg)
        for b in range(nseq):
            q = q_ref[b, :, hs]
            k = k_ref[b, :, hs]
            v = v_ref[b, :, hs]
            qr = q * cosf + pltpu.roll(q, half, 1) * sinf
            kr = (k * cosf + pltpu.roll(k, half, 1) * sinf) * (RET_DK ** -0.5)
            o_inner = _bdot(_bdot_nt(qr, kr) * decay, v)
            s_old = s_sc[b, h]
            o_cross = _bdot(qr, s_old) * cross
            kv = lax.dot_general((kr * kweight).astype(BF16), v.astype(BF16), (((0,), (0,)), ((), ())),
                                 preferred_element_type=F32)
            s_sc[b, h] = math.exp(C * lg) * s_old + kv
            o = o_inner + o_cross
            mu = jnp.mean(o, axis=-1, keepdims=True)
            var = jnp.mean(jnp.square(o - mu), axis=-1, keepdims=True)
            gate = g_ref[b, :, hs]
            o_ref[b, :, hs] = (((o - mu) * lax.rsqrt(var + EPS)) * gn_ref[:, hs]
                               * (gate * _sigmoid(gate))).astype(o_ref.dtype)

    @pl.when(c == nC - 1)
    def _():
        snew_ref[...] = s_sc[...]


def _retention(rq, rk, rv, rg, cosf, sinf, s0, gn, nseq, out_dtype):
    B, T, W = rq.shape
    C = RET_CHUNK if (T >= RET_CHUNK and T % RET_CHUNK == 0) else T
    nC = T // C
    tok = pl.BlockSpec((nseq, C, W), lambda b, c: (b, c, 0))
    tab = pl.BlockSpec((C, RET_DK), lambda b, c: (c, 0))
    st = pl.BlockSpec((nseq,) + s0.shape[1:], lambda b, c: (b, 0, 0, 0))
    return pl.pallas_call(
        functools.partial(_ret_kernel, C=C, nC=nC, nseq=nseq),
        grid=(B // nseq, nC),
        in_specs=[tok, tok, tok, tok, tab, tab, st, _pspec(gn)],
        out_specs=[tok, st],
        out_shape=[jax.ShapeDtypeStruct((B, T, W), out_dtype), jax.ShapeDtypeStruct(s0.shape, F32)],
        scratch_shapes=[pltpu.VMEM((nseq,) + s0.shape[1:], F32)],
        compiler_params=_cparams(2),
        name="retention",
    )(rq, rk, rv, rg, cosf, sinf, s0, _parg(gn))


def _shift_carry(x, k, tail8):
    r = pltpu.roll(x, k, 0)
    row8 = lax.broadcasted_iota(jnp.int32, (SUBLANES, 1), 0)
    first = jnp.where(row8 >= k, r[:SUBLANES], pltpu.roll(tail8, k, 0))
    return jnp.concatenate([first, r[SUBLANES:]], axis=0)


def _shift_seg(x, k, fill, tpos):
    return jnp.where(tpos >= k, pltpu.roll(x, k, 0), fill)


def _rglru_kernel(x_ref, gate_ref, st_ref, h0_ref, cw_ref, cb_ref, wa_ref, ba_ref, wx_ref, bx_ref, lam_ref,
                  o_ref, h_ref, tail_sc, h_sc, *, tm, seg):
    carry = seg == 0
    x = x_ref[0]
    rows = lax.broadcasted_iota(jnp.int32, (tm, 1), 0)
    if carry:
        @pl.when(pl.program_id(1) == 0)
        def _():
            tail_sc[...] = st_ref[0]
            h_sc[...] = h0_ref[0]
        tail8 = tail_sc[...]
        shifted = [_shift_carry(x, k, tail8) for k in range(1, RG_CONV)]
        tpos = rows & (SUBLANES - 1)
    else:
        tpos = rows & (seg - 1)
        shifted = [_shift_seg(x, k, st_ref[k - 1], tpos) for k in range(1, RG_CONV)]
    xc = cb_ref[...] + cw_ref[RG_CONV - 1:RG_CONV, :] * x
    for k in range(1, RG_CONV):
        xc = xc + cw_ref[RG_CONV - 1 - k:RG_CONV - k, :] * shifted[k - 1]
    r = _sigmoid(_bdot(xc, wa_ref[...]) + ba_ref[...])
    i = _sigmoid(_bdot(xc, wx_ref[...]) + bx_ref[...])
    lam = lam_ref[...]
    softplus = jnp.maximum(-lam, 0.0) + jnp.log(1.0 + jnp.exp(-jnp.abs(lam)))
    log_a = (-RG_C * r) * softplus
    a = jnp.exp(log_a)
    gap = 1.0 - a * a
    bt = jnp.where(gap > 0, gap * lax.rsqrt(gap), 0.0) * (i * xc)
    if not carry:
        bt = bt + a * h0_ref[0]
    k = 1
    while k < SUBLANES:
        ok = tpos >= k
        a_prev = jnp.where(ok, pltpu.roll(a, k, 0), 1.0)
        b_prev = jnp.where(ok, pltpu.roll(bt, k, 0), 0.0)
        bt = a * b_prev + bt
        a = a * a_prev
        k *= 2
    if carry:
        h_prev = h_sc[SUBLANES - 1:SUBLANES, :]
        groups = []
        for g0 in range(0, tm, SUBLANES):
            h_g = bt[g0:g0 + SUBLANES] + a[g0:g0 + SUBLANES] * h_prev
            groups.append(h_g)
            h_prev = h_g[SUBLANES - 1:SUBLANES, :]
        bt = jnp.concatenate(groups, axis=0)
    o_ref[0] = (bt * _gelu(gate_ref[0].astype(F32))).astype(o_ref.dtype)
    if carry:
        tail_sc[...] = x[tm - SUBLANES:]
        h_sc[...] = bt[tm - SUBLANES:]
        h_ref[0] = bt[tm - SUBLANES:]
    else:
        h_ref[0] = bt


def _rglru(rx, rgate, st, h0, rw, tm, seg):
    G, Tg, W = rx.shape
    tok = pl.BlockSpec((1, tm, W), lambda g, t: (g, t, 0))
    if seg == 0:
        st_spec = pl.BlockSpec((1, SUBLANES, W), lambda g, t: (g, 0, 0))
        h0_spec = pl.BlockSpec((1, SUBLANES, W), lambda g, t: (g, 0, 0))
        h_spec = pl.BlockSpec((1, SUBLANES, W), lambda g, t: (g, 0, 0))
        h_shape = (G, SUBLANES, W)
    else:
        st_spec = pl.BlockSpec(st.shape, lambda g, t: (0, 0, 0))
        h0_spec = tok
        h_spec = tok
        h_shape = (G, Tg, W)
    return pl.pallas_call(
        functools.partial(_rglru_kernel, tm=tm, seg=seg),
        grid=(G, Tg // tm),
        in_specs=[tok, tok, st_spec, h0_spec] + [_pspec(a) for a in rw],
        out_specs=[tok, h_spec],
        out_shape=[jax.ShapeDtypeStruct((G, Tg, W), rgate.dtype), jax.ShapeDtypeStruct(h_shape, F32)],
        scratch_shapes=[pltpu.VMEM((SUBLANES, W), F32), pltpu.VMEM((SUBLANES, W), F32)],
        compiler_params=_cparams(2),
        name="rglru",
    )(rx, rgate, st, h0, *[_parg(a) for a in rw])


def _mix_ffn_kernel(x_ref, oa_ref, or_ref, oc_ref, mg_ref, gt1_ref, wa_ref, wb_ref, wc_ref, wo_ref,
                    g_ref, sc_ref, sh_ref, gt_ref, st_ref, wup_ref, cw_ref, cb_ref, wdn_ref, fg_ref,
                    y_ref, fnew_ref, tail_sc, *, tm, seg, F, chunks, final):
    D = x_ref.shape[2]
    pa = _bdot(oa_ref[0], wa_ref[...])
    pb = _bdot(or_ref[0], wb_ref[...])
    pc = _bdot(oc_ref[0], wc_ref[...])
    gate = lambda i: _sigmoid(mg_ref[0, :, i * D:(i + 1) * D].astype(F32))
    merged = gate(0) * pa + gate(1) * pb + gate(2) * pc
    x = x_ref[0] + gt1_ref[...] * _bdot(merged, wo_ref[...])

    carry = seg == 0
    h = _rms_mod(x, g_ref[...], sc_ref[...], sh_ref[...]).astype(BF16)
    rows = lax.broadcasted_iota(jnp.int32, (tm, 1), 0)
    if carry:
        @pl.when(pl.program_id(1) == 0)
        def _():
            tail_sc[...] = st_ref[0]
    else:
        tpos = rows & (seg - 1)
    acc = jnp.zeros(x.shape, F32)
    for c0, wck in chunks:
        cs = slice(c0, c0 + wck)
        gp = jnp.dot(h, wup_ref[:, c0:c0 + wck], preferred_element_type=F32)
        val = jnp.dot(h, wup_ref[:, F + c0:F + c0 + wck], preferred_element_type=F32)
        if carry:
            tail8 = tail_sc[:, cs]
            shifted = [_shift_carry(gp, k, tail8) for k in range(1, FFN_CONV)]
            tail_sc[:, cs] = gp[tm - SUBLANES:]
            fnew_ref[0, :, cs] = gp[tm - SUBLANES:]
        else:
            shifted = [_shift_seg(gp, k, st_ref[k - 1, :, cs], tpos) for k in range(1, FFN_CONV)]
            fnew_ref[0, :, cs] = gp
        gc = cb_ref[:, cs] + cw_ref[FFN_CONV - 1:FFN_CONV, cs] * gp
        for k in range(1, FFN_CONV):
            gc = gc + cw_ref[FFN_CONV - 1 - k:FFN_CONV - k, cs] * shifted[k - 1]
        act = (gc * _sigmoid(gc)) * val
        acc = acc + _bdot(act, wdn_ref[cs, :])
    y = x + gt_ref[...] * acc
    if final:
        y = (y * lax.rsqrt(jnp.mean(y * y, axis=-1, keepdims=True) + EPS)) * fg_ref[...]
    y_ref[0] = y


def _ffn_chunks(F):
    half = -(-(F // 2) // MXU_DEPTH_V7X) * MXU_DEPTH_V7X
    return ((0, half), (half, F - half)) if 0 < half < F else ((0, F),)


def _mix_ffn(x, oa, orr, oc, mg, gt1, wa, wb, wc, wo, g, sc, sh, gt, st, wup, cw, cb, wdn, fg, tm, seg, final):
    G, Tg, D = x.shape
    F = wdn.shape[0]
    tokw = lambda w: pl.BlockSpec((1, tm, w), lambda g_, t: (g_, t, 0))
    tok = tokw(D)
    if seg == 0:
        st_spec = pl.BlockSpec((1, SUBLANES, F), lambda g_, t: (g_, 0, 0))
        fn_spec = pl.BlockSpec((1, SUBLANES, F), lambda g_, t: (g_, 0, 0))
        fn_shape = (G, SUBLANES, F)
    else:
        st_spec = pl.BlockSpec(st.shape, lambda g_, t: (0, 0, 0))
        fn_spec = pl.BlockSpec((1, tm, F), lambda g_, t: (g_, t, 0))
        fn_shape = (G, Tg, F)
    return pl.pallas_call(
        functools.partial(_mix_ffn_kernel, tm=tm, seg=seg, F=F, chunks=_ffn_chunks(F), final=final),
        grid=(G, Tg // tm),
        in_specs=[tok, tokw(oa.shape[2]), tokw(orr.shape[2]), tokw(oc.shape[2]), tokw(3 * D),
                  _mod_spec(gt1, tm, D), _pspec(wa), _pspec(wb), _pspec(wc), _pspec(wo),
                  _pspec(g), _mod_spec(sc, tm, D), _mod_spec(sh, tm, D), _mod_spec(gt, tm, D),
                  st_spec, _pspec(wup), _pspec(cw), _pspec(cb), _pspec(wdn), _pspec(fg)],
        out_specs=[tok, fn_spec],
        out_shape=[jax.ShapeDtypeStruct((G, Tg, D), F32), jax.ShapeDtypeStruct(fn_shape, F32)],
        scratch_shapes=[pltpu.VMEM((SUBLANES, F), F32)],
        compiler_params=_cparams(2),
        name="mix_ffn",
    )(x, oa, orr, oc, mg, gt1.arr, _parg(wa), _parg(wb), _parg(wc), _parg(wo),
      _parg(g), sc.arr, sh.arr, gt.arr, st, _parg(wup), _parg(cw), _parg(cb), _parg(wdn), fg)


def _head_perm():
    return np.array([(k * NSA_HPG + r) * NSA_HD + d
                     for r in range(NSA_HPG) for k in range(NSA_KV) for d in range(NSA_HD)], np.int32)


def _block_diag(w):
    n, a, b = w.shape[-3:]
    eye = jnp.eye(n, dtype=w.dtype)
    out = jnp.einsum('ij,...iab->...iajb', eye, w)
    return out.reshape(w.shape[:-3] + (n * a, n * b))


def _seg_fill(buf, k, seg):
    B, nb, C = buf.shape
    part = jnp.concatenate([buf[:, nb - k:, :], jnp.zeros((B, seg - k, C), buf.dtype)], axis=1)
    return part.reshape(B * seg, C)


def kernel(x_prompt, x_sample, cache_nsa, cache_nsa_win, state_ret, state_rglru_h, state_rglru_conv,
           state_ffn_conv, page_table, c_prompt, c_sample, norm1_g, norm2_g, w_ada, b_ada, w_in, cmp_pe,
           cmp_w1, cmp_b1, cmp_w2, ret_gn_g, rg_conv_w, rg_conv_b, rg_w_a, rg_b_a, rg_w_x, rg_b_x, rg_lambda,
           w_br_a, w_br_b, w_br_c, w_out, ffn_w_up, ffn_conv_w, ffn_conv_b, ffn_w_down, final_norm_g):
    B, T, D = x_prompt.shape
    DB, Q, _ = x_sample.shape
    L = w_in.shape[0]
    npages = page_table.shape[1]
    page = cache_nsa.shape[2]
    P = npages * page
    NQ = NSA_KV * NSA_HPG * NSA_HD
    NKV = NSA_KV * NSA_HD
    RW = RET_HEADS * RET_DK
    W = rg_conv_w.shape[2]
    F = ffn_w_down.shape[1]
    NS = DB * Q
    wbuf_len = cache_nsa_win.shape[2]
    assert Q == SUBLANES and T >= RG_CONV and P % CMP_STRIDE == 0

    mod = _ada(jnp.concatenate([c_prompt, c_sample], axis=0), w_ada, b_ada)
    cache_t = jnp.transpose(cache_nsa, (0, 1, 3, 4, 5, 2)).reshape(L, cache_nsa.shape[1], 4, NKV, page)
    win_t_all = jnp.transpose(cache_nsa_win, (0, 1, 3, 4, 5, 2)).reshape(L, DB, 2, NKV, wbuf_len)
    perm = _head_perm()

    half = RET_DK // 2
    freq = ROPE_BASE ** (-jnp.arange(half, dtype=F32) / half)

    def rope_tables(pos):
        ang = pos.astype(F32)[:, None] * freq[None, :]
        cos, sin = jnp.cos(ang), jnp.sin(ang)
        return jnp.concatenate([cos, cos], axis=1), jnp.concatenate([-sin, sin], axis=1)

    cos_p, sin_p = rope_tables(jnp.arange(T, dtype=jnp.int32))
    cos_s, sin_s = rope_tables(P + jnp.arange(Q, dtype=jnp.int32))

    widths = (NQ, 4 * NKV, 2 * NKV, LANES, RW, RW, RW, RW, W, W, 3 * D)
    dt_prompt = (F32,) * 9 + (BF16, BF16)
    dt_sample = (F32,) * 11
    offs = np.cumsum((0, NQ, 6 * NKV, 3 * NSA_KV * NSA_HPG, RW, RW, RW, RW, W, W, 3 * D))
    ngate = 3 * NSA_KV * NSA_HPG

    xp = x_prompt
    xs = x_sample.reshape(1, NS, D)
    outs_p = [[] for _ in range(6)]
    outs_s = [[] for _ in range(6)]
    tm_p = 256 if T % 256 == 0 else T

    wi = w_in.astype(BF16)
    w_cat_all = jnp.concatenate([
        wi[:, :, offs[0]:offs[1]][:, :, perm],
        wi[:, :, offs[1]:offs[2]],
        jnp.pad(wi[:, :, offs[2]:offs[3]], ((0, 0), (0, 0), (0, LANES - ngate))),
        wi[:, :, offs[3]:]], axis=2)
    grouped = lambda w: _block_diag(jnp.broadcast_to(w[..., None, :, :], w.shape[:-2] + (NSA_KV,) + w.shape[-2:]))
    cw1_all = jnp.concatenate([grouped(cmp_w1[:, :, :CMP_STRIDE]), grouped(cmp_w1[:, :, CMP_STRIDE:])], axis=-1)
    cw1_all = cw1_all.reshape(L, 2, CMP_STRIDE * NKV, 4 * LANES).astype(BF16)
    cpe_all = jnp.tile(cmp_pe, (1, 1, 1, NSA_KV)).reshape(L, 2, 2, CMP_STRIDE * NKV)
    cb1_all = jnp.tile(cmp_b1, (1, 1, NSA_KV))[:, :, None, :]
    cw2_all = grouped(cmp_w2).astype(BF16)
    row = lambda a: a[:, None, :]
    rw_all = (rg_conv_w, row(rg_conv_b), _block_diag(rg_w_a).astype(BF16), row(rg_b_a),
              _block_diag(rg_w_x).astype(BF16), row(rg_b_x), row(rg_lambda))
    wa_all = w_br_a[:, perm].astype(BF16)
    wb_all = w_br_b.astype(BF16)
    wc_all = w_br_c.astype(BF16)
    wo_all = w_out.astype(BF16)
    wup_all = ffn_w_up.astype(BF16)
    wdn_all = ffn_w_down.astype(BF16)
    g1_all, g2_all, gn_all, fcb_all = row(norm1_g), row(norm2_g), row(ret_gn_g), row(ffn_conv_b)
    fg = final_norm_g[None]
    mod_p = mod[:, :B].reshape(L, B, 1, 6 * D)
    mod_s = jnp.repeat(mod[:, B:], Q, axis=1)

    for l in range(L):
        lay = lambda a: _LayerOf(a, l)
        w_cat = lay(w_cat_all)
        cw = (lay(cw1_all), lay(cpe_all), lay(cb1_all), lay(cw2_all))
        rw = tuple(lay(a) for a in rw_all)
        wa, wb, wc, wo, wup, wdn = (lay(a) for a in (wa_all, wb_all, wc_all, wo_all, wup_all, wdn_all))
        g1, g2, gn, fcw, fcb = (lay(a) for a in (g1_all, g2_all, gn_all, ffn_conv_w, fcb_all))
        final = l == L - 1

        m = [_ModOf(mod_p, l, i) for i in range(6)]
        (nq, rows, win, gate, rq, rk, rv, rg, rx, rgate, mg) = _inproj(xp, g1, m[1], m[0], w_cat, widths,
                                                                        dt_prompt, tm_p)
        kc, vc = _compress_prompt(rows, cw)
        o_a = _nsa_prompt(nq, gate, kc, vc, rows, win)
        o_r, s_new = _retention(rq, rk, rv, rg, cos_p, sin_p,
                                jnp.zeros((B, RET_HEADS, RET_DK, RET_DK), F32), gn, B, BF16)
        zs = jnp.zeros((B, SUBLANES, W), F32)
        o_c, h_tail = _rglru(rx, rgate, zs, zs, rw, tm_p, 0)
        xp, f_tail = _mix_ffn(xp, o_a, o_r, o_c, mg, m[2], wa, wb, wc, wo, g2, m[4], m[3], m[5],
                              jnp.zeros((B, SUBLANES, F), F32), wup, fcw, fcb, wdn, fg, tm_p, 0, final)
        wn = min(WINDOW, T)
        outs_p[0].append(rows.reshape(B, T, 4, NSA_KV, NSA_HD))
        outs_p[1].append(win[:, T - wn:].reshape(B, wn, 2, NSA_KV, NSA_HD))
        outs_p[2].append(s_new)
        outs_p[3].append(h_tail[:, SUBLANES - 1])
        outs_p[4].append(rx[:, T - (RG_CONV - 1):])
        outs_p[5].append(f_tail[:, SUBLANES - (FFN_CONV - 1):])

        ms = [_ModOf(mod_s, l, i) for i in range(6)]
        (nq, rows, win, gate, rq, rk, rv, rg, rx, rgate, mg) = _inproj(xs, g1, ms[1], ms[0], w_cat, widths,
                                                                        dt_sample, NS)
        r3 = lambda a: a.reshape(DB, Q, a.shape[-1])
        o_a = _nsa_sample(page_table, r3(nq), r3(gate), r3(rows), win_t_all, r3(win), cache_t, l, cw)
        o_r, s_new = _retention(r3(rq), r3(rk), r3(rv), r3(rg), cos_s, sin_s, state_ret[l].astype(F32), gn,
                                math.gcd(DB, SUBLANES), F32)
        cbuf = state_rglru_conv[l]
        st = jnp.stack([_seg_fill(cbuf, k, Q) for k in range(1, RG_CONV)])
        h0 = jnp.pad(state_rglru_h[l].astype(F32)[:, None, :], ((0, 0), (0, Q - 1), (0, 0))).reshape(1, NS, W)
        o_c, h_all = _rglru(rx, rgate, st, h0, rw, NS, Q)
        fbuf = state_ffn_conv[l]
        fst = jnp.stack([_seg_fill(fbuf, k, Q) for k in range(1, FFN_CONV)])
        xs, g_all = _mix_ffn(xs, o_a.reshape(1, NS, NQ), o_r.reshape(1, NS, RW), o_c, mg, ms[2], wa, wb, wc, wo,
                             g2, ms[4], ms[3], ms[5], fst, wup, fcw, fcb, wdn, fg, NS, Q, final)
        outs_s[0].append(rows.reshape(DB, Q, 4, NSA_KV, NSA_HD))
        outs_s[1].append(r3(win).reshape(DB, Q, 2, NSA_KV, NSA_HD))
        outs_s[2].append(s_new)
        outs_s[3].append(h_all.reshape(DB, Q, W)[:, Q - 1])
        outs_s[4].append(jnp.concatenate([cbuf, rx.reshape(DB, Q, W)], axis=1)[:, Q:])
        outs_s[5].append(jnp.concatenate([fbuf, g_all.reshape(DB, Q, F)], axis=1)[:, Q:])

    sp = [jnp.stack(a) for a in outs_p]
    ss = [jnp.stack(a) for a in outs_s]
    keep = max(wbuf_len - Q, 0)
    ss[1] = jnp.concatenate([cache_nsa_win[:, :, wbuf_len - keep:], ss[1]], axis=2)[:, :, -wbuf_len:]
    return (xp, xs.reshape(DB, Q, D), sp[0], ss[0], sp[1], ss[1], sp[2], ss[2],
            sp[3], ss[3], sp[4], ss[4], sp[5], ss[5])
```

```python
import functools
import math

import numpy as np
import jax
import jax.numpy as jnp
from jax import lax
from jax.experimental import pallas as pl
from jax.experimental.pallas import tpu as pltpu

F32 = jnp.float32
BF16 = jnp.bfloat16

NSA_KV = 2
NSA_HPG = 4
NSA_HD = 64
CMP_STRIDE = 16
CMP_BLOCK = 32
SEL_BLOCK = 64
SEL_TOPK = 16
SEL_FORCE = 1e4
WINDOW = 512
QBLOCK = 256
RET_HEADS = 4
RET_DK = 128
RET_CHUNK = 128
ROPE_BASE = 10000.0
RG_CONV = 4
RG_C = 8.0
FFN_CONV = 3
EPS = 1e-6

NEG = -1e30
BIG = float(2 ** 60)
LOG2E = 1.4426950408889634
SUBLANES = 8
LANES = 128
VMEM_LIMIT_V7X = 56 * 1024 * 1024
SEL_CHUNK = 512
CAST_CHUNK = 1024
PAGE_UNROLL = 8
MXU_DEPTH_V7X = 256


def _cparams(n_grid):
    return pltpu.CompilerParams(dimension_semantics=("arbitrary",) * n_grid,
                                vmem_limit_bytes=VMEM_LIMIT_V7X)


def _bdot(a, b):
    return jnp.dot(a.astype(BF16), b.astype(BF16), preferred_element_type=F32)


def _bdot_nt(a, b):
    return lax.dot_general(a.astype(BF16), b.astype(BF16), (((1,), (1,)), ((), ())),
                           preferred_element_type=F32)


def _split(a):
    hi = a.astype(BF16)
    lo = (a - hi.astype(F32)).astype(BF16)
    return hi, lo


def _dot3(a, b):
    ah, al = _split(a)
    bh, bl = _split(b)
    d = functools.partial(jnp.dot, preferred_element_type=F32)
    return d(ah, bh) + d(al, bh) + d(ah, bl)


def _sigmoid(x):
    return 0.5 * jnp.tanh(0.5 * x) + 0.5


def _gelu(x):
    return 0.5 * x * (1.0 + jnp.tanh(0.7978845608028654 * (x + 0.044715 * (x * x * x))))


def _rms_mod(x, g, sc, sh):
    y = x * lax.rsqrt(jnp.mean(x * x, axis=-1, keepdims=True) + EPS)
    return (y * g) * (1.0 + sc) + sh


def _ada_kernel(c_ref, w_ref, b_ref, o_ref):
    c = c_ref[...]
    o_ref[0] = _dot3(c * _sigmoid(c), w_ref[0]) + b_ref[0]


def _ada(c_all, w_ada, b_ada):
    L, D, E = w_ada.shape
    n = c_all.shape[0]
    tn = 1536 if E % 1536 == 0 else E
    return pl.pallas_call(
        _ada_kernel,
        grid=(L, E // tn),
        in_specs=[pl.BlockSpec((n, D), lambda l, j: (0, 0)),
                  pl.BlockSpec((1, D, tn), lambda l, j: (l, 0, j)),
                  pl.BlockSpec((1, 1, tn), lambda l, j: (l, 0, j))],
        out_specs=pl.BlockSpec((1, n, tn), lambda l, j: (l, 0, j)),
        out_shape=jax.ShapeDtypeStruct((L, n, E), F32),
        compiler_params=_cparams(2),
        name="ada_mod",
    )(c_all, w_ada, b_ada.reshape(L, 1, E))


class _LayerOf:
    def __init__(self, arr, layer):
        self.arr, self.layer, self.shape = arr, layer, arr.shape[1:]


class _ModOf:
    def __init__(self, arr, layer, idx):
        self.arr, self.layer, self.idx = arr, layer, idx


def _mod_spec(m, tm, d):
    l, i = m.layer, m.idx
    if m.arr.ndim == 4:
        return pl.BlockSpec((None, None, 1, d), lambda g, t: (l, g, 0, i))
    return pl.BlockSpec((None, tm, d), lambda g, t: (l, t, i))


def _pspec(p):
    if isinstance(p, _LayerOf):
        nd, l = len(p.shape), p.layer
        return pl.BlockSpec((None,) + tuple(p.shape), lambda *a: (l,) + (0,) * nd, pipeline_mode=pl.Buffered(1))
    nd = p.ndim
    return pl.BlockSpec(p.shape, lambda *a: (0,) * nd, pipeline_mode=pl.Buffered(1))


def _parg(p):
    return p.arr if isinstance(p, _LayerOf) else p


def _inproj_kernel(x_ref, g_ref, sc_ref, sh_ref, w_ref, *o_refs, segs):
    h = _rms_mod(x_ref[0], g_ref[...], sc_ref[...], sh_ref[...]).astype(BF16)
    for (off, wd), o_ref in zip(segs, o_refs):
        o_ref[0] = jnp.dot(h, w_ref[:, off:off + wd], preferred_element_type=F32).astype(o_ref.dtype)


def _inproj(x, g, sc, sh, w, widths, dtypes, tm):
    G, Tg, D = x.shape
    segs, off = [], 0
    for wd in widths:
        segs.append((off, wd))
        off += wd
    return pl.pallas_call(
        functools.partial(_inproj_kernel, segs=tuple(segs)),
        grid=(G, Tg // tm),
        in_specs=[pl.BlockSpec((1, tm, D), lambda g_, t: (g_, t, 0)),
                  _pspec(g), _mod_spec(sc, tm, D), _mod_spec(sh, tm, D), _pspec(w)],
        out_specs=[pl.BlockSpec((1, tm, wd), lambda g_, t: (g_, t, 0)) for wd in widths],
        out_shape=[jax.ShapeDtypeStruct((G, Tg, wd), dt) for wd, dt in zip(widths, dtypes)],
        compiler_params=_cparams(2),
        name="in_proj",
    )(x, _parg(g), sc.arr, sh.arr, _parg(w))


def _compress_bias(w1_ref, pe_ref, bias_sc):
    for c in range(2):
        halves = []
        for h in range(2):
            pe_rows = jnp.broadcast_to(pe_ref[c, h:h + 1, :], (SUBLANES, pe_ref.shape[2]))
            halves.append(_bdot(pe_rows, w1_ref[c, :, h * 2 * LANES:(h + 1) * 2 * LANES]))
        bias_sc[c] = jnp.concatenate(halves, axis=1)


def _compress_x(xrefs, nch, w1_ref, bias_sc, b1_ref, w2_ref):
    outs = []
    for c in range(2):
        lhs = jnp.concatenate([xrefs[c][pl.ds(p, nch, stride=CMP_STRIDE), :].astype(BF16)
                               for p in range(CMP_STRIDE)], axis=1)
        outs.append(_compress_one(c, lhs, nch, w1_ref, bias_sc, b1_ref, w2_ref))
    return outs


def _compress_one(c, lhs, nch, w1_ref, bias_sc, b1_ref, w2_ref):
    last = lax.broadcasted_iota(jnp.int32, (nch, 1), 0) == nch - 1
    acc = jnp.dot(lhs, w1_ref[c], preferred_element_type=F32) + bias_sc[c, 0:1, :]
    lo = acc[:, :2 * LANES]
    hi = acc[:, 2 * LANES:]
    hi_next = jnp.where(last, 0.0, pltpu.roll(hi, nch - 1, 0))
    hid = _gelu(lo + hi_next + b1_ref[c])
    return _bdot(hid, w2_ref[c])


def _compress_prompt_kernel(krows_ref, vrows_ref, w1_ref, pe_ref, b1_ref, w2_ref, kc_ref, vc_ref, bias_sc, *, nch):
    @pl.when(pl.program_id(0) == 0)
    def _():
        _compress_bias(w1_ref, pe_ref, bias_sc)

    kc, vc = _compress_x((krows_ref.at[0], vrows_ref.at[0]), nch, w1_ref, bias_sc, b1_ref, w2_ref)
    kc_ref[0] = kc
    vc_ref[0] = vc


def _compress_prompt(rows, cw):
    B, T, _ = rows.shape
    nch = T // CMP_STRIDE
    w1, pe, b1, w2 = cw
    return pl.pallas_call(
        functools.partial(_compress_prompt_kernel, nch=nch),
        grid=(B,),
        in_specs=[pl.BlockSpec((1, T, LANES), lambda b: (b, 0, 0)),
                  pl.BlockSpec((1, T, LANES), lambda b: (b, 0, 1)),
                  _pspec(w1), _pspec(pe), _pspec(b1), _pspec(w2)],
        out_specs=[pl.BlockSpec((1, nch, LANES), lambda b: (b, 0, 0))] * 2,
        out_shape=[jax.ShapeDtypeStruct((B, nch, LANES), F32)] * 2,
        scratch_shapes=[pltpu.VMEM((2, SUBLANES, 4 * LANES), F32)],
        compiler_params=_cparams(1),
        name="nsa_compress_prompt",
    )(rows, rows, _parg(w1), _parg(pe), _parg(b1), _parg(w2))


def _rep_all(a):
    return jnp.concatenate([a] * (NSA_KV * NSA_HPG), axis=0)


def _rep_heads(a, Qb):
    return jnp.concatenate([a[:Qb]] * NSA_HPG + [a[Qb:]] * NSA_HPG, axis=0)


def _nsa_front(qblk, q0, Qb, kc, vc, n_cmp, n_sel):
    R2 = 2 * Qb
    npad = kc.shape[0]
    lane = lax.broadcasted_iota(jnp.int32, (Qb, LANES), 1)
    lo_half = lane < NSA_HD
    scale = NSA_HD ** -0.5 * LOG2E
    pieces = []
    for k in range(NSA_KV):
        for r in range(NSA_HPG):
            sl = qblk[:, r * LANES:(r + 1) * LANES] * scale
            pieces.append(jnp.where(lo_half if k == 0 else jnp.logical_not(lo_half), sl, 0.0))
    qs = jnp.concatenate(pieces, axis=0).astype(BF16)
    qp1 = q0 + lax.broadcasted_iota(jnp.int32, (Qb, 1), 0)

    n_idx = lax.broadcasted_iota(jnp.int32, (1, npad), 1)
    visible = (n_idx * CMP_STRIDE + (CMP_BLOCK - 1) <= qp1) & (n_idx < n_cmp)
    s = _bdot_nt(qs, kc) + _rep_all(jnp.where(visible, 0.0, -BIG))
    e = jnp.exp2(s - jnp.max(s, axis=-1, keepdims=True))
    any_visible = _rep_all((qp1 >= CMP_BLOCK - 1) & (n_cmp > 0))
    p_c = e * jnp.where(any_visible, 1.0 / jnp.sum(e, axis=-1, keepdims=True), 0.0)
    o_c = _bdot(p_c, vc)

    psum = []
    for k in range(NSA_KV):
        acc = p_c[(k * NSA_HPG) * Qb:(k * NSA_HPG + 1) * Qb]
        for r in range(1, NSA_HPG):
            acc = acc + p_c[(k * NSA_HPG + r) * Qb:(k * NSA_HPG + r + 1) * Qb]
        psum.append(acc)
    psum = jnp.concatenate(psum + [jnp.zeros((LANES - R2, npad), F32)], axis=0)
    nsr = -(-n_sel // SUBLANES) * SUBLANES
    sj = lax.broadcasted_iota(jnp.int32, (nsr, npad), 0) * SEL_BLOCK
    ci = lax.broadcasted_iota(jnp.int32, (nsr, npad), 1) * CMP_STRIDE
    selmap = jnp.where((ci < sj + SEL_BLOCK) & (ci + CMP_BLOCK > sj), 1.0, 0.0).astype(BF16)
    ph, plo = _split(psum)
    imp = _bdot_nt(selmap, ph) + _bdot_nt(selmap, plo)
    col_pos = q0 + (lax.broadcasted_iota(jnp.int32, (1, LANES), 1) & (Qb - 1))
    return qs, qp1, o_c, _select_blocks(imp, col_pos, n_sel)


def _select_blocks(imp, qpos, n_sel):
    nsr, ncol = imp.shape
    j = lax.broadcasted_iota(jnp.int32, (nsr, ncol), 0)
    jf = j.astype(F32)
    cur = qpos >> 6
    forced = (j == 0) | (j == cur) | (j == cur - 1)
    imp = jnp.where(forced, SEL_FORCE, imp)
    imp = jnp.where(j * SEL_BLOCK <= qpos, imp, -SEL_FORCE)
    imp = jnp.where(j < n_sel, imp, NEG)

    def pick(_, carry):
        imp_c, sel_c = carry
        m = jnp.max(imp_c, axis=0, keepdims=True)
        first = jnp.min(jnp.where(imp_c == m, jf, float(nsr)), axis=0, keepdims=True)
        hit = jf == first
        return jnp.where(hit, NEG, imp_c), jnp.where(hit, 1.0, sel_c)

    _, sel = lax.fori_loop(0, min(SEL_TOPK, n_sel), pick, (imp, jnp.zeros((nsr, ncol), F32)), unroll=True)
    return sel


def _nsa_combine(gate, o_c, o_s, o_w, Qb, o_ref):
    lo_half = lax.broadcasted_iota(jnp.int32, (Qb, LANES), 1) < NSA_HD
    g = _sigmoid(gate)
    for r in range(NSA_HPG):
        halves = []
        for k in range(NSA_KV):
            rs = slice((k * NSA_HPG + r) * Qb, (k * NSA_HPG + r + 1) * Qb)
            c = (k * NSA_HPG + r) * 3
            halves.append(g[:, c:c + 1] * o_c[rs] + g[:, c + 1:c + 2] * o_s[rs] + g[:, c + 2:c + 3] * o_w[rs])
        o_ref[0, :, r * LANES:(r + 1) * LANES] = jnp.where(lo_half, halves[0], halves[1])


def _block_columns(k0, n):
    key = k0 + lax.broadcasted_iota(jnp.int32, (n, LANES), 0)
    blk = lax.broadcasted_iota(jnp.int32, (n, LANES), 1)
    return jnp.where((key >> 6) == blk, BIG, 0.0).astype(BF16)


def _block_rows(k0, n):
    key = k0 + lax.broadcasted_iota(jnp.int32, (LANES, n), 1)
    blk = lax.broadcasted_iota(jnp.int32, (LANES, n), 0)
    return jnp.where((key >> 6) == blk, BIG, 0.0).astype(BF16)


def _lanes_all(a):
    return jnp.concatenate([a] * (NSA_KV * NSA_HPG), axis=1)


def _lanes_heads(a, Qb):
    return jnp.concatenate([a[:, :Qb]] * NSA_HPG + [a[:, Qb:]] * NSA_HPG, axis=1)


def _softmax_cols(s):
    e = jnp.exp2(s - jnp.max(s, axis=0, keepdims=True))
    return e, 1.0 / jnp.sum(e, axis=0, keepdims=True)


def _nsa_prompt_kernel(q_ref, gate_ref, kc_ref, vc_ref, rows_ref, win_ref, o_ref, kaug, vt,
                       *, T, Qb, n_cmp, n_sel, wl):
    i = pl.program_id(1)
    q0 = i * Qb
    R = NSA_KV * NSA_HPG * Qb
    R2 = NSA_KV * Qb

    @pl.when(i == 0)
    def _():
        def pack(c, carry):
            r0 = pl.multiple_of(c * SEL_CHUNK, SEL_CHUNK)
            kaug[pl.ds(r0, SEL_CHUNK), 0:LANES] = rows_ref[0, pl.ds(r0, SEL_CHUNK), 2 * LANES:3 * LANES].astype(BF16)
            kaug[pl.ds(r0, SEL_CHUNK), LANES:2 * LANES] = _block_columns(r0, SEL_CHUNK)
            vt[:, pl.ds(r0, SEL_CHUNK)] = rows_ref[0, pl.ds(r0, SEL_CHUNK), 3 * LANES:4 * LANES].T.astype(BF16)
            return carry
        lax.fori_loop(0, T // SEL_CHUNK, pack, 0)

    top = lax.broadcasted_iota(jnp.int32, (LANES, Qb), 0) < NSA_HD
    scale = NSA_HD ** -0.5 * LOG2E
    qblk = q_ref[0]
    q_t = [(qblk[:, r * LANES:(r + 1) * LANES] * scale).T for r in range(NSA_HPG)]
    qs = jnp.concatenate([jnp.where(top if k == 0 else jnp.logical_not(top), q_t[r], 0.0)
                          for k in range(NSA_KV) for r in range(NSA_HPG)], axis=1).astype(BF16)
    qlane = q0 + lax.broadcasted_iota(jnp.int32, (1, Qb), 1)
    qpos = _lanes_all(qlane)

    kc = kc_ref[0]
    npad = kc.shape[0]
    n_idx = lax.broadcasted_iota(jnp.int32, (npad, 1), 0)
    visible = (n_idx * CMP_STRIDE + (CMP_BLOCK - 1) <= qlane) & (n_idx < n_cmp)
    e_c, inv_c = _softmax_cols(_bdot(kc, qs) + _lanes_all(jnp.where(visible, 0.0, -BIG)))
    any_visible = _lanes_all((qlane >= CMP_BLOCK - 1) & (n_cmp > 0))
    p_c = e_c * jnp.where(any_visible, inv_c, 0.0)
    o_c = _bdot(vc_ref[0].T, p_c)

    psum = []
    for k in range(NSA_KV):
        acc = p_c[:, (k * NSA_HPG) * Qb:(k * NSA_HPG + 1) * Qb]
        for r in range(1, NSA_HPG):
            acc = acc + p_c[:, (k * NSA_HPG + r) * Qb:(k * NSA_HPG + r + 1) * Qb]
        psum.append(acc)
    psum = jnp.concatenate(psum, axis=1)
    nsr = -(-n_sel // SUBLANES) * SUBLANES
    sj = lax.broadcasted_iota(jnp.int32, (nsr, npad), 0) * SEL_BLOCK
    ci = lax.broadcasted_iota(jnp.int32, (nsr, npad), 1) * CMP_STRIDE
    selmap = jnp.where((ci < sj + SEL_BLOCK) & (ci + CMP_BLOCK > sj), 1.0, 0.0).astype(BF16)
    ph, plo = _split(psum)
    imp = (jnp.dot(selmap, ph, preferred_element_type=F32)
           + jnp.dot(selmap, plo, preferred_element_type=F32))
    sel = _select_blocks(imp, jnp.concatenate([qlane] * NSA_KV, axis=1), n_sel)
    selm =jnp.concatenate([sel - 1.0, jnp.zeros((LANES - nsr, R2), F32)], axis=0) if nsr < LANES else sel - 1.0
    qaug = jnp.concatenate([qs, _lanes_heads(selm, Qb).astype(BF16)], axis=0)

    def update(s, vcols, carry):
        m, l, acc = carry
        m_new = jnp.maximum(m, jnp.max(s, axis=0, keepdims=True))
        alpha = jnp.exp2(m - m_new)
        p = jnp.exp2(s - m_new)
        l = alpha * l + jnp.sum(p, axis=0, keepdims=True)
        acc = alpha * acc + jnp.dot(vcols, p.astype(BF16), preferred_element_type=F32)
        return m_new, l, acc

    def scores(k0):
        return jnp.dot(kaug[pl.ds(k0, SEL_CHUNK), :], qaug, preferred_element_type=F32)

    def full_chunk(c, carry):
        k0 = pl.multiple_of(c * SEL_CHUNK, SEL_CHUNK)
        return update(scores(k0), vt[:, pl.ds(k0, SEL_CHUNK)], carry)

    n_full = q0 // SEL_CHUNK
    init = (jnp.full((1, R), -4.0 * BIG, F32), jnp.zeros((1, R), F32), jnp.zeros((LANES, R), F32))
    carry = lax.fori_loop(0, n_full, full_chunk, init)
    k0 = pl.multiple_of(n_full * SEL_CHUNK, SEL_CHUNK)
    kpos = k0 + lax.broadcasted_iota(jnp.int32, (SEL_CHUNK, 1), 0)
    s_diag = jnp.where(kpos <= qpos, scores(k0), -2.0 * BIG)
    _, l_s, acc_s = update(s_diag, vt[:, pl.ds(k0, SEL_CHUNK)], carry)
    o_s = acc_s * (1.0 / l_s)

    ws = pl.multiple_of(jnp.maximum(q0 - WINDOW, 0), LANES)
    wk = win_ref[0, pl.ds(ws, wl), 0:LANES]
    wv = win_ref[0, pl.ds(ws, wl), LANES:2 * LANES]
    dpos = qlane - (ws + lax.broadcasted_iota(jnp.int32, (wl, 1), 0))
    e_w, inv_w = _softmax_cols(_bdot(wk, qs) + _lanes_all(jnp.where((dpos >= 0) & (dpos <= WINDOW), 0.0, -BIG)))
    o_w = _bdot(wv.T, e_w) * inv_w

    g = _sigmoid(gate_ref[0]).T
    for r in range(NSA_HPG):
        halves = []
        for k in range(NSA_KV):
            cs = slice((k * NSA_HPG + r) * Qb, (k * NSA_HPG + r + 1) * Qb)
            c = (k * NSA_HPG + r) * 3
            halves.append(g[c:c + 1, :] * o_c[:, cs] + g[c + 1:c + 2, :] * o_s[:, cs] + g[c + 2:c + 3, :] * o_w[:, cs])
        o_ref[0, :, r * LANES:(r + 1) * LANES] = jnp.where(top, halves[0], halves[1]).T.astype(o_ref.dtype)


def _nsa_prompt(nq, gate, kc, vc, rows, win):
    B, T, HD = nq.shape
    Qb = QBLOCK
    nch = kc.shape[1]
    n_sel = -(-T // SEL_BLOCK)
    wl = WINDOW + Qb
    assert T % SEL_CHUNK == 0 and T >= wl and SEL_TOPK <= n_sel <= LANES
    return pl.pallas_call(
        functools.partial(_nsa_prompt_kernel, T=T, Qb=Qb, n_cmp=nch - 1, n_sel=n_sel, wl=wl),
        grid=(B, T // Qb),
        in_specs=[pl.BlockSpec((1, Qb, HD), lambda b, i: (b, i, 0)),
                  pl.BlockSpec((1, Qb, LANES), lambda b, i: (b, i, 0)),
                  pl.BlockSpec((1, nch, LANES), lambda b, i: (b, 0, 0)),
                  pl.BlockSpec((1, nch, LANES), lambda b, i: (b, 0, 0)),
                  pl.BlockSpec((1, T, 4 * LANES), lambda b, i: (b, 0, 0)),
                  pl.BlockSpec((1, T, 2 * LANES), lambda b, i: (b, 0, 0))],
        out_specs=pl.BlockSpec((1, Qb, HD), lambda b, i: (b, i, 0)),
        out_shape=jax.ShapeDtypeStruct((B, T, HD), BF16),
        scratch_shapes=[pltpu.VMEM((T, 2 * LANES), BF16), pltpu.VMEM((LANES, T), BF16)],
        compiler_params=_cparams(2),
        name="nsa_attn_prompt",
    )(nq, gate, kc, vc, rows, win)


def _pages_copy(cache_hbm, layer, page, r0, dst, j, sem):
    n = cache_hbm.shape[-1]
    return pltpu.make_async_copy(cache_hbm.at[layer, page, pl.ds(r0, 2)],
                                 dst.at[:, :, pl.ds(pl.multiple_of(j * n, n), n)], sem)


def _pages_start(pt_ref, b, cache_hbm, layer, r0, dst, sem, npages):
    def issue(j, carry):
        _pages_copy(cache_hbm, layer, pt_ref[b, j], r0, dst, j, sem).start()
        return carry
    lax.fori_loop(0, npages, issue, 0, unroll=PAGE_UNROLL)


def _pages_wait(cache_hbm, layer, r0, dst, sem, npages):
    def wait(j, carry):
        _pages_copy(cache_hbm, layer, 0, r0, dst, j, sem).wait()
        return carry
    lax.fori_loop(0, npages, wait, 0, unroll=PAGE_UNROLL)


def _softmax2(s1, s2, mask2):
    s2 = jnp.where(mask2, s2, -2.0 * BIG)
    m = jnp.maximum(jnp.max(s1, axis=-1, keepdims=True), jnp.max(s2, axis=-1, keepdims=True))
    e1 = jnp.exp2(s1 - m)
    e2 = jnp.exp2(s2 - m)
    den = jnp.sum(e1, axis=-1, keepdims=True) + jnp.sum(e2, axis=-1, keepdims=True)
    return e1, e2, 1.0 / den


def _nsa_sample_kernel(pt_ref, q_ref, gate_ref, rows_ref, wt_ref, wnew_ref, cache_hbm,
                       w1_ref, pe_ref, b1_ref, w2_ref, o_ref, wout_ref,
                       cmpbuf, selbuf, xk, xv, kaug, vt, newbuf, wnewbuf, bias_sc, perm_sc, csem, ssem,
                       *, layer, npages, P, Q, wb, nb, n_sel):
    b = pl.program_id(0)
    nseq = pl.num_programs(0)
    page = cache_hbm.shape[-1]
    nch = P // CMP_STRIDE

    @pl.when(b == 0)
    def _():
        _pages_start(pt_ref, 0, cache_hbm, layer, 0, cmpbuf, csem, npages)
        _pages_start(pt_ref, 0, cache_hbm, layer, 2, selbuf, ssem, npages)
        _compress_bias(w1_ref, pe_ref, bias_sc)

        def blocks(c, carry):
            c0 = pl.multiple_of(c * CAST_CHUNK, CAST_CHUNK)
            kaug[LANES:2 * LANES, pl.ds(c0, CAST_CHUNK)] = _block_rows(c0, CAST_CHUNK)
            return carry
        lax.fori_loop(0, P // CAST_CHUNK, blocks, 0)
        newbuf[...] = jnp.zeros(newbuf.shape, F32)
        wnewbuf[...] = jnp.zeros(wnewbuf.shape, F32)
        src = lax.broadcasted_iota(jnp.int32, (2 * page, 2 * page), 0)
        dst_col = lax.broadcasted_iota(jnp.int32, (2 * page, 2 * page), 1)
        t = src & (page - 1)
        want = (src - t) + (t % CMP_STRIDE) * (page // CMP_STRIDE) + t // CMP_STRIDE
        perm_sc[...] = jnp.where(dst_col == want, 1.0, 0.0).astype(BF16)

    _pages_wait(cache_hbm, layer, 0, cmpbuf, csem, npages)
    per_page = page // CMP_STRIDE
    for c, dst in enumerate((xk, xv)):
        for jp in range(npages // 2):
            cols = cmpbuf[c, :, 2 * jp * page:2 * (jp + 1) * page].astype(BF16)
            regrouped = jnp.dot(cols, perm_sc[...], preferred_element_type=F32)
            for half in range(2):
                tok = regrouped[:, half * page:(half + 1) * page].T
                r0 = (2 * jp + half) * per_page
                for p in range(CMP_STRIDE):
                    dst[r0:r0 + per_page, p * LANES:(p + 1) * LANES] = tok[p * per_page:(p + 1) * per_page, :]
    kc = _compress_one(0, xk[...].astype(BF16), nch, w1_ref, bias_sc, b1_ref, w2_ref)
    vc = _compress_one(1, xv[...].astype(BF16), nch, w1_ref, bias_sc, b1_ref, w2_ref)

    @pl.when(b + 1 < nseq)
    def _():
        _pages_start(pt_ref, b + 1, cache_hbm, layer, 0, cmpbuf, csem, npages)

    _pages_wait(cache_hbm, layer, 2, selbuf, ssem, npages)

    def pack(c, carry):
        c0 = pl.multiple_of(c * CAST_CHUNK, CAST_CHUNK)
        kaug[0:LANES, pl.ds(c0, CAST_CHUNK)] = selbuf[0, :, pl.ds(c0, CAST_CHUNK)].astype(BF16)
        vt[:, pl.ds(c0, CAST_CHUNK)] = selbuf[1, :, pl.ds(c0, CAST_CHUNK)].astype(BF16)
        return carry
    lax.fori_loop(0, P // CAST_CHUNK, pack, 0)

    @pl.when(b + 1 < nseq)
    def _():
        _pages_start(pt_ref, b + 1, cache_hbm, layer, 2, selbuf, ssem, npages)

    newbuf[0:Q, :] = rows_ref[0, :, 2 * LANES:4 * LANES]
    wnewbuf[0:Q, :] = wnew_ref[0]

    qs, qp1, o_c, sel_t = _nsa_front(q_ref[0], P, Q, kc, vc, nch - 1, n_sel)
    qpos = _rep_all(qp1)
    lane = lax.broadcasted_iota(jnp.int32, (1, LANES), 1)
    new_pos = P + lane
    is_new = lane < Q

    def as_rows(blk):
        if blk.shape[0] < LANES:
            blk = jnp.concatenate([blk, jnp.zeros((LANES - blk.shape[0], LANES), F32)], axis=0)
        return blk.T[0:NSA_KV * Q]
    sel_rows = as_rows(sel_t[0:min(LANES, sel_t.shape[0])])
    sel_new = as_rows(sel_t[nb:nb + SUBLANES])[:, 0:1]

    qaug = jnp.concatenate([qs, _rep_heads(sel_rows - 1.0, Q).astype(BF16)], axis=1)
    s_past = jnp.dot(qaug, kaug[...], preferred_element_type=F32)
    s_new = _bdot_nt(qs, newbuf[:, 0:LANES])
    new_ok = is_new & (new_pos <= qpos) & (_rep_heads(sel_new, Q) > 0.5)
    e1, e2, inv = _softmax2(s_past, s_new, new_ok)
    o_s = (lax.dot_general(e1.astype(BF16), vt[...], (((1,), (1,)), ((), ())), preferred_element_type=F32)
           + _bdot(e2, newbuf[:, LANES:2 * LANES])) * inv

    dpast = qp1 - ((P - wb) + lax.broadcasted_iota(jnp.int32, (1, wb), 1))
    s_wp = (jnp.dot(qs, wt_ref[0, 0].astype(BF16), preferred_element_type=F32)
            + _rep_all(jnp.where((dpast >= 0) & (dpast <= WINDOW), 0.0, -BIG)))
    s_wn = _bdot_nt(qs, wnewbuf[:, 0:LANES])
    dnew = qpos - new_pos
    e1, e2, inv = _softmax2(s_wp, s_wn, is_new & (dnew >= 0) & (dnew <= WINDOW))
    o_w = (lax.dot_general(e1.astype(BF16), wt_ref[0, 1].astype(BF16), (((1,), (1,)), ((), ())),
                           preferred_element_type=F32)
           + _bdot(e2, wnewbuf[:, LANES:2 * LANES])) * inv

    _nsa_combine(gate_ref[0], o_c, o_s, o_w, Q, o_ref)

    for kv in range(2):
        new_cols = wnewbuf[:, kv * LANES:(kv + 1) * LANES].T[:, 0:Q]
        wout_ref[0, kv] = jnp.concatenate([wt_ref[0, kv][:, Q:], new_cols], axis=1)


def _nsa_sample(page_table, nq, gate, rows, win_t, win_new, cache_t, layer, cw):
    DB, Q, HD = nq.shape
    npages = page_table.shape[1]
    page = cache_t.shape[-1]
    P = npages * page
    wb = win_t.shape[-1]
    nb = P // SEL_BLOCK
    n_sel = -(-(P + Q) // SEL_BLOCK)
    w1, pe, b1, w2 = cw
    assert Q == SUBLANES and P % SEL_BLOCK == 0 and Q <= SEL_BLOCK and nb <= LANES and n_sel >= SEL_TOPK
    assert P % CAST_CHUNK == 0 and (P + Q) // CMP_STRIDE == P // CMP_STRIDE and page == LANES
    assert npages % 2 == 0 and page // CMP_STRIDE == SUBLANES and wb >= Q
    bs = lambda shape: pl.BlockSpec((1,) + shape, lambda b, pt: (b,) + (0,) * len(shape))
    win_spec = pl.BlockSpec((None, 1, 2, LANES, wb), lambda b, pt: (layer, b, 0, 0, 0))
    grid_spec = pltpu.PrefetchScalarGridSpec(
        num_scalar_prefetch=1, grid=(DB,),
        in_specs=[bs((Q, HD)), bs((Q, LANES)), bs((Q, 4 * LANES)), win_spec, bs((Q, 2 * LANES)),
                  pl.BlockSpec(memory_space=pl.ANY),
                  _pspec(w1), _pspec(pe), _pspec(b1), _pspec(w2)],
        out_specs=[bs((Q, HD)), bs((2, LANES, wb))],
        scratch_shapes=[pltpu.VMEM((2, LANES, P), F32), pltpu.VMEM((2, LANES, P), F32),
                        pltpu.VMEM((P // CMP_STRIDE, CMP_STRIDE * LANES), F32),
                        pltpu.VMEM((P // CMP_STRIDE, CMP_STRIDE * LANES), F32),
                        pltpu.VMEM((2 * LANES, P), BF16), pltpu.VMEM((LANES, P), BF16),
                        pltpu.VMEM((LANES, 2 * LANES), F32), pltpu.VMEM((LANES, 2 * LANES), F32),
                        pltpu.VMEM((2, SUBLANES, 4 * LANES), F32), pltpu.VMEM((2 * page, 2 * page), BF16),
                        pltpu.SemaphoreType.DMA(()), pltpu.SemaphoreType.DMA(())])
    return pl.pallas_call(
        functools.partial(_nsa_sample_kernel, layer=layer, npages=npages, P=P, Q=Q, wb=wb, nb=nb, n_sel=n_sel),
        grid_spec=grid_spec,
        out_shape=[jax.ShapeDtypeStruct((DB, Q, HD), F32), jax.ShapeDtypeStruct((DB, 2, LANES, wb), F32)],
        compiler_params=_cparams(1),
        name="nsa_sample",
    )(page_table, nq, gate, rows, win_t, win_new, cache_t, _parg(w1), _parg(pe), _parg(b1), _parg(w2))


def _ret_kernel(q_ref, k_ref, v_ref, g_ref, cos_ref, sin_ref, s0_ref, gn_ref, o_ref, snew_ref, s_sc,
                *, C, nC, nseq):
    c = pl.program_id(1)

    @pl.when(c == 0)
    def _():
        s_sc[...] = s0_ref[...]

    cosf = cos_ref[...]
    sinf = sin_ref[...]
    diff = (lax.broadcasted_iota(jnp.int32, (C, C), 0) - lax.broadcasted_iota(jnp.int32, (C, C), 1)).astype(F32)
    ii = lax.broadcasted_iota(jnp.int32, (C, 1), 0).astype(F32)
    half = RET_DK // 2
    for h in range(RET_HEADS):
        lg = math.log(1.0 - 2.0 ** (-5.0 - h))
        hs = slice(h * RET_DK, (h + 1) * RET_DK)
        decay = jnp.where(diff >= 0, jnp.exp(jnp.maximum(diff, 0.0) * lg), 0.0)
        cross = jnp.exp((ii + 1.0) * lg)
        kweight = jnp.exp((C - 1.0 - ii) * lg)
        for b in range(nseq):
            q = q_ref[b, :, hs]
            k = k_ref[b, :, hs]
            v = v_ref[b, :, hs]
            qr = q * cosf + pltpu.roll(q, half, 1) * sinf
            kr = (k * cosf + pltpu.roll(k, half, 1) * sinf) * (RET_DK ** -0.5)
            o_inner = _bdot(_bdot_nt(qr, kr) * decay, v)
            s_old = s_sc[b, h]
            o_cross = _bdot(qr, s_old) * cross
            kv = lax.dot_general((kr * kweight).astype(BF16), v.astype(BF16), (((0,), (0,)), ((), ())),
                                 preferred_element_type=F32)
            s_sc[b, h] = math.exp(C * lg) * s_old + kv
            o = o_inner + o_cross
            mu = jnp.mean(o, axis=-1, keepdims=True)
            var = jnp.mean(jnp.square(o - mu), axis=-1, keepdims=True)
            gate = g_ref[b, :, hs]
            o_ref[b, :, hs] = (((o - mu) * lax.rsqrt(var + EPS)) * gn_ref[:, hs]
                               * (gate * _sigmoid(gate))).astype(o_ref.dtype)

    @pl.when(c == nC - 1)
    def _():
        snew_ref[...] = s_sc[...]


def _retention(rq, rk, rv, rg, cosf, sinf, s0, gn, nseq, out_dtype):
    B, T, W = rq.shape
    C = RET_CHUNK if (T >= RET_CHUNK and T % RET_CHUNK == 0) else T
    nC = T // C
    tok = pl.BlockSpec((nseq, C, W), lambda b, c: (b, c, 0))
    tab = pl.BlockSpec((C, RET_DK), lambda b, c: (c, 0))
    st = pl.BlockSpec((nseq,) + s0.shape[1:], lambda b, c: (b, 0, 0, 0))
    return pl.pallas_call(
        functools.partial(_ret_kernel, C=C, nC=nC, nseq=nseq),
        grid=(B // nseq, nC),
        in_specs=[tok, tok, tok, tok, tab, tab, st, _pspec(gn)],
        out_specs=[tok, st],
        out_shape=[jax.ShapeDtypeStruct((B, T, W), out_dtype), jax.ShapeDtypeStruct(s0.shape, F32)],
        scratch_shapes=[pltpu.VMEM((nseq,) + s0.shape[1:], F32)],
        compiler_params=_cparams(2),
        name="retention",
    )(rq, rk, rv, rg, cosf, sinf, s0, _parg(gn))


def _shift_carry(x, k, tail8):
    r = pltpu.roll(x, k, 0)
    row8 = lax.broadcasted_iota(jnp.int32, (SUBLANES, 1), 0)
    first = jnp.where(row8 >= k, r[:SUBLANES], pltpu.roll(tail8, k, 0))
    return jnp.concatenate([first, r[SUBLANES:]], axis=0)


def _shift_seg(x, k, fill, tpos):
    return jnp.where(tpos >= k, pltpu.roll(x, k, 0), fill)


def _rglru_kernel(x_ref, gate_ref, st_ref, h0_ref, cw_ref, cb_ref, wa_ref, ba_ref, wx_ref, bx_ref, lam_ref,
                  o_ref, h_ref, tail_sc, h_sc, *, tm, seg):
    carry = seg == 0
    x = x_ref[0]
    rows = lax.broadcasted_iota(jnp.int32, (tm, 1), 0)
    if carry:
        @pl.when(pl.program_id(1) == 0)
        def _():
            tail_sc[...] = st_ref[0]
            h_sc[...] = h0_ref[0]
        tail8 = tail_sc[...]
        shifted = [_shift_carry(x, k, tail8) for k in range(1, RG_CONV)]
        tpos = rows & (SUBLANES - 1)
    else:
        tpos = rows & (seg - 1)
        shifted = [_shift_seg(x, k, st_ref[k - 1], tpos) for k in range(1, RG_CONV)]
    xc = cb_ref[...] + cw_ref[RG_CONV - 1:RG_CONV, :] * x
    for k in range(1, RG_CONV):
        xc = xc + cw_ref[RG_CONV - 1 - k:RG_CONV - k, :] * shifted[k - 1]
    r = _sigmoid(_bdot(xc, wa_ref[...]) + ba_ref[...])
    i = _sigmoid(_bdot(xc, wx_ref[...]) + bx_ref[...])
    lam = lam_ref[...]
    softplus = jnp.maximum(-lam, 0.0) + jnp.log(1.0 + jnp.exp(-jnp.abs(lam)))
    log_a = (-RG_C * r) * softplus
    a = jnp.exp(log_a)
    gap = 1.0 - a * a
    bt = jnp.where(gap > 0, gap * lax.rsqrt(gap), 0.0) * (i * xc)
    if not carry:
        bt = bt + a * h0_ref[0]
    k = 1
    while k < SUBLANES:
        ok = tpos >= k
        a_prev = jnp.where(ok, pltpu.roll(a, k, 0), 1.0)
        b_prev = jnp.where(ok, pltpu.roll(bt, k, 0), 0.0)
        bt = a * b_prev + bt
        a = a * a_prev
        k *= 2
    if carry:
        h_prev = h_sc[SUBLANES - 1:SUBLANES, :]
        groups = []
        for g0 in range(0, tm, SUBLANES):
            h_g = bt[g0:g0 + SUBLANES] + a[g0:g0 + SUBLANES] * h_prev
            groups.append(h_g)
            h_prev = h_g[SUBLANES - 1:SUBLANES, :]
        bt = jnp.concatenate(groups, axis=0)
    o_ref[0] = (bt * _gelu(gate_ref[0].astype(F32))).astype(o_ref.dtype)
    if carry:
        tail_sc[...] = x[tm - SUBLANES:]
        h_sc[...] = bt[tm - SUBLANES:]
        h_ref[0] = bt[tm - SUBLANES:]
    else:
        h_ref[0] = bt


def _rglru(rx, rgate, st, h0, rw, tm, seg):
    G, Tg, W = rx.shape
    tok = pl.BlockSpec((1, tm, W), lambda g, t: (g, t, 0))
    if seg == 0:
        st_spec = pl.BlockSpec((1, SUBLANES, W), lambda g, t: (g, 0, 0))
        h0_spec = pl.BlockSpec((1, SUBLANES, W), lambda g, t: (g, 0, 0))
        h_spec = pl.BlockSpec((1, SUBLANES, W), lambda g, t: (g, 0, 0))
        h_shape = (G, SUBLANES, W)
    else:
        st_spec = pl.BlockSpec(st.shape, lambda g, t: (0, 0, 0))
        h0_spec = tok
        h_spec = tok
        h_shape = (G, Tg, W)
    return pl.pallas_call(
        functools.partial(_rglru_kernel, tm=tm, seg=seg),
        grid=(G, Tg // tm),
        in_specs=[tok, tok, st_spec, h0_spec] + [_pspec(a) for a in rw],
        out_specs=[tok, h_spec],
        out_shape=[jax.ShapeDtypeStruct((G, Tg, W), rgate.dtype), jax.ShapeDtypeStruct(h_shape, F32)],
        scratch_shapes=[pltpu.VMEM((SUBLANES, W), F32), pltpu.VMEM((SUBLANES, W), F32)],
        compiler_params=_cparams(2),
        name="rglru",
    )(rx, rgate, st, h0, *[_parg(a) for a in rw])


def _mix_ffn_kernel(x_ref, oa_ref, or_ref, oc_ref, mg_ref, gt1_ref, wa_ref, wb_ref, wc_ref, wo_ref,
                    g_ref, sc_ref, sh_ref, gt_ref, st_ref, wup_ref, cw_ref, cb_ref, wdn_ref, fg_ref,
                    y_ref, fnew_ref, tail_sc, *, tm, seg, F, chunks, final):
    D = x_ref.shape[2]
    pa = _bdot(oa_ref[0], wa_ref[...])
    pb = _bdot(or_ref[0], wb_ref[...])
    pc = _bdot(oc_ref[0], wc_ref[...])
    gate = lambda i: _sigmoid(mg_ref[0, :, i * D:(i + 1) * D].astype(F32))
    merged = gate(0) * pa + gate(1) * pb + gate(2) * pc
    x = x_ref[0] + gt1_ref[...] * _bdot(merged, wo_ref[...])

    carry = seg == 0
    h = _rms_mod(x, g_ref[...], sc_ref[...], sh_ref[...]).astype(BF16)
    rows = lax.broadcasted_iota(jnp.int32, (tm, 1), 0)
    if carry:
        @pl.when(pl.program_id(1) == 0)
        def _():
            tail_sc[...] = st_ref[0]
    else:
        tpos = rows & (seg - 1)
    acc = jnp.zeros(x.shape, F32)
    for c0, wck in chunks:
        cs = slice(c0, c0 + wck)
        gp = jnp.dot(h, wup_ref[:, c0:c0 + wck], preferred_element_type=F32)
        val = jnp.dot(h, wup_ref[:, F + c0:F + c0 + wck], preferred_element_type=F32)
        if carry:
            tail8 = tail_sc[:, cs]
            shifted = [_shift_carry(gp, k, tail8) for k in range(1, FFN_CONV)]
            tail_sc[:, cs] = gp[tm - SUBLANES:]
            fnew_ref[0, :, cs] = gp[tm - SUBLANES:]
        else:
            shifted = [_shift_seg(gp, k, st_ref[k - 1, :, cs], tpos) for k in range(1, FFN_CONV)]
            fnew_ref[0, :, cs] = gp
        gc = cb_ref[:, cs] + cw_ref[FFN_CONV - 1:FFN_CONV, cs] * gp
        for k in range(1, FFN_CONV):
            gc = gc + cw_ref[FFN_CONV - 1 - k:FFN_CONV - k, cs] * shifted[k - 1]
        act = (gc * _sigmoid(gc)) * val
        acc = acc + _bdot(act, wdn_ref[cs, :])
    y = x + gt_ref[...] * acc
    if final:
        y = (y * lax.rsqrt(jnp.mean(y * y, axis=-1, keepdims=True) + EPS)) * fg_ref[...]
    y_ref[0] = y


def _ffn_chunks(F):
    half = -(-(F // 2) // MXU_DEPTH_V7X) * MXU_DEPTH_V7X
    return ((0, half), (half, F - half)) if 0 < half < F else ((0, F),)


def _mix_ffn(x, oa, orr, oc, mg, gt1, wa, wb, wc, wo, g, sc, sh, gt, st, wup, cw, cb, wdn, fg, tm, seg, final):
    G, Tg, D = x.shape
    F = wdn.shape[0]
    tokw = lambda w: pl.BlockSpec((1, tm, w), lambda g_, t: (g_, t, 0))
    tok = tokw(D)
    if seg == 0:
        st_spec = pl.BlockSpec((1, SUBLANES, F), lambda g_, t: (g_, 0, 0))
        fn_spec = pl.BlockSpec((1, SUBLANES, F), lambda g_, t: (g_, 0, 0))
        fn_shape = (G, SUBLANES, F)
    else:
        st_spec = pl.BlockSpec(st.shape, lambda g_, t: (0, 0, 0))
        fn_spec = pl.BlockSpec((1, tm, F), lambda g_, t: (g_, t, 0))
        fn_shape = (G, Tg, F)
    return pl.pallas_call(
        functools.partial(_mix_ffn_kernel, tm=tm, seg=seg, F=F, chunks=_ffn_chunks(F), final=final),
        grid=(G, Tg // tm),
        in_specs=[tok, tokw(oa.shape[2]), tokw(orr.shape[2]), tokw(oc.shape[2]), tokw(3 * D),
                  _mod_spec(gt1, tm, D), _pspec(wa), _pspec(wb), _pspec(wc), _pspec(wo),
                  _pspec(g), _mod_spec(sc, tm, D), _mod_spec(sh, tm, D), _mod_spec(gt, tm, D),
                  st_spec, _pspec(wup), _pspec(cw), _pspec(cb), _pspec(wdn), _pspec(fg)],
        out_specs=[tok, fn_spec],
        out_shape=[jax.ShapeDtypeStruct((G, Tg, D), F32), jax.ShapeDtypeStruct(fn_shape, F32)],
        scratch_shapes=[pltpu.VMEM((SUBLANES, F), F32)],
        compiler_params=_cparams(2),
        name="mix_ffn",
    )(x, oa, orr, oc, mg, gt1.arr, _parg(wa), _parg(wb), _parg(wc), _parg(wo),
      _parg(g), sc.arr, sh.arr, gt.arr, st, _parg(wup), _parg(cw), _parg(cb), _parg(wdn), fg)


def _head_perm():
    return np.array([(k * NSA_HPG + r) * NSA_HD + d
                     for r in range(NSA_HPG) for k in range(NSA_KV) for d in range(NSA_HD)], np.int32)


def _block_diag(w):
    n, a, b = w.shape[-3:]
    eye = jnp.eye(n, dtype=w.dtype)
    out = jnp.einsum('ij,...iab->...iajb', eye, w)
    return out.reshape(w.shape[:-3] + (n * a, n * b))


def _seg_fill(buf, k, seg):
    B, nb, C = buf.shape
    part = jnp.concatenate([buf[:, nb - k:, :], jnp.zeros((B, seg - k, C), buf.dtype)], axis=1)
    return part.reshape(B * seg, C)


def kernel(x_prompt, x_sample, cache_nsa, cache_nsa_win, state_ret, state_rglru_h, state_rglru_conv,
           state_ffn_conv, page_table, c_prompt, c_sample, norm1_g, norm2_g, w_ada, b_ada, w_in, cmp_pe,
           cmp_w1, cmp_b1, cmp_w2, ret_gn_g, rg_conv_w, rg_conv_b, rg_w_a, rg_b_a, rg_w_x, rg_b_x, rg_lambda,
           w_br_a, w_br_b, w_br_c, w_out, ffn_w_up, ffn_conv_w, ffn_conv_b, ffn_w_down, final_norm_g):
    B, T, D = x_prompt.shape
    DB, Q, _ = x_sample.shape
    L = w_in.shape[0]
    npages = page_table.shape[1]
    page = cache_nsa.shape[2]
    P = npages * page
    NQ = NSA_KV * NSA_HPG * NSA_HD
    NKV = NSA_KV * NSA_HD
    RW = RET_HEADS * RET_DK
    W = rg_conv_w.shape[2]
    F = ffn_w_down.shape[1]
    NS = DB * Q
    wbuf_len = cache_nsa_win.shape[2]
    assert Q == SUBLANES and T >= RG_CONV and P % CMP_STRIDE == 0

    mod = _ada(jnp.concatenate([c_prompt, c_sample], axis=0), w_ada, b_ada)
    cache_t = jnp.transpose(cache_nsa, (0, 1, 3, 4, 5, 2)).reshape(L, cache_nsa.shape[1], 4, NKV, page)
    win_t_all = jnp.transpose(cache_nsa_win, (0, 1, 3, 4, 5, 2)).reshape(L, DB, 2, NKV, wbuf_len)
    perm = _head_perm()

    half = RET_DK // 2
    freq = ROPE_BASE ** (-jnp.arange(half, dtype=F32) / half)

    def rope_tables(pos):
        ang = pos.astype(F32)[:, None] * freq[None, :]
        cos, sin = jnp.cos(ang), jnp.sin(ang)
        return jnp.concatenate([cos, cos], axis=1), jnp.concatenate([-sin, sin], axis=1)

    cos_p, sin_p = rope_tables(jnp.arange(T, dtype=jnp.int32))
    cos_s, sin_s = rope_tables(P + jnp.arange(Q, dtype=jnp.int32))

    widths = (NQ, 4 * NKV, 2 * NKV, LANES, RW, RW, RW, RW, W, W, 3 * D)
    dt_prompt = (F32,) * 9 + (BF16, BF16)
    dt_sample = (F32,) * 11
    offs = np.cumsum((0, NQ, 6 * NKV, 3 * NSA_KV * NSA_HPG, RW, RW, RW, RW, W, W, 3 * D))
    ngate = 3 * NSA_KV * NSA_HPG

    xp = x_prompt
    xs = x_sample.reshape(1, NS, D)
    outs_p = [[] for _ in range(6)]
    outs_s = [[] for _ in range(6)]
    tm_p = 256 if T % 256 == 0 else T

    wi = w_in.astype(BF16)
    w_cat_all = jnp.concatenate([
        wi[:, :, offs[0]:offs[1]][:, :, perm],
        wi[:, :, offs[1]:offs[2]],
        jnp.pad(wi[:, :, offs[2]:offs[3]], ((0, 0), (0, 0), (0, LANES - ngate))),
        wi[:, :, offs[3]:]], axis=2)
    grouped = lambda w: _block_diag(jnp.broadcast_to(w[..., None, :, :], w.shape[:-2] + (NSA_KV,) + w.shape[-2:]))
    cw1_all = jnp.concatenate([grouped(cmp_w1[:, :, :CMP_STRIDE]), grouped(cmp_w1[:, :, CMP_STRIDE:])], axis=-1)
    cw1_all = cw1_all.reshape(L, 2, CMP_STRIDE * NKV, 4 * LANES).astype(BF16)
    cpe_all = jnp.tile(cmp_pe, (1, 1, 1, NSA_KV)).reshape(L, 2, 2, CMP_STRIDE * NKV)
    cb1_all = jnp.tile(cmp_b1, (1, 1, NSA_KV))[:, :, None, :]
    cw2_all = grouped(cmp_w2).astype(BF16)
    row = lambda a: a[:, None, :]
    rw_all = (rg_conv_w, row(rg_conv_b), _block_diag(rg_w_a).astype(BF16), row(rg_b_a),
              _block_diag(rg_w_x).astype(BF16), row(rg_b_x), row(rg_lambda))
    wa_all = w_br_a[:, perm].astype(BF16)
    wb_all = w_br_b.astype(BF16)
    wc_all = w_br_c.astype(BF16)
    wo_all = w_out.astype(BF16)
    wup_all = ffn_w_up.astype(BF16)
    wdn_all = ffn_w_down.astype(BF16)
    g1_all, g2_all, gn_all, fcb_all = row(norm1_g), row(norm2_g), row(ret_gn_g), row(ffn_conv_b)
    fg = final_norm_g[None]
    mod_p = mod[:, :B].reshape(L, B, 1, 6 * D)
    mod_s = jnp.repeat(mod[:, B:], Q, axis=1)

    for l in range(L):
        lay = lambda a: _LayerOf(a, l)
        w_cat = lay(w_cat_all)
        cw = (lay(cw1_all), lay(cpe_all), lay(cb1_all), lay(cw2_all))
        rw = tuple(lay(a) for a in rw_all)
        wa, wb, wc, wo, wup, wdn = (lay(a) for a in (wa_all, wb_all, wc_all, wo_all, wup_all, wdn_all))
        g1, g2, gn, fcw, fcb = (lay(a) for a in (g1_all, g2_all, gn_all, ffn_conv_w, fcb_all))
        final = l == L - 1

        m = [_ModOf(mod_p, l, i) for i in range(6)]
        (nq, rows, win, gate, rq, rk, rv, rg, rx, rgate, mg) = _inproj(xp, g1, m[1], m[0], w_cat, widths,
                                                                        dt_prompt, tm_p)
        kc, vc = _compress_prompt(rows, cw)
        o_a = _nsa_prompt(nq, gate, kc, vc, rows, win)
        o_r, s_new = _retention(rq, rk, rv, rg, cos_p, sin_p,
                                jnp.zeros((B, RET_HEADS, RET_DK, RET_DK), F32), gn, B, BF16)
        zs = jnp.zeros((B, SUBLANES, W), F32)
        o_c, h_tail = _rglru(rx, rgate, zs, zs, rw, tm_p, 0)
        xp, f_tail = _mix_ffn(xp, o_a, o_r, o_c, mg, m[2], wa, wb, wc, wo, g2, m[4], m[3], m[5],
                              jnp.zeros((B, SUBLANES, F), F32), wup, fcw, fcb, wdn, fg, tm_p, 0, final)
        wn = min(WINDOW, T)
        outs_p[0].append(rows.reshape(B, T, 4, NSA_KV, NSA_HD))
        outs_p[1].append(win[:, T - wn:].reshape(B, wn, 2, NSA_KV, NSA_HD))
        outs_p[2].append(s_new)
        outs_p[3].append(h_tail[:, SUBLANES - 1])
        outs_p[4].append(rx[:, T - (RG_CONV - 1):])
        outs_p[5].append(f_tail[:, SUBLANES - (FFN_CONV - 1):])

        ms = [_ModOf(mod_s, l, i) for i in range(6)]
        (nq, rows, win, gate, rq, rk, rv, rg, rx, rgate, mg) = _inproj(xs, g1, ms[1], ms[0], w_cat, widths,
                                                                        dt_sample, NS)
        r3 = lambda a: a.reshape(DB, Q, a.shape[-1])
        o_a, win_next = _nsa_sample(page_table, r3(nq), r3(gate), r3(rows), win_t_all, r3(win), cache_t, l, cw)
        o_r, s_new = _retention(r3(rq), r3(rk), r3(rv), r3(rg), cos_s, sin_s, state_ret[l].astype(F32), gn,
                                math.gcd(DB, SUBLANES), F32)
        cbuf = state_rglru_conv[l]
        st = jnp.stack([_seg_fill(cbuf, k, Q) for k in range(1, RG_CONV)])
        h0 = jnp.pad(state_rglru_h[l].astype(F32)[:, None, :], ((0, 0), (0, Q - 1), (0, 0))).reshape(1, NS, W)
        o_c, h_all = _rglru(rx, rgate, st, h0, rw, NS, Q)
        fbuf = state_ffn_conv[l]
        fst = jnp.stack([_seg_fill(fbuf, k, Q) for k in range(1, FFN_CONV)])
        xs, g_all = _mix_ffn(xs, o_a.reshape(1, NS, NQ), o_r.reshape(1, NS, RW), o_c, mg, ms[2], wa, wb, wc, wo,
                             g2, ms[4], ms[3], ms[5], fst, wup, fcw, fcb, wdn, fg, NS, Q, final)
        outs_s[0].append(rows.reshape(DB, Q, 4, NSA_KV, NSA_HD))
        outs_s[1].append(win_next)
        outs_s[2].append(s_new)
        outs_s[3].append(h_all.reshape(DB, Q, W)[:, Q - 1])
        outs_s[4].append(jnp.concatenate([cbuf, rx.reshape(DB, Q, W)], axis=1)[:, Q:])
        outs_s[5].append(jnp.concatenate([fbuf, g_all.reshape(DB, Q, F)], axis=1)[:, Q:])

    sp = [jnp.stack(a) for a in outs_p]
    ss = [jnp.stack(a) for a in outs_s]
    ss[1] = jnp.transpose(ss[1].reshape(L, DB, 2, NSA_KV, NSA_HD, wbuf_len), (0, 1, 5, 2, 3, 4))
    return (xp, xs.reshape(DB, Q, D), sp[0], ss[0], sp[1], ss[1], sp[2], ss[2],
            sp[3], ss[3], sp[4], ss[4], sp[5], ss[5])
```

```python
import functools
import math

import numpy as np
import jax
import jax.numpy as jnp
from jax import lax
from jax.experimental import pallas as pl
from jax.experimental.pallas import tpu as pltpu

F32 = jnp.float32
BF16 = jnp.bfloat16

NSA_KV = 2
NSA_HPG = 4
NSA_HD = 64
CMP_STRIDE = 16
CMP_BLOCK = 32
SEL_BLOCK = 64
SEL_TOPK = 16
SEL_FORCE = 1e4
WINDOW = 512
QBLOCK = 256
RET_HEADS = 4
RET_DK = 128
RET_CHUNK = 128
ROPE_BASE = 10000.0
RG_CONV = 4
RG_C = 8.0
FFN_CONV = 3
EPS = 1e-6

NEG = -1e30
BIG = float(2 ** 60)
LOG2E = 1.4426950408889634
SUBLANES = 8
LANES = 128
VMEM_LIMIT_V7X = 56 * 1024 * 1024
SEL_CHUNK = 512
CAST_CHUNK = 1024
PAGE_UNROLL = 8
MXU_DEPTH_V7X = 256


def _cparams(n_grid):
    return pltpu.CompilerParams(dimension_semantics=("arbitrary",) * n_grid,
                                vmem_limit_bytes=VMEM_LIMIT_V7X)


def _bdot(a, b):
    return jnp.dot(a.astype(BF16), b.astype(BF16), preferred_element_type=F32)


def _bdot_nt(a, b):
    return lax.dot_general(a.astype(BF16), b.astype(BF16), (((1,), (1,)), ((), ())),
                           preferred_element_type=F32)


def _split(a):
    hi = a.astype(BF16)
    lo = (a - hi.astype(F32)).astype(BF16)
    return hi, lo


def _dot3(a, b):
    ah, al = _split(a)
    bh, bl = _split(b)
    d = functools.partial(jnp.dot, preferred_element_type=F32)
    return d(ah, bh) + d(al, bh) + d(ah, bl)


def _sigmoid(x):
    return 0.5 * jnp.tanh(0.5 * x) + 0.5


def _gelu(x):
    return 0.5 * x * (1.0 + jnp.tanh(0.7978845608028654 * (x + 0.044715 * (x * x * x))))


def _rms_mod(x, g, sc, sh):
    y = x * lax.rsqrt(jnp.mean(x * x, axis=-1, keepdims=True) + EPS)
    return (y * g) * (1.0 + sc) + sh


def _ada_kernel(c_ref, w_ref, b_ref, o_ref):
    c = c_ref[...]
    o_ref[0] = _dot3(c * _sigmoid(c), w_ref[0]) + b_ref[0]


def _ada(c_all, w_ada, b_ada):
    L, D, E = w_ada.shape
    n = c_all.shape[0]
    tn = 1536 if E % 1536 == 0 else E
    return pl.pallas_call(
        _ada_kernel,
        grid=(L, E // tn),
        in_specs=[pl.BlockSpec((n, D), lambda l, j: (0, 0)),
                  pl.BlockSpec((1, D, tn), lambda l, j: (l, 0, j)),
                  pl.BlockSpec((1, 1, tn), lambda l, j: (l, 0, j))],
        out_specs=pl.BlockSpec((1, n, tn), lambda l, j: (l, 0, j)),
        out_shape=jax.ShapeDtypeStruct((L, n, E), F32),
        compiler_params=_cparams(2),
        name="ada_mod",
    )(c_all, w_ada, b_ada.reshape(L, 1, E))


class _LayerOf:
    def __init__(self, arr, layer):
        self.arr, self.layer, self.shape = arr, layer, arr.shape[1:]


class _ModOf:
    def __init__(self, arr, layer, idx):
        self.arr, self.layer, self.idx = arr, layer, idx


def _mod_spec(m, tm, d):
    l, i = m.layer, m.idx
    if m.arr.ndim == 4:
        return pl.BlockSpec((None, None, 1, d), lambda g, t: (l, g, 0, i))
    return pl.BlockSpec((None, tm, d), lambda g, t: (l, t, i))


def _pspec(p):
    if isinstance(p, _LayerOf):
        nd, l = len(p.shape), p.layer
        return pl.BlockSpec((None,) + tuple(p.shape), lambda *a: (l,) + (0,) * nd, pipeline_mode=pl.Buffered(1))
    nd = p.ndim
    return pl.BlockSpec(p.shape, lambda *a: (0,) * nd, pipeline_mode=pl.Buffered(1))


def _parg(p):
    return p.arr if isinstance(p, _LayerOf) else p


def _inproj_kernel(x_ref, g_ref, sc_ref, sh_ref, w_ref, *o_refs, segs):
    h = _rms_mod(x_ref[0], g_ref[...], sc_ref[...], sh_ref[...]).astype(BF16)
    for (off, wd), o_ref in zip(segs, o_refs):
        o_ref[0] = jnp.dot(h, w_ref[:, off:off + wd], preferred_element_type=F32).astype(o_ref.dtype)


def _inproj(x, g, sc, sh, w, widths, dtypes, tm):
    G, Tg, D = x.shape
    segs, off = [], 0
    for wd in widths:
        segs.append((off, wd))
        off += wd
    return pl.pallas_call(
        functools.partial(_inproj_kernel, segs=tuple(segs)),
        grid=(G, Tg // tm),
        in_specs=[pl.BlockSpec((1, tm, D), lambda g_, t: (g_, t, 0)),
                  _pspec(g), _mod_spec(sc, tm, D), _mod_spec(sh, tm, D), _pspec(w)],
        out_specs=[pl.BlockSpec((1, tm, wd), lambda g_, t: (g_, t, 0)) for wd in widths],
        out_shape=[jax.ShapeDtypeStruct((G, Tg, wd), dt) for wd, dt in zip(widths, dtypes)],
        compiler_params=_cparams(2),
        name="in_proj",
    )(x, _parg(g), sc.arr, sh.arr, _parg(w))


def _compress_bias(w1_ref, pe_ref, bias_sc):
    for c in range(2):
        halves = []
        for h in range(2):
            pe_rows = jnp.broadcast_to(pe_ref[c, h:h + 1, :], (SUBLANES, pe_ref.shape[2]))
            halves.append(_bdot(pe_rows, w1_ref[c, :, h * 2 * LANES:(h + 1) * 2 * LANES]))
        bias_sc[c] = jnp.concatenate(halves, axis=1)


def _compress_x(xrefs, nch, w1_ref, bias_sc, b1_ref, w2_ref):
    outs = []
    for c in range(2):
        lhs = jnp.concatenate([xrefs[c][pl.ds(p, nch, stride=CMP_STRIDE), :].astype(BF16)
                               for p in range(CMP_STRIDE)], axis=1)
        outs.append(_compress_one(c, lhs, nch, w1_ref, bias_sc, b1_ref, w2_ref))
    return outs


def _compress_one(c, lhs, nch, w1_ref, bias_sc, b1_ref, w2_ref):
    last = lax.broadcasted_iota(jnp.int32, (nch, 1), 0) == nch - 1
    acc = jnp.dot(lhs, w1_ref[c], preferred_element_type=F32) + bias_sc[c, 0:1, :]
    lo = acc[:, :2 * LANES]
    hi = acc[:, 2 * LANES:]
    hi_next = jnp.where(last, 0.0, pltpu.roll(hi, nch - 1, 0))
    hid = _gelu(lo + hi_next + b1_ref[c])
    return _bdot(hid, w2_ref[c])


def _compress_prompt_kernel(krows_ref, vrows_ref, w1_ref, pe_ref, b1_ref, w2_ref, kc_ref, vc_ref, bias_sc, *, nch):
    @pl.when(pl.program_id(0) == 0)
    def _():
        _compress_bias(w1_ref, pe_ref, bias_sc)

    kc, vc = _compress_x((krows_ref.at[0], vrows_ref.at[0]), nch, w1_ref, bias_sc, b1_ref, w2_ref)
    kc_ref[0] = kc
    vc_ref[0] = vc


def _compress_prompt(rows, cw):
    B, T, _ = rows.shape
    nch = T // CMP_STRIDE
    w1, pe, b1, w2 = cw
    return pl.pallas_call(
        functools.partial(_compress_prompt_kernel, nch=nch),
        grid=(B,),
        in_specs=[pl.BlockSpec((1, T, LANES), lambda b: (b, 0, 0)),
                  pl.BlockSpec((1, T, LANES), lambda b: (b, 0, 1)),
                  _pspec(w1), _pspec(pe), _pspec(b1), _pspec(w2)],
        out_specs=[pl.BlockSpec((1, nch, LANES), lambda b: (b, 0, 0))] * 2,
        out_shape=[jax.ShapeDtypeStruct((B, nch, LANES), F32)] * 2,
        scratch_shapes=[pltpu.VMEM((2, SUBLANES, 4 * LANES), F32)],
        compiler_params=_cparams(1),
        name="nsa_compress_prompt",
    )(rows, rows, _parg(w1), _parg(pe), _parg(b1), _parg(w2))


def _rep_all(a):
    return jnp.concatenate([a] * (NSA_KV * NSA_HPG), axis=0)


def _rep_heads(a, Qb):
    return jnp.concatenate([a[:Qb]] * NSA_HPG + [a[Qb:]] * NSA_HPG, axis=0)


def _nsa_front(qblk, q0, Qb, kc, vc, n_cmp, n_sel):
    R2 = 2 * Qb
    npad = kc.shape[0]
    lane = lax.broadcasted_iota(jnp.int32, (Qb, LANES), 1)
    lo_half = lane < NSA_HD
    scale = NSA_HD ** -0.5 * LOG2E
    pieces = []
    for hd in range(NSA_KV * NSA_HPG):
        k, pair, half = hd // NSA_HPG, hd // 2, hd % 2
        sl = qblk[:, pair * LANES:(pair + 1) * LANES] * scale
        if half != k:
            sl = pltpu.roll(sl, NSA_HD, 1)
        pieces.append(jnp.where(lo_half if k == 0 else jnp.logical_not(lo_half), sl, 0.0))
    qs = jnp.concatenate(pieces, axis=0).astype(BF16)
    qp1 = q0 + lax.broadcasted_iota(jnp.int32, (Qb, 1), 0)

    n_idx = lax.broadcasted_iota(jnp.int32, (1, npad), 1)
    visible = (n_idx * CMP_STRIDE + (CMP_BLOCK - 1) <= qp1) & (n_idx < n_cmp)
    s = _bdot_nt(qs, kc) + _rep_all(jnp.where(visible, 0.0, -BIG))
    e = jnp.exp2(s - jnp.max(s, axis=-1, keepdims=True))
    any_visible = _rep_all((qp1 >= CMP_BLOCK - 1) & (n_cmp > 0))
    p_c = e * jnp.where(any_visible, 1.0 / jnp.sum(e, axis=-1, keepdims=True), 0.0)
    o_c = _bdot(p_c, vc)

    psum = []
    for k in range(NSA_KV):
        acc = p_c[(k * NSA_HPG) * Qb:(k * NSA_HPG + 1) * Qb]
        for r in range(1, NSA_HPG):
            acc = acc + p_c[(k * NSA_HPG + r) * Qb:(k * NSA_HPG + r + 1) * Qb]
        psum.append(acc)
    psum = jnp.concatenate(psum + [jnp.zeros((LANES - R2, npad), F32)], axis=0)
    nsr = -(-n_sel // SUBLANES) * SUBLANES
    sj = lax.broadcasted_iota(jnp.int32, (nsr, npad), 0) * SEL_BLOCK
    ci = lax.broadcasted_iota(jnp.int32, (nsr, npad), 1) * CMP_STRIDE
    selmap = jnp.where((ci < sj + SEL_BLOCK) & (ci + CMP_BLOCK > sj), 1.0, 0.0).astype(BF16)
    ph, plo = _split(psum)
    imp = _bdot_nt(selmap, ph) + _bdot_nt(selmap, plo)
    col_pos = q0 + (lax.broadcasted_iota(jnp.int32, (1, LANES), 1) & (Qb - 1))
    return qs, qp1, o_c, _select_blocks(imp, col_pos, n_sel)


def _select_blocks(imp, qpos, n_sel):
    nsr, ncol = imp.shape
    j = lax.broadcasted_iota(jnp.int32, (nsr, ncol), 0)
    jf = j.astype(F32)
    cur = qpos >> 6
    forced = (j == 0) | (j == cur) | (j == cur - 1)
    imp = jnp.where(forced, SEL_FORCE, imp)
    imp = jnp.where(j * SEL_BLOCK <= qpos, imp, -SEL_FORCE)
    imp = jnp.where(j < n_sel, imp, NEG)

    def pick(_, carry):
        imp_c, sel_c = carry
        m = jnp.max(imp_c, axis=0, keepdims=True)
        first = jnp.min(jnp.where(imp_c == m, jf, float(nsr)), axis=0, keepdims=True)
        hit = jf == first
        return jnp.where(hit, NEG, imp_c), jnp.where(hit, 1.0, sel_c)

    _, sel = lax.fori_loop(0, min(SEL_TOPK, n_sel), pick, (imp, jnp.zeros((nsr, ncol), F32)), unroll=True)
    return sel


def _nsa_combine(gate, o_c, o_s, o_w, Qb, o_ref):
    lo_half = lax.broadcasted_iota(jnp.int32, (Qb, LANES), 1) < NSA_HD
    g = _sigmoid(gate)
    for pair in range(NSA_KV * NSA_HPG // 2):
        halves = []
        for half, hd in enumerate((2 * pair, 2 * pair + 1)):
            k = hd // NSA_HPG
            rs = slice(hd * Qb, (hd + 1) * Qb)
            c = hd * 3
            o = g[:, c:c + 1] * o_c[rs] + g[:, c + 1:c + 2] * o_s[rs] + g[:, c + 2:c + 3] * o_w[rs]
            halves.append(o if half == k else pltpu.roll(o, NSA_HD, 1))
        o_ref[0, :, pair * LANES:(pair + 1) * LANES] = jnp.where(lo_half, halves[0], halves[1])


def _block_columns(k0, n):
    key = k0 + lax.broadcasted_iota(jnp.int32, (n, LANES), 0)
    blk = lax.broadcasted_iota(jnp.int32, (n, LANES), 1)
    return jnp.where((key >> 6) == blk, BIG, 0.0).astype(BF16)


def _block_rows(k0, n):
    key = k0 + lax.broadcasted_iota(jnp.int32, (LANES, n), 1)
    blk = lax.broadcasted_iota(jnp.int32, (LANES, n), 0)
    return jnp.where((key >> 6) == blk, BIG, 0.0).astype(BF16)


def _lanes_all(a):
    return jnp.concatenate([a] * (NSA_KV * NSA_HPG), axis=1)


def _lanes_heads(a, Qb):
    return jnp.concatenate([a[:, :Qb]] * NSA_HPG + [a[:, Qb:]] * NSA_HPG, axis=1)


def _softmax_cols(s):
    e = jnp.exp2(s - jnp.max(s, axis=0, keepdims=True))
    return e, 1.0 / jnp.sum(e, axis=0, keepdims=True)


def _nsa_prompt_kernel(q_ref, gate_ref, kc_ref, vc_ref, rows_ref, win_ref, o_ref, kaug, vt,
                       *, T, Qb, n_cmp, n_sel, wl):
    i = pl.program_id(1)
    q0 = i * Qb
    R = NSA_KV * NSA_HPG * Qb
    R2 = NSA_KV * Qb

    @pl.when(i == 0)
    def _():
        def pack(c, carry):
            r0 = pl.multiple_of(c * SEL_CHUNK, SEL_CHUNK)
            kaug[pl.ds(r0, SEL_CHUNK), 0:LANES] = rows_ref[0, pl.ds(r0, SEL_CHUNK), 2 * LANES:3 * LANES].astype(BF16)
            kaug[pl.ds(r0, SEL_CHUNK), LANES:2 * LANES] = _block_columns(r0, SEL_CHUNK)
            vt[:, pl.ds(r0, SEL_CHUNK)] = rows_ref[0, pl.ds(r0, SEL_CHUNK), 3 * LANES:4 * LANES].T.astype(BF16)
            return carry
        lax.fori_loop(0, T // SEL_CHUNK, pack, 0)

    top = lax.broadcasted_iota(jnp.int32, (LANES, Qb), 0) < NSA_HD
    scale = NSA_HD ** -0.5 * LOG2E
    qblk = q_ref[0]
    cols = []
    for hd in range(NSA_KV * NSA_HPG):
        k, pair, half = hd // NSA_HPG, hd // 2, hd % 2
        t = (qblk[:, pair * LANES:(pair + 1) * LANES] * scale).T
        if half != k:
            t = jnp.concatenate([t[NSA_HD:], t[:NSA_HD]], axis=0)
        cols.append(jnp.where(top if k == 0 else jnp.logical_not(top), t, 0.0))
    qs = jnp.concatenate(cols, axis=1).astype(BF16)
    qlane = q0 + lax.broadcasted_iota(jnp.int32, (1, Qb), 1)
    qpos = _lanes_all(qlane)

    kc = kc_ref[0]
    npad = kc.shape[0]
    n_idx = lax.broadcasted_iota(jnp.int32, (npad, 1), 0)
    visible = (n_idx * CMP_STRIDE + (CMP_BLOCK - 1) <= qlane) & (n_idx < n_cmp)
    e_c, inv_c = _softmax_cols(_bdot(kc, qs) + _lanes_all(jnp.where(visible, 0.0, -BIG)))
    any_visible = _lanes_all((qlane >= CMP_BLOCK - 1) & (n_cmp > 0))
    p_c = e_c * jnp.where(any_visible, inv_c, 0.0)
    o_c = _bdot(vc_ref[0].T, p_c)

    psum = []
    for k in range(NSA_KV):
        acc = p_c[:, (k * NSA_HPG) * Qb:(k * NSA_HPG + 1) * Qb]
        for r in range(1, NSA_HPG):
            acc = acc + p_c[:, (k * NSA_HPG + r) * Qb:(k * NSA_HPG + r + 1) * Qb]
        psum.append(acc)
    psum = jnp.concatenate(psum, axis=1)
    nsr = -(-n_sel // SUBLANES) * SUBLANES
    sj = lax.broadcasted_iota(jnp.int32, (nsr, npad), 0) * SEL_BLOCK
    ci = lax.broadcasted_iota(jnp.int32, (nsr, npad), 1) * CMP_STRIDE
    selmap = jnp.where((ci < sj + SEL_BLOCK) & (ci + CMP_BLOCK > sj), 1.0, 0.0).astype(BF16)
    ph, plo = _split(psum)
    imp = (jnp.dot(selmap, ph, preferred_element_type=F32)
           + jnp.dot(selmap, plo, preferred_element_type=F32))
    sel = _select_blocks(imp, jnp.concatenate([qlane] * NSA_KV, axis=1), n_sel)
    selm =jnp.concatenate([sel - 1.0, jnp.zeros((LANES - nsr, R2), F32)], axis=0) if nsr < LANES else sel - 1.0
    qaug = jnp.concatenate([qs, _lanes_heads(selm, Qb).astype(BF16)], axis=0)

    def update(s, vcols, carry):
        m, l, acc = carry
        m_new = jnp.maximum(m, jnp.max(s, axis=0, keepdims=True))
        alpha = jnp.exp2(m - m_new)
        p = jnp.exp2(s - m_new)
        l = alpha * l + jnp.sum(p, axis=0, keepdims=True)
        acc = alpha * acc + jnp.dot(vcols, p.astype(BF16), preferred_element_type=F32)
        return m_new, l, acc

    def scores(k0):
        return jnp.dot(kaug[pl.ds(k0, SEL_CHUNK), :], qaug, preferred_element_type=F32)

    def full_chunk(c, carry):
        k0 = pl.multiple_of(c * SEL_CHUNK, SEL_CHUNK)
        return update(scores(k0), vt[:, pl.ds(k0, SEL_CHUNK)], carry)

    n_full = q0 // SEL_CHUNK
    init = (jnp.full((1, R), -4.0 * BIG, F32), jnp.zeros((1, R), F32), jnp.zeros((LANES, R), F32))
    carry = lax.fori_loop(0, n_full, full_chunk, init)
    k0 = pl.multiple_of(n_full * SEL_CHUNK, SEL_CHUNK)
    kpos = k0 + lax.broadcasted_iota(jnp.int32, (SEL_CHUNK, 1), 0)
    s_diag = jnp.where(kpos <= qpos, scores(k0), -2.0 * BIG)
    _, l_s, acc_s = update(s_diag, vt[:, pl.ds(k0, SEL_CHUNK)], carry)
    o_s = acc_s * (1.0 / l_s)

    ws = pl.multiple_of(jnp.maximum(q0 - WINDOW, 0), LANES)
    wk = win_ref[0, pl.ds(ws, wl), 0:LANES]
    wv = win_ref[0, pl.ds(ws, wl), LANES:2 * LANES]
    dpos = qlane - (ws + lax.broadcasted_iota(jnp.int32, (wl, 1), 0))
    e_w, inv_w = _softmax_cols(_bdot(wk, qs) + _lanes_all(jnp.where((dpos >= 0) & (dpos <= WINDOW), 0.0, -BIG)))
    o_w = _bdot(wv.T, e_w) * inv_w

    g = _sigmoid(gate_ref[0]).T
    for pair in range(NSA_KV * NSA_HPG // 2):
        rows = []
        for hd in (2 * pair, 2 * pair + 1):
            k = hd // NSA_HPG
            cs = slice(hd * Qb, (hd + 1) * Qb)
            c = hd * 3
            o = g[c:c + 1, :] * o_c[:, cs] + g[c + 1:c + 2, :] * o_s[:, cs] + g[c + 2:c + 3, :] * o_w[:, cs]
            rows.append(o[k * NSA_HD:(k + 1) * NSA_HD])
        o_ref[0, :, pair * LANES:(pair + 1) * LANES] = jnp.concatenate(rows, axis=0).T.astype(o_ref.dtype)


def _nsa_prompt(nq, gate, kc, vc, rows, win):
    B, T, HD = nq.shape
    Qb = QBLOCK
    nch = kc.shape[1]
    n_sel = -(-T // SEL_BLOCK)
    wl = WINDOW + Qb
    assert T % SEL_CHUNK == 0 and T >= wl and SEL_TOPK <= n_sel <= LANES
    return pl.pallas_call(
        functools.partial(_nsa_prompt_kernel, T=T, Qb=Qb, n_cmp=nch - 1, n_sel=n_sel, wl=wl),
        grid=(B, T // Qb),
        in_specs=[pl.BlockSpec((1, Qb, HD), lambda b, i: (b, i, 0)),
                  pl.BlockSpec((1, Qb, LANES), lambda b, i: (b, i, 0)),
                  pl.BlockSpec((1, nch, LANES), lambda b, i: (b, 0, 0)),
                  pl.BlockSpec((1, nch, LANES), lambda b, i: (b, 0, 0)),
                  pl.BlockSpec((1, T, 4 * LANES), lambda b, i: (b, 0, 0)),
                  pl.BlockSpec((1, T, 2 * LANES), lambda b, i: (b, 0, 0))],
        out_specs=pl.BlockSpec((1, Qb, HD), lambda b, i: (b, i, 0)),
        out_shape=jax.ShapeDtypeStruct((B, T, HD), BF16),
        scratch_shapes=[pltpu.VMEM((T, 2 * LANES), BF16), pltpu.VMEM((LANES, T), BF16)],
        compiler_params=_cparams(2),
        name="nsa_attn_prompt",
    )(nq, gate, kc, vc, rows, win)


def _pages_copy(cache_hbm, layer, page, r0, dst, j, sem):
    n = cache_hbm.shape[-1]
    return pltpu.make_async_copy(cache_hbm.at[layer, page, pl.ds(r0, 2)],
                                 dst.at[:, :, pl.ds(pl.multiple_of(j * n, n), n)], sem)


def _pages_start(pt_ref, b, cache_hbm, layer, r0, dst, sem, npages):
    def issue(j, carry):
        _pages_copy(cache_hbm, layer, pt_ref[b, j], r0, dst, j, sem).start()
        return carry
    lax.fori_loop(0, npages, issue, 0, unroll=PAGE_UNROLL)


def _pages_wait(cache_hbm, layer, r0, dst, sem, npages):
    def wait(j, carry):
        _pages_copy(cache_hbm, layer, 0, r0, dst, j, sem).wait()
        return carry
    lax.fori_loop(0, npages, wait, 0, unroll=PAGE_UNROLL)


def _softmax2(s1, s2, mask2):
    s2 = jnp.where(mask2, s2, -2.0 * BIG)
    m = jnp.maximum(jnp.max(s1, axis=-1, keepdims=True), jnp.max(s2, axis=-1, keepdims=True))
    e1 = jnp.exp2(s1 - m)
    e2 = jnp.exp2(s2 - m)
    den = jnp.sum(e1, axis=-1, keepdims=True) + jnp.sum(e2, axis=-1, keepdims=True)
    return e1, e2, 1.0 / den


def _nsa_sample_kernel(pt_ref, q_ref, gate_ref, rows_ref, wt_ref, wnew_ref, cache_hbm,
                       w1_ref, pe_ref, b1_ref, w2_ref, o_ref, wout_ref,
                       cmpbuf, selbuf, xk, xv, kaug, vt, newbuf, wnewbuf, bias_sc, perm_sc, csem, ssem,
                       *, layer, npages, P, Q, wb, nb, n_sel):
    b = pl.program_id(0)
    nseq = pl.num_programs(0)
    page = cache_hbm.shape[-1]
    nch = P // CMP_STRIDE

    @pl.when(b == 0)
    def _():
        _pages_start(pt_ref, 0, cache_hbm, layer, 0, cmpbuf, csem, npages)
        _pages_start(pt_ref, 0, cache_hbm, layer, 2, selbuf, ssem, npages)
        _compress_bias(w1_ref, pe_ref, bias_sc)

        def blocks(c, carry):
            c0 = pl.multiple_of(c * CAST_CHUNK, CAST_CHUNK)
            kaug[LANES:2 * LANES, pl.ds(c0, CAST_CHUNK)] = _block_rows(c0, CAST_CHUNK)
            return carry
        lax.fori_loop(0, P // CAST_CHUNK, blocks, 0)
        newbuf[...] = jnp.zeros(newbuf.shape, F32)
        wnewbuf[...] = jnp.zeros(wnewbuf.shape, F32)
        src = lax.broadcasted_iota(jnp.int32, (2 * page, 2 * page), 0)
        dst_col = lax.broadcasted_iota(jnp.int32, (2 * page, 2 * page), 1)
        t = src & (page - 1)
        want = (src - t) + (t % CMP_STRIDE) * (page // CMP_STRIDE) + t // CMP_STRIDE
        perm_sc[...] = jnp.where(dst_col == want, 1.0, 0.0).astype(BF16)

    _pages_wait(cache_hbm, layer, 0, cmpbuf, csem, npages)
    per_page = page // CMP_STRIDE
    for c, dst in enumerate((xk, xv)):
        for jp in range(npages // 2):
            cols = cmpbuf[c, :, 2 * jp * page:2 * (jp + 1) * page].astype(BF16)
            regrouped = jnp.dot(cols, perm_sc[...], preferred_element_type=F32)
            for half in range(2):
                tok = regrouped[:, half * page:(half + 1) * page].T
                r0 = (2 * jp + half) * per_page
                for p in range(CMP_STRIDE):
                    dst[r0:r0 + per_page, p * LANES:(p + 1) * LANES] = tok[p * per_page:(p + 1) * per_page, :]
    kc = _compress_one(0, xk[...].astype(BF16), nch, w1_ref, bias_sc, b1_ref, w2_ref)
    vc = _compress_one(1, xv[...].astype(BF16), nch, w1_ref, bias_sc, b1_ref, w2_ref)

    @pl.when(b + 1 < nseq)
    def _():
        _pages_start(pt_ref, b + 1, cache_hbm, layer, 0, cmpbuf, csem, npages)

    _pages_wait(cache_hbm, layer, 2, selbuf, ssem, npages)

    def pack(c, carry):
        c0 = pl.multiple_of(c * CAST_CHUNK, CAST_CHUNK)
        kaug[0:LANES, pl.ds(c0, CAST_CHUNK)] = selbuf[0, :, pl.ds(c0, CAST_CHUNK)].astype(BF16)
        vt[:, pl.ds(c0, CAST_CHUNK)] = selbuf[1, :, pl.ds(c0, CAST_CHUNK)].astype(BF16)
        return carry
    lax.fori_loop(0, P // CAST_CHUNK, pack, 0)

    @pl.when(b + 1 < nseq)
    def _():
        _pages_start(pt_ref, b + 1, cache_hbm, layer, 2, selbuf, ssem, npages)

    newbuf[0:Q, :] = rows_ref[0, :, 2 * LANES:4 * LANES]
    wnewbuf[0:Q, :] = wnew_ref[0]

    qs, qp1, o_c, sel_t = _nsa_front(q_ref[0], P, Q, kc, vc, nch - 1, n_sel)
    qpos = _rep_all(qp1)
    lane = lax.broadcasted_iota(jnp.int32, (1, LANES), 1)
    new_pos = P + lane
    is_new = lane < Q

    def as_rows(blk):
        if blk.shape[0] < LANES:
            blk = jnp.concatenate([blk, jnp.zeros((LANES - blk.shape[0], LANES), F32)], axis=0)
        return blk.T[0:NSA_KV * Q]
    sel_rows = as_rows(sel_t[0:min(LANES, sel_t.shape[0])])
    sel_new = as_rows(sel_t[nb:nb + SUBLANES])[:, 0:1]

    qaug = jnp.concatenate([qs, _rep_heads(sel_rows - 1.0, Q).astype(BF16)], axis=1)
    s_past = jnp.dot(qaug, kaug[...], preferred_element_type=F32)
    s_new = _bdot_nt(qs, newbuf[:, 0:LANES])
    new_ok = is_new & (new_pos <= qpos) & (_rep_heads(sel_new, Q) > 0.5)
    e1, e2, inv = _softmax2(s_past, s_new, new_ok)
    o_s = (lax.dot_general(e1.astype(BF16), vt[...], (((1,), (1,)), ((), ())), preferred_element_type=F32)
           + _bdot(e2, newbuf[:, LANES:2 * LANES])) * inv

    dpast = qp1 - ((P - wb) + lax.broadcasted_iota(jnp.int32, (1, wb), 1))
    s_wp = (jnp.dot(qs, wt_ref[0, 0].astype(BF16), preferred_element_type=F32)
            + _rep_all(jnp.where((dpast >= 0) & (dpast <= WINDOW), 0.0, -BIG)))
    s_wn = _bdot_nt(qs, wnewbuf[:, 0:LANES])
    dnew = qpos - new_pos
    e1, e2, inv = _softmax2(s_wp, s_wn, is_new & (dnew >= 0) & (dnew <= WINDOW))
    o_w = (lax.dot_general(e1.astype(BF16), wt_ref[0, 1].astype(BF16), (((1,), (1,)), ((), ())),
                           preferred_element_type=F32)
           + _bdot(e2, wnewbuf[:, LANES:2 * LANES])) * inv

    _nsa_combine(gate_ref[0], o_c, o_s, o_w, Q, o_ref)

    for kv in range(2):
        new_cols = wnewbuf[:, kv * LANES:(kv + 1) * LANES].T[:, 0:Q]
        wout_ref[0, kv] = jnp.concatenate([wt_ref[0, kv][:, Q:], new_cols], axis=1)


def _nsa_sample(page_table, nq, gate, rows, win_t, win_new, cache_t, layer, cw):
    DB, Q, HD = nq.shape
    npages = page_table.shape[1]
    page = cache_t.shape[-1]
    P = npages * page
    wb = win_t.shape[-1]
    nb = P // SEL_BLOCK
    n_sel = -(-(P + Q) // SEL_BLOCK)
    w1, pe, b1, w2 = cw
    assert Q == SUBLANES and P % SEL_BLOCK == 0 and Q <= SEL_BLOCK and nb <= LANES and n_sel >= SEL_TOPK
    assert P % CAST_CHUNK == 0 and (P + Q) // CMP_STRIDE == P // CMP_STRIDE and page == LANES
    assert npages % 2 == 0 and page // CMP_STRIDE == SUBLANES and wb >= Q
    bs = lambda shape: pl.BlockSpec((1,) + shape, lambda b, pt: (b,) + (0,) * len(shape))
    win_spec = pl.BlockSpec((None, 1, 2, LANES, wb), lambda b, pt: (layer, b, 0, 0, 0))
    grid_spec = pltpu.PrefetchScalarGridSpec(
        num_scalar_prefetch=1, grid=(DB,),
        in_specs=[bs((Q, HD)), bs((Q, LANES)), bs((Q, 4 * LANES)), win_spec, bs((Q, 2 * LANES)),
                  pl.BlockSpec(memory_space=pl.ANY),
                  _pspec(w1), _pspec(pe), _pspec(b1), _pspec(w2)],
        out_specs=[bs((Q, HD)), bs((2, LANES, wb))],
        scratch_shapes=[pltpu.VMEM((2, LANES, P), F32), pltpu.VMEM((2, LANES, P), F32),
                        pltpu.VMEM((P // CMP_STRIDE, CMP_STRIDE * LANES), F32),
                        pltpu.VMEM((P // CMP_STRIDE, CMP_STRIDE * LANES), F32),
                        pltpu.VMEM((2 * LANES, P), BF16), pltpu.VMEM((LANES, P), BF16),
                        pltpu.VMEM((LANES, 2 * LANES), F32), pltpu.VMEM((LANES, 2 * LANES), F32),
                        pltpu.VMEM((2, SUBLANES, 4 * LANES), F32), pltpu.VMEM((2 * page, 2 * page), BF16),
                        pltpu.SemaphoreType.DMA(()), pltpu.SemaphoreType.DMA(())])
    return pl.pallas_call(
        functools.partial(_nsa_sample_kernel, layer=layer, npages=npages, P=P, Q=Q, wb=wb, nb=nb, n_sel=n_sel),
        grid_spec=grid_spec,
        out_shape=[jax.ShapeDtypeStruct((DB, Q, HD), F32), jax.ShapeDtypeStruct((DB, 2, LANES, wb), F32)],
        compiler_params=_cparams(1),
        name="nsa_sample",
    )(page_table, nq, gate, rows, win_t, win_new, cache_t, _parg(w1), _parg(pe), _parg(b1), _parg(w2))


def _ret_kernel(q_ref, k_ref, v_ref, g_ref, cos_ref, sin_ref, s0_ref, gn_ref, o_ref, snew_ref, s_sc,
                *, C, nC, nseq):
    c = pl.program_id(1)

    @pl.when(c == 0)
    def _():
        s_sc[...] = s0_ref[...]

    cosf = cos_ref[...]
    sinf = sin_ref[...]
    diff = (lax.broadcasted_iota(jnp.int32, (C, C), 0) - lax.broadcasted_iota(jnp.int32, (C, C), 1)).astype(F32)
    ii = lax.broadcasted_iota(jnp.int32, (C, 1), 0).astype(F32)
    half = RET_DK // 2
    for h in range(RET_HEADS):
        lg = math.log(1.0 - 2.0 ** (-5.0 - h))
        hs = slice(h * RET_DK, (h + 1) * RET_DK)
        decay = jnp.where(diff >= 0, jnp.exp(jnp.maximum(diff, 0.0) * lg), 0.0)
        cross = jnp.exp((ii + 1.0) * lg)
        kweight = jnp.exp((C - 1.0 - ii) * lg)
        for b in range(nseq):
            q = q_ref[b, :, hs]
            k = k_ref[b, :, hs]
            v = v_ref[b, :, hs]
            qr = q * cosf + pltpu.roll(q, half, 1) * sinf
            kr = (k * cosf + pltpu.roll(k, half, 1) * sinf) * (RET_DK ** -0.5)
            o_inner = _bdot(_bdot_nt(qr, kr) * decay, v)
            s_old = s_sc[b, h]
            o_cross = _bdot(qr, s_old) * cross
            kv = lax.dot_general((kr * kweight).astype(BF16), v.astype(BF16), (((0,), (0,)), ((), ())),
                                 preferred_element_type=F32)
            s_sc[b, h] = math.exp(C * lg) * s_old + kv
            o = o_inner + o_cross
            mu = jnp.mean(o, axis=-1, keepdims=True)
            var = jnp.mean(jnp.square(o - mu), axis=-1, keepdims=True)
            gate = g_ref[b, :, hs]
            o_ref[b, :, hs] = (((o - mu) * lax.rsqrt(var + EPS)) * gn_ref[:, hs]
                               * (gate * _sigmoid(gate))).astype(o_ref.dtype)

    @pl.when(c == nC - 1)
    def _():
        snew_ref[...] = s_sc[...]


def _retention(rq, rk, rv, rg, cosf, sinf, s0, gn, nseq, out_dtype):
    B, T, W = rq.shape
    C = RET_CHUNK if (T >= RET_CHUNK and T % RET_CHUNK == 0) else T
    nC = T // C
    tok = pl.BlockSpec((nseq, C, W), lambda b, c: (b, c, 0))
    tab = pl.BlockSpec((C, RET_DK), lambda b, c: (c, 0))
    st = pl.BlockSpec((nseq,) + s0.shape[1:], lambda b, c: (b, 0, 0, 0))
    return pl.pallas_call(
        functools.partial(_ret_kernel, C=C, nC=nC, nseq=nseq),
        grid=(B // nseq, nC),
        in_specs=[tok, tok, tok, tok, tab, tab, st, _pspec(gn)],
        out_specs=[tok, st],
        out_shape=[jax.ShapeDtypeStruct((B, T, W), out_dtype), jax.ShapeDtypeStruct(s0.shape, F32)],
        scratch_shapes=[pltpu.VMEM((nseq,) + s0.shape[1:], F32)],
        compiler_params=_cparams(2),
        name="retention",
    )(rq, rk, rv, rg, cosf, sinf, s0, _parg(gn))


def _shift_carry(x, k, tail8):
    r = pltpu.roll(x, k, 0)
    row8 = lax.broadcasted_iota(jnp.int32, (SUBLANES, 1), 0)
    first = jnp.where(row8 >= k, r[:SUBLANES], pltpu.roll(tail8, k, 0))
    return jnp.concatenate([first, r[SUBLANES:]], axis=0)


def _shift_seg(x, k, fill, tpos):
    return jnp.where(tpos >= k, pltpu.roll(x, k, 0), fill)


def _rglru_kernel(x_ref, gate_ref, st_ref, h0_ref, cw_ref, cb_ref, wa_ref, ba_ref, wx_ref, bx_ref, lam_ref,
                  o_ref, h_ref, tail_sc, h_sc, *, tm, seg):
    carry = seg == 0
    x = x_ref[0]
    rows = lax.broadcasted_iota(jnp.int32, (tm, 1), 0)
    if carry:
        @pl.when(pl.program_id(1) == 0)
        def _():
            tail_sc[...] = st_ref[0]
            h_sc[...] = h0_ref[0]
        tail8 = tail_sc[...]
        shifted = [_shift_carry(x, k, tail8) for k in range(1, RG_CONV)]
        tpos = rows & (SUBLANES - 1)
    else:
        tpos = rows & (seg - 1)
        shifted = [_shift_seg(x, k, st_ref[k - 1], tpos) for k in range(1, RG_CONV)]
    xc = cb_ref[...] + cw_ref[RG_CONV - 1:RG_CONV, :] * x
    for k in range(1, RG_CONV):
        xc = xc + cw_ref[RG_CONV - 1 - k:RG_CONV - k, :] * shifted[k - 1]
    r = _sigmoid(_bdot(xc, wa_ref[...]) + ba_ref[...])
    i = _sigmoid(_bdot(xc, wx_ref[...]) + bx_ref[...])
    lam = lam_ref[...]
    softplus = jnp.maximum(-lam, 0.0) + jnp.log(1.0 + jnp.exp(-jnp.abs(lam)))
    log_a = (-RG_C * r) * softplus
    a = jnp.exp(log_a)
    gap = 1.0 - a * a
    bt = jnp.where(gap > 0, gap * lax.rsqrt(gap), 0.0) * (i * xc)
    if not carry:
        bt = bt + a * h0_ref[0]
    k = 1
    while k < SUBLANES:
        ok = tpos >= k
        a_prev = jnp.where(ok, pltpu.roll(a, k, 0), 1.0)
        b_prev = jnp.where(ok, pltpu.roll(bt, k, 0), 0.0)
        bt = a * b_prev + bt
        a = a * a_prev
        k *= 2
    if carry:
        h_prev = h_sc[SUBLANES - 1:SUBLANES, :]
        groups = []
        for g0 in range(0, tm, SUBLANES):
            h_g = bt[g0:g0 + SUBLANES] + a[g0:g0 + SUBLANES] * h_prev
            groups.append(h_g)
            h_prev = h_g[SUBLANES - 1:SUBLANES, :]
        bt = jnp.concatenate(groups, axis=0)
    o_ref[0] = (bt * _gelu(gate_ref[0].astype(F32))).astype(o_ref.dtype)
    if carry:
        tail_sc[...] = x[tm - SUBLANES:]
        h_sc[...] = bt[tm - SUBLANES:]
        h_ref[0] = bt[tm - SUBLANES:]
    else:
        h_ref[0] = bt


def _rglru(rx, rgate, st, h0, rw, tm, seg):
    G, Tg, W = rx.shape
    tok = pl.BlockSpec((1, tm, W), lambda g, t: (g, t, 0))
    if seg == 0:
        st_spec = pl.BlockSpec((1, SUBLANES, W), lambda g, t: (g, 0, 0))
        h0_spec = pl.BlockSpec((1, SUBLANES, W), lambda g, t: (g, 0, 0))
        h_spec = pl.BlockSpec((1, SUBLANES, W), lambda g, t: (g, 0, 0))
        h_shape = (G, SUBLANES, W)
    else:
        st_spec = pl.BlockSpec(st.shape, lambda g, t: (0, 0, 0))
        h0_spec = tok
        h_spec = tok
        h_shape = (G, Tg, W)
    return pl.pallas_call(
        functools.partial(_rglru_kernel, tm=tm, seg=seg),
        grid=(G, Tg // tm),
        in_specs=[tok, tok, st_spec, h0_spec] + [_pspec(a) for a in rw],
        out_specs=[tok, h_spec],
        out_shape=[jax.ShapeDtypeStruct((G, Tg, W), rgate.dtype), jax.ShapeDtypeStruct(h_shape, F32)],
        scratch_shapes=[pltpu.VMEM((SUBLANES, W), F32), pltpu.VMEM((SUBLANES, W), F32)],
        compiler_params=_cparams(2),
        name="rglru",
    )(rx, rgate, st, h0, *[_parg(a) for a in rw])


def _mix_ffn_kernel(x_ref, oa_ref, or_ref, oc_ref, mg_ref, gt1_ref, wa_ref, wb_ref, wc_ref, wo_ref,
                    g_ref, sc_ref, sh_ref, gt_ref, st_ref, wup_ref, cw_ref, cb_ref, wdn_ref, fg_ref,
                    y_ref, fnew_ref, tail_sc, *, tm, seg, F, chunks, final):
    D = x_ref.shape[2]
    pa = _bdot(oa_ref[0], wa_ref[...])
    pb = _bdot(or_ref[0], wb_ref[...])
    pc = _bdot(oc_ref[0], wc_ref[...])
    gate = lambda i: _sigmoid(mg_ref[0, :, i * D:(i + 1) * D].astype(F32))
    merged = gate(0) * pa + gate(1) * pb + gate(2) * pc
    x = x_ref[0] + gt1_ref[...] * _bdot(merged, wo_ref[...])

    carry = seg == 0
    h = _rms_mod(x, g_ref[...], sc_ref[...], sh_ref[...]).astype(BF16)
    rows = lax.broadcasted_iota(jnp.int32, (tm, 1), 0)
    if carry:
        @pl.when(pl.program_id(1) == 0)
        def _():
            tail_sc[...] = st_ref[0]
    else:
        tpos = rows & (seg - 1)
    acc = jnp.zeros(x.shape, F32)
    for c0, wck in chunks:
        cs = slice(c0, c0 + wck)
        gp = jnp.dot(h, wup_ref[:, c0:c0 + wck], preferred_element_type=F32)
        val = jnp.dot(h, wup_ref[:, F + c0:F + c0 + wck], preferred_element_type=F32)
        if carry:
            tail8 = tail_sc[:, cs]
            shifted = [_shift_carry(gp, k, tail8) for k in range(1, FFN_CONV)]
            tail_sc[:, cs] = gp[tm - SUBLANES:]
            fnew_ref[0, :, cs] = gp[tm - SUBLANES:]
        else:
            shifted = [_shift_seg(gp, k, st_ref[k - 1, :, cs], tpos) for k in range(1, FFN_CONV)]
            fnew_ref[0, :, cs] = gp
        gc = cb_ref[:, cs] + cw_ref[FFN_CONV - 1:FFN_CONV, cs] * gp
        for k in range(1, FFN_CONV):
            gc = gc + cw_ref[FFN_CONV - 1 - k:FFN_CONV - k, cs] * shifted[k - 1]
        act = (gc * _sigmoid(gc)) * val
        acc = acc + _bdot(act, wdn_ref[cs, :])
    y = x + gt_ref[...] * acc
    if final:
        y = (y * lax.rsqrt(jnp.mean(y * y, axis=-1, keepdims=True) + EPS)) * fg_ref[...]
    y_ref[0] = y


def _ffn_chunks(F):
    half = -(-(F // 2) // MXU_DEPTH_V7X) * MXU_DEPTH_V7X
    return ((0, half), (half, F - half)) if 0 < half < F else ((0, F),)


def _mix_ffn(x, oa, orr, oc, mg, gt1, wa, wb, wc, wo, g, sc, sh, gt, st, wup, cw, cb, wdn, fg, tm, seg, final):
    G, Tg, D = x.shape
    F = wdn.shape[0]
    tokw = lambda w: pl.BlockSpec((1, tm, w), lambda g_, t: (g_, t, 0))
    tok = tokw(D)
    if seg == 0:
        st_spec = pl.BlockSpec((1, SUBLANES, F), lambda g_, t: (g_, 0, 0))
        fn_spec = pl.BlockSpec((1, SUBLANES, F), lambda g_, t: (g_, 0, 0))
        fn_shape = (G, SUBLANES, F)
    else:
        st_spec = pl.BlockSpec(st.shape, lambda g_, t: (0, 0, 0))
        fn_spec = pl.BlockSpec((1, tm, F), lambda g_, t: (g_, t, 0))
        fn_shape = (G, Tg, F)
    return pl.pallas_call(
        functools.partial(_mix_ffn_kernel, tm=tm, seg=seg, F=F, chunks=_ffn_chunks(F), final=final),
        grid=(G, Tg // tm),
        in_specs=[tok, tokw(oa.shape[2]), tokw(orr.shape[2]), tokw(oc.shape[2]), tokw(3 * D),
                  _mod_spec(gt1, tm, D), _pspec(wa), _pspec(wb), _pspec(wc), _pspec(wo),
                  _pspec(g), _mod_spec(sc, tm, D), _mod_spec(sh, tm, D), _mod_spec(gt, tm, D),
                  st_spec, _pspec(wup), _pspec(cw), _pspec(cb), _pspec(wdn), _pspec(fg)],
        out_specs=[tok, fn_spec],
        out_shape=[jax.ShapeDtypeStruct((G, Tg, D), F32), jax.ShapeDtypeStruct(fn_shape, F32)],
        scratch_shapes=[pltpu.VMEM((SUBLANES, F), F32)],
        compiler_params=_cparams(2),
        name="mix_ffn",
    )(x, oa, orr, oc, mg, gt1.arr, _parg(wa), _parg(wb), _parg(wc), _parg(wo),
      _parg(g), sc.arr, sh.arr, gt.arr, st, _parg(wup), _parg(cw), _parg(cb), _parg(wdn), fg)


def _block_diag(w):
    n, a, b = w.shape[-3:]
    eye = jnp.eye(n, dtype=w.dtype)
    out = jnp.einsum('ij,...iab->...iajb', eye, w)
    return out.reshape(w.shape[:-3] + (n * a, n * b))


def _seg_fill(buf, k, seg):
    B, nb, C = buf.shape
    part = jnp.concatenate([buf[:, nb - k:, :], jnp.zeros((B, seg - k, C), buf.dtype)], axis=1)
    return part.reshape(B * seg, C)


def kernel(x_prompt, x_sample, cache_nsa, cache_nsa_win, state_ret, state_rglru_h, state_rglru_conv,
           state_ffn_conv, page_table, c_prompt, c_sample, norm1_g, norm2_g, w_ada, b_ada, w_in, cmp_pe,
           cmp_w1, cmp_b1, cmp_w2, ret_gn_g, rg_conv_w, rg_conv_b, rg_w_a, rg_b_a, rg_w_x, rg_b_x, rg_lambda,
           w_br_a, w_br_b, w_br_c, w_out, ffn_w_up, ffn_conv_w, ffn_conv_b, ffn_w_down, final_norm_g):
    B, T, D = x_prompt.shape
    DB, Q, _ = x_sample.shape
    L = w_in.shape[0]
    npages = page_table.shape[1]
    page = cache_nsa.shape[2]
    P = npages * page
    NQ = NSA_KV * NSA_HPG * NSA_HD
    NKV = NSA_KV * NSA_HD
    RW = RET_HEADS * RET_DK
    W = rg_conv_w.shape[2]
    F = ffn_w_down.shape[1]
    NS = DB * Q
    wbuf_len = cache_nsa_win.shape[2]
    assert Q == SUBLANES and T >= RG_CONV and P % CMP_STRIDE == 0

    mod = _ada(jnp.concatenate([c_prompt, c_sample], axis=0), w_ada, b_ada)
    cache_t = jnp.transpose(cache_nsa, (0, 1, 3, 4, 5, 2)).reshape(L, cache_nsa.shape[1], 4, NKV, page)
    win_t_all = jnp.transpose(cache_nsa_win, (0, 1, 3, 4, 5, 2)).reshape(L, DB, 2, NKV, wbuf_len)

    half = RET_DK // 2
    freq = ROPE_BASE ** (-jnp.arange(half, dtype=F32) / half)

    def rope_tables(pos):
        ang = pos.astype(F32)[:, None] * freq[None, :]
        cos, sin = jnp.cos(ang), jnp.sin(ang)
        return jnp.concatenate([cos, cos], axis=1), jnp.concatenate([-sin, sin], axis=1)

    cos_p, sin_p = rope_tables(jnp.arange(T, dtype=jnp.int32))
    cos_s, sin_s = rope_tables(P + jnp.arange(Q, dtype=jnp.int32))

    widths = (NQ, 4 * NKV, 2 * NKV, LANES, RW, RW, RW, RW, W, W, 3 * D)
    dt_prompt = (F32,) * 9 + (BF16, BF16)
    dt_sample = (F32,) * 11
    offs = np.cumsum((0, NQ, 6 * NKV, 3 * NSA_KV * NSA_HPG, RW, RW, RW, RW, W, W, 3 * D))
    ngate = 3 * NSA_KV * NSA_HPG

    xp = x_prompt
    xs = x_sample.reshape(1, NS, D)
    outs_p = [[] for _ in range(6)]
    outs_s = [[] for _ in range(6)]
    tm_p = 256 if T % 256 == 0 else T

    wi = w_in.astype(BF16)
    w_cat_all = jnp.concatenate([
        wi[:, :, offs[0]:offs[2]],
        jnp.pad(wi[:, :, offs[2]:offs[3]], ((0, 0), (0, 0), (0, LANES - ngate))),
        wi[:, :, offs[3]:]], axis=2)
    grouped = lambda w: _block_diag(jnp.broadcast_to(w[..., None, :, :], w.shape[:-2] + (NSA_KV,) + w.shape[-2:]))
    cw1_all = jnp.concatenate([grouped(cmp_w1[:, :, :CMP_STRIDE]), grouped(cmp_w1[:, :, CMP_STRIDE:])], axis=-1)
    cw1_all = cw1_all.reshape(L, 2, CMP_STRIDE * NKV, 4 * LANES).astype(BF16)
    cpe_all = jnp.tile(cmp_pe, (1, 1, 1, NSA_KV)).reshape(L, 2, 2, CMP_STRIDE * NKV)
    cb1_all = jnp.tile(cmp_b1, (1, 1, NSA_KV))[:, :, None, :]
    cw2_all = grouped(cmp_w2).astype(BF16)
    row = lambda a: a[:, None, :]
    rw_all = (rg_conv_w, row(rg_conv_b), _block_diag(rg_w_a).astype(BF16), row(rg_b_a),
              _block_diag(rg_w_x).astype(BF16), row(rg_b_x), row(rg_lambda))
    wa_all = w_br_a.astype(BF16)
    wb_all = w_br_b.astype(BF16)
    wc_all = w_br_c.astype(BF16)
    wo_all = w_out.astype(BF16)
    wup_all = ffn_w_up.astype(BF16)
    wdn_all = ffn_w_down.astype(BF16)
    g1_all, g2_all, gn_all, fcb_all = row(norm1_g), row(norm2_g), row(ret_gn_g), row(ffn_conv_b)
    fg = final_norm_g[None]
    mod_p = mod[:, :B].reshape(L, B, 1, 6 * D)
    mod_s = jnp.repeat(mod[:, B:], Q, axis=1)

    for l in range(L):
        lay = lambda a: _LayerOf(a, l)
        w_cat = lay(w_cat_all)
        cw = (lay(cw1_all), lay(cpe_all), lay(cb1_all), lay(cw2_all))
        rw = tuple(lay(a) for a in rw_all)
        wa, wb, wc, wo, wup, wdn = (lay(a) for a in (wa_all, wb_all, wc_all, wo_all, wup_all, wdn_all))
        g1, g2, gn, fcw, fcb = (lay(a) for a in (g1_all, g2_all, gn_all, ffn_conv_w, fcb_all))
        final = l == L - 1

        m = [_ModOf(mod_p, l, i) for i in range(6)]
        (nq, rows, win, gate, rq, rk, rv, rg, rx, rgate, mg) = _inproj(xp, g1, m[1], m[0], w_cat, widths,
                                                                        dt_prompt, tm_p)
        kc, vc = _compress_prompt(rows, cw)
        o_a = _nsa_prompt(nq, gate, kc, vc, rows, win)
        o_r, s_new = _retention(rq, rk, rv, rg, cos_p, sin_p,
                                jnp.zeros((B, RET_HEADS, RET_DK, RET_DK), F32), gn, B, BF16)
        zs = jnp.zeros((B, SUBLANES, W), F32)
        o_c, h_tail = _rglru(rx, rgate, zs, zs, rw, tm_p, 0)
        xp, f_tail = _mix_ffn(xp, o_a, o_r, o_c, mg, m[2], wa, wb, wc, wo, g2, m[4], m[3], m[5],
                              jnp.zeros((B, SUBLANES, F), F32), wup, fcw, fcb, wdn, fg, tm_p, 0, final)
        wn = min(WINDOW, T)
        outs_p[0].append(rows.reshape(B, T, 4, NSA_KV, NSA_HD))
        outs_p[1].append(win[:, T - wn:].reshape(B, wn, 2, NSA_KV, NSA_HD))
        outs_p[2].append(s_new)
        outs_p[3].append(h_tail[:, SUBLANES - 1])
        outs_p[4].append(rx[:, T - (RG_CONV - 1):])
        outs_p[5].append(f_tail[:, SUBLANES - (FFN_CONV - 1):])

        ms = [_ModOf(mod_s, l, i) for i in range(6)]
        (nq, rows, win, gate, rq, rk, rv, rg, rx, rgate, mg) = _inproj(xs, g1, ms[1], ms[0], w_cat, widths,
                                                                        dt_sample, NS)
        r3 = lambda a: a.reshape(DB, Q, a.shape[-1])
        o_a, win_next = _nsa_sample(page_table, r3(nq), r3(gate), r3(rows), win_t_all, r3(win), cache_t, l, cw)
        o_r, s_new = _retention(r3(rq), r3(rk), r3(rv), r3(rg), cos_s, sin_s, state_ret[l].astype(F32), gn,
                                math.gcd(DB, SUBLANES), F32)
        cbuf = state_rglru_conv[l]
        st = jnp.stack([_seg_fill(cbuf, k, Q) for k in range(1, RG_CONV)])
        h0 = jnp.pad(state_rglru_h[l].astype(F32)[:, None, :], ((0, 0), (0, Q - 1), (0, 0))).reshape(1, NS, W)
        o_c, h_all = _rglru(rx, rgate, st, h0, rw, NS, Q)
        fbuf = state_ffn_conv[l]
        fst = jnp.stack([_seg_fill(fbuf, k, Q) for k in range(1, FFN_CONV)])
        xs, g_all = _mix_ffn(xs, o_a.reshape(1, NS, NQ), o_r.reshape(1, NS, RW), o_c, mg, ms[2], wa, wb, wc, wo,
                             g2, ms[4], ms[3], ms[5], fst, wup, fcw, fcb, wdn, fg, NS, Q, final)
        outs_s[0].append(rows.reshape(DB, Q, 4, NSA_KV, NSA_HD))
        outs_s[1].append(win_next)
        outs_s[2].append(s_new)
        outs_s[3].append(h_all.reshape(DB, Q, W)[:, Q - 1])
        outs_s[4].append(jnp.concatenate([cbuf, rx.reshape(DB, Q, W)], axis=1)[:, Q:])
        outs_s[5].append(jnp.concatenate([fbuf, g_all.reshape(DB, Q, F)], axis=1)[:, Q:])

    sp = [jnp.stack(a) for a in outs_p]
    ss = [jnp.stack(a) for a in outs_s]
    ss[1] = jnp.transpose(ss[1].reshape(L, DB, 2, NSA_KV, NSA_HD, wbuf_len), (0, 1, 5, 2, 3, 4))
    return (xp, xs.reshape(DB, Q, D), sp[0], ss[0], sp[1], ss[1], sp[2], ss[2],
            sp[3], ss[3], sp[4], ss[4], sp[5], ss[5])
```

```python
import functools
import math

import numpy as np
import jax
import jax.numpy as jnp
from jax import lax
from jax.experimental import pallas as pl
from jax.experimental.pallas import tpu as pltpu

F32 = jnp.float32
BF16 = jnp.bfloat16

NSA_KV = 2
NSA_HPG = 4
NSA_HD = 64
CMP_STRIDE = 16
CMP_BLOCK = 32
SEL_BLOCK = 64
SEL_TOPK = 16
SEL_FORCE = 1e4
WINDOW = 512
QBLOCK = 256
RET_HEADS = 4
RET_DK = 128
RET_CHUNK = 128
ROPE_BASE = 10000.0
RG_CONV = 4
RG_C = 8.0
FFN_CONV = 3
EPS = 1e-6

NEG = -1e30
BIG = float(2 ** 60)
LOG2E = 1.4426950408889634
SUBLANES = 8
LANES = 128
VMEM_LIMIT_V7X = 56 * 1024 * 1024
SEL_CHUNK = 512
CAST_CHUNK = 1024
PAGE_UNROLL = 8
MXU_DEPTH_V7X = 256


def _cparams(n_grid):
    return pltpu.CompilerParams(dimension_semantics=("arbitrary",) * n_grid,
                                vmem_limit_bytes=VMEM_LIMIT_V7X)


def _bdot(a, b):
    return jnp.dot(a.astype(BF16), b.astype(BF16), preferred_element_type=F32)


def _bdot_nt(a, b):
    return lax.dot_general(a.astype(BF16), b.astype(BF16), (((1,), (1,)), ((), ())),
                           preferred_element_type=F32)


def _split(a):
    hi = a.astype(BF16)
    lo = (a - hi.astype(F32)).astype(BF16)
    return hi, lo


def _dot3(a, b):
    ah, al = _split(a)
    bh, bl = _split(b)
    d = functools.partial(jnp.dot, preferred_element_type=F32)
    return d(ah, bh) + d(al, bh) + d(ah, bl)


def _sigmoid(x):
    return 0.5 * jnp.tanh(0.5 * x) + 0.5


def _gelu(x):
    return 0.5 * x * (1.0 + jnp.tanh(0.7978845608028654 * (x + 0.044715 * (x * x * x))))


def _rms_mod(x, g, sc, sh):
    y = x * lax.rsqrt(jnp.mean(x * x, axis=-1, keepdims=True) + EPS)
    return (y * g) * (1.0 + sc) + sh


def _ada_kernel(c_ref, w_ref, b_ref, o_ref):
    c = c_ref[...]
    o_ref[0] = _dot3(c * _sigmoid(c), w_ref[0]) + b_ref[0]


def _ada(c_all, w_ada, b_ada):
    L, D, E = w_ada.shape
    n = c_all.shape[0]
    tn = 1536 if E % 1536 == 0 else E
    return pl.pallas_call(
        _ada_kernel,
        grid=(L, E // tn),
        in_specs=[pl.BlockSpec((n, D), lambda l, j: (0, 0)),
                  pl.BlockSpec((1, D, tn), lambda l, j: (l, 0, j)),
                  pl.BlockSpec((1, 1, tn), lambda l, j: (l, 0, j))],
        out_specs=pl.BlockSpec((1, n, tn), lambda l, j: (l, 0, j)),
        out_shape=jax.ShapeDtypeStruct((L, n, E), F32),
        compiler_params=_cparams(2),
        name="ada_mod",
    )(c_all, w_ada, b_ada.reshape(L, 1, E))


class _LayerOf:
    def __init__(self, arr, layer):
        self.arr, self.layer, self.shape = arr, layer, arr.shape[1:]


class _ModOf:
    def __init__(self, arr, layer, idx):
        self.arr, self.layer, self.idx = arr, layer, idx


def _mod_spec(m, tm, d):
    l, i = m.layer, m.idx
    if m.arr.ndim == 4:
        return pl.BlockSpec((None, None, 1, d), lambda g, t: (l, g, 0, i))
    return pl.BlockSpec((None, tm, d), lambda g, t: (l, t, i))


def _pspec(p):
    if isinstance(p, _LayerOf):
        nd, l = len(p.shape), p.layer
        return pl.BlockSpec((None,) + tuple(p.shape), lambda *a: (l,) + (0,) * nd, pipeline_mode=pl.Buffered(1))
    nd = p.ndim
    return pl.BlockSpec(p.shape, lambda *a: (0,) * nd, pipeline_mode=pl.Buffered(1))


def _parg(p):
    return p.arr if isinstance(p, _LayerOf) else p


def _inproj_kernel(x_ref, g_ref, sc_ref, sh_ref, wh_ref, wt_ref, *o_refs, segs):
    h = _rms_mod(x_ref[0], g_ref[...], sc_ref[...], sh_ref[...]).astype(BF16)
    for (part, off, wd), o_ref in zip(segs, o_refs):
        w_ref = wt_ref if part else wh_ref
        o_ref[0] = jnp.dot(h, w_ref[:, off:off + wd], preferred_element_type=F32).astype(o_ref.dtype)


def _inproj(x, g, sc, sh, w_head, w_tail, widths, dtypes, tm):
    G, Tg, D = x.shape
    segs, off, part = [], 0, 0
    for wd in widths:
        if part == 0 and off == w_head.shape[1]:
            part, off = 1, 0
        segs.append((part, off, wd))
        off += wd
    assert part == 1 and off == w_tail.shape[1]
    return pl.pallas_call(
        functools.partial(_inproj_kernel, segs=tuple(segs)),
        grid=(G, Tg // tm),
        in_specs=[pl.BlockSpec((1, tm, D), lambda g_, t: (g_, t, 0)),
                  _pspec(g), _mod_spec(sc, tm, D), _mod_spec(sh, tm, D), _pspec(w_head), _pspec(w_tail)],
        out_specs=[pl.BlockSpec((1, tm, wd), lambda g_, t: (g_, t, 0)) for wd in widths],
        out_shape=[jax.ShapeDtypeStruct((G, Tg, wd), dt) for wd, dt in zip(widths, dtypes)],
        compiler_params=_cparams(2),
        name="in_proj",
    )(x, _parg(g), sc.arr, sh.arr, _parg(w_head), _parg(w_tail))


def _compress_bias(w1_ref, pe_ref, bias_sc):
    for c in range(2):
        halves = []
        for h in range(2):
            pe_rows = jnp.broadcast_to(pe_ref[c, h:h + 1, :], (SUBLANES, pe_ref.shape[2]))
            halves.append(_bdot(pe_rows, w1_ref[c, :, h * 2 * LANES:(h + 1) * 2 * LANES]))
        bias_sc[c] = jnp.concatenate(halves, axis=1)


def _compress_x(xrefs, nch, w1_ref, bias_sc, b1_ref, w2_ref):
    outs = []
    for c in range(2):
        lhs = jnp.concatenate([xrefs[c][pl.ds(p, nch, stride=CMP_STRIDE), :].astype(BF16)
                               for p in range(CMP_STRIDE)], axis=1)
        outs.append(_compress_one(c, lhs, nch, w1_ref, bias_sc, b1_ref, w2_ref))
    return outs


def _compress_one(c, lhs, nch, w1_ref, bias_sc, b1_ref, w2_ref):
    last = lax.broadcasted_iota(jnp.int32, (nch, 1), 0) == nch - 1
    acc = jnp.dot(lhs, w1_ref[c], preferred_element_type=F32) + bias_sc[c, 0:1, :]
    lo = acc[:, :2 * LANES]
    hi = acc[:, 2 * LANES:]
    hi_next = jnp.where(last, 0.0, pltpu.roll(hi, nch - 1, 0))
    hid = _gelu(lo + hi_next + b1_ref[c])
    return _bdot(hid, w2_ref[c])


def _compress_prompt_kernel(krows_ref, vrows_ref, w1_ref, pe_ref, b1_ref, w2_ref, kc_ref, vc_ref, bias_sc, *, nch):
    @pl.when(pl.program_id(0) == 0)
    def _():
        _compress_bias(w1_ref, pe_ref, bias_sc)

    kc, vc = _compress_x((krows_ref.at[0], vrows_ref.at[0]), nch, w1_ref, bias_sc, b1_ref, w2_ref)
    kc_ref[0] = kc
    vc_ref[0] = vc


def _compress_prompt(rows, cw):
    B, T, _ = rows.shape
    nch = T // CMP_STRIDE
    w1, pe, b1, w2 = cw
    return pl.pallas_call(
        functools.partial(_compress_prompt_kernel, nch=nch),
        grid=(B,),
        in_specs=[pl.BlockSpec((1, T, LANES), lambda b: (b, 0, 0)),
                  pl.BlockSpec((1, T, LANES), lambda b: (b, 0, 1)),
                  _pspec(w1), _pspec(pe), _pspec(b1), _pspec(w2)],
        out_specs=[pl.BlockSpec((1, nch, LANES), lambda b: (b, 0, 0))] * 2,
        out_shape=[jax.ShapeDtypeStruct((B, nch, LANES), F32)] * 2,
        scratch_shapes=[pltpu.VMEM((2, SUBLANES, 4 * LANES), F32)],
        compiler_params=_cparams(1),
        name="nsa_compress_prompt",
    )(rows, rows, _parg(w1), _parg(pe), _parg(b1), _parg(w2))


def _rep_all(a):
    return jnp.concatenate([a] * (NSA_KV * NSA_HPG), axis=0)


def _rep_heads(a, Qb):
    return jnp.concatenate([a[:Qb]] * NSA_HPG + [a[Qb:]] * NSA_HPG, axis=0)


def _nsa_front(qblk, q0, Qb, kc, vc, n_cmp, n_sel):
    R2 = 2 * Qb
    npad = kc.shape[0]
    lane = lax.broadcasted_iota(jnp.int32, (Qb, LANES), 1)
    lo_half = lane < NSA_HD
    scale = NSA_HD ** -0.5 * LOG2E
    pieces = []
    for hd in range(NSA_KV * NSA_HPG):
        k, pair, half = hd // NSA_HPG, hd // 2, hd % 2
        sl = qblk[:, pair * LANES:(pair + 1) * LANES] * scale
        if half != k:
            sl = pltpu.roll(sl, NSA_HD, 1)
        pieces.append(jnp.where(lo_half if k == 0 else jnp.logical_not(lo_half), sl, 0.0))
    qs = jnp.concatenate(pieces, axis=0).astype(BF16)
    qp1 = q0 + lax.broadcasted_iota(jnp.int32, (Qb, 1), 0)

    n_idx = lax.broadcasted_iota(jnp.int32, (1, npad), 1)
    visible = (n_idx * CMP_STRIDE + (CMP_BLOCK - 1) <= qp1) & (n_idx < n_cmp)
    s = _bdot_nt(qs, kc) + _rep_all(jnp.where(visible, 0.0, -BIG))
    e = jnp.exp2(s - jnp.max(s, axis=-1, keepdims=True))
    any_visible = _rep_all((qp1 >= CMP_BLOCK - 1) & (n_cmp > 0))
    p_c = e * jnp.where(any_visible, 1.0 / jnp.sum(e, axis=-1, keepdims=True), 0.0)
    o_c = _bdot(p_c, vc)

    psum = []
    for k in range(NSA_KV):
        acc = p_c[(k * NSA_HPG) * Qb:(k * NSA_HPG + 1) * Qb]
        for r in range(1, NSA_HPG):
            acc = acc + p_c[(k * NSA_HPG + r) * Qb:(k * NSA_HPG + r + 1) * Qb]
        psum.append(acc)
    psum = jnp.concatenate(psum + [jnp.zeros((LANES - R2, npad), F32)], axis=0)
    nsr = -(-n_sel // SUBLANES) * SUBLANES
    sj = lax.broadcasted_iota(jnp.int32, (nsr, npad), 0) * SEL_BLOCK
    ci = lax.broadcasted_iota(jnp.int32, (nsr, npad), 1) * CMP_STRIDE
    selmap = jnp.where((ci < sj + SEL_BLOCK) & (ci + CMP_BLOCK > sj), 1.0, 0.0).astype(BF16)
    ph, plo = _split(psum)
    imp = _bdot_nt(selmap, ph) + _bdot_nt(selmap, plo)
    col_pos = q0 + (lax.broadcasted_iota(jnp.int32, (1, LANES), 1) & (Qb - 1))
    return qs, qp1, o_c, _select_blocks(imp, col_pos, n_sel)


def _select_blocks(imp, qpos, n_sel):
    nsr, ncol = imp.shape
    j = lax.broadcasted_iota(jnp.int32, (nsr, ncol), 0)
    jf = j.astype(F32)
    cur = qpos >> 6
    forced = (j == 0) | (j == cur) | (j == cur - 1)
    imp = jnp.where(forced, SEL_FORCE, imp)
    imp = jnp.where(j * SEL_BLOCK <= qpos, imp, -SEL_FORCE)
    imp = jnp.where(j < n_sel, imp, NEG)

    def pick(_, carry):
        imp_c, sel_c = carry
        m = jnp.max(imp_c, axis=0, keepdims=True)
        first = jnp.min(jnp.where(imp_c == m, jf, float(nsr)), axis=0, keepdims=True)
        hit = jf == first
        return jnp.where(hit, NEG, imp_c), jnp.where(hit, 1.0, sel_c)

    _, sel = lax.fori_loop(0, min(SEL_TOPK, n_sel), pick, (imp, jnp.zeros((nsr, ncol), F32)), unroll=True)
    return sel


def _nsa_combine(gate, o_c, o_s, o_w, Qb, o_ref):
    lo_half = lax.broadcasted_iota(jnp.int32, (Qb, LANES), 1) < NSA_HD
    g = _sigmoid(gate)
    for pair in range(NSA_KV * NSA_HPG // 2):
        halves = []
        for half, hd in enumerate((2 * pair, 2 * pair + 1)):
            k = hd // NSA_HPG
            rs = slice(hd * Qb, (hd + 1) * Qb)
            c = hd * 3
            o = g[:, c:c + 1] * o_c[rs] + g[:, c + 1:c + 2] * o_s[rs] + g[:, c + 2:c + 3] * o_w[rs]
            halves.append(o if half == k else pltpu.roll(o, NSA_HD, 1))
        o_ref[0, :, pair * LANES:(pair + 1) * LANES] = jnp.where(lo_half, halves[0], halves[1])


def _block_columns(k0, n):
    key = k0 + lax.broadcasted_iota(jnp.int32, (n, LANES), 0)
    blk = lax.broadcasted_iota(jnp.int32, (n, LANES), 1)
    return jnp.where((key >> 6) == blk, BIG, 0.0).astype(BF16)


def _block_rows(k0, n):
    key = k0 + lax.broadcasted_iota(jnp.int32, (LANES, n), 1)
    blk = lax.broadcasted_iota(jnp.int32, (LANES, n), 0)
    return jnp.where((key >> 6) == blk, BIG, 0.0).astype(BF16)


def _lanes_all(a):
    return jnp.concatenate([a] * (NSA_KV * NSA_HPG), axis=1)


def _lanes_heads(a, Qb):
    return jnp.concatenate([a[:, :Qb]] * NSA_HPG + [a[:, Qb:]] * NSA_HPG, axis=1)


def _softmax_cols(s):
    e = jnp.exp2(s - jnp.max(s, axis=0, keepdims=True))
    return e, 1.0 / jnp.sum(e, axis=0, keepdims=True)


def _nsa_prompt_kernel(q_ref, gate_ref, kc_ref, vc_ref, rows_ref, win_ref, o_ref, kaug, vt,
                       *, T, Qb, n_cmp, n_sel, wl):
    i = pl.program_id(1)
    q0 = i * Qb
    R = NSA_KV * NSA_HPG * Qb
    R2 = NSA_KV * Qb

    @pl.when(i == 0)
    def _():
        def pack(c, carry):
            r0 = pl.multiple_of(c * SEL_CHUNK, SEL_CHUNK)
            kaug[pl.ds(r0, SEL_CHUNK), 0:LANES] = rows_ref[0, pl.ds(r0, SEL_CHUNK), 2 * LANES:3 * LANES].astype(BF16)
            kaug[pl.ds(r0, SEL_CHUNK), LANES:2 * LANES] = _block_columns(r0, SEL_CHUNK)
            vt[:, pl.ds(r0, SEL_CHUNK)] = rows_ref[0, pl.ds(r0, SEL_CHUNK), 3 * LANES:4 * LANES].T.astype(BF16)
            return carry
        lax.fori_loop(0, T // SEL_CHUNK, pack, 0)

    top = lax.broadcasted_iota(jnp.int32, (LANES, Qb), 0) < NSA_HD
    scale = NSA_HD ** -0.5 * LOG2E
    qblk = q_ref[0]
    cols = []
    for hd in range(NSA_KV * NSA_HPG):
        k, pair, half = hd // NSA_HPG, hd // 2, hd % 2
        t = (qblk[:, pair * LANES:(pair + 1) * LANES] * scale).T
        if half != k:
            t = jnp.concatenate([t[NSA_HD:], t[:NSA_HD]], axis=0)
        cols.append(jnp.where(top if k == 0 else jnp.logical_not(top), t, 0.0))
    qs = jnp.concatenate(cols, axis=1).astype(BF16)
    qlane = q0 + lax.broadcasted_iota(jnp.int32, (1, Qb), 1)
    qpos = _lanes_all(qlane)

    kc = kc_ref[0]
    npad = kc.shape[0]
    n_idx = lax.broadcasted_iota(jnp.int32, (npad, 1), 0)
    visible = (n_idx * CMP_STRIDE + (CMP_BLOCK - 1) <= qlane) & (n_idx < n_cmp)
    e_c, inv_c = _softmax_cols(_bdot(kc, qs) + _lanes_all(jnp.where(visible, 0.0, -BIG)))
    any_visible = _lanes_all((qlane >= CMP_BLOCK - 1) & (n_cmp > 0))
    p_c = e_c * jnp.where(any_visible, inv_c, 0.0)
    o_c = _bdot(vc_ref[0].T, p_c)

    psum = []
    for k in range(NSA_KV):
        acc = p_c[:, (k * NSA_HPG) * Qb:(k * NSA_HPG + 1) * Qb]
        for r in range(1, NSA_HPG):
            acc = acc + p_c[:, (k * NSA_HPG + r) * Qb:(k * NSA_HPG + r + 1) * Qb]
        psum.append(acc)
    psum = jnp.concatenate(psum, axis=1)
    nsr = -(-n_sel // SUBLANES) * SUBLANES
    sj = lax.broadcasted_iota(jnp.int32, (nsr, npad), 0) * SEL_BLOCK
    ci = lax.broadcasted_iota(jnp.int32, (nsr, npad), 1) * CMP_STRIDE
    selmap = jnp.where((ci < sj + SEL_BLOCK) & (ci + CMP_BLOCK > sj), 1.0, 0.0).astype(BF16)
    ph, plo = _split(psum)
    imp = (jnp.dot(selmap, ph, preferred_element_type=F32)
           + jnp.dot(selmap, plo, preferred_element_type=F32))
    sel = _select_blocks(imp, jnp.concatenate([qlane] * NSA_KV, axis=1), n_sel)
    selm =jnp.concatenate([sel - 1.0, jnp.zeros((LANES - nsr, R2), F32)], axis=0) if nsr < LANES else sel - 1.0
    qaug = jnp.concatenate([qs, _lanes_heads(selm, Qb).astype(BF16)], axis=0)

    def update(s, vcols, carry):
        m, l, acc = carry
        m_new = jnp.maximum(m, jnp.max(s, axis=0, keepdims=True))
        alpha = jnp.exp2(m - m_new)
        p = jnp.exp2(s - m_new)
        l = alpha * l + jnp.sum(p, axis=0, keepdims=True)
        acc = alpha * acc + jnp.dot(vcols, p.astype(BF16), preferred_element_type=F32)
        return m_new, l, acc

    def scores(k0):
        return jnp.dot(kaug[pl.ds(k0, SEL_CHUNK), :], qaug, preferred_element_type=F32)

    def full_chunk(c, carry):
        k0 = pl.multiple_of(c * SEL_CHUNK, SEL_CHUNK)
        return update(scores(k0), vt[:, pl.ds(k0, SEL_CHUNK)], carry)

    n_full = q0 // SEL_CHUNK
    init = (jnp.full((1, R), -4.0 * BIG, F32), jnp.zeros((1, R), F32), jnp.zeros((LANES, R), F32))
    carry = lax.fori_loop(0, n_full, full_chunk, init)
    k0 = pl.multiple_of(n_full * SEL_CHUNK, SEL_CHUNK)
    kpos = k0 + lax.broadcasted_iota(jnp.int32, (SEL_CHUNK, 1), 0)
    s_diag = jnp.where(kpos <= qpos, scores(k0), -2.0 * BIG)
    _, l_s, acc_s = update(s_diag, vt[:, pl.ds(k0, SEL_CHUNK)], carry)
    o_s = acc_s * (1.0 / l_s)

    ws = pl.multiple_of(jnp.maximum(q0 - WINDOW, 0), LANES)
    wk = win_ref[0, pl.ds(ws, wl), 0:LANES]
    wv = win_ref[0, pl.ds(ws, wl), LANES:2 * LANES]
    dpos = qlane - (ws + lax.broadcasted_iota(jnp.int32, (wl, 1), 0))
    e_w, inv_w = _softmax_cols(_bdot(wk, qs) + _lanes_all(jnp.where((dpos >= 0) & (dpos <= WINDOW), 0.0, -BIG)))
    o_w = _bdot(wv.T, e_w) * inv_w

    g = _sigmoid(gate_ref[0]).T
    for pair in range(NSA_KV * NSA_HPG // 2):
        rows = []
        for hd in (2 * pair, 2 * pair + 1):
            k = hd // NSA_HPG
            cs = slice(hd * Qb, (hd + 1) * Qb)
            c = hd * 3
            o = g[c:c + 1, :] * o_c[:, cs] + g[c + 1:c + 2, :] * o_s[:, cs] + g[c + 2:c + 3, :] * o_w[:, cs]
            rows.append(o[k * NSA_HD:(k + 1) * NSA_HD])
        o_ref[0, :, pair * LANES:(pair + 1) * LANES] = jnp.concatenate(rows, axis=0).T.astype(o_ref.dtype)


def _nsa_prompt(nq, gate, kc, vc, rows, win):
    B, T, HD = nq.shape
    Qb = QBLOCK
    nch = kc.shape[1]
    n_sel = -(-T // SEL_BLOCK)
    wl = WINDOW + Qb
    assert T % SEL_CHUNK == 0 and T >= wl and SEL_TOPK <= n_sel <= LANES
    return pl.pallas_call(
        functools.partial(_nsa_prompt_kernel, T=T, Qb=Qb, n_cmp=nch - 1, n_sel=n_sel, wl=wl),
        grid=(B, T // Qb),
        in_specs=[pl.BlockSpec((1, Qb, HD), lambda b, i: (b, i, 0)),
                  pl.BlockSpec((1, Qb, LANES), lambda b, i: (b, i, 0)),
                  pl.BlockSpec((1, nch, LANES), lambda b, i: (b, 0, 0)),
                  pl.BlockSpec((1, nch, LANES), lambda b, i: (b, 0, 0)),
                  pl.BlockSpec((1, T, 4 * LANES), lambda b, i: (b, 0, 0)),
                  pl.BlockSpec((1, T, 2 * LANES), lambda b, i: (b, 0, 0))],
        out_specs=pl.BlockSpec((1, Qb, HD), lambda b, i: (b, i, 0)),
        out_shape=jax.ShapeDtypeStruct((B, T, HD), BF16),
        scratch_shapes=[pltpu.VMEM((T, 2 * LANES), BF16), pltpu.VMEM((LANES, T), BF16)],
        compiler_params=_cparams(2),
        name="nsa_attn_prompt",
    )(nq, gate, kc, vc, rows, win)


def _pages_copy(cache_hbm, layer, page, r0, dst, j, sem):
    n = cache_hbm.shape[-1]
    return pltpu.make_async_copy(cache_hbm.at[layer, page, pl.ds(r0, 2)],
                                 dst.at[:, :, pl.ds(pl.multiple_of(j * n, n), n)], sem)


def _pages_start(pt_ref, b, cache_hbm, layer, r0, dst, sem, npages):
    def issue(j, carry):
        _pages_copy(cache_hbm, layer, pt_ref[b, j], r0, dst, j, sem).start()
        return carry
    lax.fori_loop(0, npages, issue, 0, unroll=PAGE_UNROLL)


def _pages_wait(cache_hbm, layer, r0, dst, sem, npages):
    def wait(j, carry):
        _pages_copy(cache_hbm, layer, 0, r0, dst, j, sem).wait()
        return carry
    lax.fori_loop(0, npages, wait, 0, unroll=PAGE_UNROLL)


def _softmax2(s1, s2, mask2):
    s2 = jnp.where(mask2, s2, -2.0 * BIG)
    m = jnp.maximum(jnp.max(s1, axis=-1, keepdims=True), jnp.max(s2, axis=-1, keepdims=True))
    e1 = jnp.exp2(s1 - m)
    e2 = jnp.exp2(s2 - m)
    den = jnp.sum(e1, axis=-1, keepdims=True) + jnp.sum(e2, axis=-1, keepdims=True)
    return e1, e2, 1.0 / den


def _nsa_sample_kernel(pt_ref, q_ref, gate_ref, rows_ref, wt_ref, wnew_ref, cache_hbm,
                       w1_ref, pe_ref, b1_ref, w2_ref, o_ref, wout_ref,
                       cmpbuf, selbuf, xk, xv, kaug, vt, newbuf, wnewbuf, bias_sc, perm_sc, csem, ssem,
                       *, layer, npages, P, Q, wb, nb, n_sel):
    b = pl.program_id(0)
    nseq = pl.num_programs(0)
    page = cache_hbm.shape[-1]
    nch = P // CMP_STRIDE

    @pl.when(b == 0)
    def _():
        _pages_start(pt_ref, 0, cache_hbm, layer, 0, cmpbuf, csem, npages)
        _pages_start(pt_ref, 0, cache_hbm, layer, 2, selbuf, ssem, npages)
        _compress_bias(w1_ref, pe_ref, bias_sc)

        def blocks(c, carry):
            c0 = pl.multiple_of(c * CAST_CHUNK, CAST_CHUNK)
            kaug[LANES:2 * LANES, pl.ds(c0, CAST_CHUNK)] = _block_rows(c0, CAST_CHUNK)
            return carry
        lax.fori_loop(0, P // CAST_CHUNK, blocks, 0)
        newbuf[...] = jnp.zeros(newbuf.shape, F32)
        wnewbuf[...] = jnp.zeros(wnewbuf.shape, F32)
        src = lax.broadcasted_iota(jnp.int32, (2 * page, 2 * page), 0)
        dst_col = lax.broadcasted_iota(jnp.int32, (2 * page, 2 * page), 1)
        t = src & (page - 1)
        want = (src - t) + (t % CMP_STRIDE) * (page // CMP_STRIDE) + t // CMP_STRIDE
        perm_sc[...] = jnp.where(dst_col == want, 1.0, 0.0).astype(BF16)

    _pages_wait(cache_hbm, layer, 0, cmpbuf, csem, npages)
    per_page = page // CMP_STRIDE
    for c, dst in enumerate((xk, xv)):
        for jp in range(npages // 2):
            cols = cmpbuf[c, :, 2 * jp * page:2 * (jp + 1) * page].astype(BF16)
            regrouped = jnp.dot(cols, perm_sc[...], preferred_element_type=F32)
            for half in range(2):
                tok = regrouped[:, half * page:(half + 1) * page].T
                r0 = (2 * jp + half) * per_page
                for p in range(CMP_STRIDE):
                    dst[r0:r0 + per_page, p * LANES:(p + 1) * LANES] = tok[p * per_page:(p + 1) * per_page, :]
    kc = _compress_one(0, xk[...].astype(BF16), nch, w1_ref, bias_sc, b1_ref, w2_ref)
    vc = _compress_one(1, xv[...].astype(BF16), nch, w1_ref, bias_sc, b1_ref, w2_ref)

    @pl.when(b + 1 < nseq)
    def _():
        _pages_start(pt_ref, b + 1, cache_hbm, layer, 0, cmpbuf, csem, npages)

    _pages_wait(cache_hbm, layer, 2, selbuf, ssem, npages)

    def pack(c, carry):
        c0 = pl.multiple_of(c * CAST_CHUNK, CAST_CHUNK)
        kaug[0:LANES, pl.ds(c0, CAST_CHUNK)] = selbuf[0, :, pl.ds(c0, CAST_CHUNK)].astype(BF16)
        vt[:, pl.ds(c0, CAST_CHUNK)] = selbuf[1, :, pl.ds(c0, CAST_CHUNK)].astype(BF16)
        return carry
    lax.fori_loop(0, P // CAST_CHUNK, pack, 0)

    @pl.when(b + 1 < nseq)
    def _():
        _pages_start(pt_ref, b + 1, cache_hbm, layer, 2, selbuf, ssem, npages)

    newbuf[0:Q, :] = rows_ref[0, :, 2 * LANES:4 * LANES]
    wnewbuf[0:Q, :] = wnew_ref[0]

    qs, qp1, o_c, sel_t = _nsa_front(q_ref[0], P, Q, kc, vc, nch - 1, n_sel)
    qpos = _rep_all(qp1)
    lane = lax.broadcasted_iota(jnp.int32, (1, LANES), 1)
    new_pos = P + lane
    is_new = lane < Q

    def as_rows(blk):
        if blk.shape[0] < LANES:
            blk = jnp.concatenate([blk, jnp.zeros((LANES - blk.shape[0], LANES), F32)], axis=0)
        return blk.T[0:NSA_KV * Q]
    sel_rows = as_rows(sel_t[0:min(LANES, sel_t.shape[0])])
    sel_new = as_rows(sel_t[nb:nb + SUBLANES])[:, 0:1]

    qaug = jnp.concatenate([qs, _rep_heads(sel_rows - 1.0, Q).astype(BF16)], axis=1)
    s_past = jnp.dot(qaug, kaug[...], preferred_element_type=F32)
    s_new = _bdot_nt(qs, newbuf[:, 0:LANES])
    new_ok = is_new & (new_pos <= qpos) & (_rep_heads(sel_new, Q) > 0.5)
    e1, e2, inv = _softmax2(s_past, s_new, new_ok)
    o_s = (lax.dot_general(e1.astype(BF16), vt[...], (((1,), (1,)), ((), ())), preferred_element_type=F32)
           + _bdot(e2, newbuf[:, LANES:2 * LANES])) * inv

    dpast = qp1 - ((P - wb) + lax.broadcasted_iota(jnp.int32, (1, wb), 1))
    s_wp = (jnp.dot(qs, wt_ref[0, 0].astype(BF16), preferred_element_type=F32)
            + _rep_all(jnp.where((dpast >= 0) & (dpast <= WINDOW), 0.0, -BIG)))
    s_wn = _bdot_nt(qs, wnewbuf[:, 0:LANES])
    dnew = qpos - new_pos
    e1, e2, inv = _softmax2(s_wp, s_wn, is_new & (dnew >= 0) & (dnew <= WINDOW))
    o_w = (lax.dot_general(e1.astype(BF16), wt_ref[0, 1].astype(BF16), (((1,), (1,)), ((), ())),
                           preferred_element_type=F32)
           + _bdot(e2, wnewbuf[:, LANES:2 * LANES])) * inv

    _nsa_combine(gate_ref[0], o_c, o_s, o_w, Q, o_ref)

    for kv in range(2):
        new_cols = wnewbuf[:, kv * LANES:(kv + 1) * LANES].T[:, 0:Q]
        wout_ref[0, kv] = jnp.concatenate([wt_ref[0, kv][:, Q:], new_cols], axis=1)


def _nsa_sample(page_table, nq, gate, rows, win_t, win_new, cache_t, layer, cw):
    DB, Q, HD = nq.shape
    npages = page_table.shape[1]
    page = cache_t.shape[-1]
    P = npages * page
    wb = win_t.shape[-1]
    nb = P // SEL_BLOCK
    n_sel = -(-(P + Q) // SEL_BLOCK)
    w1, pe, b1, w2 = cw
    assert Q == SUBLANES and P % SEL_BLOCK == 0 and Q <= SEL_BLOCK and nb <= LANES and n_sel >= SEL_TOPK
    assert P % CAST_CHUNK == 0 and (P + Q) // CMP_STRIDE == P // CMP_STRIDE and page == LANES
    assert npages % 2 == 0 and page // CMP_STRIDE == SUBLANES and wb >= Q
    bs = lambda shape: pl.BlockSpec((1,) + shape, lambda b, pt: (b,) + (0,) * len(shape))
    win_spec = pl.BlockSpec((None, 1, 2, LANES, wb), lambda b, pt: (layer, b, 0, 0, 0))
    grid_spec = pltpu.PrefetchScalarGridSpec(
        num_scalar_prefetch=1, grid=(DB,),
        in_specs=[bs((Q, HD)), bs((Q, LANES)), bs((Q, 4 * LANES)), win_spec, bs((Q, 2 * LANES)),
                  pl.BlockSpec(memory_space=pl.ANY),
                  _pspec(w1), _pspec(pe), _pspec(b1), _pspec(w2)],
        out_specs=[bs((Q, HD)), bs((2, LANES, wb))],
        scratch_shapes=[pltpu.VMEM((2, LANES, P), F32), pltpu.VMEM((2, LANES, P), F32),
                        pltpu.VMEM((P // CMP_STRIDE, CMP_STRIDE * LANES), F32),
                        pltpu.VMEM((P // CMP_STRIDE, CMP_STRIDE * LANES), F32),
                        pltpu.VMEM((2 * LANES, P), BF16), pltpu.VMEM((LANES, P), BF16),
                        pltpu.VMEM((LANES, 2 * LANES), F32), pltpu.VMEM((LANES, 2 * LANES), F32),
                        pltpu.VMEM((2, SUBLANES, 4 * LANES), F32), pltpu.VMEM((2 * page, 2 * page), BF16),
                        pltpu.SemaphoreType.DMA(()), pltpu.SemaphoreType.DMA(())])
    return pl.pallas_call(
        functools.partial(_nsa_sample_kernel, layer=layer, npages=npages, P=P, Q=Q, wb=wb, nb=nb, n_sel=n_sel),
        grid_spec=grid_spec,
        out_shape=[jax.ShapeDtypeStruct((DB, Q, HD), F32), jax.ShapeDtypeStruct((DB, 2, LANES, wb), F32)],
        compiler_params=_cparams(1),
        name="nsa_sample",
    )(page_table, nq, gate, rows, win_t, win_new, cache_t, _parg(w1), _parg(pe), _parg(b1), _parg(w2))


def _ret_kernel(q_ref, k_ref, v_ref, g_ref, cos_ref, sin_ref, s0_ref, gn_ref, o_ref, snew_ref, s_sc,
                *, C, nC, nseq):
    c = pl.program_id(1)

    @pl.when(c == 0)
    def _():
        s_sc[...] = s0_ref[...]

    cosf = cos_ref[...]
    sinf = sin_ref[...]
    diff = (lax.broadcasted_iota(jnp.int32, (C, C), 0) - lax.broadcasted_iota(jnp.int32, (C, C), 1)).astype(F32)
    ii = lax.broadcasted_iota(jnp.int32, (C, 1), 0).astype(F32)
    half = RET_DK // 2
    for h in range(RET_HEADS):
        lg = math.log(1.0 - 2.0 ** (-5.0 - h))
        hs = slice(h * RET_DK, (h + 1) * RET_DK)
        decay = jnp.where(diff >= 0, jnp.exp(jnp.maximum(diff, 0.0) * lg), 0.0)
        cross = jnp.exp((ii + 1.0) * lg)
        kweight = jnp.exp((C - 1.0 - ii) * lg)
        for b in range(nseq):
            q = q_ref[b, :, hs]
            k = k_ref[b, :, hs]
            v = v_ref[b, :, hs]
            qr = q * cosf + pltpu.roll(q, half, 1) * sinf
            kr = (k * cosf + pltpu.roll(k, half, 1) * sinf) * (RET_DK ** -0.5)
            o_inner = _bdot(_bdot_nt(qr, kr) * decay, v)
            s_old = s_sc[b, h]
            o_cross = _bdot(qr, s_old) * cross
            kv = lax.dot_general((kr * kweight).astype(BF16), v.astype(BF16), (((0,), (0,)), ((), ())),
                                 preferred_element_type=F32)
            s_sc[b, h] = math.exp(C * lg) * s_old + kv
            o = o_inner + o_cross
            mu = jnp.mean(o, axis=-1, keepdims=True)
            var = jnp.mean(jnp.square(o - mu), axis=-1, keepdims=True)
            gate = g_ref[b, :, hs]
            o_ref[b, :, hs] = (((o - mu) * lax.rsqrt(var + EPS)) * gn_ref[:, hs]
                               * (gate * _sigmoid(gate))).astype(o_ref.dtype)

    @pl.when(c == nC - 1)
    def _():
        snew_ref[...] = s_sc[...]


def _retention(rq, rk, rv, rg, cosf, sinf, s0, gn, nseq, out_dtype):
    B, T, W = rq.shape
    C = RET_CHUNK if (T >= RET_CHUNK and T % RET_CHUNK == 0) else T
    nC = T // C
    tok = pl.BlockSpec((nseq, C, W), lambda b, c: (b, c, 0))
    tab = pl.BlockSpec((C, RET_DK), lambda b, c: (c, 0))
    st = pl.BlockSpec((nseq,) + s0.shape[1:], lambda b, c: (b, 0, 0, 0))
    return pl.pallas_call(
        functools.partial(_ret_kernel, C=C, nC=nC, nseq=nseq),
        grid=(B // nseq, nC),
        in_specs=[tok, tok, tok, tok, tab, tab, st, _pspec(gn)],
        out_specs=[tok, st],
        out_shape=[jax.ShapeDtypeStruct((B, T, W), out_dtype), jax.ShapeDtypeStruct(s0.shape, F32)],
        scratch_shapes=[pltpu.VMEM((nseq,) + s0.shape[1:], F32)],
        compiler_params=_cparams(2),
        name="retention",
    )(rq, rk, rv, rg, cosf, sinf, s0, _parg(gn))


def _shift_carry(x, k, tail8):
    r = pltpu.roll(x, k, 0)
    row8 = lax.broadcasted_iota(jnp.int32, (SUBLANES, 1), 0)
    first = jnp.where(row8 >= k, r[:SUBLANES], pltpu.roll(tail8, k, 0))
    return jnp.concatenate([first, r[SUBLANES:]], axis=0)


def _shift_seg(x, k, fill, tpos):
    return jnp.where(tpos >= k, pltpu.roll(x, k, 0), fill)


def _rglru_kernel(x_ref, gate_ref, st_ref, h0_ref, cw_ref, cb_ref, wa_ref, ba_ref, wx_ref, bx_ref, lam_ref,
                  o_ref, h_ref, tail_sc, h_sc, *, tm, seg):
    carry = seg == 0
    x = x_ref[0]
    rows = lax.broadcasted_iota(jnp.int32, (tm, 1), 0)
    if carry:
        @pl.when(pl.program_id(1) == 0)
        def _():
            tail_sc[...] = st_ref[0]
            h_sc[...] = h0_ref[0]
        tail8 = tail_sc[...]
        shifted = [_shift_carry(x, k, tail8) for k in range(1, RG_CONV)]
        tpos = rows & (SUBLANES - 1)
    else:
        tpos = rows & (seg - 1)
        shifted = [_shift_seg(x, k, st_ref[k - 1], tpos) for k in range(1, RG_CONV)]
    xc = cb_ref[...] + cw_ref[RG_CONV - 1:RG_CONV, :] * x
    for k in range(1, RG_CONV):
        xc = xc + cw_ref[RG_CONV - 1 - k:RG_CONV - k, :] * shifted[k - 1]
    r = _sigmoid(_bdot(xc, wa_ref[...]) + ba_ref[...])
    i = _sigmoid(_bdot(xc, wx_ref[...]) + bx_ref[...])
    lam = lam_ref[...]
    softplus = jnp.maximum(-lam, 0.0) + jnp.log(1.0 + jnp.exp(-jnp.abs(lam)))
    log_a = (-RG_C * r) * softplus
    a = jnp.exp(log_a)
    gap = 1.0 - a * a
    bt = jnp.where(gap > 0, gap * lax.rsqrt(gap), 0.0) * (i * xc)
    if not carry:
        bt = bt + a * h0_ref[0]
    k = 1
    while k < SUBLANES:
        ok = tpos >= k
        a_prev = jnp.where(ok, pltpu.roll(a, k, 0), 1.0)
        b_prev = jnp.where(ok, pltpu.roll(bt, k, 0), 0.0)
        bt = a * b_prev + bt
        a = a * a_prev
        k *= 2
    if carry:
        h_prev = h_sc[SUBLANES - 1:SUBLANES, :]
        groups = []
        for g0 in range(0, tm, SUBLANES):
            h_g = bt[g0:g0 + SUBLANES] + a[g0:g0 + SUBLANES] * h_prev
            groups.append(h_g)
            h_prev = h_g[SUBLANES - 1:SUBLANES, :]
        bt = jnp.concatenate(groups, axis=0)
    o_ref[0] = (bt * _gelu(gate_ref[0].astype(F32))).astype(o_ref.dtype)
    if carry:
        tail_sc[...] = x[tm - SUBLANES:]
        h_sc[...] = bt[tm - SUBLANES:]
        h_ref[0] = bt[tm - SUBLANES:]
    else:
        h_ref[0] = bt


def _rglru(rx, rgate, st, h0, rw, tm, seg):
    G, Tg, W = rx.shape
    tok = pl.BlockSpec((1, tm, W), lambda g, t: (g, t, 0))
    if seg == 0:
        st_spec = pl.BlockSpec((1, SUBLANES, W), lambda g, t: (g, 0, 0))
        h0_spec = pl.BlockSpec((1, SUBLANES, W), lambda g, t: (g, 0, 0))
        h_spec = pl.BlockSpec((1, SUBLANES, W), lambda g, t: (g, 0, 0))
        h_shape = (G, SUBLANES, W)
    else:
        st_spec = pl.BlockSpec(st.shape, lambda g, t: (0, 0, 0))
        h0_spec = tok
        h_spec = tok
        h_shape = (G, Tg, W)
    return pl.pallas_call(
        functools.partial(_rglru_kernel, tm=tm, seg=seg),
        grid=(G, Tg // tm),
        in_specs=[tok, tok, st_spec, h0_spec] + [_pspec(a) for a in rw],
        out_specs=[tok, h_spec],
        out_shape=[jax.ShapeDtypeStruct((G, Tg, W), rgate.dtype), jax.ShapeDtypeStruct(h_shape, F32)],
        scratch_shapes=[pltpu.VMEM((SUBLANES, W), F32), pltpu.VMEM((SUBLANES, W), F32)],
        compiler_params=_cparams(2),
        name="rglru",
    )(rx, rgate, st, h0, *[_parg(a) for a in rw])


def _mix_ffn_kernel(x_ref, oa_ref, or_ref, oc_ref, mg_ref, gt1_ref, wa_ref, wb_ref, wc_ref, wo_ref,
                    g_ref, sc_ref, sh_ref, gt_ref, st_ref, wup_ref, cw_ref, cb_ref, wdn_ref, fg_ref,
                    y_ref, fnew_ref, tail_sc, *, tm, seg, F, chunks, final):
    D = x_ref.shape[2]
    pa = _bdot(oa_ref[0], wa_ref[...])
    pb = _bdot(or_ref[0], wb_ref[...])
    pc = _bdot(oc_ref[0], wc_ref[...])
    gate = lambda i: _sigmoid(mg_ref[0, :, i * D:(i + 1) * D].astype(F32))
    merged = gate(0) * pa + gate(1) * pb + gate(2) * pc
    x = x_ref[0] + gt1_ref[...] * _bdot(merged, wo_ref[...])

    carry = seg == 0
    h = _rms_mod(x, g_ref[...], sc_ref[...], sh_ref[...]).astype(BF16)
    rows = lax.broadcasted_iota(jnp.int32, (tm, 1), 0)
    if carry:
        @pl.when(pl.program_id(1) == 0)
        def _():
            tail_sc[...] = st_ref[0]
    else:
        tpos = rows & (seg - 1)
    acc = jnp.zeros(x.shape, F32)
    for c0, wck in chunks:
        cs = slice(c0, c0 + wck)
        gp = jnp.dot(h, wup_ref[:, c0:c0 + wck], preferred_element_type=F32)
        val = jnp.dot(h, wup_ref[:, F + c0:F + c0 + wck], preferred_element_type=F32)
        if carry:
            tail8 = tail_sc[:, cs]
            shifted = [_shift_carry(gp, k, tail8) for k in range(1, FFN_CONV)]
            tail_sc[:, cs] = gp[tm - SUBLANES:]
            fnew_ref[0, :, cs] = gp[tm - SUBLANES:]
        else:
            shifted = [_shift_seg(gp, k, st_ref[k - 1, :, cs], tpos) for k in range(1, FFN_CONV)]
            fnew_ref[0, :, cs] = gp
        gc = cb_ref[:, cs] + cw_ref[FFN_CONV - 1:FFN_CONV, cs] * gp
        for k in range(1, FFN_CONV):
            gc = gc + cw_ref[FFN_CONV - 1 - k:FFN_CONV - k, cs] * shifted[k - 1]
        act = (gc * _sigmoid(gc)) * val
        acc = acc + _bdot(act, wdn_ref[cs, :])
    y = x + gt_ref[...] * acc
    if final:
        y = (y * lax.rsqrt(jnp.mean(y * y, axis=-1, keepdims=True) + EPS)) * fg_ref[...]
    y_ref[0] = y


def _ffn_chunks(F):
    half = -(-(F // 2) // MXU_DEPTH_V7X) * MXU_DEPTH_V7X
    return ((0, half), (half, F - half)) if 0 < half < F else ((0, F),)


def _mix_ffn(x, oa, orr, oc, mg, gt1, wa, wb, wc, wo, g, sc, sh, gt, st, wup, cw, cb, wdn, fg, tm, seg, final):
    G, Tg, D = x.shape
    F = wdn.shape[0]
    tokw = lambda w: pl.BlockSpec((1, tm, w), lambda g_, t: (g_, t, 0))
    tok = tokw(D)
    if seg == 0:
        st_spec = pl.BlockSpec((1, SUBLANES, F), lambda g_, t: (g_, 0, 0))
        fn_spec = pl.BlockSpec((1, SUBLANES, F), lambda g_, t: (g_, 0, 0))
        fn_shape = (G, SUBLANES, F)
    else:
        st_spec = pl.BlockSpec(st.shape, lambda g_, t: (0, 0, 0))
        fn_spec = pl.BlockSpec((1, tm, F), lambda g_, t: (g_, t, 0))
        fn_shape = (G, Tg, F)
    return pl.pallas_call(
        functools.partial(_mix_ffn_kernel, tm=tm, seg=seg, F=F, chunks=_ffn_chunks(F), final=final),
        grid=(G, Tg // tm),
        in_specs=[tok, tokw(oa.shape[2]), tokw(orr.shape[2]), tokw(oc.shape[2]), tokw(3 * D),
                  _mod_spec(gt1, tm, D), _pspec(wa), _pspec(wb), _pspec(wc), _pspec(wo),
                  _pspec(g), _mod_spec(sc, tm, D), _mod_spec(sh, tm, D), _mod_spec(gt, tm, D),
                  st_spec, _pspec(wup), _pspec(cw), _pspec(cb), _pspec(wdn), _pspec(fg)],
        out_specs=[tok, fn_spec],
        out_shape=[jax.ShapeDtypeStruct((G, Tg, D), F32), jax.ShapeDtypeStruct(fn_shape, F32)],
        scratch_shapes=[pltpu.VMEM((SUBLANES, F), F32)],
        compiler_params=_cparams(2),
        name="mix_ffn",
    )(x, oa, orr, oc, mg, gt1.arr, _parg(wa), _parg(wb), _parg(wc), _parg(wo),
      _parg(g), sc.arr, sh.arr, gt.arr, st, _parg(wup), _parg(cw), _parg(cb), _parg(wdn), fg)


def _block_diag(w):
    n, a, b = w.shape[-3:]
    eye = jnp.eye(n, dtype=w.dtype)
    out = jnp.einsum('ij,...iab->...iajb', eye, w)
    return out.reshape(w.shape[:-3] + (n * a, n * b))


def _seg_fill(buf, k, seg):
    B, nb, C = buf.shape
    part = jnp.concatenate([buf[:, nb - k:, :], jnp.zeros((B, seg - k, C), buf.dtype)], axis=1)
    return part.reshape(B * seg, C)


def kernel(x_prompt, x_sample, cache_nsa, cache_nsa_win, state_ret, state_rglru_h, state_rglru_conv,
           state_ffn_conv, page_table, c_prompt, c_sample, norm1_g, norm2_g, w_ada, b_ada, w_in, cmp_pe,
           cmp_w1, cmp_b1, cmp_w2, ret_gn_g, rg_conv_w, rg_conv_b, rg_w_a, rg_b_a, rg_w_x, rg_b_x, rg_lambda,
           w_br_a, w_br_b, w_br_c, w_out, ffn_w_up, ffn_conv_w, ffn_conv_b, ffn_w_down, final_norm_g):
    B, T, D = x_prompt.shape
    DB, Q, _ = x_sample.shape
    L = w_in.shape[0]
    npages = page_table.shape[1]
    page = cache_nsa.shape[2]
    P = npages * page
    NQ = NSA_KV * NSA_HPG * NSA_HD
    NKV = NSA_KV * NSA_HD
    RW = RET_HEADS * RET_DK
    W = rg_conv_w.shape[2]
    F = ffn_w_down.shape[1]
    NS = DB * Q
    wbuf_len = cache_nsa_win.shape[2]
    assert Q == SUBLANES and T >= RG_CONV and P % CMP_STRIDE == 0

    mod = _ada(jnp.concatenate([c_prompt, c_sample], axis=0), w_ada, b_ada)
    cache_t = jnp.transpose(cache_nsa, (0, 1, 3, 4, 5, 2)).reshape(L, cache_nsa.shape[1], 4, NKV, page)
    win_t_all = jnp.transpose(cache_nsa_win, (0, 1, 3, 4, 5, 2)).reshape(L, DB, 2, NKV, wbuf_len)

    half = RET_DK // 2
    freq = ROPE_BASE ** (-jnp.arange(half, dtype=F32) / half)

    def rope_tables(pos):
        ang = pos.astype(F32)[:, None] * freq[None, :]
        cos, sin = jnp.cos(ang), jnp.sin(ang)
        return jnp.concatenate([cos, cos], axis=1), jnp.concatenate([-sin, sin], axis=1)

    cos_p, sin_p = rope_tables(jnp.arange(T, dtype=jnp.int32))
    cos_s, sin_s = rope_tables(P + jnp.arange(Q, dtype=jnp.int32))

    widths = (NQ, 4 * NKV, 2 * NKV, LANES, RW, RW, RW, RW, W, W, 3 * D)
    dt_prompt = (F32,) * 9 + (BF16, BF16)
    dt_sample = (F32,) * 11
    offs = np.cumsum((0, NQ, 6 * NKV, 3 * NSA_KV * NSA_HPG, RW, RW, RW, RW, W, W, 3 * D))
    ngate = 3 * NSA_KV * NSA_HPG

    xp = x_prompt
    xs = x_sample.reshape(1, NS, D)
    outs_p = [[] for _ in range(6)]
    outs_s = [[] for _ in range(6)]
    tm_p = 256 if T % 256 == 0 else T

    w_head_all = jnp.pad(w_in[:, :, :offs[3]], ((0, 0), (0, 0), (0, LANES - ngate))).astype(BF16)
    w_tail_all = w_in[:, :, offs[3]:].astype(BF16)
    grouped = lambda w: _block_diag(jnp.broadcast_to(w[..., None, :, :], w.shape[:-2] + (NSA_KV,) + w.shape[-2:]))
    cw1_all = jnp.concatenate([grouped(cmp_w1[:, :, :CMP_STRIDE]), grouped(cmp_w1[:, :, CMP_STRIDE:])], axis=-1)
    cw1_all = cw1_all.reshape(L, 2, CMP_STRIDE * NKV, 4 * LANES).astype(BF16)
    cpe_all = jnp.tile(cmp_pe, (1, 1, 1, NSA_KV)).reshape(L, 2, 2, CMP_STRIDE * NKV)
    cb1_all = jnp.tile(cmp_b1, (1, 1, NSA_KV))[:, :, None, :]
    cw2_all = grouped(cmp_w2).astype(BF16)
    row = lambda a: a[:, None, :]
    rw_all = (rg_conv_w, row(rg_conv_b), _block_diag(rg_w_a).astype(BF16), row(rg_b_a),
              _block_diag(rg_w_x).astype(BF16), row(rg_b_x), row(rg_lambda))
    wa_all = w_br_a.astype(BF16)
    wb_all = w_br_b.astype(BF16)
    wc_all = w_br_c.astype(BF16)
    wo_all = w_out.astype(BF16)
    wup_all = ffn_w_up.astype(BF16)
    wdn_all = ffn_w_down.astype(BF16)
    g1_all, g2_all, gn_all, fcb_all = row(norm1_g), row(norm2_g), row(ret_gn_g), row(ffn_conv_b)
    fg = final_norm_g[None]
    mod_p = mod[:, :B].reshape(L, B, 1, 6 * D)
    mod_s = jnp.repeat(mod[:, B:], Q, axis=1)

    for l in range(L):
        lay = lambda a: _LayerOf(a, l)
        w_head, w_tail = lay(w_head_all), lay(w_tail_all)
        cw = (lay(cw1_all), lay(cpe_all), lay(cb1_all), lay(cw2_all))
        rw = tuple(lay(a) for a in rw_all)
        wa, wb, wc, wo, wup, wdn = (lay(a) for a in (wa_all, wb_all, wc_all, wo_all, wup_all, wdn_all))
        g1, g2, gn, fcw, fcb = (lay(a) for a in (g1_all, g2_all, gn_all, ffn_conv_w, fcb_all))
        final = l == L - 1

        m = [_ModOf(mod_p, l, i) for i in range(6)]
        (nq, rows, win, gate, rq, rk, rv, rg, rx, rgate, mg) = _inproj(xp, g1, m[1], m[0], w_head, w_tail, widths,
                                                                        dt_prompt, tm_p)
        kc, vc = _compress_prompt(rows, cw)
        o_a = _nsa_prompt(nq, gate, kc, vc, rows, win)
        o_r, s_new = _retention(rq, rk, rv, rg, cos_p, sin_p,
                                jnp.zeros((B, RET_HEADS, RET_DK, RET_DK), F32), gn, B, BF16)
        zs = jnp.zeros((B, SUBLANES, W), F32)
        o_c, h_tail = _rglru(rx, rgate, zs, zs, rw, tm_p, 0)
        xp, f_tail = _mix_ffn(xp, o_a, o_r, o_c, mg, m[2], wa, wb, wc, wo, g2, m[4], m[3], m[5],
                              jnp.zeros((B, SUBLANES, F), F32), wup, fcw, fcb, wdn, fg, tm_p, 0, final)
        wn = min(WINDOW, T)
        outs_p[0].append(rows.reshape(B, T, 4, NSA_KV, NSA_HD))
        outs_p[1].append(win[:, T - wn:].reshape(B, wn, 2, NSA_KV, NSA_HD))
        outs_p[2].append(s_new)
        outs_p[3].append(h_tail[:, SUBLANES - 1])
        outs_p[4].append(rx[:, T - (RG_CONV - 1):])
        outs_p[5].append(f_tail[:, SUBLANES - (FFN_CONV - 1):])

        ms = [_ModOf(mod_s, l, i) for i in range(6)]
        (nq, rows, win, gate, rq, rk, rv, rg, rx, rgate, mg) = _inproj(xs, g1, ms[1], ms[0], w_head, w_tail, widths,
                                                                        dt_sample, NS)
        r3 = lambda a: a.reshape(DB, Q, a.shape[-1])
        o_a, win_next = _nsa_sample(page_table, r3(nq), r3(gate), r3(rows), win_t_all, r3(win), cache_t, l, cw)
        o_r, s_new = _retention(r3(rq), r3(rk), r3(rv), r3(rg), cos_s, sin_s, state_ret[l].astype(F32), gn,
                                math.gcd(DB, SUBLANES), F32)
        cbuf = state_rglru_conv[l]
        st = jnp.stack([_seg_fill(cbuf, k, Q) for k in range(1, RG_CONV)])
        h0 = jnp.pad(state_rglru_h[l].astype(F32)[:, None, :], ((0, 0), (0, Q - 1), (0, 0))).reshape(1, NS, W)
        o_c, h_all = _rglru(rx, rgate, st, h0, rw, NS, Q)
        fbuf = state_ffn_conv[l]
        fst = jnp.stack([_seg_fill(fbuf, k, Q) for k in range(1, FFN_CONV)])
        xs, g_all = _mix_ffn(xs, o_a.reshape(1, NS, NQ), o_r.reshape(1, NS, RW), o_c, mg, ms[2], wa, wb, wc, wo,
                             g2, ms[4], ms[3], ms[5], fst, wup, fcw, fcb, wdn, fg, NS, Q, final)
        outs_s[0].append(rows.reshape(DB, Q, 4, NSA_KV, NSA_HD))
        outs_s[1].append(win_next)
        outs_s[2].append(s_new)
        outs_s[3].append(h_all.reshape(DB, Q, W)[:, Q - 1])
        outs_s[4].append(jnp.concatenate([cbuf, rx.reshape(DB, Q, W)], axis=1)[:, Q:])
        outs_s[5].append(jnp.concatenate([fbuf, g_all.reshape(DB, Q, F)], axis=1)[:, Q:])

    sp = [jnp.stack(a) for a in outs_p]
    ss = [jnp.stack(a) for a in outs_s]
    ss[1] = jnp.transpose(ss[1].reshape(L, DB, 2, NSA_KV, NSA_HD, wbuf_len), (0, 1, 5, 2, 3, 4))
    return (xp, xs.reshape(DB, Q, D), sp[0], ss[0], sp[1], ss[1], sp[2], ss[2],
            sp[3], ss[3], sp[4], ss[4], sp[5], ss[5])
```

```python
import functools
import math

import numpy as np
import jax
import jax.numpy as jnp
from jax import lax
from jax.experimental import pallas as pl
from jax.experimental.pallas import tpu as pltpu

F32 = jnp.float32
BF16 = jnp.bfloat16

NSA_KV = 2
NSA_HPG = 4
NSA_HD = 64
CMP_STRIDE = 16
CMP_BLOCK = 32
SEL_BLOCK = 64
SEL_TOPK = 16
SEL_FORCE = 1e4
WINDOW = 512
QBLOCK = 256
RET_HEADS = 4
RET_DK = 128
RET_CHUNK = 128
ROPE_BASE = 10000.0
RG_CONV = 4
RG_C = 8.0
FFN_CONV = 3
EPS = 1e-6

NEG = -1e30
BIG = float(2 ** 60)
LOG2E = 1.4426950408889634
SUBLANES = 8
LANES = 128
VMEM_LIMIT_V7X = 56 * 1024 * 1024
SEL_CHUNK = 512
CAST_CHUNK = 1024
PAGE_UNROLL = 8
MXU_DEPTH_V7X = 256


def _cparams(n_grid):
    return pltpu.CompilerParams(dimension_semantics=("arbitrary",) * n_grid,
                                vmem_limit_bytes=VMEM_LIMIT_V7X)


def _bdot(a, b):
    return jnp.dot(a.astype(BF16), b.astype(BF16), preferred_element_type=F32)


def _bdot_nt(a, b):
    return lax.dot_general(a.astype(BF16), b.astype(BF16), (((1,), (1,)), ((), ())),
                           preferred_element_type=F32)


def _split(a):
    hi = a.astype(BF16)
    lo = (a - hi.astype(F32)).astype(BF16)
    return hi, lo


def _dot3(a, b):
    ah, al = _split(a)
    bh, bl = _split(b)
    d = functools.partial(jnp.dot, preferred_element_type=F32)
    return d(ah, bh) + d(al, bh) + d(ah, bl)


def _sigmoid(x):
    return 0.5 * jnp.tanh(0.5 * x) + 0.5


def _gelu(x):
    return 0.5 * x * (1.0 + jnp.tanh(0.7978845608028654 * (x + 0.044715 * (x * x * x))))


def _rms_mod(x, g, sc, sh):
    y = x * lax.rsqrt(jnp.mean(x * x, axis=-1, keepdims=True) + EPS)
    return (y * g) * (1.0 + sc) + sh


def _ada_kernel(c_ref, w_ref, b_ref, o_ref):
    c = c_ref[...]
    o_ref[0] = _dot3(c * _sigmoid(c), w_ref[0]) + b_ref[0]


def _ada(c_all, w_ada, b_ada):
    L, D, E = w_ada.shape
    n = c_all.shape[0]
    tn = 1536 if E % 1536 == 0 else E
    return pl.pallas_call(
        _ada_kernel,
        grid=(L, E // tn),
        in_specs=[pl.BlockSpec((n, D), lambda l, j: (0, 0)),
                  pl.BlockSpec((1, D, tn), lambda l, j: (l, 0, j)),
                  pl.BlockSpec((1, 1, tn), lambda l, j: (l, 0, j))],
        out_specs=pl.BlockSpec((1, n, tn), lambda l, j: (l, 0, j)),
        out_shape=jax.ShapeDtypeStruct((L, n, E), F32),
        compiler_params=_cparams(2),
        name="ada_mod",
    )(c_all, w_ada, b_ada.reshape(L, 1, E))


class _LayerOf:
    def __init__(self, arr, layer):
        self.arr, self.layer, self.shape = arr, layer, arr.shape[1:]


class _ModOf:
    def __init__(self, arr, layer, idx):
        self.arr, self.layer, self.idx = arr, layer, idx


def _mod_spec(m, tm, d):
    l, i = m.layer, m.idx
    if m.arr.ndim == 4:
        return pl.BlockSpec((None, None, 1, d), lambda g, t: (l, g, 0, i))
    return pl.BlockSpec((None, tm, d), lambda g, t: (l, t, i))


def _pspec(p):
    if isinstance(p, _LayerOf):
        nd, l = len(p.shape), p.layer
        return pl.BlockSpec((None,) + tuple(p.shape), lambda *a: (l,) + (0,) * nd, pipeline_mode=pl.Buffered(1))
    nd = p.ndim
    return pl.BlockSpec(p.shape, lambda *a: (0,) * nd, pipeline_mode=pl.Buffered(1))


def _parg(p):
    return p.arr if isinstance(p, _LayerOf) else p


def _inproj_kernel(x_ref, g_ref, sc_ref, sh_ref, w_ref, *o_refs, segs):
    h = _rms_mod(x_ref[0], g_ref[...], sc_ref[...], sh_ref[...]).astype(BF16)
    for (off, wd), o_ref in zip(segs, o_refs):
        o_ref[0] = jnp.dot(h, w_ref[:, off:off + wd], preferred_element_type=F32).astype(o_ref.dtype)


def _inproj(x, g, sc, sh, w, widths, dtypes, tm):
    G, Tg, D = x.shape
    segs, off = [], 0
    for wd in widths:
        segs.append((off, wd))
        off += wd
    return pl.pallas_call(
        functools.partial(_inproj_kernel, segs=tuple(segs)),
        grid=(G, Tg // tm),
        in_specs=[pl.BlockSpec((1, tm, D), lambda g_, t: (g_, t, 0)),
                  _pspec(g), _mod_spec(sc, tm, D), _mod_spec(sh, tm, D), _pspec(w)],
        out_specs=[pl.BlockSpec((1, tm, wd), lambda g_, t: (g_, t, 0)) for wd in widths],
        out_shape=[jax.ShapeDtypeStruct((G, Tg, wd), dt) for wd, dt in zip(widths, dtypes)],
        compiler_params=_cparams(2),
        name="in_proj",
    )(x, _parg(g), sc.arr, sh.arr, _parg(w))


def _compress_bias(w1_ref, pe_ref, bias_sc):
    for c in range(2):
        halves = []
        for h in range(2):
            pe_rows = jnp.broadcast_to(pe_ref[c, h:h + 1, :], (SUBLANES, pe_ref.shape[2]))
            halves.append(_bdot(pe_rows, w1_ref[c, :, h * 2 * LANES:(h + 1) * 2 * LANES]))
        bias_sc[c] = jnp.concatenate(halves, axis=1)


def _compress_x(xrefs, nch, w1_ref, bias_sc, b1_ref, w2_ref):
    outs = []
    for c in range(2):
        lhs = jnp.concatenate([xrefs[c][pl.ds(p, nch, stride=CMP_STRIDE), :].astype(BF16)
                               for p in range(CMP_STRIDE)], axis=1)
        outs.append(_compress_one(c, lhs, nch, w1_ref, bias_sc, b1_ref, w2_ref))
    return outs


def _compress_one(c, lhs, nch, w1_ref, bias_sc, b1_ref, w2_ref):
    last = lax.broadcasted_iota(jnp.int32, (nch, 1), 0) == nch - 1
    acc = jnp.dot(lhs, w1_ref[c], preferred_element_type=F32) + bias_sc[c, 0:1, :]
    lo = acc[:, :2 * LANES]
    hi = acc[:, 2 * LANES:]
    hi_next = jnp.where(last, 0.0, pltpu.roll(hi, nch - 1, 0))
    hid = _gelu(lo + hi_next + b1_ref[c])
    return _bdot(hid, w2_ref[c])


def _compress_prompt_kernel(krows_ref, vrows_ref, w1_ref, pe_ref, b1_ref, w2_ref, kc_ref, vc_ref, bias_sc, *, nch):
    @pl.when(pl.program_id(0) == 0)
    def _():
        _compress_bias(w1_ref, pe_ref, bias_sc)

    kc, vc = _compress_x((krows_ref.at[0], vrows_ref.at[0]), nch, w1_ref, bias_sc, b1_ref, w2_ref)
    kc_ref[0] = kc
    vc_ref[0] = vc


def _compress_prompt(rows, cw):
    B, T, _ = rows.shape
    nch = T // CMP_STRIDE
    w1, pe, b1, w2 = cw
    return pl.pallas_call(
        functools.partial(_compress_prompt_kernel, nch=nch),
        grid=(B,),
        in_specs=[pl.BlockSpec((1, T, LANES), lambda b: (b, 0, 0)),
                  pl.BlockSpec((1, T, LANES), lambda b: (b, 0, 1)),
                  _pspec(w1), _pspec(pe), _pspec(b1), _pspec(w2)],
        out_specs=[pl.BlockSpec((1, nch, LANES), lambda b: (b, 0, 0))] * 2,
        out_shape=[jax.ShapeDtypeStruct((B, nch, LANES), F32)] * 2,
        scratch_shapes=[pltpu.VMEM((2, SUBLANES, 4 * LANES), F32)],
        compiler_params=_cparams(1),
        name="nsa_compress_prompt",
    )(rows, rows, _parg(w1), _parg(pe), _parg(b1), _parg(w2))


def _rep_all(a):
    return jnp.concatenate([a] * (NSA_KV * NSA_HPG), axis=0)


def _rep_heads(a, Qb):
    return jnp.concatenate([a[:Qb]] * NSA_HPG + [a[Qb:]] * NSA_HPG, axis=0)


def _nsa_front(qblk, q0, Qb, kc, vc, n_cmp, n_sel):
    R2 = 2 * Qb
    npad = kc.shape[0]
    lane = lax.broadcasted_iota(jnp.int32, (Qb, LANES), 1)
    lo_half = lane < NSA_HD
    scale = NSA_HD ** -0.5 * LOG2E
    pieces = []
    for hd in range(NSA_KV * NSA_HPG):
        k, pair, half = hd // NSA_HPG, hd // 2, hd % 2
        sl = qblk[:, pair * LANES:(pair + 1) * LANES] * scale
        if half != k:
            sl = pltpu.roll(sl, NSA_HD, 1)
        pieces.append(jnp.where(lo_half if k == 0 else jnp.logical_not(lo_half), sl, 0.0))
    qs = jnp.concatenate(pieces, axis=0).astype(BF16)
    qp1 = q0 + lax.broadcasted_iota(jnp.int32, (Qb, 1), 0)

    n_idx = lax.broadcasted_iota(jnp.int32, (1, npad), 1)
    visible = (n_idx * CMP_STRIDE + (CMP_BLOCK - 1) <= qp1) & (n_idx < n_cmp)
    s = _bdot_nt(qs, kc) + _rep_all(jnp.where(visible, 0.0, -BIG))
    e = jnp.exp2(s - jnp.max(s, axis=-1, keepdims=True))
    any_visible = _rep_all((qp1 >= CMP_BLOCK - 1) & (n_cmp > 0))
    p_c = e * jnp.where(any_visible, 1.0 / jnp.sum(e, axis=-1, keepdims=True), 0.0)
    o_c = _bdot(p_c, vc)

    psum = []
    for k in range(NSA_KV):
        acc = p_c[(k * NSA_HPG) * Qb:(k * NSA_HPG + 1) * Qb]
        for r in range(1, NSA_HPG):
            acc = acc + p_c[(k * NSA_HPG + r) * Qb:(k * NSA_HPG + r + 1) * Qb]
        psum.append(acc)
    psum = jnp.concatenate(psum + [jnp.zeros((LANES - R2, npad), F32)], axis=0)
    nsr = -(-n_sel // SUBLANES) * SUBLANES
    sj = lax.broadcasted_iota(jnp.int32, (nsr, npad), 0) * SEL_BLOCK
    ci = lax.broadcasted_iota(jnp.int32, (nsr, npad), 1) * CMP_STRIDE
    selmap = jnp.where((ci < sj + SEL_BLOCK) & (ci + CMP_BLOCK > sj), 1.0, 0.0).astype(BF16)
    ph, plo = _split(psum)
    imp = _bdot_nt(selmap, ph) + _bdot_nt(selmap, plo)
    col_pos = q0 + (lax.broadcasted_iota(jnp.int32, (1, LANES), 1) & (Qb - 1))
    return qs, qp1, o_c, _select_blocks(imp, col_pos, n_sel)


def _select_blocks(imp, qpos, n_sel):
    nsr, ncol = imp.shape
    j = lax.broadcasted_iota(jnp.int32, (nsr, ncol), 0)
    jf = j.astype(F32)
    cur = qpos >> 6
    forced = (j == 0) | (j == cur) | (j == cur - 1)
    imp = jnp.where(forced, SEL_FORCE, imp)
    imp = jnp.where(j * SEL_BLOCK <= qpos, imp, -SEL_FORCE)
    imp = jnp.where(j < n_sel, imp, NEG)

    def pick(_, carry):
        imp_c, sel_c = carry
        m = jnp.max(imp_c, axis=0, keepdims=True)
        first = jnp.min(jnp.where(imp_c == m, jf, float(nsr)), axis=0, keepdims=True)
        hit = jf == first
        return jnp.where(hit, NEG, imp_c), jnp.where(hit, 1.0, sel_c)

    _, sel = lax.fori_loop(0, min(SEL_TOPK, n_sel), pick, (imp, jnp.zeros((nsr, ncol), F32)), unroll=True)
    return sel


def _nsa_combine(gate, o_c, o_s, o_w, Qb, o_ref):
    lo_half = lax.broadcasted_iota(jnp.int32, (Qb, LANES), 1) < NSA_HD
    g = _sigmoid(gate)
    for pair in range(NSA_KV * NSA_HPG // 2):
        halves = []
        for half, hd in enumerate((2 * pair, 2 * pair + 1)):
            k = hd // NSA_HPG
            rs = slice(hd * Qb, (hd + 1) * Qb)
            c = hd * 3
            o = g[:, c:c + 1] * o_c[rs] + g[:, c + 1:c + 2] * o_s[rs] + g[:, c + 2:c + 3] * o_w[rs]
            halves.append(o if half == k else pltpu.roll(o, NSA_HD, 1))
        o_ref[0, :, pair * LANES:(pair + 1) * LANES] = jnp.where(lo_half, halves[0], halves[1])


def _block_columns(k0, n):
    key = k0 + lax.broadcasted_iota(jnp.int32, (n, LANES), 0)
    blk = lax.broadcasted_iota(jnp.int32, (n, LANES), 1)
    return jnp.where((key >> 6) == blk, BIG, 0.0).astype(BF16)


def _block_rows(k0, n):
    key = k0 + lax.broadcasted_iota(jnp.int32, (LANES, n), 1)
    blk = lax.broadcasted_iota(jnp.int32, (LANES, n), 0)
    return jnp.where((key >> 6) == blk, BIG, 0.0).astype(BF16)


def _lanes_all(a):
    return jnp.concatenate([a] * (NSA_KV * NSA_HPG), axis=1)


def _lanes_heads(a, Qb):
    return jnp.concatenate([a[:, :Qb]] * NSA_HPG + [a[:, Qb:]] * NSA_HPG, axis=1)


def _softmax_cols(s):
    e = jnp.exp2(s - jnp.max(s, axis=0, keepdims=True))
    return e, 1.0 / jnp.sum(e, axis=0, keepdims=True)


def _nsa_prompt_kernel(q_ref, gate_ref, kc_ref, vc_ref, rows_ref, win_ref, o_ref, kaug, vt,
                       *, T, Qb, n_cmp, n_sel, wl):
    i = pl.program_id(1)
    q0 = i * Qb
    R = NSA_KV * NSA_HPG * Qb
    R2 = NSA_KV * Qb

    @pl.when(i == 0)
    def _():
        def pack(c, carry):
            r0 = pl.multiple_of(c * SEL_CHUNK, SEL_CHUNK)
            kaug[pl.ds(r0, SEL_CHUNK), 0:LANES] = rows_ref[0, pl.ds(r0, SEL_CHUNK), 2 * LANES:3 * LANES].astype(BF16)
            kaug[pl.ds(r0, SEL_CHUNK), LANES:2 * LANES] = _block_columns(r0, SEL_CHUNK)
            vt[:, pl.ds(r0, SEL_CHUNK)] = rows_ref[0, pl.ds(r0, SEL_CHUNK), 3 * LANES:4 * LANES].T.astype(BF16)
            return carry
        lax.fori_loop(0, T // SEL_CHUNK, pack, 0)

    top = lax.broadcasted_iota(jnp.int32, (LANES, Qb), 0) < NSA_HD
    scale = NSA_HD ** -0.5 * LOG2E
    qblk = q_ref[0]
    cols = []
    for hd in range(NSA_KV * NSA_HPG):
        k, pair, half = hd // NSA_HPG, hd // 2, hd % 2
        t = (qblk[:, pair * LANES:(pair + 1) * LANES] * scale).T
        if half != k:
            t = jnp.concatenate([t[NSA_HD:], t[:NSA_HD]], axis=0)
        cols.append(jnp.where(top if k == 0 else jnp.logical_not(top), t, 0.0))
    qs = jnp.concatenate(cols, axis=1).astype(BF16)
    qlane = q0 + lax.broadcasted_iota(jnp.int32, (1, Qb), 1)
    qpos = _lanes_all(qlane)

    kc = kc_ref[0]
    npad = kc.shape[0]
    n_idx = lax.broadcasted_iota(jnp.int32, (npad, 1), 0)
    visible = (n_idx * CMP_STRIDE + (CMP_BLOCK - 1) <= qlane) & (n_idx < n_cmp)
    e_c, inv_c = _softmax_cols(_bdot(kc, qs) + _lanes_all(jnp.where(visible, 0.0, -BIG)))
    any_visible = _lanes_all((qlane >= CMP_BLOCK - 1) & (n_cmp > 0))
    p_c = e_c * jnp.where(any_visible, inv_c, 0.0)
    o_c = _bdot(vc_ref[0].T, p_c)

    psum = []
    for k in range(NSA_KV):
        acc = p_c[:, (k * NSA_HPG) * Qb:(k * NSA_HPG + 1) * Qb]
        for r in range(1, NSA_HPG):
            acc = acc + p_c[:, (k * NSA_HPG + r) * Qb:(k * NSA_HPG + r + 1) * Qb]
        psum.append(acc)
    psum = jnp.concatenate(psum, axis=1)
    nsr = -(-n_sel // SUBLANES) * SUBLANES
    sj = lax.broadcasted_iota(jnp.int32, (nsr, npad), 0) * SEL_BLOCK
    ci = lax.broadcasted_iota(jnp.int32, (nsr, npad), 1) * CMP_STRIDE
    selmap = jnp.where((ci < sj + SEL_BLOCK) & (ci + CMP_BLOCK > sj), 1.0, 0.0).astype(BF16)
    ph, plo = _split(psum)
    imp = (jnp.dot(selmap, ph, preferred_element_type=F32)
           + jnp.dot(selmap, plo, preferred_element_type=F32))
    sel = _select_blocks(imp, jnp.concatenate([qlane] * NSA_KV, axis=1), n_sel)
    selm =jnp.concatenate([sel - 1.0, jnp.zeros((LANES - nsr, R2), F32)], axis=0) if nsr < LANES else sel - 1.0
    qaug = jnp.concatenate([qs, _lanes_heads(selm, Qb).astype(BF16)], axis=0)

    def update(s, vcols, carry):
        m, l, acc = carry
        m_new = jnp.maximum(m, jnp.max(s, axis=0, keepdims=True))
        alpha = jnp.exp2(m - m_new)
        p = jnp.exp2(s - m_new)
        l = alpha * l + jnp.sum(p, axis=0, keepdims=True)
        acc = alpha * acc + jnp.dot(vcols, p.astype(BF16), preferred_element_type=F32)
        return m_new, l, acc

    def scores(k0):
        return jnp.dot(kaug[pl.ds(k0, SEL_CHUNK), :], qaug, preferred_element_type=F32)

    def full_chunk(c, carry):
        k0 = pl.multiple_of(c * SEL_CHUNK, SEL_CHUNK)
        return update(scores(k0), vt[:, pl.ds(k0, SEL_CHUNK)], carry)

    n_full = q0 // SEL_CHUNK
    init = (jnp.full((1, R), -4.0 * BIG, F32), jnp.zeros((1, R), F32), jnp.zeros((LANES, R), F32))
    carry = lax.fori_loop(0, n_full, full_chunk, init)
    k0 = pl.multiple_of(n_full * SEL_CHUNK, SEL_CHUNK)
    kpos = k0 + lax.broadcasted_iota(jnp.int32, (SEL_CHUNK, 1), 0)
    s_diag = jnp.where(kpos <= qpos, scores(k0), -2.0 * BIG)
    _, l_s, acc_s = update(s_diag, vt[:, pl.ds(k0, SEL_CHUNK)], carry)
    o_s = acc_s * (1.0 / l_s)

    ws = pl.multiple_of(jnp.maximum(q0 - WINDOW, 0), LANES)
    wk = win_ref[0, pl.ds(ws, wl), 0:LANES]
    wv = win_ref[0, pl.ds(ws, wl), LANES:2 * LANES]
    dpos = qlane - (ws + lax.broadcasted_iota(jnp.int32, (wl, 1), 0))
    e_w, inv_w = _softmax_cols(_bdot(wk, qs) + _lanes_all(jnp.where((dpos >= 0) & (dpos <= WINDOW), 0.0, -BIG)))
    o_w = _bdot(wv.T, e_w) * inv_w

    g = _sigmoid(gate_ref[0]).T
    for pair in range(NSA_KV * NSA_HPG // 2):
        rows = []
        for hd in (2 * pair, 2 * pair + 1):
            k = hd // NSA_HPG
            cs = slice(hd * Qb, (hd + 1) * Qb)
            c = hd * 3
            o = g[c:c + 1, :] * o_c[:, cs] + g[c + 1:c + 2, :] * o_s[:, cs] + g[c + 2:c + 3, :] * o_w[:, cs]
            rows.append(o[k * NSA_HD:(k + 1) * NSA_HD])
        o_ref[0, :, pair * LANES:(pair + 1) * LANES] = jnp.concatenate(rows, axis=0).T.astype(o_ref.dtype)


def _nsa_prompt(nq, gate, kc, vc, rows, win):
    B, T, HD = nq.shape
    Qb = QBLOCK
    nch = kc.shape[1]
    n_sel = -(-T // SEL_BLOCK)
    wl = WINDOW + Qb
    assert T % SEL_CHUNK == 0 and T >= wl and SEL_TOPK <= n_sel <= LANES
    return pl.pallas_call(
        functools.partial(_nsa_prompt_kernel, T=T, Qb=Qb, n_cmp=nch - 1, n_sel=n_sel, wl=wl),
        grid=(B, T // Qb),
        in_specs=[pl.BlockSpec((1, Qb, HD), lambda b, i: (b, i, 0)),
                  pl.BlockSpec((1, Qb, LANES), lambda b, i: (b, i, 0)),
                  pl.BlockSpec((1, nch, LANES), lambda b, i: (b, 0, 0)),
                  pl.BlockSpec((1, nch, LANES), lambda b, i: (b, 0, 0)),
                  pl.BlockSpec((1, T, 4 * LANES), lambda b, i: (b, 0, 0)),
                  pl.BlockSpec((1, T, 2 * LANES), lambda b, i: (b, 0, 0))],
        out_specs=pl.BlockSpec((1, Qb, HD), lambda b, i: (b, i, 0)),
        out_shape=jax.ShapeDtypeStruct((B, T, HD), BF16),
        scratch_shapes=[pltpu.VMEM((T, 2 * LANES), BF16), pltpu.VMEM((LANES, T), BF16)],
        compiler_params=_cparams(2),
        name="nsa_attn_prompt",
    )(nq, gate, kc, vc, rows, win)


def _pages_copy(cache_hbm, layer, page, r0, dst, j, sem):
    n = cache_hbm.shape[-1]
    return pltpu.make_async_copy(cache_hbm.at[layer, page, pl.ds(r0, 2)],
                                 dst.at[:, :, pl.ds(pl.multiple_of(j * n, n), n)], sem)


def _pages_start(pt_ref, b, cache_hbm, layer, r0, dst, sem, npages):
    def issue(j, carry):
        _pages_copy(cache_hbm, layer, pt_ref[b, j], r0, dst, j, sem).start()
        return carry
    lax.fori_loop(0, npages, issue, 0, unroll=PAGE_UNROLL)


def _pages_wait(cache_hbm, layer, r0, dst, sem, npages):
    def wait(j, carry):
        _pages_copy(cache_hbm, layer, 0, r0, dst, j, sem).wait()
        return carry
    lax.fori_loop(0, npages, wait, 0, unroll=PAGE_UNROLL)


def _softmax2(s1, s2, mask2):
    s2 = jnp.where(mask2, s2, -2.0 * BIG)
    m = jnp.maximum(jnp.max(s1, axis=-1, keepdims=True), jnp.max(s2, axis=-1, keepdims=True))
    e1 = jnp.exp2(s1 - m)
    e2 = jnp.exp2(s2 - m)
    den = jnp.sum(e1, axis=-1, keepdims=True) + jnp.sum(e2, axis=-1, keepdims=True)
    return e1, e2, 1.0 / den


def _nsa_sample_kernel(pt_ref, q_ref, gate_ref, rows_ref, wt_ref, wnew_ref, cache_hbm,
                       w1_ref, pe_ref, b1_ref, w2_ref, o_ref, wout_ref,
                       cmpbuf, selbuf, xk, xv, kaug, vt, newbuf, wnewbuf, bias_sc, perm_sc, csem, ssem,
                       *, layer, npages, P, Q, wb, nb, n_sel):
    b = pl.program_id(0)
    nseq = pl.num_programs(0)
    page = cache_hbm.shape[-1]
    nch = P // CMP_STRIDE

    @pl.when(b == 0)
    def _():
        _pages_start(pt_ref, 0, cache_hbm, layer, 0, cmpbuf, csem, npages)
        _pages_start(pt_ref, 0, cache_hbm, layer, 2, selbuf, ssem, npages)
        _compress_bias(w1_ref, pe_ref, bias_sc)

        def blocks(c, carry):
            c0 = pl.multiple_of(c * CAST_CHUNK, CAST_CHUNK)
            kaug[LANES:2 * LANES, pl.ds(c0, CAST_CHUNK)] = _block_rows(c0, CAST_CHUNK)
            return carry
        lax.fori_loop(0, P // CAST_CHUNK, blocks, 0)
        newbuf[...] = jnp.zeros(newbuf.shape, F32)
        wnewbuf[...] = jnp.zeros(wnewbuf.shape, F32)
        src = lax.broadcasted_iota(jnp.int32, (2 * page, 2 * page), 0)
        dst_col = lax.broadcasted_iota(jnp.int32, (2 * page, 2 * page), 1)
        t = src & (page - 1)
        want = (src - t) + (t % CMP_STRIDE) * (page // CMP_STRIDE) + t // CMP_STRIDE
        perm_sc[...] = jnp.where(dst_col == want, 1.0, 0.0).astype(BF16)

    _pages_wait(cache_hbm, layer, 0, cmpbuf, csem, npages)
    per_page = page // CMP_STRIDE
    for c, dst in enumerate((xk, xv)):
        for jp in range(npages // 2):
            cols = cmpbuf[c, :, 2 * jp * page:2 * (jp + 1) * page].astype(BF16)
            regrouped = jnp.dot(cols, perm_sc[...], preferred_element_type=F32)
            for half in range(2):
                tok = regrouped[:, half * page:(half + 1) * page].T
                r0 = (2 * jp + half) * per_page
                for p in range(CMP_STRIDE):
                    dst[r0:r0 + per_page, p * LANES:(p + 1) * LANES] = tok[p * per_page:(p + 1) * per_page, :]
    kc = _compress_one(0, xk[...].astype(BF16), nch, w1_ref, bias_sc, b1_ref, w2_ref)
    vc = _compress_one(1, xv[...].astype(BF16), nch, w1_ref, bias_sc, b1_ref, w2_ref)

    @pl.when(b + 1 < nseq)
    def _():
        _pages_start(pt_ref, b + 1, cache_hbm, layer, 0, cmpbuf, csem, npages)

    _pages_wait(cache_hbm, layer, 2, selbuf, ssem, npages)

    def pack(c, carry):
        c0 = pl.multiple_of(c * CAST_CHUNK, CAST_CHUNK)
        kaug[0:LANES, pl.ds(c0, CAST_CHUNK)] = selbuf[0, :, pl.ds(c0, CAST_CHUNK)].astype(BF16)
        vt[:, pl.ds(c0, CAST_CHUNK)] = selbuf[1, :, pl.ds(c0, CAST_CHUNK)].astype(BF16)
        return carry
    lax.fori_loop(0, P // CAST_CHUNK, pack, 0)

    @pl.when(b + 1 < nseq)
    def _():
        _pages_start(pt_ref, b + 1, cache_hbm, layer, 2, selbuf, ssem, npages)

    newbuf[0:Q, :] = rows_ref[0, :, 2 * LANES:4 * LANES]
    wnewbuf[0:Q, :] = wnew_ref[0]

    qs, qp1, o_c, sel_t = _nsa_front(q_ref[0], P, Q, kc, vc, nch - 1, n_sel)
    qpos = _rep_all(qp1)
    lane = lax.broadcasted_iota(jnp.int32, (1, LANES), 1)
    new_pos = P + lane
    is_new = lane < Q

    def as_rows(blk):
        if blk.shape[0] < LANES:
            blk = jnp.concatenate([blk, jnp.zeros((LANES - blk.shape[0], LANES), F32)], axis=0)
        return blk.T[0:NSA_KV * Q]
    sel_rows = as_rows(sel_t[0:min(LANES, sel_t.shape[0])])
    sel_new = as_rows(sel_t[nb:nb + SUBLANES])[:, 0:1]

    qaug = jnp.concatenate([qs, _rep_heads(sel_rows - 1.0, Q).astype(BF16)], axis=1)
    s_past = jnp.dot(qaug, kaug[...], preferred_element_type=F32)
    s_new = _bdot_nt(qs, newbuf[:, 0:LANES])
    new_ok = is_new & (new_pos <= qpos) & (_rep_heads(sel_new, Q) > 0.5)
    e1, e2, inv = _softmax2(s_past, s_new, new_ok)
    o_s = (lax.dot_general(e1.astype(BF16), vt[...], (((1,), (1,)), ((), ())), preferred_element_type=F32)
           + _bdot(e2, newbuf[:, LANES:2 * LANES])) * inv

    dpast = qp1 - ((P - wb) + lax.broadcasted_iota(jnp.int32, (1, wb), 1))
    s_wp = (jnp.dot(qs, wt_ref[0, 0].astype(BF16), preferred_element_type=F32)
            + _rep_all(jnp.where((dpast >= 0) & (dpast <= WINDOW), 0.0, -BIG)))
    s_wn = _bdot_nt(qs, wnewbuf[:, 0:LANES])
    dnew = qpos - new_pos
    e1, e2, inv = _softmax2(s_wp, s_wn, is_new & (dnew >= 0) & (dnew <= WINDOW))
    o_w = (lax.dot_general(e1.astype(BF16), wt_ref[0, 1].astype(BF16), (((1,), (1,)), ((), ())),
                           preferred_element_type=F32)
           + _bdot(e2, wnewbuf[:, LANES:2 * LANES])) * inv

    _nsa_combine(gate_ref[0], o_c, o_s, o_w, Q, o_ref)

    for kv in range(2):
        new_cols = wnewbuf[:, kv * LANES:(kv + 1) * LANES].T[:, 0:Q]
        wout_ref[0, kv] = jnp.concatenate([wt_ref[0, kv][:, Q:], new_cols], axis=1)


def _nsa_sample(page_table, nq, gate, rows, win_t, win_new, cache_t, layer, cw):
    DB, Q, HD = nq.shape
    npages = page_table.shape[1]
    page = cache_t.shape[-1]
    P = npages * page
    wb = win_t.shape[-1]
    nb = P // SEL_BLOCK
    n_sel = -(-(P + Q) // SEL_BLOCK)
    w1, pe, b1, w2 = cw
    assert Q == SUBLANES and P % SEL_BLOCK == 0 and Q <= SEL_BLOCK and nb <= LANES and n_sel >= SEL_TOPK
    assert P % CAST_CHUNK == 0 and (P + Q) // CMP_STRIDE == P // CMP_STRIDE and page == LANES
    assert npages % 2 == 0 and page // CMP_STRIDE == SUBLANES and wb >= Q
    bs = lambda shape: pl.BlockSpec((1,) + shape, lambda b, pt: (b,) + (0,) * len(shape))
    win_spec = pl.BlockSpec((None, 1, 2, LANES, wb), lambda b, pt: (layer, b, 0, 0, 0))
    grid_spec = pltpu.PrefetchScalarGridSpec(
        num_scalar_prefetch=1, grid=(DB,),
        in_specs=[bs((Q, HD)), bs((Q, LANES)), bs((Q, 4 * LANES)), win_spec, bs((Q, 2 * LANES)),
                  pl.BlockSpec(memory_space=pl.ANY),
                  _pspec(w1), _pspec(pe), _pspec(b1), _pspec(w2)],
        out_specs=[bs((Q, HD)), bs((2, LANES, wb))],
        scratch_shapes=[pltpu.VMEM((2, LANES, P), F32), pltpu.VMEM((2, LANES, P), F32),
                        pltpu.VMEM((P // CMP_STRIDE, CMP_STRIDE * LANES), F32),
                        pltpu.VMEM((P // CMP_STRIDE, CMP_STRIDE * LANES), F32),
                        pltpu.VMEM((2 * LANES, P), BF16), pltpu.VMEM((LANES, P), BF16),
                        pltpu.VMEM((LANES, 2 * LANES), F32), pltpu.VMEM((LANES, 2 * LANES), F32),
                        pltpu.VMEM((2, SUBLANES, 4 * LANES), F32), pltpu.VMEM((2 * page, 2 * page), BF16),
                        pltpu.SemaphoreType.DMA(()), pltpu.SemaphoreType.DMA(())])
    return pl.pallas_call(
        functools.partial(_nsa_sample_kernel, layer=layer, npages=npages, P=P, Q=Q, wb=wb, nb=nb, n_sel=n_sel),
        grid_spec=grid_spec,
        out_shape=[jax.ShapeDtypeStruct((DB, Q, HD), F32), jax.ShapeDtypeStruct((DB, 2, LANES, wb), F32)],
        compiler_params=_cparams(1),
        name="nsa_sample",
    )(page_table, nq, gate, rows, win_t, win_new, cache_t, _parg(w1), _parg(pe), _parg(b1), _parg(w2))


def _ret_kernel(q_ref, k_ref, v_ref, g_ref, cos_ref, sin_ref, s0_ref, gn_ref, o_ref, snew_ref, s_sc,
                *, C, nC, nseq):
    c = pl.program_id(1)

    @pl.when(c == 0)
    def _():
        s_sc[...] = s0_ref[...]

    cosf = cos_ref[...]
    sinf = sin_ref[...]
    diff = (lax.broadcasted_iota(jnp.int32, (C, C), 0) - lax.broadcasted_iota(jnp.int32, (C, C), 1)).astype(F32)
    ii = lax.broadcasted_iota(jnp.int32, (C, 1), 0).astype(F32)
    half = RET_DK // 2
    for h in range(RET_HEADS):
        lg = math.log(1.0 - 2.0 ** (-5.0 - h))
        hs = slice(h * RET_DK, (h + 1) * RET_DK)
        decay = jnp.where(diff >= 0, jnp.exp(jnp.maximum(diff, 0.0) * lg), 0.0)
        cross = jnp.exp((ii + 1.0) * lg)
        kweight = jnp.exp((C - 1.0 - ii) * lg)
        for b in range(nseq):
            q = q_ref[b, :, hs]
            k = k_ref[b, :, hs]
            v = v_ref[b, :, hs]
            qr = q * cosf + pltpu.roll(q, half, 1) * sinf
            kr = (k * cosf + pltpu.roll(k, half, 1) * sinf) * (RET_DK ** -0.5)
            o_inner = _bdot(_bdot_nt(qr, kr) * decay, v)
            s_old = s_sc[b, h]
            o_cross = _bdot(qr, s_old) * cross
            kv = lax.dot_general((kr * kweight).astype(BF16), v.astype(BF16), (((0,), (0,)), ((), ())),
                                 preferred_element_type=F32)
            s_sc[b, h] = math.exp(C * lg) * s_old + kv
            o = o_inner + o_cross
            mu = jnp.mean(o, axis=-1, keepdims=True)
            var = jnp.mean(jnp.square(o - mu), axis=-1, keepdims=True)
            gate = g_ref[b, :, hs]
            o_ref[b, :, hs] = (((o - mu) * lax.rsqrt(var + EPS)) * gn_ref[:, hs]
                               * (gate * _sigmoid(gate))).astype(o_ref.dtype)

    @pl.when(c == nC - 1)
    def _():
        snew_ref[...] = s_sc[...]


def _retention(rq, rk, rv, rg, cosf, sinf, s0, gn, nseq, out_dtype):
    B, T, W = rq.shape
    C = RET_CHUNK if (T >= RET_CHUNK and T % RET_CHUNK == 0) else T
    nC = T // C
    tok = pl.BlockSpec((nseq, C, W), lambda b, c: (b, c, 0))
    tab = pl.BlockSpec((C, RET_DK), lambda b, c: (c, 0))
    st = pl.BlockSpec((nseq,) + s0.shape[1:], lambda b, c: (b, 0, 0, 0))
    return pl.pallas_call(
        functools.partial(_ret_kernel, C=C, nC=nC, nseq=nseq),
        grid=(B // nseq, nC),
        in_specs=[tok, tok, tok, tok, tab, tab, st, _pspec(gn)],
        out_specs=[tok, st],
        out_shape=[jax.ShapeDtypeStruct((B, T, W), out_dtype), jax.ShapeDtypeStruct(s0.shape, F32)],
        scratch_shapes=[pltpu.VMEM((nseq,) + s0.shape[1:], F32)],
        compiler_params=_cparams(2),
        name="retention",
    )(rq, rk, rv, rg, cosf, sinf, s0, _parg(gn))


def _shift_carry(x, k, tail8):
    r = pltpu.roll(x, k, 0)
    row8 = lax.broadcasted_iota(jnp.int32, (SUBLANES, 1), 0)
    first = jnp.where(row8 >= k, r[:SUBLANES], pltpu.roll(tail8, k, 0))
    return jnp.concatenate([first, r[SUBLANES:]], axis=0)


def _shift_seg(x, k, fill, tpos):
    return jnp.where(tpos >= k, pltpu.roll(x, k, 0), fill)


def _rglru_kernel(x_ref, gate_ref, st_ref, h0_ref, cw_ref, cb_ref, wa_ref, ba_ref, wx_ref, bx_ref, lam_ref,
                  o_ref, h_ref, tail_sc, h_sc, *, tm, seg):
    carry = seg == 0
    x = x_ref[0]
    rows = lax.broadcasted_iota(jnp.int32, (tm, 1), 0)
    if carry:
        @pl.when(pl.program_id(1) == 0)
        def _():
            tail_sc[...] = st_ref[0]
            h_sc[...] = h0_ref[0]
        tail8 = tail_sc[...]
        shifted = [_shift_carry(x, k, tail8) for k in range(1, RG_CONV)]
        tpos = rows & (SUBLANES - 1)
    else:
        tpos = rows & (seg - 1)
        shifted = [_shift_seg(x, k, st_ref[k - 1], tpos) for k in range(1, RG_CONV)]
    xc = cb_ref[...] + cw_ref[RG_CONV - 1:RG_CONV, :] * x
    for k in range(1, RG_CONV):
        xc = xc + cw_ref[RG_CONV - 1 - k:RG_CONV - k, :] * shifted[k - 1]
    r = _sigmoid(_bdot(xc, wa_ref[...]) + ba_ref[...])
    i = _sigmoid(_bdot(xc, wx_ref[...]) + bx_ref[...])
    lam = lam_ref[...]
    softplus = jnp.maximum(-lam, 0.0) + jnp.log(1.0 + jnp.exp(-jnp.abs(lam)))
    log_a = (-RG_C * r) * softplus
    a = jnp.exp(log_a)
    gap = 1.0 - a * a
    bt = jnp.where(gap > 0, gap * lax.rsqrt(gap), 0.0) * (i * xc)
    if not carry:
        bt = bt + a * h0_ref[0]
    k = 1
    while k < SUBLANES:
        ok = tpos >= k
        a_prev = jnp.where(ok, pltpu.roll(a, k, 0), 1.0)
        b_prev = jnp.where(ok, pltpu.roll(bt, k, 0), 0.0)
        bt = a * b_prev + bt
        a = a * a_prev
        k *= 2
    if carry:
        h_prev = h_sc[SUBLANES - 1:SUBLANES, :]
        groups = []
        for g0 in range(0, tm, SUBLANES):
            h_g = bt[g0:g0 + SUBLANES] + a[g0:g0 + SUBLANES] * h_prev
            groups.append(h_g)
            h_prev = h_g[SUBLANES - 1:SUBLANES, :]
        bt = jnp.concatenate(groups, axis=0)
    o_ref[0] = (bt * _gelu(gate_ref[0].astype(F32))).astype(o_ref.dtype)
    if carry:
        tail_sc[...] = x[tm - SUBLANES:]
        h_sc[...] = bt[tm - SUBLANES:]
        h_ref[0] = bt[tm - SUBLANES:]
    else:
        h_ref[0] = bt


def _rglru(rx, rgate, st, h0, rw, tm, seg):
    G, Tg, W = rx.shape
    tok = pl.BlockSpec((1, tm, W), lambda g, t: (g, t, 0))
    if seg == 0:
        st_spec = pl.BlockSpec((1, SUBLANES, W), lambda g, t: (g, 0, 0))
        h0_spec = pl.BlockSpec((1, SUBLANES, W), lambda g, t: (g, 0, 0))
        h_spec = pl.BlockSpec((1, SUBLANES, W), lambda g, t: (g, 0, 0))
        h_shape = (G, SUBLANES, W)
    else:
        st_spec = pl.BlockSpec(st.shape, lambda g, t: (0, 0, 0))
        h0_spec = tok
        h_spec = tok
        h_shape = (G, Tg, W)
    return pl.pallas_call(
        functools.partial(_rglru_kernel, tm=tm, seg=seg),
        grid=(G, Tg // tm),
        in_specs=[tok, tok, st_spec, h0_spec] + [_pspec(a) for a in rw],
        out_specs=[tok, h_spec],
        out_shape=[jax.ShapeDtypeStruct((G, Tg, W), rgate.dtype), jax.ShapeDtypeStruct(h_shape, F32)],
        scratch_shapes=[pltpu.VMEM((SUBLANES, W), F32), pltpu.VMEM((SUBLANES, W), F32)],
        compiler_params=_cparams(2),
        name="rglru",
    )(rx, rgate, st, h0, *[_parg(a) for a in rw])


def _mix_ffn_kernel(x_ref, oa_ref, or_ref, oc_ref, mg_ref, gt1_ref, wa_ref, wb_ref, wc_ref, wo_ref,
                    g_ref, sc_ref, sh_ref, gt_ref, st_ref, wup_ref, cw_ref, cb_ref, wdn_ref, fg_ref,
                    y_ref, fnew_ref, tail_sc, *, tm, seg, F, chunks, final):
    D = x_ref.shape[2]
    pa = _bdot(oa_ref[0], wa_ref[...])
    pb = _bdot(or_ref[0], wb_ref[...])
    pc = _bdot(oc_ref[0], wc_ref[...])
    gate = lambda i: _sigmoid(mg_ref[0, :, i * D:(i + 1) * D].astype(F32))
    merged = gate(0) * pa + gate(1) * pb + gate(2) * pc
    x = x_ref[0] + gt1_ref[...] * _bdot(merged, wo_ref[...])

    carry = seg == 0
    h = _rms_mod(x, g_ref[...], sc_ref[...], sh_ref[...]).astype(BF16)
    rows = lax.broadcasted_iota(jnp.int32, (tm, 1), 0)
    if carry:
        @pl.when(pl.program_id(1) == 0)
        def _():
            tail_sc[...] = st_ref[0]
    else:
        tpos = rows & (seg - 1)
    acc = jnp.zeros(x.shape, F32)
    for c0, wck in chunks:
        cs = slice(c0, c0 + wck)
        gp = jnp.dot(h, wup_ref[:, c0:c0 + wck], preferred_element_type=F32)
        val = jnp.dot(h, wup_ref[:, F + c0:F + c0 + wck], preferred_element_type=F32)
        if carry:
            tail8 = tail_sc[:, cs]
            shifted = [_shift_carry(gp, k, tail8) for k in range(1, FFN_CONV)]
            tail_sc[:, cs] = gp[tm - SUBLANES:]
            fnew_ref[0, :, cs] = gp[tm - SUBLANES:]
        else:
            shifted = [_shift_seg(gp, k, st_ref[k - 1, :, cs], tpos) for k in range(1, FFN_CONV)]
            fnew_ref[0, :, cs] = gp
        gc = cb_ref[:, cs] + cw_ref[FFN_CONV - 1:FFN_CONV, cs] * gp
        for k in range(1, FFN_CONV):
            gc = gc + cw_ref[FFN_CONV - 1 - k:FFN_CONV - k, cs] * shifted[k - 1]
        act = (gc * _sigmoid(gc)) * val
        acc = acc + _bdot(act, wdn_ref[cs, :])
    y = x + gt_ref[...] * acc
    if final:
        y = (y * lax.rsqrt(jnp.mean(y * y, axis=-1, keepdims=True) + EPS)) * fg_ref[...]
    y_ref[0] = y


def _ffn_chunks(F):
    half = -(-(F // 2) // MXU_DEPTH_V7X) * MXU_DEPTH_V7X
    return ((0, half), (half, F - half)) if 0 < half < F else ((0, F),)


def _mix_ffn(x, oa, orr, oc, mg, gt1, wa, wb, wc, wo, g, sc, sh, gt, st, wup, cw, cb, wdn, fg, tm, seg, final):
    G, Tg, D = x.shape
    F = wdn.shape[0]
    tokw = lambda w: pl.BlockSpec((1, tm, w), lambda g_, t: (g_, t, 0))
    tok = tokw(D)
    if seg == 0:
        st_spec = pl.BlockSpec((1, SUBLANES, F), lambda g_, t: (g_, 0, 0))
        fn_spec = pl.BlockSpec((1, SUBLANES, F), lambda g_, t: (g_, 0, 0))
        fn_shape = (G, SUBLANES, F)
    else:
        st_spec = pl.BlockSpec(st.shape, lambda g_, t: (0, 0, 0))
        fn_spec = pl.BlockSpec((1, tm, F), lambda g_, t: (g_, t, 0))
        fn_shape = (G, Tg, F)
    return pl.pallas_call(
        functools.partial(_mix_ffn_kernel, tm=tm, seg=seg, F=F, chunks=_ffn_chunks(F), final=final),
        grid=(G, Tg // tm),
        in_specs=[tok, tokw(oa.shape[2]), tokw(orr.shape[2]), tokw(oc.shape[2]), tokw(3 * D),
                  _mod_spec(gt1, tm, D), _pspec(wa), _pspec(wb), _pspec(wc), _pspec(wo),
                  _pspec(g), _mod_spec(sc, tm, D), _mod_spec(sh, tm, D), _mod_spec(gt, tm, D),
                  st_spec, _pspec(wup), _pspec(cw), _pspec(cb), _pspec(wdn), _pspec(fg)],
        out_specs=[tok, fn_spec],
        out_shape=[jax.ShapeDtypeStruct((G, Tg, D), F32), jax.ShapeDtypeStruct(fn_shape, F32)],
        scratch_shapes=[pltpu.VMEM((SUBLANES, F), F32)],
        compiler_params=_cparams(2),
        name="mix_ffn",
    )(x, oa, orr, oc, mg, gt1.arr, _parg(wa), _parg(wb), _parg(wc), _parg(wo),
      _parg(g), sc.arr, sh.arr, gt.arr, st, _parg(wup), _parg(cw), _parg(cb), _parg(wdn), fg)


def _block_diag(w):
    n, a, b = w.shape[-3:]
    eye = jnp.eye(n, dtype=w.dtype)
    out = jnp.einsum('ij,...iab->...iajb', eye, w)
    return out.reshape(w.shape[:-3] + (n * a, n * b))


def _seg_fill(buf, k, seg):
    B, nb, C = buf.shape
    part = jnp.concatenate([buf[:, nb - k:, :], jnp.zeros((B, seg - k, C), buf.dtype)], axis=1)
    return part.reshape(B * seg, C)


def kernel(x_prompt, x_sample, cache_nsa, cache_nsa_win, state_ret, state_rglru_h, state_rglru_conv,
           state_ffn_conv, page_table, c_prompt, c_sample, norm1_g, norm2_g, w_ada, b_ada, w_in, cmp_pe,
           cmp_w1, cmp_b1, cmp_w2, ret_gn_g, rg_conv_w, rg_conv_b, rg_w_a, rg_b_a, rg_w_x, rg_b_x, rg_lambda,
           w_br_a, w_br_b, w_br_c, w_out, ffn_w_up, ffn_conv_w, ffn_conv_b, ffn_w_down, final_norm_g):
    B, T, D = x_prompt.shape
    DB, Q, _ = x_sample.shape
    L = w_in.shape[0]
    npages = page_table.shape[1]
    page = cache_nsa.shape[2]
    P = npages * page
    NQ = NSA_KV * NSA_HPG * NSA_HD
    NKV = NSA_KV * NSA_HD
    RW = RET_HEADS * RET_DK
    W = rg_conv_w.shape[2]
    F = ffn_w_down.shape[1]
    NS = DB * Q
    wbuf_len = cache_nsa_win.shape[2]
    assert Q == SUBLANES and T >= RG_CONV and P % CMP_STRIDE == 0

    mod = _ada(jnp.concatenate([c_prompt, c_sample], axis=0), w_ada, b_ada)
    cache_t = jnp.transpose(cache_nsa, (0, 1, 3, 4, 5, 2)).reshape(L, cache_nsa.shape[1], 4, NKV, page)
    win_t_all = jnp.transpose(cache_nsa_win, (0, 1, 3, 4, 5, 2)).reshape(L, DB, 2, NKV, wbuf_len)

    half = RET_DK // 2
    freq = ROPE_BASE ** (-jnp.arange(half, dtype=F32) / half)

    def rope_tables(pos):
        ang = pos.astype(F32)[:, None] * freq[None, :]
        cos, sin = jnp.cos(ang), jnp.sin(ang)
        return jnp.concatenate([cos, cos], axis=1), jnp.concatenate([-sin, sin], axis=1)

    cos_p, sin_p = rope_tables(jnp.arange(T, dtype=jnp.int32))
    cos_s, sin_s = rope_tables(P + jnp.arange(Q, dtype=jnp.int32))

    widths = (NQ, 4 * NKV, 2 * NKV, LANES, RW, RW, RW, RW, W, W, 3 * D)
    dt_prompt = (F32,) * 9 + (BF16, BF16)
    dt_sample = (F32,) * 11
    offs = np.cumsum((0, NQ, 6 * NKV, 3 * NSA_KV * NSA_HPG, RW, RW, RW, RW, W, W, 3 * D))
    ngate = 3 * NSA_KV * NSA_HPG

    xp = x_prompt
    xs = x_sample.reshape(1, NS, D)
    outs_p = [[] for _ in range(6)]
    outs_s = [[] for _ in range(6)]
    tm_p = 512 if T % 512 == 0 else T
    tm_scan = 256 if T % 256 == 0 else T

    wi = w_in.astype(BF16)
    w_cat_all = jnp.concatenate([
        wi[:, :, offs[0]:offs[2]],
        jnp.pad(wi[:, :, offs[2]:offs[3]], ((0, 0), (0, 0), (0, LANES - ngate))),
        wi[:, :, offs[3]:]], axis=2)
    grouped = lambda w: _block_diag(jnp.broadcast_to(w[..., None, :, :], w.shape[:-2] + (NSA_KV,) + w.shape[-2:]))
    cw1_all = jnp.concatenate([grouped(cmp_w1[:, :, :CMP_STRIDE]), grouped(cmp_w1[:, :, CMP_STRIDE:])], axis=-1)
    cw1_all = cw1_all.reshape(L, 2, CMP_STRIDE * NKV, 4 * LANES).astype(BF16)
    cpe_all = jnp.tile(cmp_pe, (1, 1, 1, NSA_KV)).reshape(L, 2, 2, CMP_STRIDE * NKV)
    cb1_all = jnp.tile(cmp_b1, (1, 1, NSA_KV))[:, :, None, :]
    cw2_all = grouped(cmp_w2).astype(BF16)
    row = lambda a: a[:, None, :]
    rw_all = (rg_conv_w, row(rg_conv_b), _block_diag(rg_w_a).astype(BF16), row(rg_b_a),
              _block_diag(rg_w_x).astype(BF16), row(rg_b_x), row(rg_lambda))
    wa_all = w_br_a.astype(BF16)
    wb_all = w_br_b.astype(BF16)
    wc_all = w_br_c.astype(BF16)
    wo_all = w_out.astype(BF16)
    wup_all = ffn_w_up.astype(BF16)
    wdn_all = ffn_w_down.astype(BF16)
    g1_all, g2_all, gn_all, fcb_all = row(norm1_g), row(norm2_g), row(ret_gn_g), row(ffn_conv_b)
    fg = final_norm_g[None]
    mod_p = mod[:, :B].reshape(L, B, 1, 6 * D)
    mod_s = jnp.repeat(mod[:, B:], Q, axis=1)

    for l in range(L):
        lay = lambda a: _LayerOf(a, l)
        w_cat = lay(w_cat_all)
        cw = (lay(cw1_all), lay(cpe_all), lay(cb1_all), lay(cw2_all))
        rw = tuple(lay(a) for a in rw_all)
        wa, wb, wc, wo, wup, wdn = (lay(a) for a in (wa_all, wb_all, wc_all, wo_all, wup_all, wdn_all))
        g1, g2, gn, fcw, fcb = (lay(a) for a in (g1_all, g2_all, gn_all, ffn_conv_w, fcb_all))
        final = l == L - 1

        m = [_ModOf(mod_p, l, i) for i in range(6)]
        (nq, rows, win, gate, rq, rk, rv, rg, rx, rgate, mg) = _inproj(xp, g1, m[1], m[0], w_cat, widths,
                                                                        dt_prompt, tm_p)
        kc, vc = _compress_prompt(rows, cw)
        o_a = _nsa_prompt(nq, gate, kc, vc, rows, win)
        o_r, s_new = _retention(rq, rk, rv, rg, cos_p, sin_p,
                                jnp.zeros((B, RET_HEADS, RET_DK, RET_DK), F32), gn, B, BF16)
        zs = jnp.zeros((B, SUBLANES, W), F32)
        o_c, h_tail = _rglru(rx, rgate, zs, zs, rw, tm_scan, 0)
        xp, f_tail = _mix_ffn(xp, o_a, o_r, o_c, mg, m[2], wa, wb, wc, wo, g2, m[4], m[3], m[5],
                              jnp.zeros((B, SUBLANES, F), F32), wup, fcw, fcb, wdn, fg, tm_p, 0, final)
        wn = min(WINDOW, T)
        outs_p[0].append(rows.reshape(B, T, 4, NSA_KV, NSA_HD))
        outs_p[1].append(win[:, T - wn:].reshape(B, wn, 2, NSA_KV, NSA_HD))
        outs_p[2].append(s_new)
        outs_p[3].append(h_tail[:, SUBLANES - 1])
        outs_p[4].append(rx[:, T - (RG_CONV - 1):])
        outs_p[5].append(f_tail[:, SUBLANES - (FFN_CONV - 1):])

        ms = [_ModOf(mod_s, l, i) for i in range(6)]
        (nq, rows, win, gate, rq, rk, rv, rg, rx, rgate, mg) = _inproj(xs, g1, ms[1], ms[0], w_cat, widths,
                                                                        dt_sample, NS)
        r3 = lambda a: a.reshape(DB, Q, a.shape[-1])
        o_a, win_next = _nsa_sample(page_table, r3(nq), r3(gate), r3(rows), win_t_all, r3(win), cache_t, l, cw)
        o_r, s_new = _retention(r3(rq), r3(rk), r3(rv), r3(rg), cos_s, sin_s, state_ret[l].astype(F32), gn,
                                math.gcd(DB, SUBLANES), F32)
        cbuf = state_rglru_conv[l]
        st = jnp.stack([_seg_fill(cbuf, k, Q) for k in range(1, RG_CONV)])
        h0 = jnp.pad(state_rglru_h[l].astype(F32)[:, None, :], ((0, 0), (0, Q - 1), (0, 0))).reshape(1, NS, W)
        o_c, h_all = _rglru(rx, rgate, st, h0, rw, NS, Q)
        fbuf = state_ffn_conv[l]
        fst = jnp.stack([_seg_fill(fbuf, k, Q) for k in range(1, FFN_CONV)])
        xs, g_all = _mix_ffn(xs, o_a.reshape(1, NS, NQ), o_r.reshape(1, NS, RW), o_c, mg, ms[2], wa, wb, wc, wo,
                             g2, ms[4], ms[3], ms[5], fst, wup, fcw, fcb, wdn, fg, NS, Q, final)
        outs_s[0].append(rows.reshape(DB, Q, 4, NSA_KV, NSA_HD))
        outs_s[1].append(win_next)
        outs_s[2].append(s_new)
        outs_s[3].append(h_all.reshape(DB, Q, W)[:, Q - 1])
        outs_s[4].append(jnp.concatenate([cbuf, rx.reshape(DB, Q, W)], axis=1)[:, Q:])
        outs_s[5].append(jnp.concatenate([fbuf, g_all.reshape(DB, Q, F)], axis=1)[:, Q:])

    sp = [jnp.stack(a) for a in outs_p]
    ss = [jnp.stack(a) for a in outs_s]
    ss[1] = jnp.transpose(ss[1].reshape(L, DB, 2, NSA_KV, NSA_HD, wbuf_len), (0, 1, 5, 2, 3, 4))
    return (xp, xs.reshape(DB, Q, D), sp[0], ss[0], sp[1], ss[1], sp[2], ss[2],
            sp[3], ss[3], sp[4], ss[4], sp[5], ss[5])
```

```python
import functools
import math

import numpy as np
import jax
import jax.numpy as jnp
from jax import lax
from jax.experimental import pallas as pl
from jax.experimental.pallas import tpu as pltpu

F32 = jnp.float32
BF16 = jnp.bfloat16

NSA_KV = 2
NSA_HPG = 4
NSA_HD = 64
CMP_STRIDE = 16
CMP_BLOCK = 32
SEL_BLOCK = 64
SEL_TOPK = 16
SEL_FORCE = 1e4
WINDOW = 512
QBLOCK = 256
RET_HEADS = 4
RET_DK = 128
RET_CHUNK = 128
ROPE_BASE = 10000.0
RG_CONV = 4
RG_C = 8.0
FFN_CONV = 3
EPS = 1e-6

NEG = -1e30
BIG = float(2 ** 60)
LOG2E = 1.4426950408889634
SUBLANES = 8
LANES = 128
VMEM_LIMIT_V7X = 56 * 1024 * 1024
SEL_CHUNK = 512
CAST_CHUNK = 1024
PAGE_UNROLL = 8
MXU_DEPTH_V7X = 256


def _cparams(n_grid):
    return pltpu.CompilerParams(dimension_semantics=("arbitrary",) * n_grid,
                                vmem_limit_bytes=VMEM_LIMIT_V7X)


def _bdot(a, b):
    return jnp.dot(a.astype(BF16), b.astype(BF16), preferred_element_type=F32)


def _bdot_nt(a, b):
    return lax.dot_general(a.astype(BF16), b.astype(BF16), (((1,), (1,)), ((), ())),
                           preferred_element_type=F32)


def _split(a):
    hi = a.astype(BF16)
    lo = (a - hi.astype(F32)).astype(BF16)
    return hi, lo


def _dot3(a, b):
    ah, al = _split(a)
    bh, bl = _split(b)
    d = functools.partial(jnp.dot, preferred_element_type=F32)
    return d(ah, bh) + d(al, bh) + d(ah, bl)


def _sigmoid(x):
    return 0.5 * jnp.tanh(0.5 * x) + 0.5


def _gelu(x):
    return 0.5 * x * (1.0 + jnp.tanh(0.7978845608028654 * (x + 0.044715 * (x * x * x))))


def _rms_mod(x, g, sc, sh):
    y = x * lax.rsqrt(jnp.mean(x * x, axis=-1, keepdims=True) + EPS)
    return (y * g) * (1.0 + sc) + sh


def _ada_kernel(c_ref, w_ref, b_ref, o_ref):
    c = c_ref[...]
    o_ref[0] = _dot3(c * _sigmoid(c), w_ref[0]) + b_ref[0]


def _ada(c_all, w_ada, b_ada):
    L, D, E = w_ada.shape
    n = c_all.shape[0]
    tn = 1536 if E % 1536 == 0 else E
    return pl.pallas_call(
        _ada_kernel,
        grid=(L, E // tn),
        in_specs=[pl.BlockSpec((n, D), lambda l, j: (0, 0)),
                  pl.BlockSpec((1, D, tn), lambda l, j: (l, 0, j)),
                  pl.BlockSpec((1, 1, tn), lambda l, j: (l, 0, j))],
        out_specs=pl.BlockSpec((1, n, tn), lambda l, j: (l, 0, j)),
        out_shape=jax.ShapeDtypeStruct((L, n, E), F32),
        compiler_params=_cparams(2),
        name="ada_mod",
    )(c_all, w_ada, b_ada.reshape(L, 1, E))


class _LayerOf:
    def __init__(self, arr, layer):
        self.arr, self.layer, self.shape = arr, layer, arr.shape[1:]


class _ModOf:
    def __init__(self, arr, layer, idx):
        self.arr, self.layer, self.idx = arr, layer, idx


def _mod_spec(m, tm, d):
    l, i = m.layer, m.idx
    if m.arr.ndim == 4:
        return pl.BlockSpec((None, None, 1, d), lambda g, t: (l, g, 0, i))
    return pl.BlockSpec((None, tm, d), lambda g, t: (l, t, i))


def _pspec(p):
    if isinstance(p, _LayerOf):
        nd, l = len(p.shape), p.layer
        return pl.BlockSpec((None,) + tuple(p.shape), lambda *a: (l,) + (0,) * nd, pipeline_mode=pl.Buffered(1))
    nd = p.ndim
    return pl.BlockSpec(p.shape, lambda *a: (0,) * nd, pipeline_mode=pl.Buffered(1))


def _parg(p):
    return p.arr if isinstance(p, _LayerOf) else p


def _inproj_kernel(x_ref, g_ref, sc_ref, sh_ref, w_ref, *o_refs, segs):
    h = _rms_mod(x_ref[0], g_ref[...], sc_ref[...], sh_ref[...]).astype(BF16)
    for (off, wd), o_ref in zip(segs, o_refs):
        o_ref[0] = jnp.dot(h, w_ref[:, off:off + wd], preferred_element_type=F32).astype(o_ref.dtype)


def _inproj(x, g, sc, sh, w, widths, dtypes, tm):
    G, Tg, D = x.shape
    segs, off = [], 0
    for wd in widths:
        segs.append((off, wd))
        off += wd
    return pl.pallas_call(
        functools.partial(_inproj_kernel, segs=tuple(segs)),
        grid=(G, Tg // tm),
        in_specs=[pl.BlockSpec((1, tm, D), lambda g_, t: (g_, t, 0)),
                  _pspec(g), _mod_spec(sc, tm, D), _mod_spec(sh, tm, D), _pspec(w)],
        out_specs=[pl.BlockSpec((1, tm, wd), lambda g_, t: (g_, t, 0)) for wd in widths],
        out_shape=[jax.ShapeDtypeStruct((G, Tg, wd), dt) for wd, dt in zip(widths, dtypes)],
        compiler_params=_cparams(2),
        name="in_proj",
    )(x, _parg(g), sc.arr, sh.arr, _parg(w))


def _compress_bias(w1_ref, pe_ref, bias_sc):
    for c in range(2):
        halves = []
        for h in range(2):
            pe_rows = jnp.broadcast_to(pe_ref[c, h:h + 1, :], (SUBLANES, pe_ref.shape[2]))
            halves.append(_bdot(pe_rows, w1_ref[c, :, h * 2 * LANES:(h + 1) * 2 * LANES]))
        bias_sc[c] = jnp.concatenate(halves, axis=1)


def _compress_x(xrefs, nch, w1_ref, bias_sc, b1_ref, w2_ref):
    outs = []
    for c in range(2):
        lhs = jnp.concatenate([xrefs[c][pl.ds(p, nch, stride=CMP_STRIDE), :].astype(BF16)
                               for p in range(CMP_STRIDE)], axis=1)
        outs.append(_compress_one(c, lhs, nch, w1_ref, bias_sc, b1_ref, w2_ref))
    return outs


def _compress_one(c, lhs, nch, w1_ref, bias_sc, b1_ref, w2_ref):
    last = lax.broadcasted_iota(jnp.int32, (nch, 1), 0) == nch - 1
    acc = jnp.dot(lhs, w1_ref[c], preferred_element_type=F32) + bias_sc[c, 0:1, :]
    lo = acc[:, :2 * LANES]
    hi = acc[:, 2 * LANES:]
    hi_next = jnp.where(last, 0.0, pltpu.roll(hi, nch - 1, 0))
    hid = _gelu(lo + hi_next + b1_ref[c])
    return _bdot(hid, w2_ref[c])


def _compress_prompt_kernel(krows_ref, vrows_ref, w1_ref, pe_ref, b1_ref, w2_ref, kc_ref, vc_ref, bias_sc, *, nch):
    @pl.when(pl.program_id(0) == 0)
    def _():
        _compress_bias(w1_ref, pe_ref, bias_sc)

    kc, vc = _compress_x((krows_ref.at[0], vrows_ref.at[0]), nch, w1_ref, bias_sc, b1_ref, w2_ref)
    kc_ref[0] = kc
    vc_ref[0] = vc


def _compress_prompt(rows, cw):
    B, T, _ = rows.shape
    nch = T // CMP_STRIDE
    w1, pe, b1, w2 = cw
    return pl.pallas_call(
        functools.partial(_compress_prompt_kernel, nch=nch),
        grid=(B,),
        in_specs=[pl.BlockSpec((1, T, LANES), lambda b: (b, 0, 0)),
                  pl.BlockSpec((1, T, LANES), lambda b: (b, 0, 1)),
                  _pspec(w1), _pspec(pe), _pspec(b1), _pspec(w2)],
        out_specs=[pl.BlockSpec((1, nch, LANES), lambda b: (b, 0, 0))] * 2,
        out_shape=[jax.ShapeDtypeStruct((B, nch, LANES), F32)] * 2,
        scratch_shapes=[pltpu.VMEM((2, SUBLANES, 4 * LANES), F32)],
        compiler_params=_cparams(1),
        name="nsa_compress_prompt",
    )(rows, rows, _parg(w1), _parg(pe), _parg(b1), _parg(w2))


def _rep_all(a):
    return jnp.concatenate([a] * (NSA_KV * NSA_HPG), axis=0)


def _rep_heads(a, Qb):
    return jnp.concatenate([a[:Qb]] * NSA_HPG + [a[Qb:]] * NSA_HPG, axis=0)


def _nsa_front(qblk, q0, Qb, kc, vc, n_cmp, n_sel):
    R2 = 2 * Qb
    npad = kc.shape[0]
    lane = lax.broadcasted_iota(jnp.int32, (Qb, LANES), 1)
    lo_half = lane < NSA_HD
    scale = NSA_HD ** -0.5 * LOG2E
    pieces = []
    for hd in range(NSA_KV * NSA_HPG):
        k, pair, half = hd // NSA_HPG, hd // 2, hd % 2
        sl = qblk[:, pair * LANES:(pair + 1) * LANES] * scale
        if half != k:
            sl = pltpu.roll(sl, NSA_HD, 1)
        pieces.append(jnp.where(lo_half if k == 0 else jnp.logical_not(lo_half), sl, 0.0))
    qs = jnp.concatenate(pieces, axis=0).astype(BF16)
    qp1 = q0 + lax.broadcasted_iota(jnp.int32, (Qb, 1), 0)

    n_idx = lax.broadcasted_iota(jnp.int32, (1, npad), 1)
    visible = (n_idx * CMP_STRIDE + (CMP_BLOCK - 1) <= qp1) & (n_idx < n_cmp)
    s = _bdot_nt(qs, kc) + _rep_all(jnp.where(visible, 0.0, -BIG))
    e = jnp.exp2(s - jnp.max(s, axis=-1, keepdims=True))
    any_visible = _rep_all((qp1 >= CMP_BLOCK - 1) & (n_cmp > 0))
    p_c = e * jnp.where(any_visible, 1.0 / jnp.sum(e, axis=-1, keepdims=True), 0.0)
    o_c = _bdot(p_c, vc)

    psum = []
    for k in range(NSA_KV):
        acc = p_c[(k * NSA_HPG) * Qb:(k * NSA_HPG + 1) * Qb]
        for r in range(1, NSA_HPG):
            acc = acc + p_c[(k * NSA_HPG + r) * Qb:(k * NSA_HPG + r + 1) * Qb]
        psum.append(acc)
    psum = jnp.concatenate(psum + [jnp.zeros((LANES - R2, npad), F32)], axis=0)
    nsr = -(-n_sel // SUBLANES) * SUBLANES
    sj = lax.broadcasted_iota(jnp.int32, (nsr, npad), 0) * SEL_BLOCK
    ci = lax.broadcasted_iota(jnp.int32, (nsr, npad), 1) * CMP_STRIDE
    selmap = jnp.where((ci < sj + SEL_BLOCK) & (ci + CMP_BLOCK > sj), 1.0, 0.0).astype(BF16)
    ph, plo = _split(psum)
    imp = _bdot_nt(selmap, ph) + _bdot_nt(selmap, plo)
    col_pos = q0 + (lax.broadcasted_iota(jnp.int32, (1, LANES), 1) & (Qb - 1))
    return qs, qp1, o_c, _select_blocks(imp, col_pos, n_sel)


def _select_blocks(imp, qpos, n_sel):
    nsr, ncol = imp.shape
    j = lax.broadcasted_iota(jnp.int32, (nsr, ncol), 0)
    jf = j.astype(F32)
    cur = qpos >> 6
    forced = (j == 0) | (j == cur) | (j == cur - 1)
    imp = jnp.where(forced, SEL_FORCE, imp)
    imp = jnp.where(j * SEL_BLOCK <= qpos, imp, -SEL_FORCE)
    imp = jnp.where(j < n_sel, imp, NEG)

    def pick(_, carry):
        imp_c, sel_c = carry
        m = jnp.max(imp_c, axis=0, keepdims=True)
        first = jnp.min(jnp.where(imp_c == m, jf, float(nsr)), axis=0, keepdims=True)
        hit = jf == first
        return jnp.where(hit, NEG, imp_c), jnp.where(hit, 1.0, sel_c)

    _, sel = lax.fori_loop(0, min(SEL_TOPK, n_sel), pick, (imp, jnp.zeros((nsr, ncol), F32)), unroll=True)
    return sel


def _nsa_combine(gate, o_c, o_s, o_w, Qb, o_ref):
    lo_half = lax.broadcasted_iota(jnp.int32, (Qb, LANES), 1) < NSA_HD
    g = _sigmoid(gate)
    for pair in range(NSA_KV * NSA_HPG // 2):
        halves = []
        for half, hd in enumerate((2 * pair, 2 * pair + 1)):
            k = hd // NSA_HPG
            rs = slice(hd * Qb, (hd + 1) * Qb)
            c = hd * 3
            o = g[:, c:c + 1] * o_c[rs] + g[:, c + 1:c + 2] * o_s[rs] + g[:, c + 2:c + 3] * o_w[rs]
            halves.append(o if half == k else pltpu.roll(o, NSA_HD, 1))
        o_ref[0, :, pair * LANES:(pair + 1) * LANES] = jnp.where(lo_half, halves[0], halves[1])


def _block_columns(k0, n):
    key = k0 + lax.broadcasted_iota(jnp.int32, (n, LANES), 0)
    blk = lax.broadcasted_iota(jnp.int32, (n, LANES), 1)
    return jnp.where((key >> 6) == blk, BIG, 0.0).astype(BF16)


def _block_rows(k0, n):
    key = k0 + lax.broadcasted_iota(jnp.int32, (LANES, n), 1)
    blk = lax.broadcasted_iota(jnp.int32, (LANES, n), 0)
    return jnp.where((key >> 6) == blk, BIG, 0.0).astype(BF16)


def _lanes_all(a):
    return jnp.concatenate([a] * (NSA_KV * NSA_HPG), axis=1)


def _lanes_heads(a, Qb):
    return jnp.concatenate([a[:, :Qb]] * NSA_HPG + [a[:, Qb:]] * NSA_HPG, axis=1)


def _softmax_cols(s):
    e = jnp.exp2(s - jnp.max(s, axis=0, keepdims=True))
    return e, 1.0 / jnp.sum(e, axis=0, keepdims=True)


def _nsa_prompt_kernel(q_ref, gate_ref, kc_ref, vc_ref, rows_ref, win_ref, o_ref, kaug, vt,
                       *, T, Qb, n_cmp, n_sel, wl):
    i = pl.program_id(1)
    q0 = i * Qb
    R = NSA_KV * NSA_HPG * Qb
    R2 = NSA_KV * Qb

    @pl.when(i == 0)
    def _():
        def pack(c, carry):
            r0 = pl.multiple_of(c * SEL_CHUNK, SEL_CHUNK)
            kaug[pl.ds(r0, SEL_CHUNK), 0:LANES] = rows_ref[0, pl.ds(r0, SEL_CHUNK), 2 * LANES:3 * LANES].astype(BF16)
            kaug[pl.ds(r0, SEL_CHUNK), LANES:2 * LANES] = _block_columns(r0, SEL_CHUNK)
            vt[:, pl.ds(r0, SEL_CHUNK)] = rows_ref[0, pl.ds(r0, SEL_CHUNK), 3 * LANES:4 * LANES].T.astype(BF16)
            return carry
        lax.fori_loop(0, T // SEL_CHUNK, pack, 0)

    top = lax.broadcasted_iota(jnp.int32, (LANES, Qb), 0) < NSA_HD
    scale = NSA_HD ** -0.5 * LOG2E
    qblk = q_ref[0]
    cols = []
    for hd in range(NSA_KV * NSA_HPG):
        k, pair, half = hd // NSA_HPG, hd // 2, hd % 2
        t = (qblk[:, pair * LANES:(pair + 1) * LANES] * scale).T
        if half != k:
            t = jnp.concatenate([t[NSA_HD:], t[:NSA_HD]], axis=0)
        cols.append(jnp.where(top if k == 0 else jnp.logical_not(top), t, 0.0))
    qs = jnp.concatenate(cols, axis=1).astype(BF16)
    qlane = q0 + lax.broadcasted_iota(jnp.int32, (1, Qb), 1)
    qpos = _lanes_all(qlane)

    kc = kc_ref[0]
    npad = kc.shape[0]
    n_idx = lax.broadcasted_iota(jnp.int32, (npad, 1), 0)
    visible = (n_idx * CMP_STRIDE + (CMP_BLOCK - 1) <= qlane) & (n_idx < n_cmp)
    e_c, inv_c = _softmax_cols(_bdot(kc, qs) + _lanes_all(jnp.where(visible, 0.0, -BIG)))
    any_visible = _lanes_all((qlane >= CMP_BLOCK - 1) & (n_cmp > 0))
    p_c = e_c * jnp.where(any_visible, inv_c, 0.0)
    o_c = _bdot(vc_ref[0].T, p_c)

    psum = []
    for k in range(NSA_KV):
        acc = p_c[:, (k * NSA_HPG) * Qb:(k * NSA_HPG + 1) * Qb]
        for r in range(1, NSA_HPG):
            acc = acc + p_c[:, (k * NSA_HPG + r) * Qb:(k * NSA_HPG + r + 1) * Qb]
        psum.append(acc)
    psum = jnp.concatenate(psum, axis=1)
    nsr = -(-n_sel // SUBLANES) * SUBLANES
    sj = lax.broadcasted_iota(jnp.int32, (nsr, npad), 0) * SEL_BLOCK
    ci = lax.broadcasted_iota(jnp.int32, (nsr, npad), 1) * CMP_STRIDE
    selmap = jnp.where((ci < sj + SEL_BLOCK) & (ci + CMP_BLOCK > sj), 1.0, 0.0).astype(BF16)
    ph, plo = _split(psum)
    imp = (jnp.dot(selmap, ph, preferred_element_type=F32)
           + jnp.dot(selmap, plo, preferred_element_type=F32))
    sel = _select_blocks(imp, jnp.concatenate([qlane] * NSA_KV, axis=1), n_sel)
    selm =jnp.concatenate([sel - 1.0, jnp.zeros((LANES - nsr, R2), F32)], axis=0) if nsr < LANES else sel - 1.0
    qaug = jnp.concatenate([qs, _lanes_heads(selm, Qb).astype(BF16)], axis=0)

    def update(s, vcols, carry):
        m, l, acc = carry
        m_new = jnp.maximum(m, jnp.max(s, axis=0, keepdims=True))
        alpha = jnp.exp2(m - m_new)
        p = jnp.exp2(s - m_new)
        l = alpha * l + jnp.sum(p, axis=0, keepdims=True)
        acc = alpha * acc + jnp.dot(vcols, p.astype(BF16), preferred_element_type=F32)
        return m_new, l, acc

    def scores(k0):
        return jnp.dot(kaug[pl.ds(k0, SEL_CHUNK), :], qaug, preferred_element_type=F32)

    def full_chunk(c, carry):
        k0 = pl.multiple_of(c * SEL_CHUNK, SEL_CHUNK)
        return update(scores(k0), vt[:, pl.ds(k0, SEL_CHUNK)], carry)

    n_full = q0 // SEL_CHUNK
    init = (jnp.full((1, R), -4.0 * BIG, F32), jnp.zeros((1, R), F32), jnp.zeros((LANES, R), F32))
    carry = lax.fori_loop(0, n_full, full_chunk, init)
    k0 = pl.multiple_of(n_full * SEL_CHUNK, SEL_CHUNK)
    kpos = k0 + lax.broadcasted_iota(jnp.int32, (SEL_CHUNK, 1), 0)
    s_diag = jnp.where(kpos <= qpos, scores(k0), -2.0 * BIG)
    _, l_s, acc_s = update(s_diag, vt[:, pl.ds(k0, SEL_CHUNK)], carry)
    o_s = acc_s * (1.0 / l_s)

    ws = pl.multiple_of(jnp.maximum(q0 - WINDOW, 0), LANES)
    wk = win_ref[0, pl.ds(ws, wl), 0:LANES]
    wv = win_ref[0, pl.ds(ws, wl), LANES:2 * LANES]
    dpos = qlane - (ws + lax.broadcasted_iota(jnp.int32, (wl, 1), 0))
    e_w, inv_w = _softmax_cols(_bdot(wk, qs) + _lanes_all(jnp.where((dpos >= 0) & (dpos <= WINDOW), 0.0, -BIG)))
    o_w = _bdot(wv.T, e_w) * inv_w

    g = _sigmoid(gate_ref[0]).T
    for pair in range(NSA_KV * NSA_HPG // 2):
        rows = []
        for hd in (2 * pair, 2 * pair + 1):
            k = hd // NSA_HPG
            cs = slice(hd * Qb, (hd + 1) * Qb)
            c = hd * 3
            o = g[c:c + 1, :] * o_c[:, cs] + g[c + 1:c + 2, :] * o_s[:, cs] + g[c + 2:c + 3, :] * o_w[:, cs]
            rows.append(o[k * NSA_HD:(k + 1) * NSA_HD])
        o_ref[0, :, pair * LANES:(pair + 1) * LANES] = jnp.concatenate(rows, axis=0).T.astype(o_ref.dtype)


def _nsa_prompt(nq, gate, kc, vc, rows, win):
    B, T, HD = nq.shape
    Qb = QBLOCK
    nch = kc.shape[1]
    n_sel = -(-T // SEL_BLOCK)
    wl = WINDOW + Qb
    assert T % SEL_CHUNK == 0 and T >= wl and SEL_TOPK <= n_sel <= LANES
    return pl.pallas_call(
        functools.partial(_nsa_prompt_kernel, T=T, Qb=Qb, n_cmp=nch - 1, n_sel=n_sel, wl=wl),
        grid=(B, T // Qb),
        in_specs=[pl.BlockSpec((1, Qb, HD), lambda b, i: (b, i, 0)),
                  pl.BlockSpec((1, Qb, LANES), lambda b, i: (b, i, 0)),
                  pl.BlockSpec((1, nch, LANES), lambda b, i: (b, 0, 0)),
                  pl.BlockSpec((1, nch, LANES), lambda b, i: (b, 0, 0)),
                  pl.BlockSpec((1, T, 4 * LANES), lambda b, i: (b, 0, 0)),
                  pl.BlockSpec((1, T, 2 * LANES), lambda b, i: (b, 0, 0))],
        out_specs=pl.BlockSpec((1, Qb, HD), lambda b, i: (b, i, 0)),
        out_shape=jax.ShapeDtypeStruct((B, T, HD), BF16),
        scratch_shapes=[pltpu.VMEM((T, 2 * LANES), BF16), pltpu.VMEM((LANES, T), BF16)],
        compiler_params=_cparams(2),
        name="nsa_attn_prompt",
    )(nq, gate, kc, vc, rows, win)


def _pages_copy(cache_hbm, layer, page, r0, dst, j, sem):
    n = cache_hbm.shape[-1]
    return pltpu.make_async_copy(cache_hbm.at[layer, page, pl.ds(r0, 2)],
                                 dst.at[:, :, pl.ds(pl.multiple_of(j * n, n), n)], sem)


def _pages_start(pt_ref, b, cache_hbm, layer, r0, dst, sem, npages):
    def issue(jj, carry):
        for prio in range(2):
            j = 2 * jj + prio
            _pages_copy(cache_hbm, layer, pt_ref[b, j], r0, dst, j, sem).start(priority=prio)
        return carry
    lax.fori_loop(0, npages // 2, issue, 0, unroll=PAGE_UNROLL // 2)


def _pages_wait(cache_hbm, layer, r0, dst, sem, npages):
    def wait(j, carry):
        _pages_copy(cache_hbm, layer, 0, r0, dst, j, sem).wait()
        return carry
    lax.fori_loop(0, npages, wait, 0, unroll=PAGE_UNROLL)


def _softmax2(s1, s2, mask2):
    s2 = jnp.where(mask2, s2, -2.0 * BIG)
    m = jnp.maximum(jnp.max(s1, axis=-1, keepdims=True), jnp.max(s2, axis=-1, keepdims=True))
    e1 = jnp.exp2(s1 - m)
    e2 = jnp.exp2(s2 - m)
    den = jnp.sum(e1, axis=-1, keepdims=True) + jnp.sum(e2, axis=-1, keepdims=True)
    return e1, e2, 1.0 / den


def _nsa_sample_kernel(pt_ref, q_ref, gate_ref, rows_ref, wt_ref, wnew_ref, cache_hbm,
                       w1_ref, pe_ref, b1_ref, w2_ref, o_ref, wout_ref,
                       cmpbuf, selbuf, xk, xv, kaug, vt, newbuf, wnewbuf, bias_sc, perm_sc, csem, ssem,
                       *, layer, npages, P, Q, wb, nb, n_sel):
    b = pl.program_id(0)
    nseq = pl.num_programs(0)
    page = cache_hbm.shape[-1]
    nch = P // CMP_STRIDE

    @pl.when(b == 0)
    def _():
        _pages_start(pt_ref, 0, cache_hbm, layer, 0, cmpbuf, csem, npages)
        _pages_start(pt_ref, 0, cache_hbm, layer, 2, selbuf, ssem, npages)
        _compress_bias(w1_ref, pe_ref, bias_sc)

        def blocks(c, carry):
            c0 = pl.multiple_of(c * CAST_CHUNK, CAST_CHUNK)
            kaug[LANES:2 * LANES, pl.ds(c0, CAST_CHUNK)] = _block_rows(c0, CAST_CHUNK)
            return carry
        lax.fori_loop(0, P // CAST_CHUNK, blocks, 0)
        newbuf[...] = jnp.zeros(newbuf.shape, F32)
        wnewbuf[...] = jnp.zeros(wnewbuf.shape, F32)
        src = lax.broadcasted_iota(jnp.int32, (2 * page, 2 * page), 0)
        dst_col = lax.broadcasted_iota(jnp.int32, (2 * page, 2 * page), 1)
        t = src & (page - 1)
        want = (src - t) + (t % CMP_STRIDE) * (page // CMP_STRIDE) + t // CMP_STRIDE
        perm_sc[...] = jnp.where(dst_col == want, 1.0, 0.0).astype(BF16)

    _pages_wait(cache_hbm, layer, 0, cmpbuf, csem, npages)
    per_page = page // CMP_STRIDE
    for c, dst in enumerate((xk, xv)):
        for jp in range(npages // 2):
            cols = cmpbuf[c, :, 2 * jp * page:2 * (jp + 1) * page].astype(BF16)
            regrouped = jnp.dot(cols, perm_sc[...], preferred_element_type=F32)
            for half in range(2):
                tok = regrouped[:, half * page:(half + 1) * page].T
                r0 = (2 * jp + half) * per_page
                for p in range(CMP_STRIDE):
                    dst[r0:r0 + per_page, p * LANES:(p + 1) * LANES] = tok[p * per_page:(p + 1) * per_page, :]
    kc = _compress_one(0, xk[...].astype(BF16), nch, w1_ref, bias_sc, b1_ref, w2_ref)
    vc = _compress_one(1, xv[...].astype(BF16), nch, w1_ref, bias_sc, b1_ref, w2_ref)

    @pl.when(b + 1 < nseq)
    def _():
        _pages_start(pt_ref, b + 1, cache_hbm, layer, 0, cmpbuf, csem, npages)

    _pages_wait(cache_hbm, layer, 2, selbuf, ssem, npages)

    def pack(c, carry):
        c0 = pl.multiple_of(c * CAST_CHUNK, CAST_CHUNK)
        kaug[0:LANES, pl.ds(c0, CAST_CHUNK)] = selbuf[0, :, pl.ds(c0, CAST_CHUNK)].astype(BF16)
        vt[:, pl.ds(c0, CAST_CHUNK)] = selbuf[1, :, pl.ds(c0, CAST_CHUNK)].astype(BF16)
        return carry
    lax.fori_loop(0, P // CAST_CHUNK, pack, 0)

    @pl.when(b + 1 < nseq)
    def _():
        _pages_start(pt_ref, b + 1, cache_hbm, layer, 2, selbuf, ssem, npages)

    newbuf[0:Q, :] = rows_ref[0, :, 2 * LANES:4 * LANES]
    wnewbuf[0:Q, :] = wnew_ref[0]

    qs, qp1, o_c, sel_t = _nsa_front(q_ref[0], P, Q, kc, vc, nch - 1, n_sel)
    qpos = _rep_all(qp1)
    lane = lax.broadcasted_iota(jnp.int32, (1, LANES), 1)
    new_pos = P + lane
    is_new = lane < Q

    def as_rows(blk):
        if blk.shape[0] < LANES:
            blk = jnp.concatenate([blk, jnp.zeros((LANES - blk.shape[0], LANES), F32)], axis=0)
        return blk.T[0:NSA_KV * Q]
    sel_rows = as_rows(sel_t[0:min(LANES, sel_t.shape[0])])
    sel_new = as_rows(sel_t[nb:nb + SUBLANES])[:, 0:1]

    qaug = jnp.concatenate([qs, _rep_heads(sel_rows - 1.0, Q).astype(BF16)], axis=1)
    s_past = jnp.dot(qaug, kaug[...], preferred_element_type=F32)
    s_new = _bdot_nt(qs, newbuf[:, 0:LANES])
    new_ok = is_new & (new_pos <= qpos) & (_rep_heads(sel_new, Q) > 0.5)
    e1, e2, inv = _softmax2(s_past, s_new, new_ok)
    o_s = (lax.dot_general(e1.astype(BF16), vt[...], (((1,), (1,)), ((), ())), preferred_element_type=F32)
           + _bdot(e2, newbuf[:, LANES:2 * LANES])) * inv

    dpast = qp1 - ((P - wb) + lax.broadcasted_iota(jnp.int32, (1, wb), 1))
    s_wp = (jnp.dot(qs, wt_ref[0, 0].astype(BF16), preferred_element_type=F32)
            + _rep_all(jnp.where((dpast >= 0) & (dpast <= WINDOW), 0.0, -BIG)))
    s_wn = _bdot_nt(qs, wnewbuf[:, 0:LANES])
    dnew = qpos - new_pos
    e1, e2, inv = _softmax2(s_wp, s_wn, is_new & (dnew >= 0) & (dnew <= WINDOW))
    o_w = (lax.dot_general(e1.astype(BF16), wt_ref[0, 1].astype(BF16), (((1,), (1,)), ((), ())),
                           preferred_element_type=F32)
           + _bdot(e2, wnewbuf[:, LANES:2 * LANES])) * inv

    _nsa_combine(gate_ref[0], o_c, o_s, o_w, Q, o_ref)

    for kv in range(2):
        new_cols = wnewbuf[:, kv * LANES:(kv + 1) * LANES].T[:, 0:Q]
        wout_ref[0, kv] = jnp.concatenate([wt_ref[0, kv][:, Q:], new_cols], axis=1)


def _nsa_sample(page_table, nq, gate, rows, win_t, win_new, cache_t, layer, cw):
    DB, Q, HD = nq.shape
    npages = page_table.shape[1]
    page = cache_t.shape[-1]
    P = npages * page
    wb = win_t.shape[-1]
    nb = P // SEL_BLOCK
    n_sel = -(-(P + Q) // SEL_BLOCK)
    w1, pe, b1, w2 = cw
    assert Q == SUBLANES and P % SEL_BLOCK == 0 and Q <= SEL_BLOCK and nb <= LANES and n_sel >= SEL_TOPK
    assert P % CAST_CHUNK == 0 and (P + Q) // CMP_STRIDE == P // CMP_STRIDE and page == LANES
    assert npages % 2 == 0 and page // CMP_STRIDE == SUBLANES and wb >= Q
    bs = lambda shape: pl.BlockSpec((1,) + shape, lambda b, pt: (b,) + (0,) * len(shape))
    win_spec = pl.BlockSpec((None, 1, 2, LANES, wb), lambda b, pt: (layer, b, 0, 0, 0))
    grid_spec = pltpu.PrefetchScalarGridSpec(
        num_scalar_prefetch=1, grid=(DB,),
        in_specs=[bs((Q, HD)), bs((Q, LANES)), bs((Q, 4 * LANES)), win_spec, bs((Q, 2 * LANES)),
                  pl.BlockSpec(memory_space=pl.ANY),
                  _pspec(w1), _pspec(pe), _pspec(b1), _pspec(w2)],
        out_specs=[bs((Q, HD)), bs((2, LANES, wb))],
        scratch_shapes=[pltpu.VMEM((2, LANES, P), F32), pltpu.VMEM((2, LANES, P), F32),
                        pltpu.VMEM((P // CMP_STRIDE, CMP_STRIDE * LANES), F32),
                        pltpu.VMEM((P // CMP_STRIDE, CMP_STRIDE * LANES), F32),
                        pltpu.VMEM((2 * LANES, P), BF16), pltpu.VMEM((LANES, P), BF16),
                        pltpu.VMEM((LANES, 2 * LANES), F32), pltpu.VMEM((LANES, 2 * LANES), F32),
                        pltpu.VMEM((2, SUBLANES, 4 * LANES), F32), pltpu.VMEM((2 * page, 2 * page), BF16),
                        pltpu.SemaphoreType.DMA(()), pltpu.SemaphoreType.DMA(())])
    return pl.pallas_call(
        functools.partial(_nsa_sample_kernel, layer=layer, npages=npages, P=P, Q=Q, wb=wb, nb=nb, n_sel=n_sel),
        grid_spec=grid_spec,
        out_shape=[jax.ShapeDtypeStruct((DB, Q, HD), F32), jax.ShapeDtypeStruct((DB, 2, LANES, wb), F32)],
        compiler_params=_cparams(1),
        name="nsa_sample",
    )(page_table, nq, gate, rows, win_t, win_new, cache_t, _parg(w1), _parg(pe), _parg(b1), _parg(w2))


def _ret_kernel(q_ref, k_ref, v_ref, g_ref, cos_ref, sin_ref, s0_ref, gn_ref, o_ref, snew_ref, s_sc,
                *, C, nC, nseq):
    c = pl.program_id(1)

    @pl.when(c == 0)
    def _():
        s_sc[...] = s0_ref[...]

    cosf = cos_ref[...]
    sinf = sin_ref[...]
    diff = (lax.broadcasted_iota(jnp.int32, (C, C), 0) - lax.broadcasted_iota(jnp.int32, (C, C), 1)).astype(F32)
    ii = lax.broadcasted_iota(jnp.int32, (C, 1), 0).astype(F32)
    half = RET_DK // 2
    for h in range(RET_HEADS):
        lg = math.log(1.0 - 2.0 ** (-5.0 - h))
        hs = slice(h * RET_DK, (h + 1) * RET_DK)
        decay = jnp.where(diff >= 0, jnp.exp(jnp.maximum(diff, 0.0) * lg), 0.0)
        cross = jnp.exp((ii + 1.0) * lg)
        kweight = jnp.exp((C - 1.0 - ii) * lg)
        for b in range(nseq):
            q = q_ref[b, :, hs]
            k = k_ref[b, :, hs]
            v = v_ref[b, :, hs]
            qr = q * cosf + pltpu.roll(q, half, 1) * sinf
            kr = (k * cosf + pltpu.roll(k, half, 1) * sinf) * (RET_DK ** -0.5)
            o_inner = _bdot(_bdot_nt(qr, kr) * decay, v)
            s_old = s_sc[b, h]
            o_cross = _bdot(qr, s_old) * cross
            kv = lax.dot_general((kr * kweight).astype(BF16), v.astype(BF16), (((0,), (0,)), ((), ())),
                                 preferred_element_type=F32)
            s_sc[b, h] = math.exp(C * lg) * s_old + kv
            o = o_inner + o_cross
            mu = jnp.mean(o, axis=-1, keepdims=True)
            var = jnp.mean(jnp.square(o - mu), axis=-1, keepdims=True)
            gate = g_ref[b, :, hs]
            o_ref[b, :, hs] = (((o - mu) * lax.rsqrt(var + EPS)) * gn_ref[:, hs]
                               * (gate * _sigmoid(gate))).astype(o_ref.dtype)

    @pl.when(c == nC - 1)
    def _():
        snew_ref[...] = s_sc[...]


def _retention(rq, rk, rv, rg, cosf, sinf, s0, gn, nseq, out_dtype):
    B, T, W = rq.shape
    C = RET_CHUNK if (T >= RET_CHUNK and T % RET_CHUNK == 0) else T
    nC = T // C
    tok = pl.BlockSpec((nseq, C, W), lambda b, c: (b, c, 0))
    tab = pl.BlockSpec((C, RET_DK), lambda b, c: (c, 0))
    st = pl.BlockSpec((nseq,) + s0.shape[1:], lambda b, c: (b, 0, 0, 0))
    return pl.pallas_call(
        functools.partial(_ret_kernel, C=C, nC=nC, nseq=nseq),
        grid=(B // nseq, nC),
        in_specs=[tok, tok, tok, tok, tab, tab, st, _pspec(gn)],
        out_specs=[tok, st],
        out_shape=[jax.ShapeDtypeStruct((B, T, W), out_dtype), jax.ShapeDtypeStruct(s0.shape, F32)],
        scratch_shapes=[pltpu.VMEM((nseq,) + s0.shape[1:], F32)],
        compiler_params=_cparams(2),
        name="retention",
    )(rq, rk, rv, rg, cosf, sinf, s0, _parg(gn))


def _shift_carry(x, k, tail8):
    r = pltpu.roll(x, k, 0)
    row8 = lax.broadcasted_iota(jnp.int32, (SUBLANES, 1), 0)
    first = jnp.where(row8 >= k, r[:SUBLANES], pltpu.roll(tail8, k, 0))
    return jnp.concatenate([first, r[SUBLANES:]], axis=0)


def _shift_seg(x, k, fill, tpos):
    return jnp.where(tpos >= k, pltpu.roll(x, k, 0), fill)


def _rglru_kernel(x_ref, gate_ref, st_ref, h0_ref, cw_ref, cb_ref, wa_ref, ba_ref, wx_ref, bx_ref, lam_ref,
                  o_ref, h_ref, tail_sc, h_sc, *, tm, seg):
    carry = seg == 0
    x = x_ref[0]
    rows = lax.broadcasted_iota(jnp.int32, (tm, 1), 0)
    if carry:
        @pl.when(pl.program_id(1) == 0)
        def _():
            tail_sc[...] = st_ref[0]
            h_sc[...] = h0_ref[0]
        tail8 = tail_sc[...]
        shifted = [_shift_carry(x, k, tail8) for k in range(1, RG_CONV)]
        tpos = rows & (SUBLANES - 1)
    else:
        tpos = rows & (seg - 1)
        shifted = [_shift_seg(x, k, st_ref[k - 1], tpos) for k in range(1, RG_CONV)]
    xc = cb_ref[...] + cw_ref[RG_CONV - 1:RG_CONV, :] * x
    for k in range(1, RG_CONV):
        xc = xc + cw_ref[RG_CONV - 1 - k:RG_CONV - k, :] * shifted[k - 1]
    r = _sigmoid(_bdot(xc, wa_ref[...]) + ba_ref[...])
    i = _sigmoid(_bdot(xc, wx_ref[...]) + bx_ref[...])
    lam = lam_ref[...]
    softplus = jnp.maximum(-lam, 0.0) + jnp.log(1.0 + jnp.exp(-jnp.abs(lam)))
    log_a = (-RG_C * r) * softplus
    a = jnp.exp(log_a)
    gap = 1.0 - a * a
    bt = jnp.where(gap > 0, gap * lax.rsqrt(gap), 0.0) * (i * xc)
    if not carry:
        bt = bt + a * h0_ref[0]
    k = 1
    while k < SUBLANES:
        ok = tpos >= k
        a_prev = jnp.where(ok, pltpu.roll(a, k, 0), 1.0)
        b_prev = jnp.where(ok, pltpu.roll(bt, k, 0), 0.0)
        bt = a * b_prev + bt
        a = a * a_prev
        k *= 2
    if carry:
        h_prev = h_sc[SUBLANES - 1:SUBLANES, :]
        groups = []
        for g0 in range(0, tm, SUBLANES):
            h_g = bt[g0:g0 + SUBLANES] + a[g0:g0 + SUBLANES] * h_prev
            groups.append(h_g)
            h_prev = h_g[SUBLANES - 1:SUBLANES, :]
        bt = jnp.concatenate(groups, axis=0)
    o_ref[0] = (bt * _gelu(gate_ref[0].astype(F32))).astype(o_ref.dtype)
    if carry:
        tail_sc[...] = x[tm - SUBLANES:]
        h_sc[...] = bt[tm - SUBLANES:]
        h_ref[0] = bt[tm - SUBLANES:]
    else:
        h_ref[0] = bt


def _rglru(rx, rgate, st, h0, rw, tm, seg):
    G, Tg, W = rx.shape
    tok = pl.BlockSpec((1, tm, W), lambda g, t: (g, t, 0))
    if seg == 0:
        st_spec = pl.BlockSpec((1, SUBLANES, W), lambda g, t: (g, 0, 0))
        h0_spec = pl.BlockSpec((1, SUBLANES, W), lambda g, t: (g, 0, 0))
        h_spec = pl.BlockSpec((1, SUBLANES, W), lambda g, t: (g, 0, 0))
        h_shape = (G, SUBLANES, W)
    else:
        st_spec = pl.BlockSpec(st.shape, lambda g, t: (0, 0, 0))
        h0_spec = tok
        h_spec = tok
        h_shape = (G, Tg, W)
    return pl.pallas_call(
        functools.partial(_rglru_kernel, tm=tm, seg=seg),
        grid=(G, Tg // tm),
        in_specs=[tok, tok, st_spec, h0_spec] + [_pspec(a) for a in rw],
        out_specs=[tok, h_spec],
        out_shape=[jax.ShapeDtypeStruct((G, Tg, W), rgate.dtype), jax.ShapeDtypeStruct(h_shape, F32)],
        scratch_shapes=[pltpu.VMEM((SUBLANES, W), F32), pltpu.VMEM((SUBLANES, W), F32)],
        compiler_params=_cparams(2),
        name="rglru",
    )(rx, rgate, st, h0, *[_parg(a) for a in rw])


def _mix_ffn_kernel(x_ref, oa_ref, or_ref, oc_ref, mg_ref, gt1_ref, wa_ref, wb_ref, wc_ref, wo_ref,
                    g_ref, sc_ref, sh_ref, gt_ref, st_ref, wup_ref, cw_ref, cb_ref, wdn_ref, fg_ref,
                    y_ref, fnew_ref, tail_sc, *, tm, seg, F, chunks, final):
    D = x_ref.shape[2]
    pa = _bdot(oa_ref[0], wa_ref[...])
    pb = _bdot(or_ref[0], wb_ref[...])
    pc = _bdot(oc_ref[0], wc_ref[...])
    gate = lambda i: _sigmoid(mg_ref[0, :, i * D:(i + 1) * D].astype(F32))
    merged = gate(0) * pa + gate(1) * pb + gate(2) * pc
    x = x_ref[0] + gt1_ref[...] * _bdot(merged, wo_ref[...])

    carry = seg == 0
    h = _rms_mod(x, g_ref[...], sc_ref[...], sh_ref[...]).astype(BF16)
    rows = lax.broadcasted_iota(jnp.int32, (tm, 1), 0)
    if carry:
        @pl.when(pl.program_id(1) == 0)
        def _():
            tail_sc[...] = st_ref[0]
    else:
        tpos = rows & (seg - 1)
    acc = jnp.zeros(x.shape, F32)
    for c0, wck in chunks:
        cs = slice(c0, c0 + wck)
        gp = jnp.dot(h, wup_ref[:, c0:c0 + wck], preferred_element_type=F32)
        val = jnp.dot(h, wup_ref[:, F + c0:F + c0 + wck], preferred_element_type=F32)
        if carry:
            tail8 = tail_sc[:, cs]
            shifted = [_shift_carry(gp, k, tail8) for k in range(1, FFN_CONV)]
            tail_sc[:, cs] = gp[tm - SUBLANES:]
            fnew_ref[0, :, cs] = gp[tm - SUBLANES:]
        else:
            shifted = [_shift_seg(gp, k, st_ref[k - 1, :, cs], tpos) for k in range(1, FFN_CONV)]
            fnew_ref[0, :, cs] = gp
        gc = cb_ref[:, cs] + cw_ref[FFN_CONV - 1:FFN_CONV, cs] * gp
        for k in range(1, FFN_CONV):
            gc = gc + cw_ref[FFN_CONV - 1 - k:FFN_CONV - k, cs] * shifted[k - 1]
        act = (gc * _sigmoid(gc)) * val
        acc = acc + _bdot(act, wdn_ref[cs, :])
    y = x + gt_ref[...] * acc
    if final:
        y = (y * lax.rsqrt(jnp.mean(y * y, axis=-1, keepdims=True) + EPS)) * fg_ref[...]
    y_ref[0] = y


def _ffn_chunks(F):
    half = -(-(F // 2) // MXU_DEPTH_V7X) * MXU_DEPTH_V7X
    return ((0, half), (half, F - half)) if 0 < half < F else ((0, F),)


def _mix_ffn(x, oa, orr, oc, mg, gt1, wa, wb, wc, wo, g, sc, sh, gt, st, wup, cw, cb, wdn, fg, tm, seg, final):
    G, Tg, D = x.shape
    F = wdn.shape[0]
    tokw = lambda w: pl.BlockSpec((1, tm, w), lambda g_, t: (g_, t, 0))
    tok = tokw(D)
    if seg == 0:
        st_spec = pl.BlockSpec((1, SUBLANES, F), lambda g_, t: (g_, 0, 0))
        fn_spec = pl.BlockSpec((1, SUBLANES, F), lambda g_, t: (g_, 0, 0))
        fn_shape = (G, SUBLANES, F)
    else:
        st_spec = pl.BlockSpec(st.shape, lambda g_, t: (0, 0, 0))
        fn_spec = pl.BlockSpec((1, tm, F), lambda g_, t: (g_, t, 0))
        fn_shape = (G, Tg, F)
    return pl.pallas_call(
        functools.partial(_mix_ffn_kernel, tm=tm, seg=seg, F=F, chunks=_ffn_chunks(F), final=final),
        grid=(G, Tg // tm),
        in_specs=[tok, tokw(oa.shape[2]), tokw(orr.shape[2]), tokw(oc.shape[2]), tokw(3 * D),
                  _mod_spec(gt1, tm, D), _pspec(wa), _pspec(wb), _pspec(wc), _pspec(wo),
                  _pspec(g), _mod_spec(sc, tm, D), _mod_spec(sh, tm, D), _mod_spec(gt, tm, D),
                  st_spec, _pspec(wup), _pspec(cw), _pspec(cb), _pspec(wdn), _pspec(fg)],
        out_specs=[tok, fn_spec],
        out_shape=[jax.ShapeDtypeStruct((G, Tg, D), F32), jax.ShapeDtypeStruct(fn_shape, F32)],
        scratch_shapes=[pltpu.VMEM((SUBLANES, F), F32)],
        compiler_params=_cparams(2),
        name="mix_ffn",
    )(x, oa, orr, oc, mg, gt1.arr, _parg(wa), _parg(wb), _parg(wc), _parg(wo),
      _parg(g), sc.arr, sh.arr, gt.arr, st, _parg(wup), _parg(cw), _parg(cb), _parg(wdn), fg)


def _block_diag(w):
    n, a, b = w.shape[-3:]
    eye = jnp.eye(n, dtype=w.dtype)
    out = jnp.einsum('ij,...iab->...iajb', eye, w)
    return out.reshape(w.shape[:-3] + (n * a, n * b))


def _seg_fill(buf, k, seg):
    B, nb, C = buf.shape
    part = jnp.concatenate([buf[:, nb - k:, :], jnp.zeros((B, seg - k, C), buf.dtype)], axis=1)
    return part.reshape(B * seg, C)


def kernel(x_prompt, x_sample, cache_nsa, cache_nsa_win, state_ret, state_rglru_h, state_rglru_conv,
           state_ffn_conv, page_table, c_prompt, c_sample, norm1_g, norm2_g, w_ada, b_ada, w_in, cmp_pe,
           cmp_w1, cmp_b1, cmp_w2, ret_gn_g, rg_conv_w, rg_conv_b, rg_w_a, rg_b_a, rg_w_x, rg_b_x, rg_lambda,
           w_br_a, w_br_b, w_br_c, w_out, ffn_w_up, ffn_conv_w, ffn_conv_b, ffn_w_down, final_norm_g):
    B, T, D = x_prompt.shape
    DB, Q, _ = x_sample.shape
    L = w_in.shape[0]
    npages = page_table.shape[1]
    page = cache_nsa.shape[2]
    P = npages * page
    NQ = NSA_KV * NSA_HPG * NSA_HD
    NKV = NSA_KV * NSA_HD
    RW = RET_HEADS * RET_DK
    W = rg_conv_w.shape[2]
    F = ffn_w_down.shape[1]
    NS = DB * Q
    wbuf_len = cache_nsa_win.shape[2]
    assert Q == SUBLANES and T >= RG_CONV and P % CMP_STRIDE == 0

    mod = _ada(jnp.concatenate([c_prompt, c_sample], axis=0), w_ada, b_ada)
    cache_t = jnp.transpose(cache_nsa, (0, 1, 3, 4, 5, 2)).reshape(L, cache_nsa.shape[1], 4, NKV, page)
    win_t_all = jnp.transpose(cache_nsa_win, (0, 1, 3, 4, 5, 2)).reshape(L, DB, 2, NKV, wbuf_len)

    half = RET_DK // 2
    freq = ROPE_BASE ** (-jnp.arange(half, dtype=F32) / half)

    def rope_tables(pos):
        ang = pos.astype(F32)[:, None] * freq[None, :]
        cos, sin = jnp.cos(ang), jnp.sin(ang)
        return jnp.concatenate([cos, cos], axis=1), jnp.concatenate([-sin, sin], axis=1)

    cos_p, sin_p = rope_tables(jnp.arange(T, dtype=jnp.int32))
    cos_s, sin_s = rope_tables(P + jnp.arange(Q, dtype=jnp.int32))

    widths = (NQ, 4 * NKV, 2 * NKV, LANES, RW, RW, RW, RW, W, W, 3 * D)
    dt_prompt = (F32,) * 9 + (BF16, BF16)
    dt_sample = (F32,) * 11
    offs = np.cumsum((0, NQ, 6 * NKV, 3 * NSA_KV * NSA_HPG, RW, RW, RW, RW, W, W, 3 * D))
    ngate = 3 * NSA_KV * NSA_HPG

    xp = x_prompt
    xs = x_sample.reshape(1, NS, D)
    outs_p = [[] for _ in range(6)]
    outs_s = [[] for _ in range(6)]
    tm_p = 512 if T % 512 == 0 else T
    tm_scan = 256 if T % 256 == 0 else T

    wi = w_in.astype(BF16)
    w_cat_all = jnp.concatenate([
        wi[:, :, offs[0]:offs[2]],
        jnp.pad(wi[:, :, offs[2]:offs[3]], ((0, 0), (0, 0), (0, LANES - ngate))),
        wi[:, :, offs[3]:]], axis=2)
    grouped = lambda w: _block_diag(jnp.broadcast_to(w[..., None, :, :], w.shape[:-2] + (NSA_KV,) + w.shape[-2:]))
    cw1_all = jnp.concatenate([grouped(cmp_w1[:, :, :CMP_STRIDE]), grouped(cmp_w1[:, :, CMP_STRIDE:])], axis=-1)
    cw1_all = cw1_all.reshape(L, 2, CMP_STRIDE * NKV, 4 * LANES).astype(BF16)
    cpe_all = jnp.tile(cmp_pe, (1, 1, 1, NSA_KV)).reshape(L, 2, 2, CMP_STRIDE * NKV)
    cb1_all = jnp.tile(cmp_b1, (1, 1, NSA_KV))[:, :, None, :]
    cw2_all = grouped(cmp_w2).astype(BF16)
    row = lambda a: a[:, None, :]
    rw_all = (rg_conv_w, row(rg_conv_b), _block_diag(rg_w_a).astype(BF16), row(rg_b_a),
              _block_diag(rg_w_x).astype(BF16), row(rg_b_x), row(rg_lambda))
    wa_all = w_br_a.astype(BF16)
    wb_all = w_br_b.astype(BF16)
    wc_all = w_br_c.astype(BF16)
    wo_all = w_out.astype(BF16)
    wup_all = ffn_w_up.astype(BF16)
    wdn_all = ffn_w_down.astype(BF16)
    g1_all, g2_all, gn_all, fcb_all = row(norm1_g), row(norm2_g), row(ret_gn_g), row(ffn_conv_b)
    fg = final_norm_g[None]
    mod_p = mod[:, :B].reshape(L, B, 1, 6 * D)
    mod_s = jnp.repeat(mod[:, B:], Q, axis=1)

    for l in range(L):
        lay = lambda a: _LayerOf(a, l)
        w_cat = lay(w_cat_all)
        cw = (lay(cw1_all), lay(cpe_all), lay(cb1_all), lay(cw2_all))
        rw = tuple(lay(a) for a in rw_all)
        wa, wb, wc, wo, wup, wdn = (lay(a) for a in (wa_all, wb_all, wc_all, wo_all, wup_all, wdn_all))
        g1, g2, gn, fcw, fcb = (lay(a) for a in (g1_all, g2_all, gn_all, ffn_conv_w, fcb_all))
        final = l == L - 1

        m = [_ModOf(mod_p, l, i) for i in range(6)]
        (nq, rows, win, gate, rq, rk, rv, rg, rx, rgate, mg) = _inproj(xp, g1, m[1], m[0], w_cat, widths,
                                                                        dt_prompt, tm_p)
        kc, vc = _compress_prompt(rows, cw)
        o_a = _nsa_prompt(nq, gate, kc, vc, rows, win)
        o_r, s_new = _retention(rq, rk, rv, rg, cos_p, sin_p,
                                jnp.zeros((B, RET_HEADS, RET_DK, RET_DK), F32), gn, B, BF16)
        zs = jnp.zeros((B, SUBLANES, W), F32)
        o_c, h_tail = _rglru(rx, rgate, zs, zs, rw, tm_scan, 0)
        xp, f_tail = _mix_ffn(xp, o_a, o_r, o_c, mg, m[2], wa, wb, wc, wo, g2, m[4], m[3], m[5],
                              jnp.zeros((B, SUBLANES, F), F32), wup, fcw, fcb, wdn, fg, tm_p, 0, final)
        wn = min(WINDOW, T)
        outs_p[0].append(rows.reshape(B, T, 4, NSA_KV, NSA_HD))
        outs_p[1].append(win[:, T - wn:].reshape(B, wn, 2, NSA_KV, NSA_HD))
        outs_p[2].append(s_new)
        outs_p[3].append(h_tail[:, SUBLANES - 1])
        outs_p[4].append(rx[:, T - (RG_CONV - 1):])
        outs_p[5].append(f_tail[:, SUBLANES - (FFN_CONV - 1):])

        ms = [_ModOf(mod_s, l, i) for i in range(6)]
        (nq, rows, win, gate, rq, rk, rv, rg, rx, rgate, mg) = _inproj(xs, g1, ms[1], ms[0], w_cat, widths,
                                                                        dt_sample, NS)
        r3 = lambda a: a.reshape(DB, Q, a.shape[-1])
        o_a, win_next = _nsa_sample(page_table, r3(nq), r3(gate), r3(rows), win_t_all, r3(win), cache_t, l, cw)
        o_r, s_new = _retention(r3(rq), r3(rk), r3(rv), r3(rg), cos_s, sin_s, state_ret[l].astype(F32), gn,
                                math.gcd(DB, SUBLANES), F32)
        cbuf = state_rglru_conv[l]
        st = jnp.stack([_seg_fill(cbuf, k, Q) for k in range(1, RG_CONV)])
        h0 = jnp.pad(state_rglru_h[l].astype(F32)[:, None, :], ((0, 0), (0, Q - 1), (0, 0))).reshape(1, NS, W)
        o_c, h_all = _rglru(rx, rgate, st, h0, rw, NS, Q)
        fbuf = state_ffn_conv[l]
        fst = jnp.stack([_seg_fill(fbuf, k, Q) for k in range(1, FFN_CONV)])
        xs, g_all = _mix_ffn(xs, o_a.reshape(1, NS, NQ), o_r.reshape(1, NS, RW), o_c, mg, ms[2], wa, wb, wc, wo,
                             g2, ms[4], ms[3], ms[5], fst, wup, fcw, fcb, wdn, fg, NS, Q, final)
        outs_s[0].append(rows.reshape(DB, Q, 4, NSA_KV, NSA_HD))
        outs_s[1].append(win_next)
        outs_s[2].append(s_new)
        outs_s[3].append(h_all.reshape(DB, Q, W)[:, Q - 1])
        outs_s[4].append(jnp.concatenate([cbuf, rx.reshape(DB, Q, W)], axis=1)[:, Q:])
        outs_s[5].append(jnp.concatenate([fbuf, g_all.reshape(DB, Q, F)], axis=1)[:, Q:])

    sp = [jnp.stack(a) for a in outs_p]
    ss = [jnp.stack(a) for a in outs_s]
    ss[1] = jnp.transpose(ss[1].reshape(L, DB, 2, NSA_KV, NSA_HD, wbuf_len), (0, 1, 5, 2, 3, 4))
    return (xp, xs.reshape(DB, Q, D), sp[0], ss[0], sp[1], ss[1], sp[2], ss[2],
            sp[3], ss[3], sp[4], ss[4], sp[5], ss[5])
```
